```python
import numpy as np
import jax
import jax.numpy as jnp
from jax import lax

D_MODEL = 1024
BATCH = 8
SEQ = 2048
DEPTH = 2

CTX_LEN = 256
GRID_W = 64
MIX_DIM = D_MODEL
HEAD_DIM = 64
NA_DIM = MIX_DIM // 2
NA_HEADS = NA_DIM // HEAD_DIM
NA_WIN_ROWS = 8
NA_WIN_COLS = 16
POOL_DIM = MIX_DIM // 4
POOL_WINDOWS = (2, 4, 8, 16)
POOL_GROUPS = len(POOL_WINDOWS)
POOL_GROUP_DIM = POOL_DIM // POOL_GROUPS
ML_DIM = MIX_DIM // 4
ML_HEADS = ML_DIM // HEAD_DIM
ML_CHUNK = 64
ML_CONV_W = 5
ML_N_GATES = 4 * ML_HEADS
IN_COLS = 3 * NA_DIM + POOL_DIM + 4 * ML_DIM + ML_N_GATES
PEER_KEYS = 128
PEER_EXPERTS = PEER_KEYS * PEER_KEYS
PEER_HEADS = 8
PEER_TOPK = 16
PEER_DKEY = 256
PEER_BLOCK = 128
ROPE_BASE = 10000.0
EPS = 1e-6

kernel_name = 'hybrid_na_pool_mlstm_peer_dit'


def _rms(x, g):
    xf = x.astype(jnp.float32)
    y = xf * lax.rsqrt(jnp.mean(xf * xf, axis=-1, keepdims=True) + EPS)
    return (y * g.astype(jnp.float32)).astype(x.dtype)


def _heads(u, n_heads):
    return u.reshape(u.shape[:-1] + (n_heads, HEAD_DIM))


def _split_cols(p):
    sizes = (NA_DIM, NA_DIM, NA_DIM, POOL_DIM, ML_DIM, ML_DIM, ML_DIM, ML_DIM, ML_N_GATES)
    offs = np.cumsum(sizes)[:-1].tolist()
    return jnp.split(p, offs, axis=-1)


def _na_latent(q, k, v, k_ctx, v_ctx, rpb):
    B, S, H, d = q.shape
    R = S // GRID_W
    KR = min(NA_WIN_ROWS, R)
    qg = q.reshape(B, R, GRID_W, H, d)
    kg = k.reshape(B, R, GRID_W, H, d)
    vg = v.reshape(B, R, GRID_W, H, d)
    r = jnp.arange(R)
    r0 = jnp.clip(r - KR // 2, 0, R - KR)
    rows = r0[:, None] + jnp.arange(KR)[None, :]
    k_strip = kg[:, rows]
    v_strip = vg[:, rows]
    cq = jnp.arange(GRID_W)
    c0 = jnp.clip(cq - NA_WIN_COLS // 2, 0, GRID_W - NA_WIN_COLS)
    in_win = (cq[None, :] >= c0[:, None]) & (cq[None, :] < c0[:, None] + NA_WIN_COLS)
    br = rows - r[:, None] + (NA_WIN_ROWS - 1)
    bc = jnp.clip(cq[None, :] - cq[:, None] + (NA_WIN_COLS - 1), 0, 2 * NA_WIN_COLS - 2)
    bias = rpb[:, br[:, None, :, None], bc[None, :, None, :]]
    bias = jnp.transpose(bias, (1, 0, 2, 3, 4)).astype(jnp.float32)
    scale = HEAD_DIM ** -0.5
    s_win = jnp.einsum('brwhd,brkxhd->brhwkx', qg, k_strip).astype(jnp.float32) * scale + bias
    s_win = jnp.where(in_win[:, None, :], s_win, -jnp.inf).reshape(B, R, H, GRID_W, KR * GRID_W)
    s_ctx = jnp.einsum('brwhd,bchd->brhwc', qg, k_ctx).astype(jnp.float32) * scale
    p = jax.nn.softmax(jnp.concatenate([s_win, s_ctx], axis=-1), axis=-1).astype(v.dtype)
    p_win = p[..., :KR * GRID_W].reshape(B, R, H, GRID_W, KR, GRID_W)
    p_ctx = p[..., KR * GRID_W:]
    o = jnp.einsum('brhwkx,brkxhd->brwhd', p_win, v_strip) + jnp.einsum('brhwc,bchd->brwhd', p_ctx, v_ctx)
    return o.reshape(B, S, H * d)


def _na_context(q, k, v):
    s = jnp.einsum('bqhd,bkhd->bhqk', q, k).astype(jnp.float32) * HEAD_DIM ** -0.5
    p = jax.nn.softmax(s, axis=-1).astype(v.dtype)
    o = jnp.einsum('bhqk,bkhd->bqhd', p, v)
    return o.reshape(o.shape[:2] + (NA_DIM,))


def _pool_mix(u, pool_w, pool_scale):
    B, T, _ = u.shape
    ug = u.reshape(B, T, POOL_GROUPS, POOL_GROUP_DIM)
    csum = jnp.concatenate([jnp.zeros((B, 1, POOL_GROUPS, POOL_GROUP_DIM), jnp.float32),
                            jnp.cumsum(ug.astype(jnp.float32), axis=1)], axis=1)
    t = jnp.arange(T)
    outs = []
    for gi, w in enumerate(POOL_WINDOWS):
        lo = jnp.clip(t - w // 2, 0, T - 1)
        hi = jnp.clip(t + (w - w // 2 - 1), 0, T - 1)
        total = csum[:, hi + 1, gi] - csum[:, lo, gi]
        cnt = (hi - lo + 1).astype(jnp.float32)[None, :, None]
        outs.append(total / cnt)
    pooled = jnp.stack(outs, axis=2).astype(u.dtype) - ug
    mixed = jnp.einsum('btgc,gcd->btgd', pooled, pool_w)
    return mixed.reshape(B, T, POOL_DIM) * pool_scale


def _dwconv(u, w):
    C = u.shape[-1]
    return lax.conv_general_dilated(u, w[:, None, :].astype(u.dtype), window_strides=(1,), padding='SAME',
                                    dimension_numbers=('NWC', 'WIO', 'NWC'), feature_group_count=C)


def _axial_rope(T):
    t = jnp.arange(T)
    row = (t // GRID_W).astype(jnp.float32)
    col = (t % GRID_W).astype(jnp.float32)
    nf = HEAD_DIM // 4
    inv = ROPE_BASE ** (-jnp.arange(nf, dtype=jnp.float32) / nf)
    ang = jnp.stack([row[:, None] * inv, col[:, None] * inv], axis=1)
    return jnp.cos(ang), jnp.sin(ang)


def _apply_rope(x, cos, sin):
    xs = x.reshape(x.shape[:-1] + (2, 2, HEAD_DIM // 4))
    a, b = xs[..., 0, :], xs[..., 1, :]
    out = jnp.stack([a * cos - b * sin, a * sin + b * cos], axis=-2)
    return out.reshape(x.shape)


def _mlstm_inputs(uq, uk, uv, ug, conv_w, gate_b, rope):
    qk = jax.nn.silu(_dwconv(jnp.concatenate([uq, uk], axis=-1), conv_w))
    q, k = jnp.split(qk, 2, axis=-1)
    to_h = lambda u: jnp.moveaxis(_heads(u, ML_HEADS).astype(jnp.float32), 2, 1)
    q, k, v = to_h(q), to_h(k), to_h(uv)
    if rope is not None:
        q = _apply_rope(q, *rope)
        k = _apply_rope(k, *rope)
    k = k * HEAD_DIM ** -0.5
    g = ug.astype(jnp.float32) + gate_b.astype(jnp.float32)
    g = jnp.moveaxis(g.reshape(g.shape[:2] + (4, ML_HEADS)), (2, 3), (0, 2))
    gates = (g[0], jax.nn.log_sigmoid(g[1]), g[2], jax.nn.log_sigmoid(g[3]))
    return q, k, v, gates


def _mlstm_chunked(q, k, v, log_i, log_f, state):
    B, H, T, d = q.shape
    L = min(ML_CHUNK, T)
    nc = T // L

    def chunks(a):
        return jnp.moveaxis(a.reshape(a.shape[:2] + (nc, L) + a.shape[3:]), 2, 0)

    lower = jnp.tril(jnp.ones((L, L), dtype=bool))

    def step(carry, inp):
        C, n, m = carry
        qc, kc, vc, ic, fc = inp
        b = jnp.cumsum(fc, axis=-1)
        log_d = jnp.where(lower, b[..., :, None] - b[..., None, :] + ic[..., None, :], -jnp.inf)
        inter = b + m[..., None]
        m_t = jnp.maximum(inter, jnp.max(log_d, axis=-1))
        w_intra = jnp.einsum('bhtd,bhsd->bhts', qc, kc) * jnp.exp(log_d - m_t[..., None])
        w_inter = jnp.exp(inter - m_t)
        num = jnp.einsum('bhts,bhsd->bhtd', w_intra, vc) + w_inter[..., None] * jnp.einsum('bhtd,bhde->bhte', qc, C)
        den = jnp.sum(w_intra, axis=-1) + w_inter * jnp.einsum('bhtd,bhd->bht', qc, n)
        h = num / jnp.maximum(jnp.abs(den), jnp.exp(-m_t))[..., None]
        b_last = b[..., -1]
        log_w = b_last[..., None] - b + ic
        m_new = jnp.maximum(b_last + m, jnp.max(log_w, axis=-1))
        w_s = jnp.exp(log_w - m_new[..., None])
        decay = jnp.exp(b_last + m - m_new)
        C_new = decay[..., None, None] * C + jnp.einsum('bhs,bhsd,bhse->bhde', w_s, kc, vc)
        n_new = decay[..., None] * n + jnp.einsum('bhs,bhsd->bhd', w_s, kc)
        return (C_new, n_new, m_new), h

    state, h = lax.scan(step, state, (chunks(q), chunks(k), chunks(v), chunks(log_i), chunks(log_f)))
    return jnp.moveaxis(h, 0, 2).reshape(B, H, T, d), state


def _mlstm_out(h, uo, norm_g):
    hf = h * lax.rsqrt(jnp.mean(h * h, axis=-1, keepdims=True) + EPS)
    hf = jnp.moveaxis(hf, 1, 2).reshape(h.shape[0], h.shape[2], ML_DIM)
    return (hf * norm_g.astype(jnp.float32)).astype(uo.dtype) * jax.nn.sigmoid(uo)


def _token_mix(xn, hn, w_in, ml_gate_b, na_q_g, na_k_g, na_rpb, pool_w, pool_scale, ml_conv, ml_norm_g,
               need_ctx_out):
    lq, lk, lv, lpool, lmq, lmk, lmv, lmo, lmg = _split_cols(xn @ w_in)
    cq, ck, cv, cpool, cmq, cmk, cmv, cmo, cmg = _split_cols(hn @ w_in)
    qn = lambda u: _rms(_heads(u, NA_HEADS), na_q_g)
    kn = lambda u: _rms(_heads(u, NA_HEADS), na_k_g)
    k_ctx, v_ctx = kn(ck), _heads(cv, NA_HEADS)
    na_lat = _na_latent(qn(lq), kn(lk), _heads(lv, NA_HEADS), k_ctx, v_ctx, na_rpb)
    pool_lat = _pool_mix(lpool, pool_w, pool_scale)
    T = xn.shape[1]
    ql, kl, vl, gl = _mlstm_inputs(lmq, lmk, lmv, lmg, ml_conv, ml_gate_b, _axial_rope(T))
    qc, kc, vc, gc = _mlstm_inputs(cmq, cmk, cmv, cmg, ml_conv, ml_gate_b, None)
    B = xn.shape[0]
    zero = (jnp.zeros((B, ML_HEADS, HEAD_DIM, HEAD_DIM), jnp.float32),
            jnp.zeros((B, ML_HEADS, HEAD_DIM), jnp.float32),
            jnp.zeros((B, ML_HEADS), jnp.float32))
    fl = lambda a: jnp.flip(a, axis=2)
    h_cf, st_f = _mlstm_chunked(qc, kc, vc, gc[0], gc[1], zero)
    h_cb, st_b = _mlstm_chunked(fl(qc), fl(kc), fl(vc), fl(gc[2]), fl(gc[3]), zero)
    h_lf, _ = _mlstm_chunked(ql, kl, vl, gl[0], gl[1], st_f)
    h_lb, _ = _mlstm_chunked(fl(ql), fl(kl), fl(vl), fl(gl[2]), fl(gl[3]), st_b)
    ml_lat = _mlstm_out(h_lf + fl(h_lb), lmo, ml_norm_g)
    y_lat = jnp.concatenate([na_lat, pool_lat, ml_lat], axis=-1)
    if not need_ctx_out:
        return y_lat, None
    na_ctx = _na_context(qn(cq), k_ctx, v_ctx)
    pool_ctx = _pool_mix(cpool, pool_w, pool_scale)
    ml_ctx = _mlstm_out(h_cf + fl(h_cb), cmo, ml_norm_g)
    y_ctx = jnp.concatenate([na_ctx, pool_ctx, ml_ctx], axis=-1)
    return y_lat, y_ctx


def _peer(xn, wq, sub_keys, u_tab, v_tab):
    B, T, D = xn.shape
    n = B * T
    tok = xn.reshape(n, D)
    q = (tok @ wq).reshape(n, PEER_HEADS, 2, PEER_DKEY // 2).astype(jnp.float32)
    s = jnp.einsum('nhpd,hpkd->nhpk', q, sub_keys.astype(jnp.float32))
    s1, i1 = lax.top_k(s[:, :, 0], PEER_TOPK)
    s2, i2 = lax.top_k(s[:, :, 1], PEER_TOPK)
    cand_s = (s1[..., :, None] + s2[..., None, :]).reshape(n, PEER_HEADS, PEER_TOPK * PEER_TOPK)
    cand_i = (i1[..., :, None] * PEER_KEYS + i2[..., None, :]).reshape(n, PEER_HEADS, PEER_TOPK * PEER_TOPK)
    top_s, pos = lax.top_k(cand_s, PEER_TOPK)
    idx = jnp.take_along_axis(cand_i, pos, axis=-1)
    gate = jax.nn.softmax(top_s, axis=-1).astype(xn.dtype)
    nb = n // PEER_BLOCK

    def block(args):
        xb, ib, gb = args
        u_e = u_tab[ib]
        v_e = v_tab[ib]
        a = jax.nn.gelu(jnp.einsum('nd,nhkd->nhk', xb, u_e)) * gb
        return jnp.einsum('nhk,nhkd->nd', a, v_e)

    out = lax.map(block, (tok.reshape(nb, PEER_BLOCK, D),
                          idx.reshape(nb, PEER_BLOCK, PEER_HEADS, PEER_TOPK),
                          gate.reshape(nb, PEER_BLOCK, PEER_HEADS, PEER_TOPK)))
    return out.reshape(B, T, D)


def setup_inputs(seed: int = 0) -> dict:
    key = jax.random.key(seed)
    ks = jax.random.split(key, 24)
    D = D_MODEL

    def nrm(k, shape, s):
        return jax.random.normal(k, shape, jnp.float32) * s

    f_init = jnp.linspace(3.0, 6.0, ML_HEADS, dtype=jnp.float32)
    z = jnp.zeros((ML_HEADS,), jnp.float32)
    gate_base = jnp.concatenate([z, f_init, z, f_init])
    return {
        'x': nrm(ks[0], (BATCH, SEQ, D), 1.0),
        'c': nrm(ks[1], (BATCH, D), 1.0),
        'ctx': nrm(ks[2], (BATCH, CTX_LEN, D), 1.0),
        'c_ctx': nrm(ks[3], (D,), 1.0),
        'w_ada': nrm(ks[4], (DEPTH, D, 6 * D), 0.5 * D ** -0.5),
        'b_ada': nrm(ks[5], (DEPTH, 6 * D), 0.02),
        'norm1_g': 1.0 + nrm(ks[6], (DEPTH, D), 0.02),
        'w_in': nrm(ks[7], (DEPTH, D, IN_COLS), D ** -0.5),
        'ml_gate_b': gate_base[None, :] + nrm(ks[8], (DEPTH, ML_N_GATES), 0.1),
        'na_q_g': 1.0 + nrm(ks[9], (DEPTH, HEAD_DIM), 0.02),
        'na_k_g': 1.0 + nrm(ks[10], (DEPTH, HEAD_DIM), 0.02),
        'na_rpb': nrm(ks[11], (DEPTH, NA_HEADS, 2 * NA_WIN_ROWS - 1, 2 * NA_WIN_COLS - 1), 0.1),
        'pool_w': nrm(ks[12], (DEPTH, POOL_GROUPS, POOL_GROUP_DIM, POOL_GROUP_DIM), POOL_GROUP_DIM ** -0.5),
        'pool_scale': 1.0 + nrm(ks[13], (DEPTH, POOL_DIM), 0.02),
        'ml_conv': nrm(ks[14], (DEPTH, ML_CONV_W, 2 * ML_DIM), ML_CONV_W ** -0.5),
        'ml_norm_g': 1.0 + nrm(ks[15], (DEPTH, ML_DIM), 0.02),
        'w_out': nrm(ks[16], (DEPTH, MIX_DIM, D), MIX_DIM ** -0.5),
        'norm2_g': 1.0 + nrm(ks[17], (DEPTH, D), 0.02),
        'peer_wq': nrm(ks[18], (DEPTH, D, PEER_HEADS * PEER_DKEY), D ** -0.5),
        'peer_keys': nrm(ks[19], (DEPTH, PEER_HEADS, 2, PEER_KEYS, PEER_DKEY // 2), (PEER_DKEY // 2) ** -0.5),
        'peer_u': nrm(ks[20], (DEPTH, PEER_EXPERTS, D), D ** -0.5),
        'peer_v': nrm(ks[21], (DEPTH, PEER_EXPERTS, D), 0.25),
    }


def reference(x, c, ctx, c_ctx, w_ada, b_ada, norm1_g, w_in, ml_gate_b, na_q_g, na_k_g, na_rpb, pool_w,
              pool_scale, ml_conv, ml_norm_g, w_out, norm2_g, peer_wq, peer_keys, peer_u, peer_v):
    hc = ctx
    for l in range(DEPTH):
        need_ctx = l < DEPTH - 1
        mod_lat = jax.nn.silu(c) @ w_ada[l] + b_ada[l]
        mod_ctx = jax.nn.silu(c_ctx) @ w_ada[l] + b_ada[l]
        sh1, sc1, g1, sh2, sc2, g2 = jnp.split(mod_lat[:, None, :], 6, axis=-1)
        csh1, csc1, cg1, csh2, csc2, cg2 = jnp.split(mod_ctx[None, None, :], 6, axis=-1)
        xn = _rms(x, norm1_g[l]) * (1.0 + sc1) + sh1
        hn = _rms(hc, norm1_g[l]) * (1.0 + csc1) + csh1
        y_lat, y_ctx = _token_mix(xn, hn, w_in[l], ml_gate_b[l], na_q_g[l], na_k_g[l], na_rpb[l], pool_w[l],
                                  pool_scale[l], ml_conv[l], ml_norm_g[l], need_ctx)
        x = x + g1 * (y_lat @ w_out[l])
        x = x + g2 * _peer(_rms(x, norm2_g[l]) * (1.0 + sc2) + sh2, peer_wq[l], peer_keys[l], peer_u[l], peer_v[l])
        if need_ctx:
            hc = hc + cg1 * (y_ctx @ w_out[l])
            hc = hc + cg2 * _peer(_rms(hc, norm2_g[l]) * (1.0 + csc2) + csh2, peer_wq[l], peer_keys[l],
                                  peer_u[l], peer_v[l])
    return x
```

```python
import functools

import numpy as np
import jax
import jax.numpy as jnp
from jax import lax
from jax.experimental import pallas as pl
from jax.experimental.pallas import tpu as pltpu

D_MODEL = 1024
GRID_W = 64
HEAD_DIM = 64
NA_DIM = D_MODEL // 2
NA_HEADS = NA_DIM // HEAD_DIM
NA_WIN_ROWS = 8
NA_WIN_COLS = 16
POOL_DIM = D_MODEL // 4
POOL_WINDOWS = (2, 4, 8, 16)
POOL_GROUPS = len(POOL_WINDOWS)
POOL_GROUP_DIM = POOL_DIM // POOL_GROUPS
ML_DIM = D_MODEL // 4
ML_HEADS = ML_DIM // HEAD_DIM
ML_CHUNK = 64
ML_N_GATES = 4 * ML_HEADS
PEER_KEYS = 128
PEER_EXPERTS = PEER_KEYS * PEER_KEYS
PEER_HEADS = 8
PEER_TOPK = 16
PEER_DKEY = 256
ROPE_BASE = 10000.0
EPS = 1e-6

PEER_TOKEN_TILE = 1024
PEER_EXPERT_TILE = 1024
VMEM_LIMIT_BYTES = 56 * 1024 * 1024

_GELU_C0 = float(np.sqrt(2.0 / np.pi))
_GELU_C1 = 0.044715 * _GELU_C0


def _gelu_tanh(x):
    inner = x * (_GELU_C0 + _GELU_C1 * (x * x))
    hx = 0.5 * x
    return hx + hx * jnp.tanh(inner)


def _peer_dense_kernel(x_ref, u_ref, vt_ref, cnt_ref, e1_ref, rank_ref, e2_ref, o_ref):
    j = pl.program_id(1)

    @pl.when(j == 0)
    def _():
        o_ref[...] = jnp.zeros_like(o_ref)

    te = u_ref.shape[0]
    act = lax.dot_general(u_ref[...], x_ref[...], (((1,), (1,)), ((), ())),
                          preferred_element_type=jnp.float32)
    gates = []
    for k in range(te // PEER_KEYS):
        g = None
        for h in range(PEER_HEADS):
            cnt = cnt_ref[h, k:k + 1, :].astype(jnp.bfloat16)
            e1 = e1_ref[h, k:k + 1, :].astype(jnp.bfloat16)
            term = jnp.where(rank_ref[h] < cnt, e2_ref[h], jnp.zeros((), e2_ref.dtype)) * e1
            g = term if g is None else g + term
        gates.append(g)
    gate = jnp.concatenate(gates, axis=0)
    p = _gelu_tanh(act).astype(jnp.bfloat16) * gate
    o_ref[...] += jnp.dot(vt_ref[...], p, preferred_element_type=jnp.float32)


def _peer_dense(tok, u_bf, vt_bf, cnt, e1, rank, e2):
    n, d = tok.shape
    tn, te = PEER_TOKEN_TILE, PEER_EXPERT_TILE
    assert n % tn == 0 and PEER_EXPERTS % te == 0 and te % PEER_KEYS == 0
    tab_a = pl.BlockSpec((PEER_HEADS, te // PEER_KEYS, tn), lambda i, j: (0, j, i))
    tab_b = pl.BlockSpec((PEER_HEADS, PEER_KEYS, tn), lambda i, j: (0, 0, i))
    return pl.pallas_call(
        _peer_dense_kernel,
        grid=(n // tn, PEER_EXPERTS // te),
        in_specs=[
            pl.BlockSpec((tn, d), lambda i, j: (i, 0)),
            pl.BlockSpec((te, d), lambda i, j: (j, 0)),
            pl.BlockSpec((d, te), lambda i, j: (0, j)),
            tab_a, tab_a, tab_b, tab_b,
        ],
        out_specs=pl.BlockSpec((d, tn), lambda i, j: (0, i)),
        out_shape=jax.ShapeDtypeStruct((d, n), jnp.float32),
        compiler_params=pltpu.CompilerParams(
            dimension_semantics=("parallel", "arbitrary"), vmem_limit_bytes=VMEM_LIMIT_BYTES),
        name="peer_dense",
    )(tok, u_bf, vt_bf, cnt, e1, rank, e2)


def _peer_gate_tables(tok, wq, sub_keys):
    n = tok.shape[0]
    q = (tok @ wq).reshape(n, PEER_HEADS, 2, PEER_DKEY // 2).astype(jnp.float32)
    s = jnp.einsum('nhpd,hpkd->nhpk', q, sub_keys.astype(jnp.float32))
    s1, s2 = s[:, :, 0], s[:, :, 1]
    v1 = lax.top_k(s1, PEER_TOPK)[0]
    v2 = lax.top_k(s2, PEER_TOPK)[0]
    cand = (v1[..., :, None] + v2[..., None, :]).reshape(n, PEER_HEADS, PEER_TOPK * PEER_TOPK)
    top_s = lax.top_k(cand, PEER_TOPK)[0]
    tau = top_s[..., PEER_TOPK - 1]
    z = jnp.sum(jnp.exp(top_s - top_s[..., :1]), axis=-1)
    rank = jnp.sum(v2[..., None, :] > s2[..., :, None], axis=-1)
    cnt = jnp.sum(s1[..., :, None] + v2[..., None, :] >= tau[..., None, None], axis=-1)
    cnt = jnp.where(s1 >= v1[..., PEER_TOPK - 1:], cnt, 0)
    e1 = jnp.exp(s1 - v1[..., :1])
    e2 = jnp.exp(s2 - v2[..., :1]) / z[..., None]
    to_tab = lambda t, dt: jnp.transpose(t, (1, 2, 0)).astype(dt)
    return (to_tab(cnt, jnp.float32), to_tab(e1, jnp.float32),
            to_tab(rank, jnp.bfloat16), to_tab(e2, jnp.bfloat16))


def _peer(tok, wq, sub_keys, u_bf, vt_bf):
    tabs = _peer_gate_tables(tok, wq, sub_keys)
    out_t = _peer_dense(tok.astype(jnp.bfloat16), u_bf, vt_bf, *tabs)
    return out_t.T


def _rms(x, g):
    xf = x.astype(jnp.float32)
    y = xf * lax.rsqrt(jnp.mean(xf * xf, axis=-1, keepdims=True) + EPS)
    return (y * g.astype(jnp.float32)).astype(x.dtype)


def _heads(u, n_heads):
    return u.reshape(u.shape[:-1] + (n_heads, HEAD_DIM))


def _split_cols(p):
    sizes = (NA_DIM, NA_DIM, NA_DIM, POOL_DIM, ML_DIM, ML_DIM, ML_DIM, ML_DIM, ML_N_GATES)
    offs = np.cumsum(sizes)[:-1].tolist()
    return jnp.split(p, offs, axis=-1)


def _na_latent(q, k, v, k_ctx, v_ctx, rpb):
    B, S, H, d = q.shape
    R = S // GRID_W
    KR = min(NA_WIN_ROWS, R)
    qg = q.reshape(B, R, GRID_W, H, d)
    kg = k.reshape(B, R, GRID_W, H, d)
    vg = v.reshape(B, R, GRID_W, H, d)
    r = jnp.arange(R)
    r0 = jnp.clip(r - KR // 2, 0, R - KR)
    rows = r0[:, None] + jnp.arange(KR)[None, :]
    k_strip = kg[:, rows]
    v_strip = vg[:, rows]
    cq = jnp.arange(GRID_W)
    c0 = jnp.clip(cq - NA_WIN_COLS // 2, 0, GRID_W - NA_WIN_COLS)
    in_win = (cq[None, :] >= c0[:, None]) & (cq[None, :] < c0[:, None] + NA_WIN_COLS)
    br = rows - r[:, None] + (NA_WIN_ROWS - 1)
    bc = jnp.clip(cq[None, :] - cq[:, None] + (NA_WIN_COLS - 1), 0, 2 * NA_WIN_COLS - 2)
    bias = rpb[:, br[:, None, :, None], bc[None, :, None, :]]
    bias = jnp.transpose(bias, (1, 0, 2, 3, 4)).astype(jnp.float32)
    scale = HEAD_DIM ** -0.5
    s_win = jnp.einsum('brwhd,brkxhd->brhwkx', qg, k_strip).astype(jnp.float32) * scale + bias
    s_win = jnp.where(in_win[:, None, :], s_win, -jnp.inf).reshape(B, R, H, GRID_W, KR * GRID_W)
    s_ctx = jnp.einsum('brwhd,bchd->brhwc', qg, k_ctx).astype(jnp.float32) * scale
    p = jax.nn.softmax(jnp.concatenate([s_win, s_ctx], axis=-1), axis=-1).astype(v.dtype)
    p_win = p[..., :KR * GRID_W].reshape(B, R, H, GRID_W, KR, GRID_W)
    p_ctx = p[..., KR * GRID_W:]
    o = jnp.einsum('brhwkx,brkxhd->brwhd', p_win, v_strip) + jnp.einsum('brhwc,bchd->brwhd', p_ctx, v_ctx)
    return o.reshape(B, S, H * d)


def _na_context(q, k, v):
    s = jnp.einsum('bqhd,bkhd->bhqk', q, k).astype(jnp.float32) * HEAD_DIM ** -0.5
    p = jax.nn.softmax(s, axis=-1).astype(v.dtype)
    o = jnp.einsum('bhqk,bkhd->bqhd', p, v)
    return o.reshape(o.shape[:2] + (NA_DIM,))


def _pool_mix(u, pool_w, pool_scale):
    B, T, _ = u.shape
    ug = u.reshape(B, T, POOL_GROUPS, POOL_GROUP_DIM)
    csum = jnp.concatenate([jnp.zeros((B, 1, POOL_GROUPS, POOL_GROUP_DIM), jnp.float32),
                            jnp.cumsum(ug.astype(jnp.float32), axis=1)], axis=1)
    t = jnp.arange(T)
    outs = []
    for gi, w in enumerate(POOL_WINDOWS):
        lo = jnp.clip(t - w // 2, 0, T - 1)
        hi = jnp.clip(t + (w - w // 2 - 1), 0, T - 1)
        total = csum[:, hi + 1, gi] - csum[:, lo, gi]
        cnt = (hi - lo + 1).astype(jnp.float32)[None, :, None]
        outs.append(total / cnt)
    pooled = jnp.stack(outs, axis=2).astype(u.dtype) - ug
    mixed = jnp.einsum('btgc,gcd->btgd', pooled, pool_w)
    return mixed.reshape(B, T, POOL_DIM) * pool_scale


def _dwconv(u, w):
    C = u.shape[-1]
    return lax.conv_general_dilated(u, w[:, None, :].astype(u.dtype), window_strides=(1,), padding='SAME',
                                    dimension_numbers=('NWC', 'WIO', 'NWC'), feature_group_count=C)


def _axial_rope(T):
    t = jnp.arange(T)
    row = (t // GRID_W).astype(jnp.float32)
    col = (t % GRID_W).astype(jnp.float32)
    nf = HEAD_DIM // 4
    inv = ROPE_BASE ** (-jnp.arange(nf, dtype=jnp.float32) / nf)
    ang = jnp.stack([row[:, None] * inv, col[:, None] * inv], axis=1)
    return jnp.cos(ang), jnp.sin(ang)


def _apply_rope(x, cos, sin):
    xs = x.reshape(x.shape[:-1] + (2, 2, HEAD_DIM // 4))
    a, b = xs[..., 0, :], xs[..., 1, :]
    out = jnp.stack([a * cos - b * sin, a * sin + b * cos], axis=-2)
    return out.reshape(x.shape)


def _mlstm_inputs(uq, uk, uv, ug, conv_w, gate_b, rope):
    qk = jax.nn.silu(_dwconv(jnp.concatenate([uq, uk], axis=-1), conv_w))
    q, k = jnp.split(qk, 2, axis=-1)
    to_h = lambda u: jnp.moveaxis(_heads(u, ML_HEADS).astype(jnp.float32), 2, 1)
    q, k, v = to_h(q), to_h(k), to_h(uv)
    if rope is not None:
        q = _apply_rope(q, *rope)
        k = _apply_rope(k, *rope)
    k = k * HEAD_DIM ** -0.5
    g = ug.astype(jnp.float32) + gate_b.astype(jnp.float32)
    g = jnp.moveaxis(g.reshape(g.shape[:2] + (4, ML_HEADS)), (2, 3), (0, 2))
    gates = (g[0], jax.nn.log_sigmoid(g[1]), g[2], jax.nn.log_sigmoid(g[3]))
    return q, k, v, gates


def _mlstm_chunked(q, k, v, log_i, log_f, state):
    B, H, T, d = q.shape
    L = min(ML_CHUNK, T)
    nc = T // L

    def chunks(a):
        return jnp.moveaxis(a.reshape(a.shape[:2] + (nc, L) + a.shape[3:]), 2, 0)

    lower = jnp.tril(jnp.ones((L, L), dtype=bool))

    def step(carry, inp):
        C, n, m = carry
        qc, kc, vc, ic, fc = inp
        b = jnp.cumsum(fc, axis=-1)
        log_d = jnp.where(lower, b[..., :, None] - b[..., None, :] + ic[..., None, :], -jnp.inf)
        inter = b + m[..., None]
        m_t = jnp.maximum(inter, jnp.max(log_d, axis=-1))
        w_intra = jnp.einsum('bhtd,bhsd->bhts', qc, kc) * jnp.exp(log_d - m_t[..., None])
        w_inter = jnp.exp(inter - m_t)
        num = jnp.einsum('bhts,bhsd->bhtd', w_intra, vc) + w_inter[..., None] * jnp.einsum('bhtd,bhde->bhte', qc, C)
        den = jnp.sum(w_intra, axis=-1) + w_inter * jnp.einsum('bhtd,bhd->bht', qc, n)
        h = num / jnp.maximum(jnp.abs(den), jnp.exp(-m_t))[..., None]
        b_last = b[..., -1]
        log_w = b_last[..., None] - b + ic
        m_new = jnp.maximum(b_last + m, jnp.max(log_w, axis=-1))
        w_s = jnp.exp(log_w - m_new[..., None])
        decay = jnp.exp(b_last + m - m_new)
        C_new = decay[..., None, None] * C + jnp.einsum('bhs,bhsd,bhse->bhde', w_s, kc, vc)
        n_new = decay[..., None] * n + jnp.einsum('bhs,bhsd->bhd', w_s, kc)
        return (C_new, n_new, m_new), h

    state, h = lax.scan(step, state, (chunks(q), chunks(k), chunks(v), chunks(log_i), chunks(log_f)))
    return jnp.moveaxis(h, 0, 2).reshape(B, H, T, d), state


def _mlstm_out(h, uo, norm_g):
    hf = h * lax.rsqrt(jnp.mean(h * h, axis=-1, keepdims=True) + EPS)
    hf = jnp.moveaxis(hf, 1, 2).reshape(h.shape[0], h.shape[2], ML_DIM)
    return (hf * norm_g.astype(jnp.float32)).astype(uo.dtype) * jax.nn.sigmoid(uo)


def _token_mix(xn, hn, w_in, ml_gate_b, na_q_g, na_k_g, na_rpb, pool_w, pool_scale, ml_conv, ml_norm_g,
               need_ctx_out):
    lq, lk, lv, lpool, lmq, lmk, lmv, lmo, lmg = _split_cols(xn @ w_in)
    cq, ck, cv, cpool, cmq, cmk, cmv, cmo, cmg = _split_cols(hn @ w_in)
    qn = lambda u: _rms(_heads(u, NA_HEADS), na_q_g)
    kn = lambda u: _rms(_heads(u, NA_HEADS), na_k_g)
    k_ctx, v_ctx = kn(ck), _heads(cv, NA_HEADS)
    na_lat = _na_latent(qn(lq), kn(lk), _heads(lv, NA_HEADS), k_ctx, v_ctx, na_rpb)
    pool_lat = _pool_mix(lpool, pool_w, pool_scale)
    T = xn.shape[1]
    ql, kl, vl, gl = _mlstm_inputs(lmq, lmk, lmv, lmg, ml_conv, ml_gate_b, _axial_rope(T))
    qc, kc, vc, gc = _mlstm_inputs(cmq, cmk, cmv, cmg, ml_conv, ml_gate_b, None)
    B = xn.shape[0]
    zero = (jnp.zeros((B, ML_HEADS, HEAD_DIM, HEAD_DIM), jnp.float32),
            jnp.zeros((B, ML_HEADS, HEAD_DIM), jnp.float32),
            jnp.zeros((B, ML_HEADS), jnp.float32))
    fl = lambda a: jnp.flip(a, axis=2)
    h_cf, st_f = _mlstm_chunked(qc, kc, vc, gc[0], gc[1], zero)
    h_cb, st_b = _mlstm_chunked(fl(qc), fl(kc), fl(vc), fl(gc[2]), fl(gc[3]), zero)
    h_lf, _ = _mlstm_chunked(ql, kl, vl, gl[0], gl[1], st_f)
    h_lb, _ = _mlstm_chunked(fl(ql), fl(kl), fl(vl), fl(gl[2]), fl(gl[3]), st_b)
    ml_lat = _mlstm_out(h_lf + fl(h_lb), lmo, ml_norm_g)
    y_lat = jnp.concatenate([na_lat, pool_lat, ml_lat], axis=-1)
    if not need_ctx_out:
        return y_lat, None
    na_ctx = _na_context(qn(cq), k_ctx, v_ctx)
    pool_ctx = _pool_mix(cpool, pool_w, pool_scale)
    ml_ctx = _mlstm_out(h_cf + fl(h_cb), cmo, ml_norm_g)
    y_ctx = jnp.concatenate([na_ctx, pool_ctx, ml_ctx], axis=-1)
    return y_lat, y_ctx


def kernel(x, c, ctx, c_ctx, w_ada, b_ada, norm1_g, w_in, ml_gate_b, na_q_g, na_k_g, na_rpb, pool_w,
           pool_scale, ml_conv, ml_norm_g, w_out, norm2_g, peer_wq, peer_keys, peer_u, peer_v):
    depth = w_in.shape[0]
    B, S, D = x.shape
    Lc = ctx.shape[1]
    hc = ctx
    for l in range(depth):
        need_ctx = l < depth - 1
        u_bf = peer_u[l].astype(jnp.bfloat16)
        vt_bf = peer_v[l].astype(jnp.bfloat16).T
        mod_lat = jax.nn.silu(c) @ w_ada[l] + b_ada[l]
        mod_ctx = jax.nn.silu(c_ctx) @ w_ada[l] + b_ada[l]
        sh1, sc1, g1, sh2, sc2, g2 = jnp.split(mod_lat[:, None, :], 6, axis=-1)
        csh1, csc1, cg1, csh2, csc2, cg2 = jnp.split(mod_ctx[None, None, :], 6, axis=-1)
        xn = _rms(x, norm1_g[l]) * (1.0 + sc1) + sh1
        hn = _rms(hc, norm1_g[l]) * (1.0 + csc1) + csh1
        y_lat, y_ctx = _token_mix(xn, hn, w_in[l], ml_gate_b[l], na_q_g[l], na_k_g[l], na_rpb[l], pool_w[l],
                                  pool_scale[l], ml_conv[l], ml_norm_g[l], need_ctx)
        x = x + g1 * (y_lat @ w_out[l])
        tok = (_rms(x, norm2_g[l]) * (1.0 + sc2) + sh2).reshape(B * S, D)
        if need_ctx:
            hc = hc + cg1 * (y_ctx @ w_out[l])
            tok_c = (_rms(hc, norm2_g[l]) * (1.0 + csc2) + csh2).reshape(B * Lc, D)
            tok = jnp.concatenate([tok, tok_c], axis=0)
        po = _peer(tok, peer_wq[l], peer_keys[l], u_bf, vt_bf)
        x = x + g2 * po[:B * S].reshape(B, S, D)
        if need_ctx:
            hc = hc + cg2 * po[B * S:].reshape(B, Lc, D)
    return x
```

```python
import functools

import numpy as np
import jax
import jax.numpy as jnp
from jax import lax
from jax.experimental import pallas as pl
from jax.experimental.pallas import tpu as pltpu

D_MODEL = 1024
GRID_W = 64
HEAD_DIM = 64
NA_DIM = D_MODEL // 2
NA_HEADS = NA_DIM // HEAD_DIM
NA_WIN_ROWS = 8
NA_WIN_COLS = 16
POOL_DIM = D_MODEL // 4
POOL_WINDOWS = (2, 4, 8, 16)
POOL_GROUPS = len(POOL_WINDOWS)
POOL_GROUP_DIM = POOL_DIM // POOL_GROUPS
ML_DIM = D_MODEL // 4
ML_HEADS = ML_DIM // HEAD_DIM
ML_CHUNK = 64
ML_N_GATES = 4 * ML_HEADS
PEER_KEYS = 128
PEER_EXPERTS = PEER_KEYS * PEER_KEYS
PEER_HEADS = 8
PEER_TOPK = 16
PEER_DKEY = 256
ROPE_BASE = 10000.0
EPS = 1e-6

PEER_TOKEN_TILE = 1024
PEER_EXPERT_TILE = 1024
ROUTE_TOKEN_TILE = 512
VMEM_LIMIT_BYTES = 56 * 1024 * 1024

_GELU_C0 = float(np.sqrt(2.0 / np.pi))
_GELU_C1 = 0.044715 * _GELU_C0


def _gelu_tanh(x):
    inner = x * (_GELU_C0 + _GELU_C1 * (x * x))
    hx = 0.5 * x
    return hx + hx * jnp.tanh(inner)


def _peer_dense_kernel(x_ref, u_ref, vt_ref, cnt_ref, e1_ref, rank_ref, e2_ref, o_ref):
    j = pl.program_id(1)

    @pl.when(j == 0)
    def _():
        o_ref[...] = jnp.zeros_like(o_ref)

    te = u_ref.shape[0]
    act = lax.dot_general(u_ref[...], x_ref[...], (((1,), (1,)), ((), ())),
                          preferred_element_type=jnp.float32)
    gates = []
    for k in range(te // PEER_KEYS):
        g = None
        for h in range(PEER_HEADS):
            cnt = cnt_ref[h, k:k + 1, :].astype(jnp.bfloat16)
            e1 = e1_ref[h, k:k + 1, :].astype(jnp.bfloat16)
            term = jnp.where(rank_ref[h] < cnt, e2_ref[h], jnp.zeros((), e2_ref.dtype)) * e1
            g = term if g is None else g + term
        gates.append(g)
    gate = jnp.concatenate(gates, axis=0)
    p = _gelu_tanh(act).astype(jnp.bfloat16) * gate
    o_ref[...] += jnp.dot(vt_ref[...], p, preferred_element_type=jnp.float32)


def _peer_dense(tok, u_bf, vt_bf, cnt, e1, rank, e2):
    n, d = tok.shape
    tn, te = PEER_TOKEN_TILE, PEER_EXPERT_TILE
    assert n % tn == 0 and PEER_EXPERTS % te == 0 and te % PEER_KEYS == 0
    tab_a = pl.BlockSpec((PEER_HEADS, te // PEER_KEYS, tn), lambda i, j: (0, j, i))
    tab_b = pl.BlockSpec((PEER_HEADS, PEER_KEYS, tn), lambda i, j: (0, 0, i))
    return pl.pallas_call(
        _peer_dense_kernel,
        grid=(n // tn, PEER_EXPERTS // te),
        in_specs=[
            pl.BlockSpec((tn, d), lambda i, j: (i, 0)),
            pl.BlockSpec((te, d), lambda i, j: (j, 0)),
            pl.BlockSpec((d, te), lambda i, j: (0, j)),
            tab_a, tab_a, tab_b, tab_b,
        ],
        out_specs=pl.BlockSpec((d, tn), lambda i, j: (0, i)),
        out_shape=jax.ShapeDtypeStruct((d, n), jnp.float32),
        compiler_params=pltpu.CompilerParams(
            dimension_semantics=("parallel", "arbitrary"), vmem_limit_bytes=VMEM_LIMIT_BYTES),
        name="peer_dense",
    )(tok, u_bf, vt_bf, cnt, e1, rank, e2)


def _top16(work, want_rank):
    rows = lax.broadcasted_iota(jnp.int32, (PEER_TOPK, work.shape[1]), 0)
    vals = jnp.zeros((PEER_TOPK, work.shape[1]), jnp.float32)
    rank = jnp.full(work.shape, float(PEER_TOPK), jnp.float32) if want_rank else None
    tops = []
    for i in range(PEER_TOPK):
        m = jnp.max(work, axis=0, keepdims=True)
        hit = work == m
        if want_rank:
            rank = jnp.where(hit, float(i), rank)
        work = jnp.where(hit, -jnp.inf, work)
        vals = jnp.where(rows == i, m, vals)
        tops.append(m)
    return tops, vals, rank


def _peer_route_kernel(x_ref, sc_ref, sh_ref, g_ref, wqt_ref, keys_ref,
                       tok_ref, cnt_ref, e1_ref, rank_ref, e2_ref, s_scr):
    x = x_ref[0]
    y = x * lax.rsqrt(jnp.mean(x * x, axis=-1, keepdims=True) + EPS) * g_ref[...]
    tok = (y * (1.0 + sc_ref[0]) + sh_ref[0]).astype(jnp.bfloat16)
    tok_ref[...] = tok
    qt = lax.dot_general(wqt_ref[...], tok, (((1,), (1,)), ((), ())),
                         preferred_element_type=jnp.float32).astype(jnp.bfloat16)
    dk = PEER_DKEY // 2
    for hp in range(2 * PEER_HEADS):
        s_scr[hp] = jnp.dot(keys_ref[hp], qt[hp * dk:(hp + 1) * dk], preferred_element_type=jnp.float32)

    lane_chunks = x.shape[0] // 128
    row8 = lax.broadcasted_iota(jnp.int32, (8, 128), 0)

    def body(it, carry):
        h = it // lane_chunks
        lanes = pl.ds(pl.multiple_of((it % lane_chunks) * 128, 128), 128)
        s1 = s_scr[2 * h, :, lanes]
        s2 = s_scr[2 * h + 1, :, lanes]
        t1, v1, _ = _top16(s1, False)
        t2, v2, rank = _top16(s2, True)
        groups = [t1[0] + v2, t1[1] + v2[:8]]
        for i in range(2, 8):
            groups.append(jnp.where(row8 < PEER_TOPK // (i + 1), t1[i] + v2[:8], -jnp.inf))
        groups.append(v1[8:] + t2[0])
        tp, vp, _ = _top16(jnp.concatenate(groups, axis=0), False)
        tau = tp[PEER_TOPK - 1]
        z = jnp.sum(jnp.exp(vp - tp[0]), axis=0, keepdims=True)
        cnt = jnp.zeros(s1.shape, jnp.float32)
        for jj in range(PEER_TOPK):
            cnt = jnp.where(s1 + t2[jj] >= tau, float(jj + 1), cnt)
        cnt = jnp.where(s1 >= t1[PEER_TOPK - 1], cnt, 0.0)
        cnt_ref[h, :, lanes] = cnt
        e1_ref[h, :, lanes] = jnp.exp(s1 - t1[0])
        rank_ref[h, :, lanes] = rank.astype(jnp.bfloat16)
        e2_ref[h, :, lanes] = (jnp.exp(s2 - t2[0]) * (1.0 / z)).astype(jnp.bfloat16)
        return carry

    lax.fori_loop(0, PEER_HEADS * lane_chunks, body, 0)


def _peer_route(x3, sc, sh, g, wqt_bf, keys_bf):
    B, T, d = x3.shape
    tn = min(ROUTE_TOKEN_TILE, T)
    assert T % tn == 0 and tn % 128 == 0
    nt = T // tn
    n = B * T
    tab = pl.BlockSpec((PEER_HEADS, PEER_KEYS, tn), lambda b, t: (0, 0, b * nt + t))
    tab_shape = lambda dt: jax.ShapeDtypeStruct((PEER_HEADS, PEER_KEYS, n), dt)
    mod = pl.BlockSpec((1, 1, d), lambda b, t: (b, 0, 0))
    return pl.pallas_call(
        _peer_route_kernel,
        grid=(B, nt),
        in_specs=[
            pl.BlockSpec((1, tn, d), lambda b, t: (b, t, 0)),
            mod, mod,
            pl.BlockSpec((1, d), lambda b, t: (0, 0)),
            pl.BlockSpec(wqt_bf.shape, lambda b, t: (0, 0)),
            pl.BlockSpec(keys_bf.shape, lambda b, t: (0, 0, 0)),
        ],
        out_specs=[pl.BlockSpec((tn, d), lambda b, t: (b * nt + t, 0)), tab, tab, tab, tab],
        out_shape=[jax.ShapeDtypeStruct((n, d), jnp.bfloat16), tab_shape(jnp.float32), tab_shape(jnp.float32),
                   tab_shape(jnp.bfloat16), tab_shape(jnp.bfloat16)],
        scratch_shapes=[pltpu.VMEM((2 * PEER_HEADS, PEER_KEYS, tn), jnp.float32)],
        compiler_params=pltpu.CompilerParams(
            dimension_semantics=("parallel", "parallel"), vmem_limit_bytes=VMEM_LIMIT_BYTES),
        name="peer_route",
    )(x3, sc, sh, g, wqt_bf, keys_bf)


def _residual_t_kernel(x_ref, g_ref, yt_ref, o_ref):
    o_ref[0] = x_ref[0] + g_ref[0] * yt_ref[...].T


def _residual_t(x3, gate, yt):
    B, T, d = x3.shape
    tn = min(ROUTE_TOKEN_TILE, T)
    nt = T // tn
    return pl.pallas_call(
        _residual_t_kernel,
        grid=(B, nt),
        in_specs=[pl.BlockSpec((1, tn, d), lambda b, t: (b, t, 0)),
                  pl.BlockSpec((1, 1, d), lambda b, t: (b, 0, 0)),
                  pl.BlockSpec((d, tn), lambda b, t: (0, b * nt + t))],
        out_specs=pl.BlockSpec((1, tn, d), lambda b, t: (b, t, 0)),
        out_shape=jax.ShapeDtypeStruct(x3.shape, x3.dtype),
        compiler_params=pltpu.CompilerParams(dimension_semantics=("parallel", "parallel")),
        name="residual_t",
    )(x3, gate, yt)


def _peer_block(x3, sc, sh, gate, norm_g, wqt_bf, keys_bf, u_bf, vt_bf):
    tok, cnt, e1, rank, e2 = _peer_route(x3, sc, sh, norm_g[None, :], wqt_bf, keys_bf)
    out_t = _peer_dense(tok, u_bf, vt_bf, cnt, e1, rank, e2)
    return _residual_t(x3, gate, out_t)


def _qk_norm_kernel(q_ref, k_ref, v_ref, gq_ref, gk_ref, e_ref, qo_ref, ko_ref, vo_ref):
    def norm(x, g):
        ss = jnp.dot(x * x, e_ref[...], preferred_element_type=jnp.float32, precision=lax.Precision.HIGHEST)
        return x * lax.rsqrt(ss * (1.0 / HEAD_DIM) + EPS) * g

    qo_ref[0] = (norm(q_ref[0], gq_ref[...]) * HEAD_DIM ** -0.5).astype(jnp.bfloat16)
    ko_ref[0] = norm(k_ref[0], gk_ref[...]).astype(jnp.bfloat16)
    vo_ref[0] = v_ref[0].astype(jnp.bfloat16)


def _qk_norm(q, k, v, gq, gk):
    B, T, d = q.shape
    tn = min(512, T)
    lane_head = np.arange(d) // HEAD_DIM
    same_head = jnp.asarray((lane_head[:, None] == lane_head[None, :]).astype(np.float32))
    blk = pl.BlockSpec((1, tn, d), lambda b, t: (b, t, 0))
    vec = pl.BlockSpec((1, d), lambda b, t: (0, 0))
    out = jax.ShapeDtypeStruct((B, T, d), jnp.bfloat16)
    return pl.pallas_call(
        _qk_norm_kernel,
        grid=(B, T // tn),
        in_specs=[blk, blk, blk, vec, vec, pl.BlockSpec((d, d), lambda b, t: (0, 0))],
        out_specs=[blk, blk, blk],
        out_shape=[out, out, out],
        compiler_params=pltpu.CompilerParams(dimension_semantics=("parallel", "parallel")),
        name="qk_norm",
    )(q, k, v, jnp.tile(gq, NA_HEADS)[None, :], jnp.tile(gk, NA_HEADS)[None, :], same_head)


def _attend_pair(q2, segments, lane):
    zero = jnp.zeros((), q2.dtype)
    outs = []
    for half in range(2):
        own = (lane < HEAD_DIM) if half == 0 else (lane >= HEAD_DIM)
        qh = jnp.where(own, q2, zero)
        scores = []
        for k2, _, bias in segments:
            s = lax.dot_general(qh, k2, (((1,), (1,)), ((), ())), preferred_element_type=jnp.float32)
            scores.append(s if bias is None else s + bias[half])
        m = functools.reduce(jnp.maximum, [jnp.max(s, axis=-1, keepdims=True) for s in scores])
        ps = [jnp.exp(s - m) for s in scores]
        denom = functools.reduce(jnp.add, [jnp.sum(p, axis=-1, keepdims=True) for p in ps])
        acc = functools.reduce(jnp.add, [jnp.dot(p.astype(v2.dtype), v2, preferred_element_type=jnp.float32)
                                         for p, (_, v2, _) in zip(ps, segments)])
        outs.append(acc / denom)
    return jnp.where(lane < HEAD_DIM, outs[0], outs[1])


def _na_latent_kernel(q_ref, k_ref, v_ref, kc_ref, vc_ref, bias_ref, o_ref):
    r = pl.program_id(1)
    n_rows = k_ref.shape[1] // GRID_W
    r0 = jnp.clip(r - NA_WIN_ROWS // 2, 0, n_rows - NA_WIN_ROWS)
    strip = pl.ds(pl.multiple_of(r0 * GRID_W, GRID_W), NA_WIN_ROWS * GRID_W)
    lane = lax.broadcasted_iota(jnp.int32, (GRID_W, 128), 1)
    for p in range(NA_DIM // 128):
        lanes = slice(p * 128, (p + 1) * 128)
        segs = [(k_ref[0, strip, lanes], v_ref[0, strip, lanes], (bias_ref[0, 2 * p], bias_ref[0, 2 * p + 1])),
                (kc_ref[0, :, lanes], vc_ref[0, :, lanes], None)]
        o_ref[0, :, lanes] = _attend_pair(q_ref[0, :, lanes], segs, lane)


def _na_bias_strips(rpb, n_rows):
    cq = np.arange(GRID_W)
    c0 = np.clip(cq - NA_WIN_COLS // 2, 0, GRID_W - NA_WIN_COLS)
    in_win = (cq[None, :] >= c0[:, None]) & (cq[None, :] < c0[:, None] + NA_WIN_COLS)
    bc = np.clip(cq[None, :] - cq[:, None] + (NA_WIN_COLS - 1), 0, 2 * NA_WIN_COLS - 2)
    toe = jnp.where(in_win[None, None], rpb[:, :, bc], -jnp.inf)
    strips = [jnp.moveaxis(toe[:, s:s + NA_WIN_ROWS], 1, 2).reshape(NA_HEADS, GRID_W, NA_WIN_ROWS * GRID_W)
              for s in range(NA_WIN_ROWS)]
    return jnp.stack(strips, axis=0).astype(jnp.float32)


def _na_latent(q, k, v, k_ctx, v_ctx, rpb):
    B, S, d = q.shape
    Lc = k_ctx.shape[1]
    n_rows = S // GRID_W
    assert n_rows >= NA_WIN_ROWS
    bias = _na_bias_strips(rpb, n_rows)

    def bias_start(b, r):
        r0 = jnp.clip(r - NA_WIN_ROWS // 2, 0, n_rows - NA_WIN_ROWS)
        return (r0 - r + NA_WIN_ROWS - 1, 0, 0, 0)

    whole = lambda L: pl.BlockSpec((1, L, d), lambda b, r: (b, 0, 0))
    return pl.pallas_call(
        _na_latent_kernel,
        grid=(B, n_rows),
        in_specs=[pl.BlockSpec((1, GRID_W, d), lambda b, r: (b, r, 0)), whole(S), whole(S), whole(Lc), whole(Lc),
                  pl.BlockSpec((1,) + bias.shape[1:], bias_start)],
        out_specs=pl.BlockSpec((1, GRID_W, d), lambda b, r: (b, r, 0)),
        out_shape=jax.ShapeDtypeStruct((B, S, d), jnp.float32),
        compiler_params=pltpu.CompilerParams(dimension_semantics=("parallel", "arbitrary")),
        name="na_latent",
    )(q, k, v, k_ctx, v_ctx, bias)


def _na_context_kernel(q_ref, k_ref, v_ref, o_ref):
    lane = lax.broadcasted_iota(jnp.int32, (q_ref.shape[1], 128), 1)
    for p in range(NA_DIM // 128):
        lanes = slice(p * 128, (p + 1) * 128)
        o_ref[0, :, lanes] = _attend_pair(q_ref[0, :, lanes], [(k_ref[0, :, lanes], v_ref[0, :, lanes], None)], lane)


def _na_context(q, k, v):
    B, Lc, d = q.shape
    blk = pl.BlockSpec((1, Lc, d), lambda b: (b, 0, 0))
    return pl.pallas_call(
        _na_context_kernel,
        grid=(B,),
        in_specs=[blk, blk, blk],
        out_specs=blk,
        out_shape=jax.ShapeDtypeStruct((B, Lc, d), jnp.float32),
        compiler_params=pltpu.CompilerParams(dimension_semantics=("parallel",)),
        name="na_context",
    )(q, k, v)


def _rms(x, g):
    xf = x.astype(jnp.float32)
    y = xf * lax.rsqrt(jnp.mean(xf * xf, axis=-1, keepdims=True) + EPS)
    return (y * g.astype(jnp.float32)).astype(x.dtype)


def _heads(u, n_heads):
    return u.reshape(u.shape[:-1] + (n_heads, HEAD_DIM))


def _split_cols(p):
    sizes = (NA_DIM, NA_DIM, NA_DIM, POOL_DIM, ML_DIM, ML_DIM, ML_DIM, ML_DIM, ML_N_GATES)
    offs = np.cumsum(sizes)[:-1].tolist()
    return jnp.split(p, offs, axis=-1)


def _pool_mix(u, pool_w, pool_scale):
    B, T, _ = u.shape
    ug = u.reshape(B, T, POOL_GROUPS, POOL_GROUP_DIM)
    csum = jnp.concatenate([jnp.zeros((B, 1, POOL_GROUPS, POOL_GROUP_DIM), jnp.float32),
                            jnp.cumsum(ug.astype(jnp.float32), axis=1)], axis=1)
    t = jnp.arange(T)
    outs = []
    for gi, w in enumerate(POOL_WINDOWS):
        lo = jnp.clip(t - w // 2, 0, T - 1)
        hi = jnp.clip(t + (w - w // 2 - 1), 0, T - 1)
        total = csum[:, hi + 1, gi] - csum[:, lo, gi]
        cnt = (hi - lo + 1).astype(jnp.float32)[None, :, None]
        outs.append(total / cnt)
    pooled = jnp.stack(outs, axis=2).astype(u.dtype) - ug
    mixed = jnp.einsum('btgc,gcd->btgd', pooled, pool_w)
    return mixed.reshape(B, T, POOL_DIM) * pool_scale


def _dwconv(u, w):
    C = u.shape[-1]
    return lax.conv_general_dilated(u, w[:, None, :].astype(u.dtype), window_strides=(1,), padding='SAME',
                                    dimension_numbers=('NWC', 'WIO', 'NWC'), feature_group_count=C)


def _axial_rope(T):
    t = jnp.arange(T)
    row = (t // GRID_W).astype(jnp.float32)
    col = (t % GRID_W).astype(jnp.float32)
    nf = HEAD_DIM // 4
    inv = ROPE_BASE ** (-jnp.arange(nf, dtype=jnp.float32) / nf)
    ang = jnp.stack([row[:, None] * inv, col[:, None] * inv], axis=1)
    return jnp.cos(ang), jnp.sin(ang)


def _apply_rope(x, cos, sin):
    xs = x.reshape(x.shape[:-1] + (2, 2, HEAD_DIM // 4))
    a, b = xs[..., 0, :], xs[..., 1, :]
    out = jnp.stack([a * cos - b * sin, a * sin + b * cos], axis=-2)
    return out.reshape(x.shape)


def _mlstm_inputs(uq, uk, uv, ug, conv_w, gate_b, rope):
    qk = jax.nn.silu(_dwconv(jnp.concatenate([uq, uk], axis=-1), conv_w))
    q, k = jnp.split(qk, 2, axis=-1)
    to_h = lambda u: jnp.moveaxis(_heads(u, ML_HEADS).astype(jnp.float32), 2, 1)
    q, k, v = to_h(q), to_h(k), to_h(uv)
    if rope is not None:
        q = _apply_rope(q, *rope)
        k = _apply_rope(k, *rope)
    k = k * HEAD_DIM ** -0.5
    g = ug.astype(jnp.float32) + gate_b.astype(jnp.float32)
    g = jnp.moveaxis(g.reshape(g.shape[:2] + (4, ML_HEADS)), (2, 3), (0, 2))
    gates = (g[0], jax.nn.log_sigmoid(g[1]), g[2], jax.nn.log_sigmoid(g[3]))
    return q, k, v, gates


def _mlstm_chunked(q, k, v, log_i, log_f, state):
    B, H, T, d = q.shape
    L = min(ML_CHUNK, T)
    nc = T // L

    def chunks(a):
        return jnp.moveaxis(a.reshape(a.shape[:2] + (nc, L) + a.shape[3:]), 2, 0)

    lower = jnp.tril(jnp.ones((L, L), dtype=bool))

    def step(carry, inp):
        C, n, m = carry
        qc, kc, vc, ic, fc = inp
        b = jnp.cumsum(fc, axis=-1)
        log_d = jnp.where(lower, b[..., :, None] - b[..., None, :] + ic[..., None, :], -jnp.inf)
        inter = b + m[..., None]
        m_t = jnp.maximum(inter, jnp.max(log_d, axis=-1))
        w_intra = jnp.einsum('bhtd,bhsd->bhts', qc, kc) * jnp.exp(log_d - m_t[..., None])
        w_inter = jnp.exp(inter - m_t)
        num = jnp.einsum('bhts,bhsd->bhtd', w_intra, vc) + w_inter[..., None] * jnp.einsum('bhtd,bhde->bhte', qc, C)
        den = jnp.sum(w_intra, axis=-1) + w_inter * jnp.einsum('bhtd,bhd->bht', qc, n)
        h = num / jnp.maximum(jnp.abs(den), jnp.exp(-m_t))[..., None]
        b_last = b[..., -1]
        log_w = b_last[..., None] - b + ic
        m_new = jnp.maximum(b_last + m, jnp.max(log_w, axis=-1))
        w_s = jnp.exp(log_w - m_new[..., None])
        decay = jnp.exp(b_last + m - m_new)
        C_new = decay[..., None, None] * C + jnp.einsum('bhs,bhsd,bhse->bhde', w_s, kc, vc)
        n_new = decay[..., None] * n + jnp.einsum('bhs,bhsd->bhd', w_s, kc)
        return (C_new, n_new, m_new), h

    state, h = lax.scan(step, state, (chunks(q), chunks(k), chunks(v), chunks(log_i), chunks(log_f)))
    return jnp.moveaxis(h, 0, 2).reshape(B, H, T, d), state


def _mlstm_out(h, uo, norm_g):
    hf = h * lax.rsqrt(jnp.mean(h * h, axis=-1, keepdims=True) + EPS)
    hf = jnp.moveaxis(hf, 1, 2).reshape(h.shape[0], h.shape[2], ML_DIM)
    return (hf * norm_g.astype(jnp.float32)).astype(uo.dtype) * jax.nn.sigmoid(uo)


def _token_mix(xn, hn, w_in, ml_gate_b, na_q_g, na_k_g, na_rpb, pool_w, pool_scale, ml_conv, ml_norm_g,
               need_ctx_out):
    lq, lk, lv, lpool, lmq, lmk, lmv, lmo, lmg = _split_cols(xn @ w_in)
    cq, ck, cv, cpool, cmq, cmk, cmv, cmo, cmg = _split_cols(hn @ w_in)
    lqn, lkn, lvb = _qk_norm(lq, lk, lv, na_q_g, na_k_g)
    cqn, k_ctx, v_ctx = _qk_norm(cq, ck, cv, na_q_g, na_k_g)
    na_lat = _na_latent(lqn, lkn, lvb, k_ctx, v_ctx, na_rpb)
    pool_lat = _pool_mix(lpool, pool_w, pool_scale)
    T = xn.shape[1]
    ql, kl, vl, gl = _mlstm_inputs(lmq, lmk, lmv, lmg, ml_conv, ml_gate_b, _axial_rope(T))
    qc, kc, vc, gc = _mlstm_inputs(cmq, cmk, cmv, cmg, ml_conv, ml_gate_b, None)
    B = xn.shape[0]
    zero = (jnp.zeros((B, ML_HEADS, HEAD_DIM, HEAD_DIM), jnp.float32),
            jnp.zeros((B, ML_HEADS, HEAD_DIM), jnp.float32),
            jnp.zeros((B, ML_HEADS), jnp.float32))
    fl = lambda a: jnp.flip(a, axis=2)
    h_cf, st_f = _mlstm_chunked(qc, kc, vc, gc[0], gc[1], zero)
    h_cb, st_b = _mlstm_chunked(fl(qc), fl(kc), fl(vc), fl(gc[2]), fl(gc[3]), zero)
    h_lf, _ = _mlstm_chunked(ql, kl, vl, gl[0], gl[1], st_f)
    h_lb, _ = _mlstm_chunked(fl(ql), fl(kl), fl(vl), fl(gl[2]), fl(gl[3]), st_b)
    ml_lat = _mlstm_out(h_lf + fl(h_lb), lmo, ml_norm_g)
    y_lat = jnp.concatenate([na_lat, pool_lat, ml_lat], axis=-1)
    if not need_ctx_out:
        return y_lat, None
    na_ctx = _na_context(cqn, k_ctx, v_ctx)
    pool_ctx = _pool_mix(cpool, pool_w, pool_scale)
    ml_ctx = _mlstm_out(h_cf + fl(h_cb), cmo, ml_norm_g)
    y_ctx = jnp.concatenate([na_ctx, pool_ctx, ml_ctx], axis=-1)
    return y_lat, y_ctx


def kernel(x, c, ctx, c_ctx, w_ada, b_ada, norm1_g, w_in, ml_gate_b, na_q_g, na_k_g, na_rpb, pool_w,
           pool_scale, ml_conv, ml_norm_g, w_out, norm2_g, peer_wq, peer_keys, peer_u, peer_v):
    depth = w_in.shape[0]
    B, S, D = x.shape
    Lc = ctx.shape[1]
    hc = ctx
    for l in range(depth):
        need_ctx = l < depth - 1
        u_bf = peer_u[l].astype(jnp.bfloat16)
        vt_bf = peer_v[l].astype(jnp.bfloat16).T
        wqt_bf = peer_wq[l].astype(jnp.bfloat16).T
        keys_bf = peer_keys[l].astype(jnp.bfloat16).reshape(2 * PEER_HEADS, PEER_KEYS, PEER_DKEY // 2)
        mod_lat = jax.nn.silu(c) @ w_ada[l] + b_ada[l]
        mod_ctx = jax.nn.silu(c_ctx) @ w_ada[l] + b_ada[l]
        sh1, sc1, g1, sh2, sc2, g2 = jnp.split(mod_lat[:, None, :], 6, axis=-1)
        csh1, csc1, cg1, csh2, csc2, cg2 = jnp.split(mod_ctx[None, None, :], 6, axis=-1)
        xn = _rms(x, norm1_g[l]) * (1.0 + sc1) + sh1
        hn = _rms(hc, norm1_g[l]) * (1.0 + csc1) + csh1
        y_lat, y_ctx = _token_mix(xn, hn, w_in[l], ml_gate_b[l], na_q_g[l], na_k_g[l], na_rpb[l], pool_w[l],
                                  pool_scale[l], ml_conv[l], ml_norm_g[l], need_ctx)
        x = x + g1 * (y_lat @ w_out[l])
        x = _peer_block(x, sc2, sh2, g2, norm2_g[l], wqt_bf, keys_bf, u_bf, vt_bf)
        if need_ctx:
            hc = hc + cg1 * (y_ctx @ w_out[l])
            hc = _peer_block(hc.reshape(1, B * Lc, D), csc2, csh2, cg2, norm2_g[l], wqt_bf, keys_bf, u_bf,
                             vt_bf).reshape(B, Lc, D)
    return x
```

```python
import functools

import numpy as np
import jax
import jax.numpy as jnp
from jax import lax
from jax.experimental import pallas as pl
from jax.experimental.pallas import tpu as pltpu

D_MODEL = 1024
GRID_W = 64
HEAD_DIM = 64
NA_DIM = D_MODEL // 2
NA_HEADS = NA_DIM // HEAD_DIM
NA_WIN_ROWS = 8
NA_WIN_COLS = 16
POOL_DIM = D_MODEL // 4
POOL_WINDOWS = (2, 4, 8, 16)
POOL_GROUPS = len(POOL_WINDOWS)
POOL_GROUP_DIM = POOL_DIM // POOL_GROUPS
ML_DIM = D_MODEL // 4
ML_HEADS = ML_DIM // HEAD_DIM
ML_CONV_W = 5
ML_N_GATES = 4 * ML_HEADS
PEER_KEYS = 128
PEER_EXPERTS = PEER_KEYS * PEER_KEYS
PEER_HEADS = 8
PEER_TOPK = 16
PEER_DKEY = 256
ROPE_BASE = 10000.0
EPS = 1e-6

PEER_TOKEN_TILE = 1024
PEER_EXPERT_TILE = 1024
ROUTE_TOKEN_TILE = 512
POOL_BLOCK = 128
POOL_HALO = 8
ML_L = 128
VMEM_LIMIT_BYTES = 56 * 1024 * 1024

_GELU_C0 = float(np.sqrt(2.0 / np.pi))
_GELU_C1 = 0.044715 * _GELU_C0


def _gelu_tanh(x):
    inner = x * (_GELU_C0 + _GELU_C1 * (x * x))
    hx = 0.5 * x
    return hx + hx * jnp.tanh(inner)


def _peer_dense_kernel(x_ref, u_ref, vt_ref, cnt_ref, e1_ref, rank_ref, e2_ref, o_ref):
    j = pl.program_id(1)

    @pl.when(j == 0)
    def _():
        o_ref[...] = jnp.zeros_like(o_ref)

    te = u_ref.shape[0]
    act = lax.dot_general(u_ref[...], x_ref[...], (((1,), (1,)), ((), ())),
                          preferred_element_type=jnp.float32)
    gates = []
    for k in range(te // PEER_KEYS):
        g = None
        for h in range(PEER_HEADS):
            cnt = cnt_ref[h, k:k + 1, :].astype(jnp.bfloat16)
            e1 = e1_ref[h, k:k + 1, :].astype(jnp.bfloat16)
            term = jnp.where(rank_ref[h] < cnt, e2_ref[h], jnp.zeros((), e2_ref.dtype)) * e1
            g = term if g is None else g + term
        gates.append(g)
    gate = jnp.concatenate(gates, axis=0)
    p = _gelu_tanh(act).astype(jnp.bfloat16) * gate
    o_ref[...] += jnp.dot(vt_ref[...], p, preferred_element_type=jnp.float32)


def _peer_dense(tok, u_bf, vt_bf, cnt, e1, rank, e2):
    n, d = tok.shape
    tn, te = PEER_TOKEN_TILE, PEER_EXPERT_TILE
    assert n % tn == 0 and PEER_EXPERTS % te == 0 and te % PEER_KEYS == 0
    tab_a = pl.BlockSpec((PEER_HEADS, te // PEER_KEYS, tn), lambda i, j: (0, j, i))
    tab_b = pl.BlockSpec((PEER_HEADS, PEER_KEYS, tn), lambda i, j: (0, 0, i))
    return pl.pallas_call(
        _peer_dense_kernel,
        grid=(n // tn, PEER_EXPERTS // te),
        in_specs=[
            pl.BlockSpec((tn, d), lambda i, j: (i, 0)),
            pl.BlockSpec((te, d), lambda i, j: (j, 0)),
            pl.BlockSpec((d, te), lambda i, j: (0, j)),
            tab_a, tab_a, tab_b, tab_b,
        ],
        out_specs=pl.BlockSpec((d, tn), lambda i, j: (0, i)),
        out_shape=jax.ShapeDtypeStruct((d, n), jnp.float32),
        compiler_params=pltpu.CompilerParams(
            dimension_semantics=("parallel", "arbitrary"), vmem_limit_bytes=VMEM_LIMIT_BYTES),
        name="peer_dense",
    )(tok, u_bf, vt_bf, cnt, e1, rank, e2)


def _top16(work, want_rank):
    rows = lax.broadcasted_iota(jnp.int32, (PEER_TOPK, work.shape[1]), 0)
    vals = jnp.zeros((PEER_TOPK, work.shape[1]), jnp.float32)
    rank = jnp.full(work.shape, float(PEER_TOPK), jnp.float32) if want_rank else None
    tops = []
    for i in range(PEER_TOPK):
        m = jnp.max(work, axis=0, keepdims=True)
        hit = work == m
        if want_rank:
            rank = jnp.where(hit, float(i), rank)
        work = jnp.where(hit, -jnp.inf, work)
        vals = jnp.where(rows == i, m, vals)
        tops.append(m)
    return tops, vals, rank


def _peer_route_kernel(x_ref, sc_ref, sh_ref, g_ref, wqt_ref, keys_ref,
                       tok_ref, cnt_ref, e1_ref, rank_ref, e2_ref, s_scr):
    x = x_ref[0]
    y = x * lax.rsqrt(jnp.mean(x * x, axis=-1, keepdims=True) + EPS) * g_ref[...]
    tok = (y * (1.0 + sc_ref[0]) + sh_ref[0]).astype(jnp.bfloat16)
    tok_ref[...] = tok
    qt = lax.dot_general(wqt_ref[...], tok, (((1,), (1,)), ((), ())),
                         preferred_element_type=jnp.float32).astype(jnp.bfloat16)
    dk = PEER_DKEY // 2
    for hp in range(2 * PEER_HEADS):
        s_scr[hp] = jnp.dot(keys_ref[hp], qt[hp * dk:(hp + 1) * dk], preferred_element_type=jnp.float32)

    lane_chunks = x.shape[0] // 128
    row8 = lax.broadcasted_iota(jnp.int32, (8, 128), 0)

    def body(it, carry):
        h = it // lane_chunks
        lanes = pl.ds(pl.multiple_of((it % lane_chunks) * 128, 128), 128)
        s1 = s_scr[2 * h, :, lanes]
        s2 = s_scr[2 * h + 1, :, lanes]
        t1, v1, _ = _top16(s1, False)
        t2, v2, rank = _top16(s2, True)
        groups = [t1[0] + v2, t1[1] + v2[:8]]
        for i in range(2, 8):
            groups.append(jnp.where(row8 < PEER_TOPK // (i + 1), t1[i] + v2[:8], -jnp.inf))
        groups.append(v1[8:] + t2[0])
        tp, vp, _ = _top16(jnp.concatenate(groups, axis=0), False)
        tau = tp[PEER_TOPK - 1]
        z = jnp.sum(jnp.exp(vp - tp[0]), axis=0, keepdims=True)
        cnt = jnp.zeros(s1.shape, jnp.float32)
        for jj in range(PEER_TOPK):
            cnt = jnp.where(s1 + t2[jj] >= tau, float(jj + 1), cnt)
        cnt = jnp.where(s1 >= t1[PEER_TOPK - 1], cnt, 0.0)
        cnt_ref[h, :, lanes] = cnt
        e1_ref[h, :, lanes] = jnp.exp(s1 - t1[0])
        rank_ref[h, :, lanes] = rank.astype(jnp.bfloat16)
        e2_ref[h, :, lanes] = (jnp.exp(s2 - t2[0]) * (1.0 / z)).astype(jnp.bfloat16)
        return carry

    lax.fori_loop(0, PEER_HEADS * lane_chunks, body, 0)


def _peer_route(x3, sc, sh, g, wqt_bf, keys_bf):
    B, T, d = x3.shape
    tn = min(ROUTE_TOKEN_TILE, T)
    assert T % tn == 0 and tn % 128 == 0
    nt = T // tn
    n = B * T
    tab = pl.BlockSpec((PEER_HEADS, PEER_KEYS, tn), lambda b, t: (0, 0, b * nt + t))
    tab_shape = lambda dt: jax.ShapeDtypeStruct((PEER_HEADS, PEER_KEYS, n), dt)
    mod = pl.BlockSpec((1, 1, d), lambda b, t: (b, 0, 0))
    return pl.pallas_call(
        _peer_route_kernel,
        grid=(B, nt),
        in_specs=[
            pl.BlockSpec((1, tn, d), lambda b, t: (b, t, 0)),
            mod, mod,
            pl.BlockSpec((1, d), lambda b, t: (0, 0)),
            pl.BlockSpec(wqt_bf.shape, lambda b, t: (0, 0)),
            pl.BlockSpec(keys_bf.shape, lambda b, t: (0, 0, 0)),
        ],
        out_specs=[pl.BlockSpec((tn, d), lambda b, t: (b * nt + t, 0)), tab, tab, tab, tab],
        out_shape=[jax.ShapeDtypeStruct((n, d), jnp.bfloat16), tab_shape(jnp.float32), tab_shape(jnp.float32),
                   tab_shape(jnp.bfloat16), tab_shape(jnp.bfloat16)],
        scratch_shapes=[pltpu.VMEM((2 * PEER_HEADS, PEER_KEYS, tn), jnp.float32)],
        compiler_params=pltpu.CompilerParams(
            dimension_semantics=("parallel", "parallel"), vmem_limit_bytes=VMEM_LIMIT_BYTES),
        name="peer_route",
    )(x3, sc, sh, g, wqt_bf, keys_bf)


def _residual_t_kernel(x_ref, g_ref, yt_ref, o_ref):
    o_ref[0] = x_ref[0] + g_ref[0] * yt_ref[...].T


def _residual_t(x3, gate, yt):
    B, T, d = x3.shape
    tn = min(ROUTE_TOKEN_TILE, T)
    nt = T // tn
    return pl.pallas_call(
        _residual_t_kernel,
        grid=(B, nt),
        in_specs=[pl.BlockSpec((1, tn, d), lambda b, t: (b, t, 0)),
                  pl.BlockSpec((1, 1, d), lambda b, t: (b, 0, 0)),
                  pl.BlockSpec((d, tn), lambda b, t: (0, b * nt + t))],
        out_specs=pl.BlockSpec((1, tn, d), lambda b, t: (b, t, 0)),
        out_shape=jax.ShapeDtypeStruct(x3.shape, x3.dtype),
        compiler_params=pltpu.CompilerParams(dimension_semantics=("parallel", "parallel")),
        name="residual_t",
    )(x3, gate, yt)


def _peer_block(x3, sc, sh, gate, norm_g, wqt_bf, keys_bf, u_bf, vt_bf):
    tok, cnt, e1, rank, e2 = _peer_route(x3, sc, sh, norm_g[None, :], wqt_bf, keys_bf)
    out_t = _peer_dense(tok, u_bf, vt_bf, cnt, e1, rank, e2)
    return _residual_t(x3, gate, out_t)


def _qk_norm_kernel(q_ref, k_ref, v_ref, gq_ref, gk_ref, e_ref, qo_ref, ko_ref, vo_ref):
    def norm(x, g):
        ss = jnp.dot(x * x, e_ref[...], preferred_element_type=jnp.float32, precision=lax.Precision.HIGHEST)
        return x * lax.rsqrt(ss * (1.0 / HEAD_DIM) + EPS) * g

    qo_ref[0] = (norm(q_ref[0], gq_ref[...]) * HEAD_DIM ** -0.5).astype(jnp.bfloat16)
    ko_ref[0] = norm(k_ref[0], gk_ref[...]).astype(jnp.bfloat16)
    vo_ref[0] = v_ref[0].astype(jnp.bfloat16)


def _qk_norm(q, k, v, gq, gk):
    B, T, d = q.shape
    tn = min(512, T)
    lane_head = np.arange(d) // HEAD_DIM
    same_head = jnp.asarray((lane_head[:, None] == lane_head[None, :]).astype(np.float32))
    blk = pl.BlockSpec((1, tn, d), lambda b, t: (b, t, 0))
    vec = pl.BlockSpec((1, d), lambda b, t: (0, 0))
    out = jax.ShapeDtypeStruct((B, T, d), jnp.bfloat16)
    return pl.pallas_call(
        _qk_norm_kernel,
        grid=(B, T // tn),
        in_specs=[blk, blk, blk, vec, vec, pl.BlockSpec((d, d), lambda b, t: (0, 0))],
        out_specs=[blk, blk, blk],
        out_shape=[out, out, out],
        compiler_params=pltpu.CompilerParams(dimension_semantics=("parallel", "parallel")),
        name="qk_norm",
    )(q, k, v, jnp.tile(gq, NA_HEADS)[None, :], jnp.tile(gk, NA_HEADS)[None, :], same_head)


def _attend_pair(q2, segments, lane):
    zero = jnp.zeros((), q2.dtype)
    outs = []
    for half in range(2):
        own = (lane < HEAD_DIM) if half == 0 else (lane >= HEAD_DIM)
        qh = jnp.where(own, q2, zero)
        scores = []
        for k2, _, bias in segments:
            s = lax.dot_general(qh, k2, (((1,), (1,)), ((), ())), preferred_element_type=jnp.float32)
            scores.append(s if bias is None else s + bias[half])
        m = functools.reduce(jnp.maximum, [jnp.max(s, axis=-1, keepdims=True) for s in scores])
        ps = [jnp.exp(s - m) for s in scores]
        denom = functools.reduce(jnp.add, [jnp.sum(p, axis=-1, keepdims=True) for p in ps])
        acc = functools.reduce(jnp.add, [jnp.dot(p.astype(v2.dtype), v2, preferred_element_type=jnp.float32)
                                         for p, (_, v2, _) in zip(ps, segments)])
        outs.append(acc / denom)
    return jnp.where(lane < HEAD_DIM, outs[0], outs[1])


def _na_latent_kernel(q_ref, k_ref, v_ref, kc_ref, vc_ref, bias_ref, o_ref):
    r = pl.program_id(1)
    n_rows = k_ref.shape[1] // GRID_W
    r0 = jnp.clip(r - NA_WIN_ROWS // 2, 0, n_rows - NA_WIN_ROWS)
    strip = pl.ds(pl.multiple_of(r0 * GRID_W, GRID_W), NA_WIN_ROWS * GRID_W)
    lane = lax.broadcasted_iota(jnp.int32, (GRID_W, 128), 1)
    for p in range(NA_DIM // 128):
        lanes = slice(p * 128, (p + 1) * 128)
        segs = [(k_ref[0, strip, lanes], v_ref[0, strip, lanes], (bias_ref[0, 2 * p], bias_ref[0, 2 * p + 1])),
                (kc_ref[0, :, lanes], vc_ref[0, :, lanes], None)]
        o_ref[0, :, lanes] = _attend_pair(q_ref[0, :, lanes], segs, lane)


def _na_bias_strips(rpb, n_rows):
    cq = np.arange(GRID_W)
    c0 = np.clip(cq - NA_WIN_COLS // 2, 0, GRID_W - NA_WIN_COLS)
    in_win = (cq[None, :] >= c0[:, None]) & (cq[None, :] < c0[:, None] + NA_WIN_COLS)
    bc = np.clip(cq[None, :] - cq[:, None] + (NA_WIN_COLS - 1), 0, 2 * NA_WIN_COLS - 2)
    toe = jnp.where(in_win[None, None], rpb[:, :, bc], -jnp.inf)
    strips = [jnp.moveaxis(toe[:, s:s + NA_WIN_ROWS], 1, 2).reshape(NA_HEADS, GRID_W, NA_WIN_ROWS * GRID_W)
              for s in range(NA_WIN_ROWS)]
    return jnp.stack(strips, axis=0).astype(jnp.float32)


def _na_latent(q, k, v, k_ctx, v_ctx, rpb):
    B, S, d = q.shape
    Lc = k_ctx.shape[1]
    n_rows = S // GRID_W
    assert n_rows >= NA_WIN_ROWS
    bias = _na_bias_strips(rpb, n_rows)

    def bias_start(b, r):
        r0 = jnp.clip(r - NA_WIN_ROWS // 2, 0, n_rows - NA_WIN_ROWS)
        return (r0 - r + NA_WIN_ROWS - 1, 0, 0, 0)

    whole = lambda L: pl.BlockSpec((1, L, d), lambda b, r: (b, 0, 0))
    return pl.pallas_call(
        _na_latent_kernel,
        grid=(B, n_rows),
        in_specs=[pl.BlockSpec((1, GRID_W, d), lambda b, r: (b, r, 0)), whole(S), whole(S), whole(Lc), whole(Lc),
                  pl.BlockSpec((1,) + bias.shape[1:], bias_start)],
        out_specs=pl.BlockSpec((1, GRID_W, d), lambda b, r: (b, r, 0)),
        out_shape=jax.ShapeDtypeStruct((B, S, d), jnp.float32),
        compiler_params=pltpu.CompilerParams(dimension_semantics=("parallel", "arbitrary")),
        name="na_latent",
    )(q, k, v, k_ctx, v_ctx, bias)


def _na_context_kernel(q_ref, k_ref, v_ref, o_ref):
    lane = lax.broadcasted_iota(jnp.int32, (q_ref.shape[1], 128), 1)
    for p in range(NA_DIM // 128):
        lanes = slice(p * 128, (p + 1) * 128)
        o_ref[0, :, lanes] = _attend_pair(q_ref[0, :, lanes], [(k_ref[0, :, lanes], v_ref[0, :, lanes], None)], lane)


def _na_context(q, k, v):
    B, Lc, d = q.shape
    blk = pl.BlockSpec((1, Lc, d), lambda b: (b, 0, 0))
    return pl.pallas_call(
        _na_context_kernel,
        grid=(B,),
        in_specs=[blk, blk, blk],
        out_specs=blk,
        out_shape=jax.ShapeDtypeStruct((B, Lc, d), jnp.float32),
        compiler_params=pltpu.CompilerParams(dimension_semantics=("parallel",)),
        name="na_context",
    )(q, k, v)


def _pool_matrices(T):
    tb, halo = POOL_BLOCK, POOL_HALO
    mats = np.zeros((len(POOL_WINDOWS), 3, tb, tb + 2 * halo), np.float32)
    for wi, w in enumerate(POOL_WINDOWS):
        for kind, start in enumerate((0, tb, T - tb)):
            for i in range(tb):
                t = start + i
                lo = min(max(t - w // 2, 0), T - 1)
                hi = min(max(t + (w - w // 2 - 1), 0), T - 1)
                mats[wi, kind, i, lo - start + halo:hi - start + halo + 1] = 1.0 / (hi - lo + 1)
                mats[wi, kind, i, i + halo] -= 1.0
    return jnp.asarray(mats)


def _pool_kernel(u_ref, a_ref, w_ref, s_ref, o_ref):
    T = u_ref.shape[1]
    tb, halo = POOL_BLOCK, POOL_HALO
    nb = T // tb
    group = lax.broadcasted_iota(jnp.int32, (tb, POOL_DIM), 1) // POOL_GROUP_DIM
    zeros = jnp.zeros((halo, POOL_DIM), jnp.float32)
    for blk in range(nb):
        kind = 0 if blk == 0 else (2 if blk == nb - 1 else 1)
        before = zeros if blk == 0 else u_ref[0, blk * tb - halo:blk * tb, :]
        after = zeros if blk == nb - 1 else u_ref[0, (blk + 1) * tb:(blk + 1) * tb + halo, :]
        xcat = jnp.concatenate([before, u_ref[0, blk * tb:(blk + 1) * tb, :], after], axis=0)
        pooled = None
        for wi in range(len(POOL_WINDOWS)):
            pw = jnp.dot(a_ref[wi, kind], xcat, preferred_element_type=jnp.float32, precision=lax.Precision.HIGHEST)
            pooled = pw if pooled is None else jnp.where(group == wi, pw, pooled)
        mixed = jnp.dot(pooled.astype(jnp.bfloat16), w_ref[...], preferred_element_type=jnp.float32)
        o_ref[0, blk * tb:(blk + 1) * tb, :] = mixed * s_ref[...]


def _pool_mix(u, pool_w, pool_scale):
    B, T, d = u.shape
    assert T % POOL_BLOCK == 0 and T >= 2 * POOL_BLOCK
    w_bd = jax.scipy.linalg.block_diag(*[pool_w[g] for g in range(POOL_GROUPS)]).astype(jnp.bfloat16)
    mats = _pool_matrices(T)
    blk = pl.BlockSpec((1, T, d), lambda b: (b, 0, 0))
    return pl.pallas_call(
        _pool_kernel,
        grid=(B,),
        in_specs=[blk, pl.BlockSpec(mats.shape, lambda b: (0, 0, 0, 0)), pl.BlockSpec((d, d), lambda b: (0, 0)),
                  pl.BlockSpec((1, d), lambda b: (0, 0))],
        out_specs=blk,
        out_shape=jax.ShapeDtypeStruct((B, T, d), jnp.float32),
        compiler_params=pltpu.CompilerParams(dimension_semantics=("parallel",)),
        name="pool_mix",
    )(u, mats, w_bd, pool_scale[None, :])


def _log_sigmoid(x):
    return jnp.minimum(x, 0.0) - jnp.log1p(jnp.exp(-jnp.abs(x)))


def _ml_prep(q, k, conv_ref, cos_ref, sin_ref):
    T = q.shape[0]
    row = lax.broadcasted_iota(jnp.int32, (T, 1), 0)
    lane = lax.broadcasted_iota(jnp.int32, (1, ML_DIM), 1) % (HEAD_DIM // 2)
    outs = []
    for idx, x in enumerate((q, k)):
        acc = None
        for j in range(ML_CONV_W):
            d = j - ML_CONV_W // 2
            w = conv_ref[j:j + 1, idx * ML_DIM:(idx + 1) * ML_DIM]
            if d == 0:
                term = x * w
            else:
                shifted = pltpu.roll(x, (-d) % T, 0)
                term = jnp.where((row + d >= 0) & (row + d < T), shifted, 0.0) * w
            acc = term if acc is None else acc + term
        y = acc * jax.nn.sigmoid(acc)
        if cos_ref is not None:
            quarter = HEAD_DIM // 4
            partner = jnp.where(lane < quarter, pltpu.roll(y, ML_DIM - quarter, 1), pltpu.roll(y, quarter, 1))
            y = y * cos_ref[...] + partner * sin_ref[...]
        outs.append(y)
    return outs


def _ml_chain(qs, kts, vs, rows, gi_c, gi_r, bc, br, ci, ii, head, backward, cn_ref, m_ref, slot, lane, tri):
    L = ML_L
    pair = slice((head // 2) * 128, (head // 2) * 128 + 128)
    own = (lane < HEAD_DIM) if head % 2 == 0 else (lane >= HEAD_DIM)
    one_lane = HEAD_DIM if head % 2 == 0 else 0
    q2 = jnp.where(own, qs[rows, pair], jnp.zeros((), jnp.bfloat16))
    v1 = jnp.where(own, vs[rows, pair].astype(jnp.float32), jnp.where(lane == one_lane, 1.0, 0.0))
    own_rows = lax.broadcasted_iota(jnp.int32, (128, L), 0)
    own_rows = (own_rows < HEAD_DIM) if head % 2 == 0 else (own_rows >= HEAD_DIM)
    kt2 = jnp.where(own_rows, kts[pair, rows], jnp.zeros((), jnp.bfloat16))
    m_prev = m_ref[slot]
    cn = cn_ref[slot]
    bcol = bc[:, ci:ci + 1]
    brow = br[ci:ci + 1, :]
    log_d = jnp.where(tri, bcol - brow + gi_r[ii:ii + 1, :], -jnp.inf)
    inter = bcol + m_prev
    m_t = jnp.maximum(inter, jnp.max(log_d, axis=-1, keepdims=True))
    s = jnp.dot(q2, kt2, preferred_element_type=jnp.float32)
    w_intra = (s * jnp.exp(log_d - m_t)).astype(jnp.bfloat16)
    nd = (jnp.dot(w_intra, v1.astype(jnp.bfloat16), preferred_element_type=jnp.float32)
          + jnp.exp(inter - m_t) * jnp.dot(q2, cn.astype(jnp.bfloat16), preferred_element_type=jnp.float32))
    den = nd[:, one_lane:one_lane + 1]
    h = nd / jnp.maximum(jnp.abs(den), jnp.exp(-m_t))
    b_last = bc[0:1, ci:ci + 1] if backward else bc[L - 1:L, ci:ci + 1]
    log_w = b_last - bcol + gi_c[:, ii:ii + 1]
    m_new = jnp.maximum(b_last + m_prev, jnp.max(log_w, axis=0, keepdims=True))
    wv = (jnp.exp(log_w - m_new) * v1).astype(jnp.bfloat16)
    cn_ref[slot] = jnp.exp(b_last + m_prev - m_new) * cn + jnp.dot(kt2, wv, preferred_element_type=jnp.float32)
    m_ref[slot] = m_new
    return h


def _ml_scan(T, qs, kts, vs, gc_ref, gr_ref, gbc, gbr, hf, hb, cn_ref, m_ref, tril, triu):
    L = ML_L
    nc = T // L
    lane = lax.broadcasted_iota(jnp.int32, (L, 128), 1)
    t_idx = lax.broadcasted_iota(jnp.int32, (L, L), 0)
    s_idx = lax.broadcasted_iota(jnp.int32, (L, L), 1)
    hi = lax.Precision.HIGHEST

    def body(c, carry):
        for backward in (False, True):
            cc = (nc - 1 - c) if backward else c
            rows = pl.ds(pl.multiple_of(cc * L, L), L)
            gc = gc_ref[0, rows, :] + gbc
            gr = gr_ref[0, :, rows] + gbr
            bc = jnp.dot(triu if backward else tril, _log_sigmoid(gc), preferred_element_type=jnp.float32, precision=hi)
            br = jnp.dot(_log_sigmoid(gr), tril if backward else triu, preferred_element_type=jnp.float32, precision=hi)
            tri = (s_idx >= t_idx) if backward else (s_idx <= t_idx)
            out = hb if backward else hf
            base = 2 * ML_HEADS if backward else 0
            for pair in range(ML_HEADS // 2):
                hs = []
                for head in (2 * pair, 2 * pair + 1):
                    hs.append(_ml_chain(qs, kts, vs, rows, gc, gr, bc, br, base + ML_HEADS + head, base + head, head,
                                        backward, cn_ref, m_ref, (ML_HEADS if backward else 0) + head, lane, tri))
                out[rows, pair * 128:(pair + 1) * 128] = jnp.where(lane < HEAD_DIM, hs[0], hs[1])
        return carry

    lax.fori_loop(0, nc, body, 0)


def _mlstm_kernel(q_ref, k_ref, v_ref, o_ref, gc_ref, gr_ref, cq_ref, ck_ref, cv_ref, co_ref, cgc_ref, cgr_ref,
                  cos_ref, sin_ref, conv_ref, gbc_ref, gbr_ref, ng_ref, e_ref, tril_ref, triu_ref,
                  y_ref, cy_ref, qs, kts, vs, hf, hb, cqs, ckts, cvs, chf, chb, cn_ref, m_ref):
    cn_ref[...] = jnp.zeros_like(cn_ref)
    m_ref[...] = jnp.zeros_like(m_ref)
    tril, triu = tril_ref[...], triu_ref[...]

    def stage(q_r, k_r, v_r, rope, qd, ktd, vd):
        qp, kp = _ml_prep(q_r[0], k_r[0], conv_ref, cos_ref if rope else None, sin_ref if rope else None)
        qd[...] = qp.astype(jnp.bfloat16)
        ktd[...] = (kp * HEAD_DIM ** -0.5).T.astype(jnp.bfloat16)
        vd[...] = v_r[0].astype(jnp.bfloat16)

    def finish(h_f, h_b, o_r, y_r):
        h = h_f[...] + h_b[...]
        ss = jnp.dot(h * h, e_ref[...], preferred_element_type=jnp.float32, precision=lax.Precision.HIGHEST)
        y_r[0] = h * lax.rsqrt(ss * (1.0 / HEAD_DIM) + EPS) * ng_ref[...] * jax.nn.sigmoid(o_r[0])

    gbc, gbr = gbc_ref[...], gbr_ref[...]
    stage(cq_ref, ck_ref, cv_ref, False, cqs, ckts, cvs)
    _ml_scan(cq_ref.shape[1], cqs, ckts, cvs, cgc_ref, cgr_ref, gbc, gbr, chf, chb, cn_ref, m_ref, tril, triu)
    finish(chf, chb, co_ref, cy_ref)
    stage(q_ref, k_ref, v_ref, True, qs, kts, vs)
    _ml_scan(q_ref.shape[1], qs, kts, vs, gc_ref, gr_ref, gbc, gbr, hf, hb, cn_ref, m_ref, tril, triu)
    finish(hf, hb, o_ref, y_ref)


def _rope_tables(T):
    t = jnp.arange(T)
    row = (t // GRID_W).astype(jnp.float32)
    col = (t % GRID_W).astype(jnp.float32)
    nf = HEAD_DIM // 4
    inv = ROPE_BASE ** (-jnp.arange(nf, dtype=jnp.float32) / nf)
    cr, sr, cc, sc = jnp.cos(row[:, None] * inv), jnp.sin(row[:, None] * inv), jnp.cos(col[:, None] * inv), \
        jnp.sin(col[:, None] * inv)
    cos = jnp.concatenate([cr, cr, cc, cc], axis=-1)
    sin = jnp.concatenate([-sr, sr, -sc, sc], axis=-1)
    return jnp.tile(cos, (1, ML_HEADS)), jnp.tile(sin, (1, ML_HEADS))


def _mlstm_mix(lq, lk, lv, lo, lg, cq, ck, cv, co, cg, conv_w, gate_b, norm_g):
    B, T, d = lq.shape
    Lc = cq.shape[1]
    L = ML_L
    assert T % L == 0 and Lc % L == 0
    cos, sin = _rope_tables(T)
    lane_head = np.arange(d) // HEAD_DIM
    same_head = jnp.asarray((lane_head[:, None] == lane_head[None, :]).astype(np.float32))
    tril = jnp.asarray(np.tril(np.ones((L, L), np.float32)))
    seq = lambda n: pl.BlockSpec((1, n, d), lambda b: (b, 0, 0))
    full = lambda a: pl.BlockSpec(a.shape, lambda b: (0,) * a.ndim)
    consts = [cos, sin, conv_w, gate_b[None, :], gate_b[:, None], norm_g[None, :], same_head, tril, tril.T]
    f32, bf16 = jnp.float32, jnp.bfloat16
    return pl.pallas_call(
        _mlstm_kernel,
        grid=(B,),
        in_specs=[seq(T)] * 4 + [pl.BlockSpec((1, T, ML_N_GATES), lambda b: (b, 0, 0)),
                                 pl.BlockSpec((1, ML_N_GATES, T), lambda b: (b, 0, 0))]
        + [seq(Lc)] * 4 + [pl.BlockSpec((1, Lc, ML_N_GATES), lambda b: (b, 0, 0)),
                           pl.BlockSpec((1, ML_N_GATES, Lc), lambda b: (b, 0, 0))]
        + [full(a) for a in consts],
        out_specs=[seq(T), seq(Lc)],
        out_shape=[jax.ShapeDtypeStruct((B, T, d), f32), jax.ShapeDtypeStruct((B, Lc, d), f32)],
        scratch_shapes=[pltpu.VMEM((T, d), bf16), pltpu.VMEM((d, T), bf16), pltpu.VMEM((T, d), bf16),
                        pltpu.VMEM((T, d), f32), pltpu.VMEM((T, d), f32),
                        pltpu.VMEM((Lc, d), bf16), pltpu.VMEM((d, Lc), bf16), pltpu.VMEM((Lc, d), bf16),
                        pltpu.VMEM((Lc, d), f32), pltpu.VMEM((Lc, d), f32),
                        pltpu.VMEM((2 * ML_HEADS, 128, 128), f32), pltpu.VMEM((2 * ML_HEADS, 1, 1), f32)],
        compiler_params=pltpu.CompilerParams(dimension_semantics=("parallel",), vmem_limit_bytes=VMEM_LIMIT_BYTES),
        name="mlstm",
    )(lq, lk, lv, lo, lg, jnp.swapaxes(lg, 1, 2), cq, ck, cv, co, cg, jnp.swapaxes(cg, 1, 2), *consts)


def _rms(x, g):
    xf = x.astype(jnp.float32)
    y = xf * lax.rsqrt(jnp.mean(xf * xf, axis=-1, keepdims=True) + EPS)
    return (y * g.astype(jnp.float32)).astype(x.dtype)


def _split_cols(p):
    sizes = (NA_DIM, NA_DIM, NA_DIM, POOL_DIM, ML_DIM, ML_DIM, ML_DIM, ML_DIM, ML_N_GATES)
    offs = np.cumsum(sizes)[:-1].tolist()
    return jnp.split(p, offs, axis=-1)


def _token_mix(xn, hn, w_in, ml_gate_b, na_q_g, na_k_g, na_rpb, pool_w, pool_scale, ml_conv, ml_norm_g,
               need_ctx_out):
    lq, lk, lv, lpool, lmq, lmk, lmv, lmo, lmg = _split_cols(xn @ w_in)
    cq, ck, cv, cpool, cmq, cmk, cmv, cmo, cmg = _split_cols(hn @ w_in)
    lqn, lkn, lvb = _qk_norm(lq, lk, lv, na_q_g, na_k_g)
    cqn, k_ctx, v_ctx = _qk_norm(cq, ck, cv, na_q_g, na_k_g)
    na_lat = _na_latent(lqn, lkn, lvb, k_ctx, v_ctx, na_rpb)
    pool_lat = _pool_mix(lpool, pool_w, pool_scale)
    ml_lat, ml_ctx = _mlstm_mix(lmq, lmk, lmv, lmo, lmg, cmq, cmk, cmv, cmo, cmg, ml_conv, ml_gate_b, ml_norm_g)
    y_lat = jnp.concatenate([na_lat, pool_lat, ml_lat], axis=-1)
    if not need_ctx_out:
        return y_lat, None
    na_ctx = _na_context(cqn, k_ctx, v_ctx)
    pool_ctx = _pool_mix(cpool, pool_w, pool_scale)
    y_ctx = jnp.concatenate([na_ctx, pool_ctx, ml_ctx], axis=-1)
    return y_lat, y_ctx


def kernel(x, c, ctx, c_ctx, w_ada, b_ada, norm1_g, w_in, ml_gate_b, na_q_g, na_k_g, na_rpb, pool_w,
           pool_scale, ml_conv, ml_norm_g, w_out, norm2_g, peer_wq, peer_keys, peer_u, peer_v):
    depth = w_in.shape[0]
    B, S, D = x.shape
    Lc = ctx.shape[1]
    hc = ctx
    for l in range(depth):
        need_ctx = l < depth - 1
        u_bf = peer_u[l].astype(jnp.bfloat16)
        vt_bf = peer_v[l].astype(jnp.bfloat16).T
        wqt_bf = peer_wq[l].astype(jnp.bfloat16).T
        keys_bf = peer_keys[l].astype(jnp.bfloat16).reshape(2 * PEER_HEADS, PEER_KEYS, PEER_DKEY // 2)
        mod_lat = jax.nn.silu(c) @ w_ada[l] + b_ada[l]
        mod_ctx = jax.nn.silu(c_ctx) @ w_ada[l] + b_ada[l]
        sh1, sc1, g1, sh2, sc2, g2 = jnp.split(mod_lat[:, None, :], 6, axis=-1)
        csh1, csc1, cg1, csh2, csc2, cg2 = jnp.split(mod_ctx[None, None, :], 6, axis=-1)
        xn = _rms(x, norm1_g[l]) * (1.0 + sc1) + sh1
        hn = _rms(hc, norm1_g[l]) * (1.0 + csc1) + csh1
        y_lat, y_ctx = _token_mix(xn, hn, w_in[l], ml_gate_b[l], na_q_g[l], na_k_g[l], na_rpb[l], pool_w[l],
                                  pool_scale[l], ml_conv[l], ml_norm_g[l], need_ctx)
        x = x + g1 * (y_lat @ w_out[l])
        x = _peer_block(x, sc2, sh2, g2, norm2_g[l], wqt_bf, keys_bf, u_bf, vt_bf)
        if need_ctx:
            hc = hc + cg1 * (y_ctx @ w_out[l])
            hc = _peer_block(hc.reshape(1, B * Lc, D), csc2, csh2, cg2, norm2_g[l], wqt_bf, keys_bf, u_bf,
                             vt_bf).reshape(B, Lc, D)
    return x
```

```python
import functools

import numpy as np
import jax
import jax.numpy as jnp
from jax import lax
from jax.experimental import pallas as pl
from jax.experimental.pallas import tpu as pltpu

D_MODEL = 1024
GRID_W = 64
HEAD_DIM = 64
NA_DIM = D_MODEL // 2
NA_HEADS = NA_DIM // HEAD_DIM
NA_WIN_ROWS = 8
NA_WIN_COLS = 16
POOL_DIM = D_MODEL // 4
POOL_WINDOWS = (2, 4, 8, 16)
POOL_GROUPS = len(POOL_WINDOWS)
POOL_GROUP_DIM = POOL_DIM // POOL_GROUPS
ML_DIM = D_MODEL // 4
ML_HEADS = ML_DIM // HEAD_DIM
ML_CONV_W = 5
ML_N_GATES = 4 * ML_HEADS
PEER_KEYS = 128
PEER_EXPERTS = PEER_KEYS * PEER_KEYS
PEER_HEADS = 8
PEER_TOPK = 16
PEER_DKEY = 256
ROPE_BASE = 10000.0
EPS = 1e-6

PEER_TOKEN_TILE = 1024
PEER_EXPERT_TILE = 1024
ROUTE_TOKEN_TILE = 512
IN_TOKEN_TILE = 256
ADA_COL_TILE = 1536
POOL_BLOCK = 128
POOL_HALO = 8
ML_L = 128
VMEM_LIMIT_BYTES = 56 * 1024 * 1024

_GELU_C0 = float(np.sqrt(2.0 / np.pi))
_GELU_C1 = 0.044715 * _GELU_C0


def _gelu_tanh(x):
    inner = x * (_GELU_C0 + _GELU_C1 * (x * x))
    hx = 0.5 * x
    return hx + hx * jnp.tanh(inner)


def _peer_dense_kernel(x_ref, u_ref, vt_ref, cnt_ref, e1_ref, rank_ref, e2_ref, o_ref):
    j = pl.program_id(1)

    @pl.when(j == 0)
    def _():
        o_ref[...] = jnp.zeros_like(o_ref)

    te = u_ref.shape[0]
    act = lax.dot_general(u_ref[...], x_ref[...], (((1,), (1,)), ((), ())),
                          preferred_element_type=jnp.float32)
    gates = []
    for k in range(te // PEER_KEYS):
        g = None
        for h in range(PEER_HEADS):
            cnt = cnt_ref[h, k:k + 1, :].astype(jnp.bfloat16)
            e1 = e1_ref[h, k:k + 1, :].astype(jnp.bfloat16)
            term = jnp.where(rank_ref[h] < cnt, e2_ref[h], jnp.zeros((), e2_ref.dtype)) * e1
            g = term if g is None else g + term
        gates.append(g)
    gate = jnp.concatenate(gates, axis=0)
    p = _gelu_tanh(act).astype(jnp.bfloat16) * gate
    o_ref[...] += jnp.dot(vt_ref[...], p, preferred_element_type=jnp.float32)


def _peer_dense(tok, u_bf, vt_bf, cnt, e1, rank, e2):
    n, d = tok.shape
    tn, te = PEER_TOKEN_TILE, PEER_EXPERT_TILE
    assert n % tn == 0 and PEER_EXPERTS % te == 0 and te % PEER_KEYS == 0
    tab_a = pl.BlockSpec((PEER_HEADS, te // PEER_KEYS, tn), lambda i, j: (0, j, i))
    tab_b = pl.BlockSpec((PEER_HEADS, PEER_KEYS, tn), lambda i, j: (0, 0, i))
    return pl.pallas_call(
        _peer_dense_kernel,
        grid=(n // tn, PEER_EXPERTS // te),
        in_specs=[
            pl.BlockSpec((tn, d), lambda i, j: (i, 0)),
            pl.BlockSpec((te, d), lambda i, j: (j, 0)),
            pl.BlockSpec((d, te), lambda i, j: (0, j)),
            tab_a, tab_a, tab_b, tab_b,
        ],
        out_specs=pl.BlockSpec((d, tn), lambda i, j: (0, i)),
        out_shape=jax.ShapeDtypeStruct((d, n), jnp.float32),
        compiler_params=pltpu.CompilerParams(
            dimension_semantics=("parallel", "arbitrary"), vmem_limit_bytes=VMEM_LIMIT_BYTES),
        name="peer_dense",
    )(tok, u_bf, vt_bf, cnt, e1, rank, e2)


def _top16(work, want_rank):
    rows = lax.broadcasted_iota(jnp.int32, (PEER_TOPK, work.shape[1]), 0)
    vals = jnp.zeros((PEER_TOPK, work.shape[1]), jnp.float32)
    rank = jnp.full(work.shape, float(PEER_TOPK), jnp.float32) if want_rank else None
    tops = []
    for i in range(PEER_TOPK):
        m = jnp.max(work, axis=0, keepdims=True)
        hit = work == m
        if want_rank:
            rank = jnp.where(hit, float(i), rank)
        work = jnp.where(hit, -jnp.inf, work)
        vals = jnp.where(rows == i, m, vals)
        tops.append(m)
    return tops, vals, rank


def _peer_route_kernel(x_ref, sc_ref, sh_ref, g_ref, wqt_ref, keys_ref,
                       tok_ref, cnt_ref, e1_ref, rank_ref, e2_ref, s_scr):
    x = x_ref[0]
    y = x * lax.rsqrt(jnp.mean(x * x, axis=-1, keepdims=True) + EPS) * g_ref[...]
    tok = (y * (1.0 + sc_ref[0]) + sh_ref[0]).astype(jnp.bfloat16)
    tok_ref[...] = tok
    qt = lax.dot_general(wqt_ref[...], tok, (((1,), (1,)), ((), ())),
                         preferred_element_type=jnp.float32).astype(jnp.bfloat16)
    dk = PEER_DKEY // 2
    for hp in range(2 * PEER_HEADS):
        s_scr[hp] = jnp.dot(keys_ref[hp], qt[hp * dk:(hp + 1) * dk], preferred_element_type=jnp.float32)

    lane_chunks = x.shape[0] // 128
    row8 = lax.broadcasted_iota(jnp.int32, (8, 128), 0)

    def body(it, carry):
        h = it // lane_chunks
        lanes = pl.ds(pl.multiple_of((it % lane_chunks) * 128, 128), 128)
        s1 = s_scr[2 * h, :, lanes]
        s2 = s_scr[2 * h + 1, :, lanes]
        t1, v1, _ = _top16(s1, False)
        t2, v2, rank = _top16(s2, True)
        groups = [t1[0] + v2, t1[1] + v2[:8]]
        for i in range(2, 8):
            groups.append(jnp.where(row8 < PEER_TOPK // (i + 1), t1[i] + v2[:8], -jnp.inf))
        groups.append(v1[8:] + t2[0])
        tp, vp, _ = _top16(jnp.concatenate(groups, axis=0), False)
        tau = tp[PEER_TOPK - 1]
        z = jnp.sum(jnp.exp(vp - tp[0]), axis=0, keepdims=True)
        cnt = jnp.zeros(s1.shape, jnp.float32)
        for jj in range(PEER_TOPK):
            cnt = jnp.where(s1 + t2[jj] >= tau, float(jj + 1), cnt)
        cnt = jnp.where(s1 >= t1[PEER_TOPK - 1], cnt, 0.0)
        cnt_ref[h, :, lanes] = cnt
        e1_ref[h, :, lanes] = jnp.exp(s1 - t1[0])
        rank_ref[h, :, lanes] = rank.astype(jnp.bfloat16)
        e2_ref[h, :, lanes] = (jnp.exp(s2 - t2[0]) * (1.0 / z)).astype(jnp.bfloat16)
        return carry

    lax.fori_loop(0, PEER_HEADS * lane_chunks, body, 0)


def _peer_route(x3, sc, sh, g, wqt_bf, keys_bf):
    B, T, d = x3.shape
    tn = min(ROUTE_TOKEN_TILE, T)
    assert T % tn == 0 and tn % 128 == 0
    nt = T // tn
    n = B * T
    tab = pl.BlockSpec((PEER_HEADS, PEER_KEYS, tn), lambda b, t: (0, 0, b * nt + t))
    tab_shape = lambda dt: jax.ShapeDtypeStruct((PEER_HEADS, PEER_KEYS, n), dt)
    mod = pl.BlockSpec((1, 1, d), lambda b, t: (b, 0, 0))
    return pl.pallas_call(
        _peer_route_kernel,
        grid=(B, nt),
        in_specs=[
            pl.BlockSpec((1, tn, d), lambda b, t: (b, t, 0)),
            mod, mod,
            pl.BlockSpec((1, d), lambda b, t: (0, 0)),
            pl.BlockSpec(wqt_bf.shape, lambda b, t: (0, 0)),
            pl.BlockSpec(keys_bf.shape, lambda b, t: (0, 0, 0)),
        ],
        out_specs=[pl.BlockSpec((tn, d), lambda b, t: (b * nt + t, 0)), tab, tab, tab, tab],
        out_shape=[jax.ShapeDtypeStruct((n, d), jnp.bfloat16), tab_shape(jnp.float32), tab_shape(jnp.float32),
                   tab_shape(jnp.bfloat16), tab_shape(jnp.bfloat16)],
        scratch_shapes=[pltpu.VMEM((2 * PEER_HEADS, PEER_KEYS, tn), jnp.float32)],
        compiler_params=pltpu.CompilerParams(
            dimension_semantics=("parallel", "parallel"), vmem_limit_bytes=VMEM_LIMIT_BYTES),
        name="peer_route",
    )(x3, sc, sh, g, wqt_bf, keys_bf)


def _residual_t_kernel(x_ref, g_ref, yt_ref, o_ref):
    o_ref[0] = x_ref[0] + g_ref[0] * yt_ref[...].T


def _residual_t(x3, gate, yt):
    B, T, d = x3.shape
    tn = min(ROUTE_TOKEN_TILE, T)
    nt = T // tn
    return pl.pallas_call(
        _residual_t_kernel,
        grid=(B, nt),
        in_specs=[pl.BlockSpec((1, tn, d), lambda b, t: (b, t, 0)),
                  pl.BlockSpec((1, 1, d), lambda b, t: (b, 0, 0)),
                  pl.BlockSpec((d, tn), lambda b, t: (0, b * nt + t))],
        out_specs=pl.BlockSpec((1, tn, d), lambda b, t: (b, t, 0)),
        out_shape=jax.ShapeDtypeStruct(x3.shape, x3.dtype),
        compiler_params=pltpu.CompilerParams(dimension_semantics=("parallel", "parallel")),
        name="residual_t",
    )(x3, gate, yt)


def _peer_block(x3, sc, sh, gate, norm_g, wqt_bf, keys_bf, u_bf, vt_bf):
    tok, cnt, e1, rank, e2 = _peer_route(x3, sc, sh, norm_g[None, :], wqt_bf, keys_bf)
    out_t = _peer_dense(tok, u_bf, vt_bf, cnt, e1, rank, e2)
    return _residual_t(x3, gate, out_t)


def _attend_pair(q2, segments, lane):
    zero = jnp.zeros((), q2.dtype)
    outs = []
    for half in range(2):
        own = (lane < HEAD_DIM) if half == 0 else (lane >= HEAD_DIM)
        qh = jnp.where(own, q2, zero)
        scores = []
        for k2, _, bias in segments:
            s = lax.dot_general(qh, k2, (((1,), (1,)), ((), ())), preferred_element_type=jnp.float32)
            scores.append(s if bias is None else s + bias[half])
        m = functools.reduce(jnp.maximum, [jnp.max(s, axis=-1, keepdims=True) for s in scores])
        ps = [jnp.exp(s - m) for s in scores]
        denom = functools.reduce(jnp.add, [jnp.sum(p, axis=-1, keepdims=True) for p in ps])
        acc = functools.reduce(jnp.add, [jnp.dot(p.astype(v2.dtype), v2, preferred_element_type=jnp.float32)
                                         for p, (_, v2, _) in zip(ps, segments)])
        outs.append(acc / denom)
    return jnp.where(lane < HEAD_DIM, outs[0], outs[1])


def _na_latent_kernel(q_ref, k_ref, v_ref, kc_ref, vc_ref, bias_ref, o_ref):
    r = pl.program_id(1)
    n_rows = k_ref.shape[1] // GRID_W
    r0 = jnp.clip(r - NA_WIN_ROWS // 2, 0, n_rows - NA_WIN_ROWS)
    strip = pl.ds(pl.multiple_of(r0 * GRID_W, GRID_W), NA_WIN_ROWS * GRID_W)
    lane = lax.broadcasted_iota(jnp.int32, (GRID_W, 128), 1)
    for p in range(NA_DIM // 128):
        lanes = slice(p * 128, (p + 1) * 128)
        segs = [(k_ref[0, strip, lanes], v_ref[0, strip, lanes], (bias_ref[0, 2 * p], bias_ref[0, 2 * p + 1])),
                (kc_ref[0, :, lanes], vc_ref[0, :, lanes], None)]
        o_ref[0, :, lanes] = _attend_pair(q_ref[0, :, lanes], segs, lane)


def _na_bias_strips(rpb, n_rows):
    cq = np.arange(GRID_W)
    c0 = np.clip(cq - NA_WIN_COLS // 2, 0, GRID_W - NA_WIN_COLS)
    in_win = (cq[None, :] >= c0[:, None]) & (cq[None, :] < c0[:, None] + NA_WIN_COLS)
    bc = np.clip(cq[None, :] - cq[:, None] + (NA_WIN_COLS - 1), 0, 2 * NA_WIN_COLS - 2)
    toe = jnp.where(in_win[None, None], rpb[:, :, bc], -jnp.inf)
    strips = [jnp.moveaxis(toe[:, s:s + NA_WIN_ROWS], 1, 2).reshape(NA_HEADS, GRID_W, NA_WIN_ROWS * GRID_W)
              for s in range(NA_WIN_ROWS)]
    return jnp.stack(strips, axis=0).astype(jnp.float32)


def _na_latent(q, k, v, k_ctx, v_ctx, rpb):
    B, S, d = q.shape
    Lc = k_ctx.shape[1]
    n_rows = S // GRID_W
    assert n_rows >= NA_WIN_ROWS
    bias = _na_bias_strips(rpb, n_rows)

    def bias_start(b, r):
        r0 = jnp.clip(r - NA_WIN_ROWS // 2, 0, n_rows - NA_WIN_ROWS)
        return (r0 - r + NA_WIN_ROWS - 1, 0, 0, 0)

    whole = lambda L: pl.BlockSpec((1, L, d), lambda b, r: (b, 0, 0))
    return pl.pallas_call(
        _na_latent_kernel,
        grid=(B, n_rows),
        in_specs=[pl.BlockSpec((1, GRID_W, d), lambda b, r: (b, r, 0)), whole(S), whole(S), whole(Lc), whole(Lc),
                  pl.BlockSpec((1,) + bias.shape[1:], bias_start)],
        out_specs=pl.BlockSpec((1, GRID_W, d), lambda b, r: (b, r, 0)),
        out_shape=jax.ShapeDtypeStruct((B, S, d), jnp.float32),
        compiler_params=pltpu.CompilerParams(dimension_semantics=("parallel", "arbitrary")),
        name="na_latent",
    )(q, k, v, k_ctx, v_ctx, bias)


def _na_context_kernel(q_ref, k_ref, v_ref, o_ref):
    lane = lax.broadcasted_iota(jnp.int32, (q_ref.shape[1], 128), 1)
    for p in range(NA_DIM // 128):
        lanes = slice(p * 128, (p + 1) * 128)
        o_ref[0, :, lanes] = _attend_pair(q_ref[0, :, lanes], [(k_ref[0, :, lanes], v_ref[0, :, lanes], None)], lane)


def _na_context(q, k, v):
    B, Lc, d = q.shape
    blk = pl.BlockSpec((1, Lc, d), lambda b: (b, 0, 0))
    return pl.pallas_call(
        _na_context_kernel,
        grid=(B,),
        in_specs=[blk, blk, blk],
        out_specs=blk,
        out_shape=jax.ShapeDtypeStruct((B, Lc, d), jnp.float32),
        compiler_params=pltpu.CompilerParams(dimension_semantics=("parallel",)),
        name="na_context",
    )(q, k, v)


def _pool_matrices(T):
    tb, halo = POOL_BLOCK, POOL_HALO
    mats = np.zeros((len(POOL_WINDOWS), 3, tb, tb + 2 * halo), np.float32)
    for wi, w in enumerate(POOL_WINDOWS):
        for kind, start in enumerate((0, tb, T - tb)):
            for i in range(tb):
                t = start + i
                lo = min(max(t - w // 2, 0), T - 1)
                hi = min(max(t + (w - w // 2 - 1), 0), T - 1)
                mats[wi, kind, i, lo - start + halo:hi - start + halo + 1] = 1.0 / (hi - lo + 1)
                mats[wi, kind, i, i + halo] -= 1.0
    return jnp.asarray(mats)


def _pool_kernel(u_ref, a_ref, w_ref, s_ref, o_ref):
    T = u_ref.shape[1]
    tb, halo = POOL_BLOCK, POOL_HALO
    nb = T // tb
    group = lax.broadcasted_iota(jnp.int32, (tb, POOL_DIM), 1) // POOL_GROUP_DIM
    zeros = jnp.zeros((halo, POOL_DIM), jnp.float32)
    for blk in range(nb):
        kind = 0 if blk == 0 else (2 if blk == nb - 1 else 1)
        before = zeros if blk == 0 else u_ref[0, blk * tb - halo:blk * tb, :]
        after = zeros if blk == nb - 1 else u_ref[0, (blk + 1) * tb:(blk + 1) * tb + halo, :]
        xcat = jnp.concatenate([before, u_ref[0, blk * tb:(blk + 1) * tb, :], after], axis=0)
        pooled = None
        for wi in range(len(POOL_WINDOWS)):
            pw = jnp.dot(a_ref[wi, kind], xcat, preferred_element_type=jnp.float32, precision=lax.Precision.HIGHEST)
            pooled = pw if pooled is None else jnp.where(group == wi, pw, pooled)
        mixed = jnp.dot(pooled.astype(jnp.bfloat16), w_ref[...], preferred_element_type=jnp.float32)
        o_ref[0, blk * tb:(blk + 1) * tb, :] = mixed * s_ref[...]


def _pool_mix(u, pool_w, pool_scale):
    B, T, d = u.shape
    assert T % POOL_BLOCK == 0 and T >= 2 * POOL_BLOCK
    w_bd = jax.scipy.linalg.block_diag(*[pool_w[g] for g in range(POOL_GROUPS)]).astype(jnp.bfloat16)
    mats = _pool_matrices(T)
    blk = pl.BlockSpec((1, T, d), lambda b: (b, 0, 0))
    return pl.pallas_call(
        _pool_kernel,
        grid=(B,),
        in_specs=[blk, pl.BlockSpec(mats.shape, lambda b: (0, 0, 0, 0)), pl.BlockSpec((d, d), lambda b: (0, 0)),
                  pl.BlockSpec((1, d), lambda b: (0, 0))],
        out_specs=blk,
        out_shape=jax.ShapeDtypeStruct((B, T, d), jnp.float32),
        compiler_params=pltpu.CompilerParams(dimension_semantics=("parallel",)),
        name="pool_mix",
    )(u, mats, w_bd, pool_scale[None, :])


def _log_sigmoid(x):
    return jnp.minimum(x, 0.0) - jnp.log1p(jnp.exp(-jnp.abs(x)))


def _ml_prep(q, k, conv_ref, cos_ref, sin_ref):
    T = q.shape[0]
    row = lax.broadcasted_iota(jnp.int32, (T, 1), 0)
    lane = lax.broadcasted_iota(jnp.int32, (1, ML_DIM), 1) % (HEAD_DIM // 2)
    outs = []
    for idx, x in enumerate((q, k)):
        acc = None
        for j in range(ML_CONV_W):
            d = j - ML_CONV_W // 2
            w = conv_ref[j:j + 1, idx * ML_DIM:(idx + 1) * ML_DIM]
            if d == 0:
                term = x * w
            else:
                shifted = pltpu.roll(x, (-d) % T, 0)
                term = jnp.where((row + d >= 0) & (row + d < T), shifted, 0.0) * w
            acc = term if acc is None else acc + term
        y = acc * jax.nn.sigmoid(acc)
        if cos_ref is not None:
            quarter = HEAD_DIM // 4
            partner = jnp.where(lane < quarter, pltpu.roll(y, ML_DIM - quarter, 1), pltpu.roll(y, quarter, 1))
            y = y * cos_ref[...] + partner * sin_ref[...]
        outs.append(y)
    return outs


def _ml_chain(qs, kts, vs, rows, gi_c, gi_r, bc, br, ci, ii, head, backward, cn_ref, m_ref, slot, lane, tri):
    L = ML_L
    pair = slice((head // 2) * 128, (head // 2) * 128 + 128)
    own = (lane < HEAD_DIM) if head % 2 == 0 else (lane >= HEAD_DIM)
    one_lane = HEAD_DIM if head % 2 == 0 else 0
    q2 = jnp.where(own, qs[rows, pair], jnp.zeros((), jnp.bfloat16))
    v1 = jnp.where(own, vs[rows, pair].astype(jnp.float32), jnp.where(lane == one_lane, 1.0, 0.0))
    own_rows = lax.broadcasted_iota(jnp.int32, (128, L), 0)
    own_rows = (own_rows < HEAD_DIM) if head % 2 == 0 else (own_rows >= HEAD_DIM)
    kt2 = jnp.where(own_rows, kts[pair, rows], jnp.zeros((), jnp.bfloat16))
    m_prev = m_ref[slot]
    cn = cn_ref[slot]
    bcol = bc[:, ci:ci + 1]
    brow = br[ci:ci + 1, :]
    log_d = jnp.where(tri, bcol - brow + gi_r[ii:ii + 1, :], -jnp.inf)
    inter = bcol + m_prev
    m_t = jnp.maximum(inter, jnp.max(log_d, axis=-1, keepdims=True))
    s = jnp.dot(q2, kt2, preferred_element_type=jnp.float32)
    w_intra = (s * jnp.exp(log_d - m_t)).astype(jnp.bfloat16)
    nd = (jnp.dot(w_intra, v1.astype(jnp.bfloat16), preferred_element_type=jnp.float32)
          + jnp.exp(inter - m_t) * jnp.dot(q2, cn.astype(jnp.bfloat16), preferred_element_type=jnp.float32))
    den = nd[:, one_lane:one_lane + 1]
    h = nd / jnp.maximum(jnp.abs(den), jnp.exp(-m_t))
    b_last = bc[0:1, ci:ci + 1] if backward else bc[L - 1:L, ci:ci + 1]
    log_w = b_last - bcol + gi_c[:, ii:ii + 1]
    m_new = jnp.maximum(b_last + m_prev, jnp.max(log_w, axis=0, keepdims=True))
    wv = (jnp.exp(log_w - m_new) * v1).astype(jnp.bfloat16)
    cn_ref[slot] = jnp.exp(b_last + m_prev - m_new) * cn + jnp.dot(kt2, wv, preferred_element_type=jnp.float32)
    m_ref[slot] = m_new
    return h


def _ml_scan(T, qs, kts, vs, gc_ref, gr_ref, gbc, gbr, hf, hb, cn_ref, m_ref, tril, triu):
    L = ML_L
    nc = T // L
    lane = lax.broadcasted_iota(jnp.int32, (L, 128), 1)
    t_idx = lax.broadcasted_iota(jnp.int32, (L, L), 0)
    s_idx = lax.broadcasted_iota(jnp.int32, (L, L), 1)
    hi = lax.Precision.HIGHEST

    def body(c, carry):
        for backward in (False, True):
            cc = (nc - 1 - c) if backward else c
            rows = pl.ds(pl.multiple_of(cc * L, L), L)
            gc = gc_ref[0, rows, :] + gbc
            gr = gr_ref[0, :, rows] + gbr
            bc = jnp.dot(triu if backward else tril, _log_sigmoid(gc), preferred_element_type=jnp.float32, precision=hi)
            br = jnp.dot(_log_sigmoid(gr), tril if backward else triu, preferred_element_type=jnp.float32, precision=hi)
            tri = (s_idx >= t_idx) if backward else (s_idx <= t_idx)
            out = hb if backward else hf
            base = 2 * ML_HEADS if backward else 0
            for pair in range(ML_HEADS // 2):
                hs = []
                for head in (2 * pair, 2 * pair + 1):
                    hs.append(_ml_chain(qs, kts, vs, rows, gc, gr, bc, br, base + ML_HEADS + head, base + head, head,
                                        backward, cn_ref, m_ref, (ML_HEADS if backward else 0) + head, lane, tri))
                out[rows, pair * 128:(pair + 1) * 128] = jnp.where(lane < HEAD_DIM, hs[0], hs[1])
        return carry

    lax.fori_loop(0, nc, body, 0)


def _mlstm_kernel(q_ref, k_ref, v_ref, o_ref, gc_ref, gr_ref, cq_ref, ck_ref, cv_ref, co_ref, cgc_ref, cgr_ref,
                  cos_ref, sin_ref, conv_ref, gbc_ref, gbr_ref, ng_ref, e_ref, tril_ref, triu_ref,
                  y_ref, cy_ref, qs, kts, vs, hf, hb, cqs, ckts, cvs, chf, chb, cn_ref, m_ref):
    cn_ref[...] = jnp.zeros_like(cn_ref)
    m_ref[...] = jnp.zeros_like(m_ref)
    tril, triu = tril_ref[...], triu_ref[...]

    def stage(q_r, k_r, v_r, rope, qd, ktd, vd):
        qp, kp = _ml_prep(q_r[0], k_r[0], conv_ref, cos_ref if rope else None, sin_ref if rope else None)
        qd[...] = qp.astype(jnp.bfloat16)
        ktd[...] = (kp * HEAD_DIM ** -0.5).T.astype(jnp.bfloat16)
        vd[...] = v_r[0].astype(jnp.bfloat16)

    def finish(h_f, h_b, o_r, y_r):
        h = h_f[...] + h_b[...]
        ss = jnp.dot(h * h, e_ref[...], preferred_element_type=jnp.float32, precision=lax.Precision.HIGHEST)
        y_r[0] = h * lax.rsqrt(ss * (1.0 / HEAD_DIM) + EPS) * ng_ref[...] * jax.nn.sigmoid(o_r[0])

    gbc, gbr = gbc_ref[...], gbr_ref[...]
    stage(cq_ref, ck_ref, cv_ref, False, cqs, ckts, cvs)
    _ml_scan(cq_ref.shape[1], cqs, ckts, cvs, cgc_ref, cgr_ref, gbc, gbr, chf, chb, cn_ref, m_ref, tril, triu)
    finish(chf, chb, co_ref, cy_ref)
    stage(q_ref, k_ref, v_ref, True, qs, kts, vs)
    _ml_scan(q_ref.shape[1], qs, kts, vs, gc_ref, gr_ref, gbc, gbr, hf, hb, cn_ref, m_ref, tril, triu)
    finish(hf, hb, o_ref, y_ref)


def _rope_tables(T):
    t = jnp.arange(T)
    row = (t // GRID_W).astype(jnp.float32)
    col = (t % GRID_W).astype(jnp.float32)
    nf = HEAD_DIM // 4
    inv = ROPE_BASE ** (-jnp.arange(nf, dtype=jnp.float32) / nf)
    cr, sr, cc, sc = jnp.cos(row[:, None] * inv), jnp.sin(row[:, None] * inv), jnp.cos(col[:, None] * inv), \
        jnp.sin(col[:, None] * inv)
    cos = jnp.concatenate([cr, cr, cc, cc], axis=-1)
    sin = jnp.concatenate([-sr, sr, -sc, sc], axis=-1)
    return jnp.tile(cos, (1, ML_HEADS)), jnp.tile(sin, (1, ML_HEADS))


def _mlstm_mix(lq, lk, lv, lo, lg, lgr, cq, ck, cv, co, cg, cgr, conv_w, gate_b, norm_g):
    B, T, d = lq.shape
    Lc = cq.shape[1]
    L = ML_L
    assert T % L == 0 and Lc % L == 0
    cos, sin = _rope_tables(T)
    lane_head = np.arange(d) // HEAD_DIM
    same_head = jnp.asarray((lane_head[:, None] == lane_head[None, :]).astype(np.float32))
    tril = jnp.asarray(np.tril(np.ones((L, L), np.float32)))
    seq = lambda n: pl.BlockSpec((1, n, d), lambda b: (b, 0, 0))
    full = lambda a: pl.BlockSpec(a.shape, lambda b: (0,) * a.ndim)
    consts = [cos, sin, conv_w, gate_b[None, :], gate_b[:, None], norm_g[None, :], same_head, tril, tril.T]
    f32, bf16 = jnp.float32, jnp.bfloat16
    return pl.pallas_call(
        _mlstm_kernel,
        grid=(B,),
        in_specs=[seq(T)] * 4 + [pl.BlockSpec((1, T, ML_N_GATES), lambda b: (b, 0, 0)),
                                 pl.BlockSpec((1, ML_N_GATES, T), lambda b: (b, 0, 0))]
        + [seq(Lc)] * 4 + [pl.BlockSpec((1, Lc, ML_N_GATES), lambda b: (b, 0, 0)),
                           pl.BlockSpec((1, ML_N_GATES, Lc), lambda b: (b, 0, 0))]
        + [full(a) for a in consts],
        out_specs=[seq(T), seq(Lc)],
        out_shape=[jax.ShapeDtypeStruct((B, T, d), f32), jax.ShapeDtypeStruct((B, Lc, d), f32)],
        scratch_shapes=[pltpu.VMEM((T, d), bf16), pltpu.VMEM((d, T), bf16), pltpu.VMEM((T, d), bf16),
                        pltpu.VMEM((T, d), f32), pltpu.VMEM((T, d), f32),
                        pltpu.VMEM((Lc, d), bf16), pltpu.VMEM((d, Lc), bf16), pltpu.VMEM((Lc, d), bf16),
                        pltpu.VMEM((Lc, d), f32), pltpu.VMEM((Lc, d), f32),
                        pltpu.VMEM((2 * ML_HEADS, 128, 128), f32), pltpu.VMEM((2 * ML_HEADS, 1, 1), f32)],
        compiler_params=pltpu.CompilerParams(dimension_semantics=("parallel",), vmem_limit_bytes=VMEM_LIMIT_BYTES),
        name="mlstm",
    )(lq, lk, lv, lo, lg, lgr, cq, ck, cv, co, cg, cgr, *consts)


def _ada_kernel(c_ref, w_ref, b_ref, o_ref):
    c = c_ref[...]
    act = (c * jax.nn.sigmoid(c)).astype(jnp.bfloat16)
    o_ref[...] = jnp.dot(act, w_ref[...].astype(jnp.bfloat16), preferred_element_type=jnp.float32) + b_ref[...]


def _ada_mod(cond, w, b):
    rows, d = cond.shape
    n = w.shape[1]
    tile = ADA_COL_TILE
    assert n % tile == 0
    return pl.pallas_call(
        _ada_kernel,
        grid=(n // tile,),
        in_specs=[pl.BlockSpec((rows, d), lambda j: (0, 0)), pl.BlockSpec((d, tile), lambda j: (0, j)),
                  pl.BlockSpec((1, tile), lambda j: (0, j))],
        out_specs=pl.BlockSpec((rows, tile), lambda j: (0, j)),
        out_shape=jax.ShapeDtypeStruct((rows, n), jnp.float32),
        compiler_params=pltpu.CompilerParams(dimension_semantics=("parallel",)),
        name="ada_mod",
    )(cond, w, b[None, :])


_IN_SPLITS = (("q", NA_DIM), ("k", NA_DIM), ("v", NA_DIM), ("pool", POOL_DIM), ("mq", ML_DIM), ("mk", ML_DIM),
              ("mv", ML_DIM), ("mo", ML_DIM), ("gates", ML_N_GATES))
IN_COLS = sum(n for _, n in _IN_SPLITS)
IN_COLS_PAD = -(-IN_COLS // 128) * 128


def _in_proj_kernel(x_ref, sc_ref, sh_ref, g_ref, w_ref, wgt_ref, gq_ref, gk_ref, e_ref,
                    q_ref, k_ref, v_ref, pool_ref, mq_ref, mk_ref, mv_ref, mo_ref, gc_ref, gr_ref):
    x = x_ref[0]
    y = x * lax.rsqrt(jnp.mean(x * x, axis=-1, keepdims=True) + EPS) * g_ref[...]
    xn = (y * (1.0 + sc_ref[0]) + sh_ref[0]).astype(jnp.bfloat16)
    p = jnp.dot(xn, w_ref[...], preferred_element_type=jnp.float32)
    off = {}
    o = 0
    for name, n in _IN_SPLITS:
        off[name] = slice(o, o + n)
        o += n

    def head_norm(u, gain):
        sq = u * u
        hi = sq.astype(jnp.bfloat16)
        lo = (sq - hi.astype(jnp.float32)).astype(jnp.bfloat16)
        ss = (jnp.dot(hi, e_ref[...], preferred_element_type=jnp.float32)
              + jnp.dot(lo, e_ref[...], preferred_element_type=jnp.float32))
        return u * lax.rsqrt(ss * (1.0 / HEAD_DIM) + EPS) * gain

    q_ref[0] = (head_norm(p[:, off["q"]], gq_ref[...]) * HEAD_DIM ** -0.5).astype(jnp.bfloat16)
    k_ref[0] = head_norm(p[:, off["k"]], gk_ref[...]).astype(jnp.bfloat16)
    v_ref[0] = p[:, off["v"]].astype(jnp.bfloat16)
    pool_ref[0] = p[:, off["pool"]]
    mq_ref[0] = p[:, off["mq"]]
    mk_ref[0] = p[:, off["mk"]]
    mv_ref[0] = p[:, off["mv"]]
    mo_ref[0] = p[:, off["mo"]]
    gc_ref[0] = p[:, off["gates"]]
    gr_ref[0] = lax.dot_general(wgt_ref[...], xn, (((1,), (1,)), ((), ())), preferred_element_type=jnp.float32)


def _in_proj(x3, sc, sh, norm_g, w_in, gq, gk):
    B, T, d = x3.shape
    tn = min(IN_TOKEN_TILE, T)
    assert T % tn == 0
    w_bf = jnp.pad(w_in, ((0, 0), (0, IN_COLS_PAD - IN_COLS))).astype(jnp.bfloat16)
    wgt_bf = w_in[:, IN_COLS - ML_N_GATES:].T.astype(jnp.bfloat16)
    lane_head = np.arange(NA_DIM) // HEAD_DIM
    same_head = jnp.asarray((lane_head[:, None] == lane_head[None, :]).astype(np.float32)).astype(jnp.bfloat16)
    mod = pl.BlockSpec((1, 1, d), lambda b, t: (b, 0, 0))
    full = lambda a: pl.BlockSpec(a.shape, lambda b, t: (0,) * a.ndim)
    tok = lambda n: pl.BlockSpec((1, tn, n), lambda b, t: (b, t, 0))
    shp = lambda n, dt: jax.ShapeDtypeStruct((B, T, n), dt)
    f32, bf16 = jnp.float32, jnp.bfloat16
    consts = [norm_g[None, :], w_bf, wgt_bf, jnp.tile(gq, NA_HEADS)[None, :], jnp.tile(gk, NA_HEADS)[None, :], same_head]
    outs = pl.pallas_call(
        _in_proj_kernel,
        grid=(B, T // tn),
        in_specs=[pl.BlockSpec((1, tn, d), lambda b, t: (b, t, 0)), mod, mod] + [full(a) for a in consts],
        out_specs=[tok(NA_DIM)] * 3 + [tok(POOL_DIM)] + [tok(ML_DIM)] * 4 + [tok(ML_N_GATES)]
        + [pl.BlockSpec((1, ML_N_GATES, tn), lambda b, t: (b, 0, t))],
        out_shape=[shp(NA_DIM, bf16)] * 3 + [shp(POOL_DIM, f32)] + [shp(ML_DIM, f32)] * 4 + [shp(ML_N_GATES, f32)]
        + [jax.ShapeDtypeStruct((B, ML_N_GATES, T), f32)],
        compiler_params=pltpu.CompilerParams(
            dimension_semantics=("parallel", "parallel"), vmem_limit_bytes=VMEM_LIMIT_BYTES),
        name="in_proj",
    )(x3, sc, sh, *consts)
    return dict(zip(("q", "k", "v", "pool", "mq", "mk", "mv", "mo", "gc", "gr"), outs))


def _out_proj_kernel(x_ref, g_ref, na_ref, pool_ref, ml_ref, w_ref, o_ref):
    bf = jnp.bfloat16
    y = jnp.dot(na_ref[0].astype(bf), w_ref[0:NA_DIM, :], preferred_element_type=jnp.float32)
    y += jnp.dot(pool_ref[0].astype(bf), w_ref[NA_DIM:NA_DIM + POOL_DIM, :], preferred_element_type=jnp.float32)
    y += jnp.dot(ml_ref[0].astype(bf), w_ref[NA_DIM + POOL_DIM:, :], preferred_element_type=jnp.float32)
    o_ref[0] = x_ref[0] + g_ref[0] * y


def _out_proj(x3, gate, na, pool, ml, w_out):
    B, T, d = x3.shape
    tn = min(IN_TOKEN_TILE, T)
    tok = lambda n: pl.BlockSpec((1, tn, n), lambda b, t: (b, t, 0))
    return pl.pallas_call(
        _out_proj_kernel,
        grid=(B, T // tn),
        in_specs=[tok(d), pl.BlockSpec((1, 1, d), lambda b, t: (b, 0, 0)), tok(NA_DIM), tok(POOL_DIM), tok(ML_DIM),
                  pl.BlockSpec(w_out.shape, lambda b, t: (0, 0))],
        out_specs=tok(d),
        out_shape=jax.ShapeDtypeStruct(x3.shape, x3.dtype),
        compiler_params=pltpu.CompilerParams(dimension_semantics=("parallel", "parallel")),
        name="out_proj",
    )(x3, gate, na, pool, ml, w_out.astype(jnp.bfloat16))


def kernel(x, c, ctx, c_ctx, w_ada, b_ada, norm1_g, w_in, ml_gate_b, na_q_g, na_k_g, na_rpb, pool_w,
           pool_scale, ml_conv, ml_norm_g, w_out, norm2_g, peer_wq, peer_keys, peer_u, peer_v):
    depth = w_in.shape[0]
    B, S, D = x.shape
    Lc = ctx.shape[1]
    hc = ctx
    for l in range(depth):
        need_ctx = l < depth - 1
        u_bf = peer_u[l].astype(jnp.bfloat16)
        vt_bf = peer_v[l].astype(jnp.bfloat16).T
        wqt_bf = peer_wq[l].astype(jnp.bfloat16).T
        keys_bf = peer_keys[l].astype(jnp.bfloat16).reshape(2 * PEER_HEADS, PEER_KEYS, PEER_DKEY // 2)
        mod = _ada_mod(jnp.concatenate([c, c_ctx[None, :]], axis=0), w_ada[l], b_ada[l]).reshape(B + 1, 6, 1, D)
        sh1, sc1, g1, sh2, sc2, g2 = (mod[:B, i] for i in range(6))
        csh1, csc1, cg1, csh2, csc2, cg2 = (mod[B:, i] for i in range(6))
        hc1 = hc.reshape(1, B * Lc, D)
        a = _in_proj(x, sc1, sh1, norm1_g[l], w_in[l], na_q_g[l], na_k_g[l])
        ac = _in_proj(hc1, csc1, csh1, norm1_g[l], w_in[l], na_q_g[l], na_k_g[l])
        ac = {k: v.reshape(B, Lc, v.shape[-1]) for k, v in ac.items() if k != "gr"} | {
            "gr": jnp.transpose(ac["gr"].reshape(ML_N_GATES, B, Lc), (1, 0, 2))}
        na_lat = _na_latent(a["q"], a["k"], a["v"], ac["k"], ac["v"], na_rpb[l])
        pool_lat = _pool_mix(a["pool"], pool_w[l], pool_scale[l])
        ml_lat, ml_ctx = _mlstm_mix(a["mq"], a["mk"], a["mv"], a["mo"], a["gc"], a["gr"],
                                    ac["mq"], ac["mk"], ac["mv"], ac["mo"], ac["gc"], ac["gr"],
                                    ml_conv[l], ml_gate_b[l], ml_norm_g[l])
        x = _out_proj(x, g1, na_lat, pool_lat, ml_lat, w_out[l])
        x = _peer_block(x, sc2, sh2, g2, norm2_g[l], wqt_bf, keys_bf, u_bf, vt_bf)
        if need_ctx:
            na_ctx = _na_context(ac["q"], ac["k"], ac["v"])
            pool_ctx = _pool_mix(ac["pool"], pool_w[l], pool_scale[l])
            flat = lambda t: t.reshape(1, B * Lc, t.shape[-1])
            hc1 = _out_proj(hc1, cg1, flat(na_ctx), flat(pool_ctx), flat(ml_ctx), w_out[l])
            hc = _peer_block(hc1, csc2, csh2, cg2, norm2_g[l], wqt_bf, keys_bf, u_bf, vt_bf).reshape(B, Lc, D)
    return x
```

```python
import functools

import numpy as np
import jax
import jax.numpy as jnp
from jax import lax
from jax.experimental import pallas as pl
from jax.experimental.pallas import tpu as pltpu

D_MODEL = 1024
GRID_W = 64
HEAD_DIM = 64
NA_DIM = D_MODEL // 2
NA_HEADS = NA_DIM // HEAD_DIM
NA_WIN_ROWS = 8
NA_WIN_COLS = 16
POOL_DIM = D_MODEL // 4
POOL_WINDOWS = (2, 4, 8, 16)
POOL_GROUPS = len(POOL_WINDOWS)
POOL_GROUP_DIM = POOL_DIM // POOL_GROUPS
ML_DIM = D_MODEL // 4
ML_HEADS = ML_DIM // HEAD_DIM
ML_CONV_W = 5
ML_N_GATES = 4 * ML_HEADS
PEER_KEYS = 128
PEER_EXPERTS = PEER_KEYS * PEER_KEYS
PEER_HEADS = 8
PEER_TOPK = 16
PEER_DKEY = 256
ROPE_BASE = 10000.0
EPS = 1e-6

PEER_TOKEN_TILE = 1024
PEER_EXPERT_TILE = 1024
ROUTE_TOKEN_TILE = 512
ROUTE_LANE_CHUNK = 256
IN_TOKEN_TILE = 256
NA_ROWS_PER_STEP = 4
ADA_COL_TILE = 1536
POOL_BLOCK = 128
POOL_HALO = 8
ML_L = 128
VMEM_LIMIT_BYTES = 56 * 1024 * 1024

_GELU_C0 = float(np.sqrt(2.0 / np.pi))
_GELU_C1 = 0.044715 * _GELU_C0


def _gelu_tanh(x):
    inner = x * (_GELU_C0 + _GELU_C1 * (x * x))
    hx = 0.5 * x
    return hx + hx * jnp.tanh(inner)


def _peer_dense_kernel(x_ref, u_ref, vt_ref, cnt_ref, e1_ref, rank_ref, e2_ref, o_ref):
    j = pl.program_id(1)

    @pl.when(j == 0)
    def _():
        o_ref[...] = jnp.zeros_like(o_ref)

    te = u_ref.shape[0]
    act = lax.dot_general(u_ref[...], x_ref[...], (((1,), (1,)), ((), ())),
                          preferred_element_type=jnp.float32)
    gates = []
    for k in range(te // PEER_KEYS):
        g = None
        for h in range(PEER_HEADS):
            cnt = cnt_ref[h, k:k + 1, :].astype(jnp.bfloat16)
            e1 = e1_ref[h, k:k + 1, :].astype(jnp.bfloat16)
            term = jnp.where(rank_ref[h] < cnt, e2_ref[h], jnp.zeros((), e2_ref.dtype)) * e1
            g = term if g is None else g + term
        gates.append(g)
    gate = jnp.concatenate(gates, axis=0)
    p = _gelu_tanh(act).astype(jnp.bfloat16) * gate
    o_ref[...] += jnp.dot(vt_ref[...], p, preferred_element_type=jnp.float32)


def _peer_dense(tok, u_bf, vt_bf, cnt, e1, rank, e2):
    n, d = tok.shape
    tn, te = PEER_TOKEN_TILE, PEER_EXPERT_TILE
    assert n % tn == 0 and PEER_EXPERTS % te == 0 and te % PEER_KEYS == 0
    tab_a = pl.BlockSpec((PEER_HEADS, te // PEER_KEYS, tn), lambda i, j: (0, j, i))
    tab_b = pl.BlockSpec((PEER_HEADS, PEER_KEYS, tn), lambda i, j: (0, 0, i))
    return pl.pallas_call(
        _peer_dense_kernel,
        grid=(n // tn, PEER_EXPERTS // te),
        in_specs=[
            pl.BlockSpec((tn, d), lambda i, j: (i, 0)),
            pl.BlockSpec((te, d), lambda i, j: (j, 0)),
            pl.BlockSpec((d, te), lambda i, j: (0, j)),
            tab_a, tab_a, tab_b, tab_b,
        ],
        out_specs=pl.BlockSpec((d, tn), lambda i, j: (0, i)),
        out_shape=jax.ShapeDtypeStruct((d, n), jnp.float32),
        compiler_params=pltpu.CompilerParams(
            dimension_semantics=("parallel", "arbitrary"), vmem_limit_bytes=VMEM_LIMIT_BYTES),
        name="peer_dense",
    )(tok, u_bf, vt_bf, cnt, e1, rank, e2)


def _top16(works, want_ranks):
    lanes = works[0].shape[1]
    rows = lax.broadcasted_iota(jnp.int32, (PEER_TOPK, lanes), 0)
    vals = [jnp.zeros((PEER_TOPK, lanes), jnp.float32) for _ in works]
    ranks = [jnp.full(w.shape, float(PEER_TOPK), jnp.float32) if r else None for w, r in zip(works, want_ranks)]
    tops = [[] for _ in works]
    works = list(works)
    for i in range(PEER_TOPK):
        ms = [jnp.max(w, axis=0, keepdims=True) for w in works]
        hits = [w == m for w, m in zip(works, ms)]
        ranks = [None if r is None else jnp.where(hit, float(i), r) for r, hit in zip(ranks, hits)]
        works = [jnp.where(hit, -jnp.inf, w) for w, hit in zip(works, hits)]
        vals = [jnp.where(rows == i, m, v) for v, m in zip(vals, ms)]
        for t, m in zip(tops, ms):
            t.append(m)
    return tops, vals, ranks


def _peer_route_kernel(x_ref, sc_ref, sh_ref, g_ref, wqt_ref, keys_ref,
                       tok_ref, cnt_ref, e1_ref, rank_ref, e2_ref, s_scr):
    x = x_ref[0]
    y = x * lax.rsqrt(jnp.mean(x * x, axis=-1, keepdims=True) + EPS) * g_ref[...]
    tok = (y * (1.0 + sc_ref[0]) + sh_ref[0]).astype(jnp.bfloat16)
    tok_ref[...] = tok
    qt = lax.dot_general(wqt_ref[...], tok, (((1,), (1,)), ((), ())),
                         preferred_element_type=jnp.float32).astype(jnp.bfloat16)
    dk = PEER_DKEY // 2
    for hp in range(2 * PEER_HEADS):
        s_scr[hp] = jnp.dot(keys_ref[hp], qt[hp * dk:(hp + 1) * dk], preferred_element_type=jnp.float32)

    cw = ROUTE_LANE_CHUNK
    lane_chunks = x.shape[0] // cw
    row8 = lax.broadcasted_iota(jnp.int32, (8, cw), 0)

    def body(it, carry):
        h = it // lane_chunks
        lanes = pl.ds(pl.multiple_of((it % lane_chunks) * cw, cw), cw)
        s1 = s_scr[2 * h, :, lanes]
        s2 = s_scr[2 * h + 1, :, lanes]
        (t1, t2), (v1, v2), (_, rank) = _top16([s1, s2], [False, True])
        groups = [t1[0] + v2, t1[1] + v2[:8]]
        for i in range(2, 8):
            groups.append(jnp.where(row8 < PEER_TOPK // (i + 1), t1[i] + v2[:8], -jnp.inf))
        groups.append(v1[8:] + t2[0])
        (tp,), (vp,), _ = _top16([jnp.concatenate(groups, axis=0)], [False])
        tau = tp[PEER_TOPK - 1]
        z = jnp.sum(jnp.exp(vp - tp[0]), axis=0, keepdims=True)
        cnt = jnp.zeros(s1.shape, jnp.float32)
        for jj in range(PEER_TOPK):
            cnt = jnp.where(s1 + t2[jj] >= tau, float(jj + 1), cnt)
        cnt = jnp.where(s1 >= t1[PEER_TOPK - 1], cnt, 0.0)
        cnt_ref[h, :, lanes] = cnt
        e1_ref[h, :, lanes] = jnp.exp(s1 - t1[0])
        rank_ref[h, :, lanes] = rank.astype(jnp.bfloat16)
        e2_ref[h, :, lanes] = (jnp.exp(s2 - t2[0]) * (1.0 / z)).astype(jnp.bfloat16)
        return carry

    lax.fori_loop(0, PEER_HEADS * lane_chunks, body, 0)


def _peer_route(x3, sc, sh, g, wqt_bf, keys_bf):
    B, T, d = x3.shape
    tn = min(ROUTE_TOKEN_TILE, T)
    assert T % tn == 0 and tn % 128 == 0
    nt = T // tn
    n = B * T
    tab = pl.BlockSpec((PEER_HEADS, PEER_KEYS, tn), lambda b, t: (0, 0, b * nt + t))
    tab_shape = lambda dt: jax.ShapeDtypeStruct((PEER_HEADS, PEER_KEYS, n), dt)
    mod = pl.BlockSpec((1, 1, d), lambda b, t: (b, 0, 0))
    return pl.pallas_call(
        _peer_route_kernel,
        grid=(B, nt),
        in_specs=[
            pl.BlockSpec((1, tn, d), lambda b, t: (b, t, 0)),
            mod, mod,
            pl.BlockSpec((1, d), lambda b, t: (0, 0)),
            pl.BlockSpec(wqt_bf.shape, lambda b, t: (0, 0)),
            pl.BlockSpec(keys_bf.shape, lambda b, t: (0, 0, 0)),
        ],
        out_specs=[pl.BlockSpec((tn, d), lambda b, t: (b * nt + t, 0)), tab, tab, tab, tab],
        out_shape=[jax.ShapeDtypeStruct((n, d), jnp.bfloat16), tab_shape(jnp.float32), tab_shape(jnp.float32),
                   tab_shape(jnp.bfloat16), tab_shape(jnp.bfloat16)],
        scratch_shapes=[pltpu.VMEM((2 * PEER_HEADS, PEER_KEYS, tn), jnp.float32)],
        compiler_params=pltpu.CompilerParams(
            dimension_semantics=("parallel", "parallel"), vmem_limit_bytes=VMEM_LIMIT_BYTES),
        name="peer_route",
    )(x3, sc, sh, g, wqt_bf, keys_bf)


def _residual_t_kernel(x_ref, g_ref, yt_ref, o_ref):
    o_ref[0] = x_ref[0] + g_ref[0] * yt_ref[...].T


def _residual_t(x3, gate, yt):
    B, T, d = x3.shape
    tn = min(ROUTE_TOKEN_TILE, T)
    nt = T // tn
    return pl.pallas_call(
        _residual_t_kernel,
        grid=(B, nt),
        in_specs=[pl.BlockSpec((1, tn, d), lambda b, t: (b, t, 0)),
                  pl.BlockSpec((1, 1, d), lambda b, t: (b, 0, 0)),
                  pl.BlockSpec((d, tn), lambda b, t: (0, b * nt + t))],
        out_specs=pl.BlockSpec((1, tn, d), lambda b, t: (b, t, 0)),
        out_shape=jax.ShapeDtypeStruct(x3.shape, x3.dtype),
        compiler_params=pltpu.CompilerParams(dimension_semantics=("parallel", "parallel")),
        name="residual_t",
    )(x3, gate, yt)


def _peer_block(x3, sc, sh, gate, norm_g, wqt_bf, keys_bf, u_bf, vt_bf):
    tok, cnt, e1, rank, e2 = _peer_route(x3, sc, sh, norm_g[None, :], wqt_bf, keys_bf)
    out_t = _peer_dense(tok, u_bf, vt_bf, cnt, e1, rank, e2)
    return _residual_t(x3, gate, out_t)


def _attend_pair(q2, segments, lane):
    zero = jnp.zeros((), q2.dtype)
    outs = []
    for half in range(2):
        own = (lane < HEAD_DIM) if half == 0 else (lane >= HEAD_DIM)
        qh = jnp.where(own, q2, zero)
        scores = []
        for k2, _, bias in segments:
            s = lax.dot_general(qh, k2, (((1,), (1,)), ((), ())), preferred_element_type=jnp.float32)
            scores.append(s if bias is None else s + bias[half])
        m = functools.reduce(jnp.maximum, [jnp.max(s, axis=-1, keepdims=True) for s in scores])
        ps = [jnp.exp(s - m) for s in scores]
        denom = functools.reduce(jnp.add, [jnp.sum(p, axis=-1, keepdims=True) for p in ps])
        acc = functools.reduce(jnp.add, [jnp.dot(p.astype(v2.dtype), v2, preferred_element_type=jnp.float32)
                                         for p, (_, v2, _) in zip(ps, segments)])
        outs.append(acc / denom)
    return jnp.where(lane < HEAD_DIM, outs[0], outs[1])


def _na_latent_kernel(q_ref, k_ref, v_ref, kc_ref, vc_ref, bias_ref, o_ref):
    n_rows = k_ref.shape[1] // GRID_W
    union = bias_ref.shape[3] // GRID_W
    u0 = _na_union_start(pl.program_id(1), n_rows, union)
    keys = pl.ds(pl.multiple_of(u0 * GRID_W, GRID_W), union * GRID_W)
    lane = lax.broadcasted_iota(jnp.int32, (q_ref.shape[1], 128), 1)
    for p in range(NA_DIM // 128):
        lanes = slice(p * 128, (p + 1) * 128)
        segs = [(k_ref[0, keys, lanes], v_ref[0, keys, lanes], (bias_ref[0, 2 * p], bias_ref[0, 2 * p + 1])),
                (kc_ref[0, :, lanes], vc_ref[0, :, lanes], None)]
        o_ref[0, :, lanes] = _attend_pair(q_ref[0, :, lanes], segs, lane)


def _na_union_start(g, n_rows, union):
    lo = g * NA_ROWS_PER_STEP - NA_WIN_ROWS // 2
    if isinstance(g, int):
        return min(min(max(lo, 0), n_rows - NA_WIN_ROWS), n_rows - union)
    return jnp.minimum(jnp.clip(lo, 0, n_rows - NA_WIN_ROWS), n_rows - union)


def _na_group_bias(rpb, n_rows):
    G, KR = NA_ROWS_PER_STEP, NA_WIN_ROWS
    union = KR + G - 1
    n_groups = n_rows // G
    cq = np.arange(GRID_W)
    c0 = np.clip(cq - NA_WIN_COLS // 2, 0, GRID_W - NA_WIN_COLS)
    in_win = (cq[None, :] >= c0[:, None]) & (cq[None, :] < c0[:, None] + NA_WIN_COLS)
    bc = np.clip(cq[None, :] - cq[:, None] + (NA_WIN_COLS - 1), 0, 2 * NA_WIN_COLS - 2)

    def layout(g):
        r = g * G + np.arange(G)
        key_row = _na_union_start(g, n_rows, union) + np.arange(union)
        r0 = np.clip(r - KR // 2, 0, n_rows - KR)
        ok = (key_row[None, :] >= r0[:, None]) & (key_row[None, :] < r0[:, None] + KR)
        br = np.clip(key_row[None, :] - r[:, None] + (KR - 1), 0, 2 * KR - 2)
        return ok, br

    layouts = [layout(g) for g in range(n_groups)]
    same = lambda a, b: all(np.array_equal(x, y) for x, y in zip(a, b))
    assert n_groups >= 3 and all(same(layouts[g], layouts[1]) for g in range(1, n_groups - 1))
    tables = []
    for ok, br in (layouts[0], layouts[1], layouts[-1]):
        vals = rpb[:, br[:, None, :, None], bc[None, :, None, :]]
        keep = ok[:, None, :, None] & in_win[None, :, None, :]
        tables.append(jnp.where(keep[None], vals, -jnp.inf).reshape(NA_HEADS, G * GRID_W, union * GRID_W))
    return jnp.stack(tables, axis=0).astype(jnp.float32)


def _na_latent(q, k, v, k_ctx, v_ctx, rpb):
    B, S, d = q.shape
    Lc = k_ctx.shape[1]
    n_rows = S // GRID_W
    assert n_rows >= NA_WIN_ROWS + NA_ROWS_PER_STEP - 1 and n_rows % NA_ROWS_PER_STEP == 0
    n_groups = n_rows // NA_ROWS_PER_STEP
    bias = _na_group_bias(rpb, n_rows)
    tq = NA_ROWS_PER_STEP * GRID_W
    whole = lambda L: pl.BlockSpec((1, L, d), lambda b, g: (b, 0, 0))
    kind = lambda b, g: (jnp.where(g == 0, 0, jnp.where(g == n_groups - 1, 2, 1)), 0, 0, 0)
    return pl.pallas_call(
        _na_latent_kernel,
        grid=(B, n_groups),
        in_specs=[pl.BlockSpec((1, tq, d), lambda b, g: (b, g, 0)), whole(S), whole(S), whole(Lc), whole(Lc),
                  pl.BlockSpec((1,) + bias.shape[1:], kind)],
        out_specs=pl.BlockSpec((1, tq, d), lambda b, g: (b, g, 0)),
        out_shape=jax.ShapeDtypeStruct((B, S, d), jnp.float32),
        compiler_params=pltpu.CompilerParams(
            dimension_semantics=("parallel", "arbitrary"), vmem_limit_bytes=VMEM_LIMIT_BYTES),
        name="na_latent",
    )(q, k, v, k_ctx, v_ctx, bias)


def _na_context_kernel(q_ref, k_ref, v_ref, o_ref):
    lane = lax.broadcasted_iota(jnp.int32, (q_ref.shape[1], 128), 1)
    for p in range(NA_DIM // 128):
        lanes = slice(p * 128, (p + 1) * 128)
        o_ref[0, :, lanes] = _attend_pair(q_ref[0, :, lanes], [(k_ref[0, :, lanes], v_ref[0, :, lanes], None)], lane)


def _na_context(q, k, v):
    B, Lc, d = q.shape
    blk = pl.BlockSpec((1, Lc, d), lambda b: (b, 0, 0))
    return pl.pallas_call(
        _na_context_kernel,
        grid=(B,),
        in_specs=[blk, blk, blk],
        out_specs=blk,
        out_shape=jax.ShapeDtypeStruct((B, Lc, d), jnp.float32),
        compiler_params=pltpu.CompilerParams(dimension_semantics=("parallel",)),
        name="na_context",
    )(q, k, v)


def _pool_matrices(T):
    tb, halo = POOL_BLOCK, POOL_HALO
    mats = np.zeros((len(POOL_WINDOWS), 3, tb, tb + 2 * halo), np.float32)
    for wi, w in enumerate(POOL_WINDOWS):
        for kind, start in enumerate((0, tb, T - tb)):
            for i in range(tb):
                t = start + i
                lo = min(max(t - w // 2, 0), T - 1)
                hi = min(max(t + (w - w // 2 - 1), 0), T - 1)
                mats[wi, kind, i, lo - start + halo:hi - start + halo + 1] = 1.0 / (hi - lo + 1)
                mats[wi, kind, i, i + halo] -= 1.0
    return jnp.asarray(mats)


def _pool_kernel(u_ref, a_ref, w_ref, s_ref, o_ref):
    T = u_ref.shape[1]
    tb, halo = POOL_BLOCK, POOL_HALO
    nb = T // tb
    group = lax.broadcasted_iota(jnp.int32, (tb, POOL_DIM), 1) // POOL_GROUP_DIM
    zeros = jnp.zeros((halo, POOL_DIM), jnp.float32)
    for blk in range(nb):
        kind = 0 if blk == 0 else (2 if blk == nb - 1 else 1)
        before = zeros if blk == 0 else u_ref[0, blk * tb - halo:blk * tb, :]
        after = zeros if blk == nb - 1 else u_ref[0, (blk + 1) * tb:(blk + 1) * tb + halo, :]
        xcat = jnp.concatenate([before, u_ref[0, blk * tb:(blk + 1) * tb, :], after], axis=0)
        pooled = None
        for wi in range(len(POOL_WINDOWS)):
            pw = jnp.dot(a_ref[wi, kind], xcat, preferred_element_type=jnp.float32, precision=lax.Precision.HIGHEST)
            pooled = pw if pooled is None else jnp.where(group == wi, pw, pooled)
        mixed = jnp.dot(pooled.astype(jnp.bfloat16), w_ref[...], preferred_element_type=jnp.float32)
        o_ref[0, blk * tb:(blk + 1) * tb, :] = mixed * s_ref[...]


def _pool_mix(u, pool_w, pool_scale):
    B, T, d = u.shape
    assert T % POOL_BLOCK == 0 and T >= 2 * POOL_BLOCK
    w_bd = jax.scipy.linalg.block_diag(*[pool_w[g] for g in range(POOL_GROUPS)]).astype(jnp.bfloat16)
    mats = _pool_matrices(T)
    blk = pl.BlockSpec((1, T, d), lambda b: (b, 0, 0))
    return pl.pallas_call(
        _pool_kernel,
        grid=(B,),
        in_specs=[blk, pl.BlockSpec(mats.shape, lambda b: (0, 0, 0, 0)), pl.BlockSpec((d, d), lambda b: (0, 0)),
                  pl.BlockSpec((1, d), lambda b: (0, 0))],
        out_specs=blk,
        out_shape=jax.ShapeDtypeStruct((B, T, d), jnp.float32),
        compiler_params=pltpu.CompilerParams(dimension_semantics=("parallel",)),
        name="pool_mix",
    )(u, mats, w_bd, pool_scale[None, :])


def _log_sigmoid(x):
    return jnp.minimum(x, 0.0) - jnp.log1p(jnp.exp(-jnp.abs(x)))


def _ml_prep(q, k, conv_ref, cos_ref, sin_ref):
    T = q.shape[0]
    row = lax.broadcasted_iota(jnp.int32, (T, 1), 0)
    lane = lax.broadcasted_iota(jnp.int32, (1, ML_DIM), 1) % (HEAD_DIM // 2)
    outs = []
    for idx, x in enumerate((q, k)):
        acc = None
        for j in range(ML_CONV_W):
            d = j - ML_CONV_W // 2
            w = conv_ref[j:j + 1, idx * ML_DIM:(idx + 1) * ML_DIM]
            if d == 0:
                term = x * w
            else:
                shifted = pltpu.roll(x, (-d) % T, 0)
                term = jnp.where((row + d >= 0) & (row + d < T), shifted, 0.0) * w
            acc = term if acc is None else acc + term
        y = acc * jax.nn.sigmoid(acc)
        if cos_ref is not None:
            quarter = HEAD_DIM // 4
            partner = jnp.where(lane < quarter, pltpu.roll(y, ML_DIM - quarter, 1), pltpu.roll(y, quarter, 1))
            y = y * cos_ref[...] + partner * sin_ref[...]
        outs.append(y)
    return outs


def _ml_chain(qs, kts, vs, rows, gi_c, gi_r, bc, br, ci, ii, head, backward, cn_ref, m_ref, slot, lane, tri):
    L = ML_L
    pair = slice((head // 2) * 128, (head // 2) * 128 + 128)
    own = (lane < HEAD_DIM) if head % 2 == 0 else (lane >= HEAD_DIM)
    one_lane = HEAD_DIM if head % 2 == 0 else 0
    q2 = jnp.where(own, qs[rows, pair], jnp.zeros((), jnp.bfloat16))
    v1 = jnp.where(own, vs[rows, pair].astype(jnp.float32), jnp.where(lane == one_lane, 1.0, 0.0))
    own_rows = lax.broadcasted_iota(jnp.int32, (128, L), 0)
    own_rows = (own_rows < HEAD_DIM) if head % 2 == 0 else (own_rows >= HEAD_DIM)
    kt2 = jnp.where(own_rows, kts[pair, rows], jnp.zeros((), jnp.bfloat16))
    m_prev = m_ref[slot]
    cn = cn_ref[slot]
    bcol = bc[:, ci:ci + 1]
    brow = br[ci:ci + 1, :]
    log_d = jnp.where(tri, bcol - brow + gi_r[ii:ii + 1, :], -jnp.inf)
    inter = bcol + m_prev
    m_t = jnp.maximum(inter, jnp.max(log_d, axis=-1, keepdims=True))
    s = jnp.dot(q2, kt2, preferred_element_type=jnp.float32)
    w_intra = (s * jnp.exp(log_d - m_t)).astype(jnp.bfloat16)
    nd = (jnp.dot(w_intra, v1.astype(jnp.bfloat16), preferred_element_type=jnp.float32)
          + jnp.exp(inter - m_t) * jnp.dot(q2, cn.astype(jnp.bfloat16), preferred_element_type=jnp.float32))
    den = nd[:, one_lane:one_lane + 1]
    h = nd / jnp.maximum(jnp.abs(den), jnp.exp(-m_t))
    b_last = bc[0:1, ci:ci + 1] if backward else bc[L - 1:L, ci:ci + 1]
    log_w = b_last - bcol + gi_c[:, ii:ii + 1]
    m_new = jnp.maximum(b_last + m_prev, jnp.max(log_w, axis=0, keepdims=True))
    wv = (jnp.exp(log_w - m_new) * v1).astype(jnp.bfloat16)
    cn_ref[slot] = jnp.exp(b_last + m_prev - m_new) * cn + jnp.dot(kt2, wv, preferred_element_type=jnp.float32)
    m_ref[slot] = m_new
    return h


def _ml_scan(T, qs, kts, vs, gc_ref, gr_ref, gbc, gbr, hf, hb, cn_ref, m_ref, tril, triu):
    L = ML_L
    nc = T // L
    lane = lax.broadcasted_iota(jnp.int32, (L, 128), 1)
    t_idx = lax.broadcasted_iota(jnp.int32, (L, L), 0)
    s_idx = lax.broadcasted_iota(jnp.int32, (L, L), 1)
    hi = lax.Precision.HIGHEST

    def body(c, carry):
        for backward in (False, True):
            cc = (nc - 1 - c) if backward else c
            rows = pl.ds(pl.multiple_of(cc * L, L), L)
            gc = gc_ref[0, rows, :] + gbc
            gr = gr_ref[0, :, rows] + gbr
            bc = jnp.dot(triu if backward else tril, _log_sigmoid(gc), preferred_element_type=jnp.float32, precision=hi)
            br = jnp.dot(_log_sigmoid(gr), tril if backward else triu, preferred_element_type=jnp.float32, precision=hi)
            tri = (s_idx >= t_idx) if backward else (s_idx <= t_idx)
            out = hb if backward else hf
            base = 2 * ML_HEADS if backward else 0
            for pair in range(ML_HEADS // 2):
                hs = []
                for head in (2 * pair, 2 * pair + 1):
                    hs.append(_ml_chain(qs, kts, vs, rows, gc, gr, bc, br, base + ML_HEADS + head, base + head, head,
                                        backward, cn_ref, m_ref, (ML_HEADS if backward else 0) + head, lane, tri))
                out[rows, pair * 128:(pair + 1) * 128] = jnp.where(lane < HEAD_DIM, hs[0], hs[1])
        return carry

    lax.fori_loop(0, nc, body, 0)


def _mlstm_kernel(q_ref, k_ref, v_ref, o_ref, gc_ref, gr_ref, cq_ref, ck_ref, cv_ref, co_ref, cgc_ref, cgr_ref,
                  cos_ref, sin_ref, conv_ref, gbc_ref, gbr_ref, ng_ref, e_ref, tril_ref, triu_ref,
                  y_ref, cy_ref, qs, kts, vs, hf, hb, cqs, ckts, cvs, chf, chb, cn_ref, m_ref):
    cn_ref[...] = jnp.zeros_like(cn_ref)
    m_ref[...] = jnp.zeros_like(m_ref)
    tril, triu = tril_ref[...], triu_ref[...]

    def stage(q_r, k_r, v_r, rope, qd, ktd, vd):
        qp, kp = _ml_prep(q_r[0], k_r[0], conv_ref, cos_ref if rope else None, sin_ref if rope else None)
        qd[...] = qp.astype(jnp.bfloat16)
        ktd[...] = (kp * HEAD_DIM ** -0.5).T.astype(jnp.bfloat16)
        vd[...] = v_r[0].astype(jnp.bfloat16)

    def finish(h_f, h_b, o_r, y_r):
        h = h_f[...] + h_b[...]
        ss = jnp.dot(h * h, e_ref[...], preferred_element_type=jnp.float32, precision=lax.Precision.HIGHEST)
        y_r[0] = h * lax.rsqrt(ss * (1.0 / HEAD_DIM) + EPS) * ng_ref[...] * jax.nn.sigmoid(o_r[0])

    gbc, gbr = gbc_ref[...], gbr_ref[...]
    stage(cq_ref, ck_ref, cv_ref, False, cqs, ckts, cvs)
    _ml_scan(cq_ref.shape[1], cqs, ckts, cvs, cgc_ref, cgr_ref, gbc, gbr, chf, chb, cn_ref, m_ref, tril, triu)
    finish(chf, chb, co_ref, cy_ref)
    stage(q_ref, k_ref, v_ref, True, qs, kts, vs)
    _ml_scan(q_ref.shape[1], qs, kts, vs, gc_ref, gr_ref, gbc, gbr, hf, hb, cn_ref, m_ref, tril, triu)
    finish(hf, hb, o_ref, y_ref)


def _rope_tables(T):
    t = jnp.arange(T)
    row = (t // GRID_W).astype(jnp.float32)
    col = (t % GRID_W).astype(jnp.float32)
    nf = HEAD_DIM // 4
    inv = ROPE_BASE ** (-jnp.arange(nf, dtype=jnp.float32) / nf)
    cr, sr, cc, sc = jnp.cos(row[:, None] * inv), jnp.sin(row[:, None] * inv), jnp.cos(col[:, None] * inv), \
        jnp.sin(col[:, None] * inv)
    cos = jnp.concatenate([cr, cr, cc, cc], axis=-1)
    sin = jnp.concatenate([-sr, sr, -sc, sc], axis=-1)
    return jnp.tile(cos, (1, ML_HEADS)), jnp.tile(sin, (1, ML_HEADS))


def _mlstm_mix(lq, lk, lv, lo, lg, lgr, cq, ck, cv, co, cg, cgr, conv_w, gate_b, norm_g):
    B, T, d = lq.shape
    Lc = cq.shape[1]
    L = ML_L
    assert T % L == 0 and Lc % L == 0
    cos, sin = _rope_tables(T)
    lane_head = np.arange(d) // HEAD_DIM
    same_head = jnp.asarray((lane_head[:, None] == lane_head[None, :]).astype(np.float32))
    tril = jnp.asarray(np.tril(np.ones((L, L), np.float32)))
    seq = lambda n: pl.BlockSpec((1, n, d), lambda b: (b, 0, 0))
    full = lambda a: pl.BlockSpec(a.shape, lambda b: (0,) * a.ndim)
    consts = [cos, sin, conv_w, gate_b[None, :], gate_b[:, None], norm_g[None, :], same_head, tril, tril.T]
    f32, bf16 = jnp.float32, jnp.bfloat16
    return pl.pallas_call(
        _mlstm_kernel,
        grid=(B,),
        in_specs=[seq(T)] * 4 + [pl.BlockSpec((1, T, ML_N_GATES), lambda b: (b, 0, 0)),
                                 pl.BlockSpec((1, ML_N_GATES, T), lambda b: (b, 0, 0))]
        + [seq(Lc)] * 4 + [pl.BlockSpec((1, Lc, ML_N_GATES), lambda b: (b, 0, 0)),
                           pl.BlockSpec((1, ML_N_GATES, Lc), lambda b: (b, 0, 0))]
        + [full(a) for a in consts],
        out_specs=[seq(T), seq(Lc)],
        out_shape=[jax.ShapeDtypeStruct((B, T, d), f32), jax.ShapeDtypeStruct((B, Lc, d), f32)],
        scratch_shapes=[pltpu.VMEM((T, d), bf16), pltpu.VMEM((d, T), bf16), pltpu.VMEM((T, d), bf16),
                        pltpu.VMEM((T, d), f32), pltpu.VMEM((T, d), f32),
                        pltpu.VMEM((Lc, d), bf16), pltpu.VMEM((d, Lc), bf16), pltpu.VMEM((Lc, d), bf16),
                        pltpu.VMEM((Lc, d), f32), pltpu.VMEM((Lc, d), f32),
                        pltpu.VMEM((2 * ML_HEADS, 128, 128), f32), pltpu.VMEM((2 * ML_HEADS, 1, 1), f32)],
        compiler_params=pltpu.CompilerParams(dimension_semantics=("parallel",), vmem_limit_bytes=VMEM_LIMIT_BYTES),
        name="mlstm",
    )(lq, lk, lv, lo, lg, lgr, cq, ck, cv, co, cg, cgr, *consts)


def _ada_kernel(c_ref, w_ref, b_ref, o_ref):
    c = c_ref[...]
    act = (c * jax.nn.sigmoid(c)).astype(jnp.bfloat16)
    o_ref[...] = jnp.dot(act, w_ref[...].astype(jnp.bfloat16), preferred_element_type=jnp.float32) + b_ref[...]


def _ada_mod(cond, w, b):
    rows, d = cond.shape
    n = w.shape[1]
    tile = ADA_COL_TILE
    assert n % tile == 0
    return pl.pallas_call(
        _ada_kernel,
        grid=(n // tile,),
        in_specs=[pl.BlockSpec((rows, d), lambda j: (0, 0)), pl.BlockSpec((d, tile), lambda j: (0, j)),
                  pl.BlockSpec((1, tile), lambda j: (0, j))],
        out_specs=pl.BlockSpec((rows, tile), lambda j: (0, j)),
        out_shape=jax.ShapeDtypeStruct((rows, n), jnp.float32),
        compiler_params=pltpu.CompilerParams(dimension_semantics=("parallel",)),
        name="ada_mod",
    )(cond, w, b[None, :])


_IN_SPLITS = (("q", NA_DIM), ("k", NA_DIM), ("v", NA_DIM), ("pool", POOL_DIM), ("mq", ML_DIM), ("mk", ML_DIM),
              ("mv", ML_DIM), ("mo", ML_DIM), ("gates", ML_N_GATES))
IN_COLS = sum(n for _, n in _IN_SPLITS)
IN_COLS_PAD = -(-IN_COLS // 128) * 128


def _in_proj_kernel(x_ref, sc_ref, sh_ref, g_ref, w_ref, wgt_ref, gq_ref, gk_ref, e_ref,
                    q_ref, k_ref, v_ref, pool_ref, mq_ref, mk_ref, mv_ref, mo_ref, gc_ref, gr_ref):
    x = x_ref[0]
    y = x * lax.rsqrt(jnp.mean(x * x, axis=-1, keepdims=True) + EPS) * g_ref[...]
    xn = (y * (1.0 + sc_ref[0]) + sh_ref[0]).astype(jnp.bfloat16)
    p = jnp.dot(xn, w_ref[...], preferred_element_type=jnp.float32)
    off = {}
    o = 0
    for name, n in _IN_SPLITS:
        off[name] = slice(o, o + n)
        o += n

    def head_norm(u, gain):
        sq = u * u
        hi = sq.astype(jnp.bfloat16)
        lo = (sq - hi.astype(jnp.float32)).astype(jnp.bfloat16)
        ss = (jnp.dot(hi, e_ref[...], preferred_element_type=jnp.float32)
              + jnp.dot(lo, e_ref[...], preferred_element_type=jnp.float32))
        return u * lax.rsqrt(ss * (1.0 / HEAD_DIM) + EPS) * gain

    q_ref[0] = (head_norm(p[:, off["q"]], gq_ref[...]) * HEAD_DIM ** -0.5).astype(jnp.bfloat16)
    k_ref[0] = head_norm(p[:, off["k"]], gk_ref[...]).astype(jnp.bfloat16)
    v_ref[0] = p[:, off["v"]].astype(jnp.bfloat16)
    pool_ref[0] = p[:, off["pool"]]
    mq_ref[0] = p[:, off["mq"]]
    mk_ref[0] = p[:, off["mk"]]
    mv_ref[0] = p[:, off["mv"]]
    mo_ref[0] = p[:, off["mo"]]
    gc_ref[0] = p[:, off["gates"]]
    gr_ref[0] = lax.dot_general(wgt_ref[...], xn, (((1,), (1,)), ((), ())), preferred_element_type=jnp.float32)


def _in_proj(x3, sc, sh, norm_g, w_in, gq, gk):
    B, T, d = x3.shape
    tn = min(IN_TOKEN_TILE, T)
    assert T % tn == 0
    w_bf = jnp.pad(w_in, ((0, 0), (0, IN_COLS_PAD - IN_COLS))).astype(jnp.bfloat16)
    wgt_bf = w_in[:, IN_COLS - ML_N_GATES:].T.astype(jnp.bfloat16)
    lane_head = np.arange(NA_DIM) // HEAD_DIM
    same_head = jnp.asarray((lane_head[:, None] == lane_head[None, :]).astype(np.float32)).astype(jnp.bfloat16)
    mod = pl.BlockSpec((1, 1, d), lambda b, t: (b, 0, 0))
    full = lambda a: pl.BlockSpec(a.shape, lambda b, t: (0,) * a.ndim)
    tok = lambda n: pl.BlockSpec((1, tn, n), lambda b, t: (b, t, 0))
    shp = lambda n, dt: jax.ShapeDtypeStruct((B, T, n), dt)
    f32, bf16 = jnp.float32, jnp.bfloat16
    consts = [norm_g[None, :], w_bf, wgt_bf, jnp.tile(gq, NA_HEADS)[None, :], jnp.tile(gk, NA_HEADS)[None, :], same_head]
    outs = pl.pallas_call(
        _in_proj_kernel,
        grid=(B, T // tn),
        in_specs=[pl.BlockSpec((1, tn, d), lambda b, t: (b, t, 0)), mod, mod] + [full(a) for a in consts],
        out_specs=[tok(NA_DIM)] * 3 + [tok(POOL_DIM)] + [tok(ML_DIM)] * 4 + [tok(ML_N_GATES)]
        + [pl.BlockSpec((1, ML_N_GATES, tn), lambda b, t: (b, 0, t))],
        out_shape=[shp(NA_DIM, bf16)] * 3 + [shp(POOL_DIM, f32)] + [shp(ML_DIM, f32)] * 4 + [shp(ML_N_GATES, f32)]
        + [jax.ShapeDtypeStruct((B, ML_N_GATES, T), f32)],
        compiler_params=pltpu.CompilerParams(
            dimension_semantics=("parallel", "parallel"), vmem_limit_bytes=VMEM_LIMIT_BYTES),
        name="in_proj",
    )(x3, sc, sh, *consts)
    return dict(zip(("q", "k", "v", "pool", "mq", "mk", "mv", "mo", "gc", "gr"), outs))


def _out_proj_kernel(x_ref, g_ref, na_ref, pool_ref, ml_ref, w_ref, o_ref):
    bf = jnp.bfloat16
    y = jnp.dot(na_ref[0].astype(bf), w_ref[0:NA_DIM, :], preferred_element_type=jnp.float32)
    y += jnp.dot(pool_ref[0].astype(bf), w_ref[NA_DIM:NA_DIM + POOL_DIM, :], preferred_element_type=jnp.float32)
    y += jnp.dot(ml_ref[0].astype(bf), w_ref[NA_DIM + POOL_DIM:, :], preferred_element_type=jnp.float32)
    o_ref[0] = x_ref[0] + g_ref[0] * y


def _out_proj(x3, gate, na, pool, ml, w_out):
    B, T, d = x3.shape
    tn = min(IN_TOKEN_TILE, T)
    tok = lambda n: pl.BlockSpec((1, tn, n), lambda b, t: (b, t, 0))
    return pl.pallas_call(
        _out_proj_kernel,
        grid=(B, T // tn),
        in_specs=[tok(d), pl.BlockSpec((1, 1, d), lambda b, t: (b, 0, 0)), tok(NA_DIM), tok(POOL_DIM), tok(ML_DIM),
                  pl.BlockSpec(w_out.shape, lambda b, t: (0, 0))],
        out_specs=tok(d),
        out_shape=jax.ShapeDtypeStruct(x3.shape, x3.dtype),
        compiler_params=pltpu.CompilerParams(dimension_semantics=("parallel", "parallel")),
        name="out_proj",
    )(x3, gate, na, pool, ml, w_out.astype(jnp.bfloat16))


def kernel(x, c, ctx, c_ctx, w_ada, b_ada, norm1_g, w_in, ml_gate_b, na_q_g, na_k_g, na_rpb, pool_w,
           pool_scale, ml_conv, ml_norm_g, w_out, norm2_g, peer_wq, peer_keys, peer_u, peer_v):
    depth = w_in.shape[0]
    B, S, D = x.shape
    Lc = ctx.shape[1]
    hc = ctx
    for l in range(depth):
        need_ctx = l < depth - 1
        u_bf = peer_u[l].astype(jnp.bfloat16)
        vt_bf = peer_v[l].astype(jnp.bfloat16).T
        wqt_bf = peer_wq[l].astype(jnp.bfloat16).T
        keys_bf = peer_keys[l].astype(jnp.bfloat16).reshape(2 * PEER_HEADS, PEER_KEYS, PEER_DKEY // 2)
        mod = _ada_mod(jnp.concatenate([c, c_ctx[None, :]], axis=0), w_ada[l], b_ada[l]).reshape(B + 1, 6, 1, D)
        sh1, sc1, g1, sh2, sc2, g2 = (mod[:B, i] for i in range(6))
        csh1, csc1, cg1, csh2, csc2, cg2 = (mod[B:, i] for i in range(6))
        hc1 = hc.reshape(1, B * Lc, D)
        a = _in_proj(x, sc1, sh1, norm1_g[l], w_in[l], na_q_g[l], na_k_g[l])
        ac = _in_proj(hc1, csc1, csh1, norm1_g[l], w_in[l], na_q_g[l], na_k_g[l])
        ac = {k: v.reshape(B, Lc, v.shape[-1]) for k, v in ac.items() if k != "gr"} | {
            "gr": jnp.transpose(ac["gr"].reshape(ML_N_GATES, B, Lc), (1, 0, 2))}
        na_lat = _na_latent(a["q"], a["k"], a["v"], ac["k"], ac["v"], na_rpb[l])
        pool_lat = _pool_mix(a["pool"], pool_w[l], pool_scale[l])
        ml_lat, ml_ctx = _mlstm_mix(a["mq"], a["mk"], a["mv"], a["mo"], a["gc"], a["gr"],
                                    ac["mq"], ac["mk"], ac["mv"], ac["mo"], ac["gc"], ac["gr"],
                                    ml_conv[l], ml_gate_b[l], ml_norm_g[l])
        x = _out_proj(x, g1, na_lat, pool_lat, ml_lat, w_out[l])
        x = _peer_block(x, sc2, sh2, g2, norm2_g[l], wqt_bf, keys_bf, u_bf, vt_bf)
        if need_ctx:
            na_ctx = _na_context(ac["q"], ac["k"], ac["v"])
            pool_ctx = _pool_mix(ac["pool"], pool_w[l], pool_scale[l])
            flat = lambda t: t.reshape(1, B * Lc, t.shape[-1])
            hc1 = _out_proj(hc1, cg1, flat(na_ctx), flat(pool_ctx), flat(ml_ctx), w_out[l])
            hc = _peer_block(hc1, csc2, csh2, cg2, norm2_g[l], wqt_bf, keys_bf, u_bf, vt_bf).reshape(B, Lc, D)
    return x
```

```python
import functools

import numpy as np
import jax
import jax.numpy as jnp
from jax import lax
from jax.experimental import pallas as pl
from jax.experimental.pallas import tpu as pltpu

D_MODEL = 1024
GRID_W = 64
HEAD_DIM = 64
NA_DIM = D_MODEL // 2
NA_HEADS = NA_DIM // HEAD_DIM
NA_WIN_ROWS = 8
NA_WIN_COLS = 16
POOL_DIM = D_MODEL // 4
POOL_WINDOWS = (2, 4, 8, 16)
POOL_GROUPS = len(POOL_WINDOWS)
POOL_GROUP_DIM = POOL_DIM // POOL_GROUPS
ML_DIM = D_MODEL // 4
ML_HEADS = ML_DIM // HEAD_DIM
ML_CONV_W = 5
ML_N_GATES = 4 * ML_HEADS
PEER_KEYS = 128
PEER_EXPERTS = PEER_KEYS * PEER_KEYS
PEER_HEADS = 8
PEER_TOPK = 16
PEER_DKEY = 256
ROPE_BASE = 10000.0
EPS = 1e-6

PEER_TOKEN_TILE = 1024
PEER_EXPERT_TILE = 1024
ROUTE_TOKEN_TILE = 512
ROUTE_LANE_CHUNK = 256
IN_TOKEN_TILE = 256
NA_ROWS_PER_STEP = 4
ADA_COL_TILE = 1536
POOL_BLOCK = 128
POOL_HALO = 8
ML_L = 128
VMEM_LIMIT_BYTES = 56 * 1024 * 1024

_GELU_C0 = float(np.sqrt(2.0 / np.pi))
_GELU_C1 = 0.044715 * _GELU_C0


def _gelu_tanh(x):
    inner = x * (_GELU_C0 + _GELU_C1 * (x * x))
    hx = 0.5 * x
    return hx + hx * jnp.tanh(inner)


def _peer_dense_kernel(x_ref, u_ref, vt_ref, cnt_ref, e1_ref, rank_ref, e2_ref, o_ref):
    j = pl.program_id(1)

    @pl.when(j == 0)
    def _():
        o_ref[...] = jnp.zeros_like(o_ref)

    te = u_ref.shape[0]
    act = lax.dot_general(u_ref[...], x_ref[...], (((1,), (1,)), ((), ())),
                          preferred_element_type=jnp.float32)
    gates = []
    for k in range(te // PEER_KEYS):
        g = None
        for h in range(PEER_HEADS):
            cnt = cnt_ref[h, k:k + 1, :].astype(jnp.bfloat16)
            e1 = e1_ref[h, k:k + 1, :].astype(jnp.bfloat16)
            term = jnp.where(rank_ref[h] < cnt, e2_ref[h], jnp.zeros((), e2_ref.dtype)) * e1
            g = term if g is None else g + term
        gates.append(g)
    gate = jnp.concatenate(gates, axis=0)
    p = _gelu_tanh(act).astype(jnp.bfloat16) * gate
    o_ref[...] += jnp.dot(vt_ref[...], p, preferred_element_type=jnp.float32)


def _peer_dense(tok, u_bf, vt_bf, cnt, e1, rank, e2):
    n, d = tok.shape
    tn, te = PEER_TOKEN_TILE, PEER_EXPERT_TILE
    assert n % tn == 0 and PEER_EXPERTS % te == 0 and te % PEER_KEYS == 0
    tab_a = pl.BlockSpec((PEER_HEADS, te // PEER_KEYS, tn), lambda i, j: (0, j, i))
    tab_b = pl.BlockSpec((PEER_HEADS, PEER_KEYS, tn), lambda i, j: (0, 0, i))
    return pl.pallas_call(
        _peer_dense_kernel,
        grid=(n // tn, PEER_EXPERTS // te),
        in_specs=[
            pl.BlockSpec((tn, d), lambda i, j: (i, 0)),
            pl.BlockSpec((te, d), lambda i, j: (j, 0)),
            pl.BlockSpec((d, te), lambda i, j: (0, j)),
            tab_a, tab_a, tab_b, tab_b,
        ],
        out_specs=pl.BlockSpec((d, tn), lambda i, j: (0, i)),
        out_shape=jax.ShapeDtypeStruct((d, n), jnp.float32),
        compiler_params=pltpu.CompilerParams(
            dimension_semantics=("parallel", "arbitrary"), vmem_limit_bytes=VMEM_LIMIT_BYTES),
        name="peer_dense",
    )(tok, u_bf, vt_bf, cnt, e1, rank, e2)


def _top16(works, want_ranks):
    lanes = works[0].shape[1]
    rows = lax.broadcasted_iota(jnp.int32, (PEER_TOPK, lanes), 0)
    vals = [jnp.zeros((PEER_TOPK, lanes), jnp.float32) for _ in works]
    ranks = [jnp.full(w.shape, float(PEER_TOPK), jnp.float32) if r else None for w, r in zip(works, want_ranks)]
    tops = [[] for _ in works]
    works = list(works)
    for i in range(PEER_TOPK):
        ms = [jnp.max(w, axis=0, keepdims=True) for w in works]
        hits = [w == m for w, m in zip(works, ms)]
        ranks = [None if r is None else jnp.where(hit, float(i), r) for r, hit in zip(ranks, hits)]
        works = [jnp.where(hit, -jnp.inf, w) for w, hit in zip(works, hits)]
        vals = [jnp.where(rows == i, m, v) for v, m in zip(vals, ms)]
        for t, m in zip(tops, ms):
            t.append(m)
    return tops, vals, ranks


def _peer_route_kernel(x_ref, sc_ref, sh_ref, g_ref, wqt_ref, keys_ref,
                       tok_ref, cnt_ref, e1_ref, rank_ref, e2_ref, s_scr):
    x = x_ref[0]
    y = x * lax.rsqrt(jnp.mean(x * x, axis=-1, keepdims=True) + EPS) * g_ref[...]
    tok = (y * (1.0 + sc_ref[0]) + sh_ref[0]).astype(jnp.bfloat16)
    tok_ref[...] = tok
    qt = lax.dot_general(wqt_ref[...], tok, (((1,), (1,)), ((), ())),
                         preferred_element_type=jnp.float32).astype(jnp.bfloat16)
    dk = PEER_DKEY // 2
    for hp in range(2 * PEER_HEADS):
        s_scr[hp] = jnp.dot(keys_ref[hp], qt[hp * dk:(hp + 1) * dk], preferred_element_type=jnp.float32)

    cw = ROUTE_LANE_CHUNK
    lane_chunks = x.shape[0] // cw
    row8 = lax.broadcasted_iota(jnp.int32, (8, cw), 0)

    def body(it, carry):
        h = it // lane_chunks
        lanes = pl.ds(pl.multiple_of((it % lane_chunks) * cw, cw), cw)
        s1 = s_scr[2 * h, :, lanes]
        s2 = s_scr[2 * h + 1, :, lanes]
        (t1, t2), (v1, v2), (_, rank) = _top16([s1, s2], [False, True])
        groups = [t1[0] + v2, t1[1] + v2[:8]]
        for i in range(2, 8):
            groups.append(jnp.where(row8 < PEER_TOPK // (i + 1), t1[i] + v2[:8], -jnp.inf))
        groups.append(v1[8:] + t2[0])
        (tp,), (vp,), _ = _top16([jnp.concatenate(groups, axis=0)], [False])
        tau = tp[PEER_TOPK - 1]
        z = jnp.sum(jnp.exp(vp - tp[0]), axis=0, keepdims=True)
        cnt = jnp.zeros(s1.shape, jnp.float32)
        for jj in range(PEER_TOPK):
            cnt = jnp.where(s1 + t2[jj] >= tau, float(jj + 1), cnt)
        cnt = jnp.where(s1 >= t1[PEER_TOPK - 1], cnt, 0.0)
        cnt_ref[h, :, lanes] = cnt
        e1_ref[h, :, lanes] = jnp.exp(s1 - t1[0])
        rank_ref[h, :, lanes] = rank.astype(jnp.bfloat16)
        e2_ref[h, :, lanes] = (jnp.exp(s2 - t2[0]) * (1.0 / z)).astype(jnp.bfloat16)
        return carry

    lax.fori_loop(0, PEER_HEADS * lane_chunks, body, 0)


def _peer_route(x3, sc, sh, g, wqt_bf, keys_bf):
    B, T, d = x3.shape
    tn = min(ROUTE_TOKEN_TILE, T)
    assert T % tn == 0 and tn % 128 == 0
    nt = T // tn
    n = B * T
    tab = pl.BlockSpec((PEER_HEADS, PEER_KEYS, tn), lambda b, t: (0, 0, b * nt + t))
    tab_shape = lambda dt: jax.ShapeDtypeStruct((PEER_HEADS, PEER_KEYS, n), dt)
    mod = pl.BlockSpec((1, 1, d), lambda b, t: (b, 0, 0))
    return pl.pallas_call(
        _peer_route_kernel,
        grid=(B, nt),
        in_specs=[
            pl.BlockSpec((1, tn, d), lambda b, t: (b, t, 0)),
            mod, mod,
            pl.BlockSpec((1, d), lambda b, t: (0, 0)),
            pl.BlockSpec(wqt_bf.shape, lambda b, t: (0, 0)),
            pl.BlockSpec(keys_bf.shape, lambda b, t: (0, 0, 0)),
        ],
        out_specs=[pl.BlockSpec((tn, d), lambda b, t: (b * nt + t, 0)), tab, tab, tab, tab],
        out_shape=[jax.ShapeDtypeStruct((n, d), jnp.bfloat16), tab_shape(jnp.float32), tab_shape(jnp.float32),
                   tab_shape(jnp.bfloat16), tab_shape(jnp.bfloat16)],
        scratch_shapes=[pltpu.VMEM((2 * PEER_HEADS, PEER_KEYS, tn), jnp.float32)],
        compiler_params=pltpu.CompilerParams(
            dimension_semantics=("parallel", "parallel"), vmem_limit_bytes=VMEM_LIMIT_BYTES),
        name="peer_route",
    )(x3, sc, sh, g, wqt_bf, keys_bf)


def _residual_t_kernel(x_ref, g_ref, yt_ref, o_ref):
    o_ref[0] = x_ref[0] + g_ref[0] * yt_ref[...].T


def _residual_t(x3, gate, yt):
    B, T, d = x3.shape
    tn = min(ROUTE_TOKEN_TILE, T)
    nt = T // tn
    return pl.pallas_call(
        _residual_t_kernel,
        grid=(B, nt),
        in_specs=[pl.BlockSpec((1, tn, d), lambda b, t: (b, t, 0)),
                  pl.BlockSpec((1, 1, d), lambda b, t: (b, 0, 0)),
                  pl.BlockSpec((d, tn), lambda b, t: (0, b * nt + t))],
        out_specs=pl.BlockSpec((1, tn, d), lambda b, t: (b, t, 0)),
        out_shape=jax.ShapeDtypeStruct(x3.shape, x3.dtype),
        compiler_params=pltpu.CompilerParams(dimension_semantics=("parallel", "parallel")),
        name="residual_t",
    )(x3, gate, yt)


def _peer_block(x3, sc, sh, gate, norm_g, wqt_bf, keys_bf, u_bf, vt_bf):
    tok, cnt, e1, rank, e2 = _peer_route(x3, sc, sh, norm_g[None, :], wqt_bf, keys_bf)
    out_t = _peer_dense(tok, u_bf, vt_bf, cnt, e1, rank, e2)
    return _residual_t(x3, gate, out_t)


def _attend_pair(q2, segments, lane):
    zero = jnp.zeros((), q2.dtype)
    outs = []
    for half in range(2):
        own = (lane < HEAD_DIM) if half == 0 else (lane >= HEAD_DIM)
        qh = jnp.where(own, q2, zero)
        scores = []
        for k2, _, bias in segments:
            s = lax.dot_general(qh, k2, (((1,), (1,)), ((), ())), preferred_element_type=jnp.float32)
            scores.append(s if bias is None else s + bias[half])
        m = functools.reduce(jnp.maximum, [jnp.max(s, axis=-1, keepdims=True) for s in scores])
        ps = [jnp.exp(s - m) for s in scores]
        denom = functools.reduce(jnp.add, [jnp.sum(p, axis=-1, keepdims=True) for p in ps])
        acc = functools.reduce(jnp.add, [jnp.dot(p.astype(v2.dtype), v2, preferred_element_type=jnp.float32)
                                         for p, (_, v2, _) in zip(ps, segments)])
        outs.append(acc / denom)
    return jnp.where(lane < HEAD_DIM, outs[0], outs[1])


def _na_latent_kernel(q_ref, k_ref, v_ref, kc_ref, vc_ref, bias_ref, o_ref):
    n_rows = k_ref.shape[1] // GRID_W
    union = bias_ref.shape[3] // GRID_W
    u0 = _na_union_start(pl.program_id(1), n_rows, union)
    keys = pl.ds(pl.multiple_of(u0 * GRID_W, GRID_W), union * GRID_W)
    lane = lax.broadcasted_iota(jnp.int32, (q_ref.shape[1], 128), 1)
    for p in range(NA_DIM // 128):
        lanes = slice(p * 128, (p + 1) * 128)
        segs = [(k_ref[0, keys, lanes], v_ref[0, keys, lanes], (bias_ref[0, 2 * p], bias_ref[0, 2 * p + 1])),
                (kc_ref[0, :, lanes], vc_ref[0, :, lanes], None)]
        o_ref[0, :, lanes] = _attend_pair(q_ref[0, :, lanes], segs, lane)


def _na_union_start(g, n_rows, union):
    lo = g * NA_ROWS_PER_STEP - NA_WIN_ROWS // 2
    if isinstance(g, int):
        return min(min(max(lo, 0), n_rows - NA_WIN_ROWS), n_rows - union)
    return jnp.minimum(jnp.clip(lo, 0, n_rows - NA_WIN_ROWS), n_rows - union)


def _na_group_bias(rpb, n_rows):
    G, KR = NA_ROWS_PER_STEP, NA_WIN_ROWS
    union = KR + G - 1
    n_groups = n_rows // G
    cq = np.arange(GRID_W)
    c0 = np.clip(cq - NA_WIN_COLS // 2, 0, GRID_W - NA_WIN_COLS)
    in_win = (cq[None, :] >= c0[:, None]) & (cq[None, :] < c0[:, None] + NA_WIN_COLS)
    bc = np.clip(cq[None, :] - cq[:, None] + (NA_WIN_COLS - 1), 0, 2 * NA_WIN_COLS - 2)

    def layout(g):
        r = g * G + np.arange(G)
        key_row = _na_union_start(g, n_rows, union) + np.arange(union)
        r0 = np.clip(r - KR // 2, 0, n_rows - KR)
        ok = (key_row[None, :] >= r0[:, None]) & (key_row[None, :] < r0[:, None] + KR)
        br = np.clip(key_row[None, :] - r[:, None] + (KR - 1), 0, 2 * KR - 2)
        return ok, br

    layouts = [layout(g) for g in range(n_groups)]
    same = lambda a, b: all(np.array_equal(x, y) for x, y in zip(a, b))
    assert n_groups >= 3 and all(same(layouts[g], layouts[1]) for g in range(1, n_groups - 1))
    toe = jnp.where(in_win[None, None], rpb[:, :, bc], -jnp.inf).astype(jnp.float32)
    tables = []
    for ok, br in (layouts[0], layouts[1], layouts[-1]):
        blocks = jnp.where(ok[None, :, :, None, None], toe[:, br], -jnp.inf)
        tables.append(jnp.transpose(blocks, (0, 1, 3, 2, 4)).reshape(NA_HEADS, G * GRID_W, union * GRID_W))
    return jnp.stack(tables, axis=0)


def _na_latent(q, k, v, k_ctx, v_ctx, rpb):
    B, S, d = q.shape
    Lc = k_ctx.shape[1]
    n_rows = S // GRID_W
    assert n_rows >= NA_WIN_ROWS + NA_ROWS_PER_STEP - 1 and n_rows % NA_ROWS_PER_STEP == 0
    n_groups = n_rows // NA_ROWS_PER_STEP
    bias = _na_group_bias(rpb, n_rows)
    tq = NA_ROWS_PER_STEP * GRID_W
    whole = lambda L: pl.BlockSpec((1, L, d), lambda b, g: (b, 0, 0))
    kind = lambda b, g: (jnp.where(g == 0, 0, jnp.where(g == n_groups - 1, 2, 1)), 0, 0, 0)
    return pl.pallas_call(
        _na_latent_kernel,
        grid=(B, n_groups),
        in_specs=[pl.BlockSpec((1, tq, d), lambda b, g: (b, g, 0)), whole(S), whole(S), whole(Lc), whole(Lc),
                  pl.BlockSpec((1,) + bias.shape[1:], kind)],
        out_specs=pl.BlockSpec((1, tq, d), lambda b, g: (b, g, 0)),
        out_shape=jax.ShapeDtypeStruct((B, S, d), jnp.float32),
        compiler_params=pltpu.CompilerParams(
            dimension_semantics=("parallel", "arbitrary"), vmem_limit_bytes=VMEM_LIMIT_BYTES),
        name="na_latent",
    )(q, k, v, k_ctx, v_ctx, bias)


def _na_context_kernel(q_ref, k_ref, v_ref, o_ref):
    lane = lax.broadcasted_iota(jnp.int32, (q_ref.shape[1], 128), 1)
    for p in range(NA_DIM // 128):
        lanes = slice(p * 128, (p + 1) * 128)
        o_ref[0, :, lanes] = _attend_pair(q_ref[0, :, lanes], [(k_ref[0, :, lanes], v_ref[0, :, lanes], None)], lane)


def _na_context(q, k, v):
    B, Lc, d = q.shape
    blk = pl.BlockSpec((1, Lc, d), lambda b: (b, 0, 0))
    return pl.pallas_call(
        _na_context_kernel,
        grid=(B,),
        in_specs=[blk, blk, blk],
        out_specs=blk,
        out_shape=jax.ShapeDtypeStruct((B, Lc, d), jnp.float32),
        compiler_params=pltpu.CompilerParams(dimension_semantics=("parallel",)),
        name="na_context",
    )(q, k, v)


def _pool_matrices(T):
    tb, halo = POOL_BLOCK, POOL_HALO
    mats = np.zeros((len(POOL_WINDOWS), 3, tb, tb + 2 * halo), np.float32)
    for wi, w in enumerate(POOL_WINDOWS):
        for kind, start in enumerate((0, tb, T - tb)):
            for i in range(tb):
                t = start + i
                lo = min(max(t - w // 2, 0), T - 1)
                hi = min(max(t + (w - w // 2 - 1), 0), T - 1)
                mats[wi, kind, i, lo - start + halo:hi - start + halo + 1] = 1.0 / (hi - lo + 1)
                mats[wi, kind, i, i + halo] -= 1.0
    return jnp.asarray(mats)


def _pool_kernel(u_ref, a_ref, w_ref, s_ref, o_ref):
    T = u_ref.shape[1]
    tb, halo = POOL_BLOCK, POOL_HALO
    nb = T // tb
    group = lax.broadcasted_iota(jnp.int32, (tb, POOL_DIM), 1) // POOL_GROUP_DIM
    zeros = jnp.zeros((halo, POOL_DIM), jnp.float32)
    for blk in range(nb):
        kind = 0 if blk == 0 else (2 if blk == nb - 1 else 1)
        before = zeros if blk == 0 else u_ref[0, blk * tb - halo:blk * tb, :]
        after = zeros if blk == nb - 1 else u_ref[0, (blk + 1) * tb:(blk + 1) * tb + halo, :]
        xcat = jnp.concatenate([before, u_ref[0, blk * tb:(blk + 1) * tb, :], after], axis=0)
        pooled = None
        for wi in range(len(POOL_WINDOWS)):
            pw = jnp.dot(a_ref[wi, kind], xcat, preferred_element_type=jnp.float32, precision=lax.Precision.HIGHEST)
            pooled = pw if pooled is None else jnp.where(group == wi, pw, pooled)
        mixed = jnp.dot(pooled.astype(jnp.bfloat16), w_ref[...], preferred_element_type=jnp.float32)
        o_ref[0, blk * tb:(blk + 1) * tb, :] = mixed * s_ref[...]


def _pool_mix(u, pool_w, pool_scale):
    B, T, d = u.shape
    assert T % POOL_BLOCK == 0 and T >= 2 * POOL_BLOCK
    w_bd = jax.scipy.linalg.block_diag(*[pool_w[g] for g in range(POOL_GROUPS)]).astype(jnp.bfloat16)
    mats = _pool_matrices(T)
    blk = pl.BlockSpec((1, T, d), lambda b: (b, 0, 0))
    return pl.pallas_call(
        _pool_kernel,
        grid=(B,),
        in_specs=[blk, pl.BlockSpec(mats.shape, lambda b: (0, 0, 0, 0)), pl.BlockSpec((d, d), lambda b: (0, 0)),
                  pl.BlockSpec((1, d), lambda b: (0, 0))],
        out_specs=blk,
        out_shape=jax.ShapeDtypeStruct((B, T, d), jnp.float32),
        compiler_params=pltpu.CompilerParams(dimension_semantics=("parallel",)),
        name="pool_mix",
    )(u, mats, w_bd, pool_scale[None, :])


def _log_sigmoid(x):
    return jnp.minimum(x, 0.0) - jnp.log1p(jnp.exp(-jnp.abs(x)))


def _ml_prep(q, k, conv_ref, cos_ref, sin_ref):
    T = q.shape[0]
    row = lax.broadcasted_iota(jnp.int32, (T, 1), 0)
    lane = lax.broadcasted_iota(jnp.int32, (1, ML_DIM), 1) % (HEAD_DIM // 2)
    outs = []
    for idx, x in enumerate((q, k)):
        acc = None
        for j in range(ML_CONV_W):
            d = j - ML_CONV_W // 2
            w = conv_ref[j:j + 1, idx * ML_DIM:(idx + 1) * ML_DIM]
            if d == 0:
                term = x * w
            else:
                shifted = pltpu.roll(x, (-d) % T, 0)
                term = jnp.where((row + d >= 0) & (row + d < T), shifted, 0.0) * w
            acc = term if acc is None else acc + term
        y = acc * jax.nn.sigmoid(acc)
        if cos_ref is not None:
            quarter = HEAD_DIM // 4
            partner = jnp.where(lane < quarter, pltpu.roll(y, ML_DIM - quarter, 1), pltpu.roll(y, quarter, 1))
            y = y * cos_ref[...] + partner * sin_ref[...]
        outs.append(y)
    return outs


def _ml_chain(qs, kts, vs, rows, gi_c, gi_r, bc, br, ci, ii, head, backward, cn_ref, m_ref, slot, lane, tri):
    L = ML_L
    pair = slice((head // 2) * 128, (head // 2) * 128 + 128)
    own = (lane < HEAD_DIM) if head % 2 == 0 else (lane >= HEAD_DIM)
    one_lane = HEAD_DIM if head % 2 == 0 else 0
    q2 = jnp.where(own, qs[rows, pair], jnp.zeros((), jnp.bfloat16))
    v1 = jnp.where(own, vs[rows, pair].astype(jnp.float32), jnp.where(lane == one_lane, 1.0, 0.0))
    own_rows = lax.broadcasted_iota(jnp.int32, (128, L), 0)
    own_rows = (own_rows < HEAD_DIM) if head % 2 == 0 else (own_rows >= HEAD_DIM)
    kt2 = jnp.where(own_rows, kts[pair, rows], jnp.zeros((), jnp.bfloat16))
    m_prev = m_ref[slot]
    cn = cn_ref[slot]
    bcol = bc[:, ci:ci + 1]
    brow = br[ci:ci + 1, :]
    log_d = jnp.where(tri, bcol - brow + gi_r[ii:ii + 1, :], -jnp.inf)
    inter = bcol + m_prev
    m_t = jnp.maximum(inter, jnp.max(log_d, axis=-1, keepdims=True))
    s = jnp.dot(q2, kt2, preferred_element_type=jnp.float32)
    w_intra = (s * jnp.exp(log_d - m_t)).astype(jnp.bfloat16)
    nd = (jnp.dot(w_intra, v1.astype(jnp.bfloat16), preferred_element_type=jnp.float32)
          + jnp.exp(inter - m_t) * jnp.dot(q2, cn.astype(jnp.bfloat16), preferred_element_type=jnp.float32))
    den = nd[:, one_lane:one_lane + 1]
    h = nd / jnp.maximum(jnp.abs(den), jnp.exp(-m_t))
    b_last = bc[0:1, ci:ci + 1] if backward else bc[L - 1:L, ci:ci + 1]
    log_w = b_last - bcol + gi_c[:, ii:ii + 1]
    m_new = jnp.maximum(b_last + m_prev, jnp.max(log_w, axis=0, keepdims=True))
    wv = (jnp.exp(log_w - m_new) * v1).astype(jnp.bfloat16)
    cn_ref[slot] = jnp.exp(b_last + m_prev - m_new) * cn + jnp.dot(kt2, wv, preferred_element_type=jnp.float32)
    m_ref[slot] = m_new
    return h


def _ml_scan(T, qs, kts, vs, gc_ref, gr_ref, gbc, gbr, hf, hb, cn_ref, m_ref, tril, triu):
    L = ML_L
    nc = T // L
    lane = lax.broadcasted_iota(jnp.int32, (L, 128), 1)
    t_idx = lax.broadcasted_iota(jnp.int32, (L, L), 0)
    s_idx = lax.broadcasted_iota(jnp.int32, (L, L), 1)
    hi = lax.Precision.HIGHEST

    def body(c, carry):
        for backward in (False, True):
            cc = (nc - 1 - c) if backward else c
            rows = pl.ds(pl.multiple_of(cc * L, L), L)
            gc = gc_ref[0, rows, :] + gbc
            gr = gr_ref[0, :, rows] + gbr
            bc = jnp.dot(triu if backward else tril, _log_sigmoid(gc), preferred_element_type=jnp.float32, precision=hi)
            br = jnp.dot(_log_sigmoid(gr), tril if backward else triu, preferred_element_type=jnp.float32, precision=hi)
            tri = (s_idx >= t_idx) if backward else (s_idx <= t_idx)
            out = hb if backward else hf
            base = 2 * ML_HEADS if backward else 0
            for pair in range(ML_HEADS // 2):
                hs = []
                for head in (2 * pair, 2 * pair + 1):
                    hs.append(_ml_chain(qs, kts, vs, rows, gc, gr, bc, br, base + ML_HEADS + head, base + head, head,
                                        backward, cn_ref, m_ref, (ML_HEADS if backward else 0) + head, lane, tri))
                out[rows, pair * 128:(pair + 1) * 128] = jnp.where(lane < HEAD_DIM, hs[0], hs[1])
        return carry

    lax.fori_loop(0, nc, body, 0)


def _mlstm_kernel(q_ref, k_ref, v_ref, o_ref, gc_ref, gr_ref, cq_ref, ck_ref, cv_ref, co_ref, cgc_ref, cgr_ref,
                  cos_ref, sin_ref, conv_ref, gbc_ref, gbr_ref, ng_ref, e_ref, tril_ref, triu_ref,
                  y_ref, cy_ref, qs, kts, vs, hf, hb, cqs, ckts, cvs, chf, chb, cn_ref, m_ref):
    cn_ref[...] = jnp.zeros_like(cn_ref)
    m_ref[...] = jnp.zeros_like(m_ref)
    tril, triu = tril_ref[...], triu_ref[...]

    def stage(q_r, k_r, v_r, rope, qd, ktd, vd):
        qp, kp = _ml_prep(q_r[0], k_r[0], conv_ref, cos_ref if rope else None, sin_ref if rope else None)
        qd[...] = qp.astype(jnp.bfloat16)
        ktd[...] = (kp * HEAD_DIM ** -0.5).T.astype(jnp.bfloat16)
        vd[...] = v_r[0].astype(jnp.bfloat16)

    def finish(h_f, h_b, o_r, y_r):
        h = h_f[...] + h_b[...]
        ss = jnp.dot(h * h, e_ref[...], preferred_element_type=jnp.float32, precision=lax.Precision.HIGHEST)
        y_r[0] = h * lax.rsqrt(ss * (1.0 / HEAD_DIM) + EPS) * ng_ref[...] * jax.nn.sigmoid(o_r[0])

    gbc, gbr = gbc_ref[...], gbr_ref[...]
    stage(cq_ref, ck_ref, cv_ref, False, cqs, ckts, cvs)
    _ml_scan(cq_ref.shape[1], cqs, ckts, cvs, cgc_ref, cgr_ref, gbc, gbr, chf, chb, cn_ref, m_ref, tril, triu)
    finish(chf, chb, co_ref, cy_ref)
    stage(q_ref, k_ref, v_ref, True, qs, kts, vs)
    _ml_scan(q_ref.shape[1], qs, kts, vs, gc_ref, gr_ref, gbc, gbr, hf, hb, cn_ref, m_ref, tril, triu)
    finish(hf, hb, o_ref, y_ref)


def _rope_tables(T):
    t = jnp.arange(T)
    row = (t // GRID_W).astype(jnp.float32)
    col = (t % GRID_W).astype(jnp.float32)
    nf = HEAD_DIM // 4
    inv = ROPE_BASE ** (-jnp.arange(nf, dtype=jnp.float32) / nf)
    cr, sr, cc, sc = jnp.cos(row[:, None] * inv), jnp.sin(row[:, None] * inv), jnp.cos(col[:, None] * inv), \
        jnp.sin(col[:, None] * inv)
    cos = jnp.concatenate([cr, cr, cc, cc], axis=-1)
    sin = jnp.concatenate([-sr, sr, -sc, sc], axis=-1)
    return jnp.tile(cos, (1, ML_HEADS)), jnp.tile(sin, (1, ML_HEADS))


def _mlstm_mix(lq, lk, lv, lo, lg, lgr, cq, ck, cv, co, cg, cgr, conv_w, gate_b, norm_g):
    B, T, d = lq.shape
    Lc = cq.shape[1]
    L = ML_L
    assert T % L == 0 and Lc % L == 0
    cos, sin = _rope_tables(T)
    lane_head = np.arange(d) // HEAD_DIM
    same_head = jnp.asarray((lane_head[:, None] == lane_head[None, :]).astype(np.float32))
    tril = jnp.asarray(np.tril(np.ones((L, L), np.float32)))
    seq = lambda n: pl.BlockSpec((1, n, d), lambda b: (b, 0, 0))
    full = lambda a: pl.BlockSpec(a.shape, lambda b: (0,) * a.ndim)
    consts = [cos, sin, conv_w, gate_b[None, :], gate_b[:, None], norm_g[None, :], same_head, tril, tril.T]
    f32, bf16 = jnp.float32, jnp.bfloat16
    return pl.pallas_call(
        _mlstm_kernel,
        grid=(B,),
        in_specs=[seq(T)] * 4 + [pl.BlockSpec((1, T, ML_N_GATES), lambda b: (b, 0, 0)),
                                 pl.BlockSpec((1, ML_N_GATES, T), lambda b: (b, 0, 0))]
        + [seq(Lc)] * 4 + [pl.BlockSpec((1, Lc, ML_N_GATES), lambda b: (b, 0, 0)),
                           pl.BlockSpec((1, ML_N_GATES, Lc), lambda b: (b, 0, 0))]
        + [full(a) for a in consts],
        out_specs=[seq(T), seq(Lc)],
        out_shape=[jax.ShapeDtypeStruct((B, T, d), f32), jax.ShapeDtypeStruct((B, Lc, d), f32)],
        scratch_shapes=[pltpu.VMEM((T, d), bf16), pltpu.VMEM((d, T), bf16), pltpu.VMEM((T, d), bf16),
                        pltpu.VMEM((T, d), f32), pltpu.VMEM((T, d), f32),
                        pltpu.VMEM((Lc, d), bf16), pltpu.VMEM((d, Lc), bf16), pltpu.VMEM((Lc, d), bf16),
                        pltpu.VMEM((Lc, d), f32), pltpu.VMEM((Lc, d), f32),
                        pltpu.VMEM((2 * ML_HEADS, 128, 128), f32), pltpu.VMEM((2 * ML_HEADS, 1, 1), f32)],
        compiler_params=pltpu.CompilerParams(dimension_semantics=("parallel",), vmem_limit_bytes=VMEM_LIMIT_BYTES),
        name="mlstm",
    )(lq, lk, lv, lo, lg, lgr, cq, ck, cv, co, cg, cgr, *consts)


def _ada_kernel(c_ref, w_ref, b_ref, o_ref):
    c = c_ref[...]
    act = (c * jax.nn.sigmoid(c)).astype(jnp.bfloat16)
    o_ref[...] = jnp.dot(act, w_ref[...].astype(jnp.bfloat16), preferred_element_type=jnp.float32) + b_ref[...]


def _ada_mod(cond, w, b):
    rows, d = cond.shape
    n = w.shape[1]
    tile = ADA_COL_TILE
    assert n % tile == 0
    return pl.pallas_call(
        _ada_kernel,
        grid=(n // tile,),
        in_specs=[pl.BlockSpec((rows, d), lambda j: (0, 0)), pl.BlockSpec((d, tile), lambda j: (0, j)),
                  pl.BlockSpec((1, tile), lambda j: (0, j))],
        out_specs=pl.BlockSpec((rows, tile), lambda j: (0, j)),
        out_shape=jax.ShapeDtypeStruct((rows, n), jnp.float32),
        compiler_params=pltpu.CompilerParams(dimension_semantics=("parallel",)),
        name="ada_mod",
    )(cond, w, b[None, :])


_IN_SPLITS = (("q", NA_DIM), ("k", NA_DIM), ("v", NA_DIM), ("pool", POOL_DIM), ("mq", ML_DIM), ("mk", ML_DIM),
              ("mv", ML_DIM), ("mo", ML_DIM), ("gates", ML_N_GATES))
IN_COLS = sum(n for _, n in _IN_SPLITS)
IN_COLS_PAD = -(-IN_COLS // 128) * 128


def _in_proj_kernel(x_ref, sc_ref, sh_ref, g_ref, w_ref, wgt_ref, gq_ref, gk_ref, e_ref,
                    q_ref, k_ref, v_ref, pool_ref, mq_ref, mk_ref, mv_ref, mo_ref, gc_ref, gr_ref):
    x = x_ref[0]
    y = x * lax.rsqrt(jnp.mean(x * x, axis=-1, keepdims=True) + EPS) * g_ref[...]
    xn = (y * (1.0 + sc_ref[0]) + sh_ref[0]).astype(jnp.bfloat16)
    p = jnp.dot(xn, w_ref[...], preferred_element_type=jnp.float32)
    off = {}
    o = 0
    for name, n in _IN_SPLITS:
        off[name] = slice(o, o + n)
        o += n

    def head_norm(u, gain):
        sq = u * u
        hi = sq.astype(jnp.bfloat16)
        lo = (sq - hi.astype(jnp.float32)).astype(jnp.bfloat16)
        ss = (jnp.dot(hi, e_ref[...], preferred_element_type=jnp.float32)
              + jnp.dot(lo, e_ref[...], preferred_element_type=jnp.float32))
        return u * lax.rsqrt(ss * (1.0 / HEAD_DIM) + EPS) * gain

    q_ref[0] = (head_norm(p[:, off["q"]], gq_ref[...]) * HEAD_DIM ** -0.5).astype(jnp.bfloat16)
    k_ref[0] = head_norm(p[:, off["k"]], gk_ref[...]).astype(jnp.bfloat16)
    v_ref[0] = p[:, off["v"]].astype(jnp.bfloat16)
    pool_ref[0] = p[:, off["pool"]]
    mq_ref[0] = p[:, off["mq"]]
    mk_ref[0] = p[:, off["mk"]]
    mv_ref[0] = p[:, off["mv"]]
    mo_ref[0] = p[:, off["mo"]]
    gc_ref[0] = p[:, off["gates"]]
    gr_ref[0] = lax.dot_general(wgt_ref[...], xn, (((1,), (1,)), ((), ())), preferred_element_type=jnp.float32)


def _in_proj(x3, sc, sh, norm_g, w_in, gq, gk):
    B, T, d = x3.shape
    tn = min(IN_TOKEN_TILE, T)
    assert T % tn == 0
    w_bf = jnp.pad(w_in, ((0, 0), (0, IN_COLS_PAD - IN_COLS))).astype(jnp.bfloat16)
    wgt_bf = w_in[:, IN_COLS - ML_N_GATES:].T.astype(jnp.bfloat16)
    lane_head = np.arange(NA_DIM) // HEAD_DIM
    same_head = jnp.asarray((lane_head[:, None] == lane_head[None, :]).astype(np.float32)).astype(jnp.bfloat16)
    mod = pl.BlockSpec((1, 1, d), lambda b, t: (b, 0, 0))
    full = lambda a: pl.BlockSpec(a.shape, lambda b, t: (0,) * a.ndim)
    tok = lambda n: pl.BlockSpec((1, tn, n), lambda b, t: (b, t, 0))
    shp = lambda n, dt: jax.ShapeDtypeStruct((B, T, n), dt)
    f32, bf16 = jnp.float32, jnp.bfloat16
    consts = [norm_g[None, :], w_bf, wgt_bf, jnp.tile(gq, NA_HEADS)[None, :], jnp.tile(gk, NA_HEADS)[None, :], same_head]
    outs = pl.pallas_call(
        _in_proj_kernel,
        grid=(B, T // tn),
        in_specs=[pl.BlockSpec((1, tn, d), lambda b, t: (b, t, 0)), mod, mod] + [full(a) for a in consts],
        out_specs=[tok(NA_DIM)] * 3 + [tok(POOL_DIM)] + [tok(ML_DIM)] * 4 + [tok(ML_N_GATES)]
        + [pl.BlockSpec((1, ML_N_GATES, tn), lambda b, t: (b, 0, t))],
        out_shape=[shp(NA_DIM, bf16)] * 3 + [shp(POOL_DIM, f32)] + [shp(ML_DIM, f32)] * 4 + [shp(ML_N_GATES, f32)]
        + [jax.ShapeDtypeStruct((B, ML_N_GATES, T), f32)],
        compiler_params=pltpu.CompilerParams(
            dimension_semantics=("parallel", "parallel"), vmem_limit_bytes=VMEM_LIMIT_BYTES),
        name="in_proj",
    )(x3, sc, sh, *consts)
    return dict(zip(("q", "k", "v", "pool", "mq", "mk", "mv", "mo", "gc", "gr"), outs))


def _out_proj_kernel(x_ref, g_ref, na_ref, pool_ref, ml_ref, w_ref, o_ref):
    bf = jnp.bfloat16
    y = jnp.dot(na_ref[0].astype(bf), w_ref[0:NA_DIM, :], preferred_element_type=jnp.float32)
    y += jnp.dot(pool_ref[0].astype(bf), w_ref[NA_DIM:NA_DIM + POOL_DIM, :], preferred_element_type=jnp.float32)
    y += jnp.dot(ml_ref[0].astype(bf), w_ref[NA_DIM + POOL_DIM:, :], preferred_element_type=jnp.float32)
    o_ref[0] = x_ref[0] + g_ref[0] * y


def _out_proj(x3, gate, na, pool, ml, w_out):
    B, T, d = x3.shape
    tn = min(IN_TOKEN_TILE, T)
    tok = lambda n: pl.BlockSpec((1, tn, n), lambda b, t: (b, t, 0))
    return pl.pallas_call(
        _out_proj_kernel,
        grid=(B, T // tn),
        in_specs=[tok(d), pl.BlockSpec((1, 1, d), lambda b, t: (b, 0, 0)), tok(NA_DIM), tok(POOL_DIM), tok(ML_DIM),
                  pl.BlockSpec(w_out.shape, lambda b, t: (0, 0))],
        out_specs=tok(d),
        out_shape=jax.ShapeDtypeStruct(x3.shape, x3.dtype),
        compiler_params=pltpu.CompilerParams(dimension_semantics=("parallel", "parallel")),
        name="out_proj",
    )(x3, gate, na, pool, ml, w_out.astype(jnp.bfloat16))


def kernel(x, c, ctx, c_ctx, w_ada, b_ada, norm1_g, w_in, ml_gate_b, na_q_g, na_k_g, na_rpb, pool_w,
           pool_scale, ml_conv, ml_norm_g, w_out, norm2_g, peer_wq, peer_keys, peer_u, peer_v):
    depth = w_in.shape[0]
    B, S, D = x.shape
    Lc = ctx.shape[1]
    hc = ctx
    for l in range(depth):
        need_ctx = l < depth - 1
        u_bf = peer_u[l].astype(jnp.bfloat16)
        vt_bf = peer_v[l].astype(jnp.bfloat16).T
        wqt_bf = peer_wq[l].astype(jnp.bfloat16).T
        keys_bf = peer_keys[l].astype(jnp.bfloat16).reshape(2 * PEER_HEADS, PEER_KEYS, PEER_DKEY // 2)
        mod = _ada_mod(jnp.concatenate([c, c_ctx[None, :]], axis=0), w_ada[l], b_ada[l]).reshape(B + 1, 6, 1, D)
        sh1, sc1, g1, sh2, sc2, g2 = (mod[:B, i] for i in range(6))
        csh1, csc1, cg1, csh2, csc2, cg2 = (mod[B:, i] for i in range(6))
        hc1 = hc.reshape(1, B * Lc, D)
        a = _in_proj(x, sc1, sh1, norm1_g[l], w_in[l], na_q_g[l], na_k_g[l])
        ac = _in_proj(hc1, csc1, csh1, norm1_g[l], w_in[l], na_q_g[l], na_k_g[l])
        ac = {k: v.reshape(B, Lc, v.shape[-1]) for k, v in ac.items() if k != "gr"} | {
            "gr": jnp.transpose(ac["gr"].reshape(ML_N_GATES, B, Lc), (1, 0, 2))}
        na_lat = _na_latent(a["q"], a["k"], a["v"], ac["k"], ac["v"], na_rpb[l])
        pool_lat = _pool_mix(a["pool"], pool_w[l], pool_scale[l])
        ml_lat, ml_ctx = _mlstm_mix(a["mq"], a["mk"], a["mv"], a["mo"], a["gc"], a["gr"],
                                    ac["mq"], ac["mk"], ac["mv"], ac["mo"], ac["gc"], ac["gr"],
                                    ml_conv[l], ml_gate_b[l], ml_norm_g[l])
        x = _out_proj(x, g1, na_lat, pool_lat, ml_lat, w_out[l])
        x = _peer_block(x, sc2, sh2, g2, norm2_g[l], wqt_bf, keys_bf, u_bf, vt_bf)
        if need_ctx:
            na_ctx = _na_context(ac["q"], ac["k"], ac["v"])
            pool_ctx = _pool_mix(ac["pool"], pool_w[l], pool_scale[l])
            flat = lambda t: t.reshape(1, B * Lc, t.shape[-1])
            hc1 = _out_proj(hc1, cg1, flat(na_ctx), flat(pool_ctx), flat(ml_ctx), w_out[l])
            hc = _peer_block(hc1, csc2, csh2, cg2, norm2_g[l], wqt_bf, keys_bf, u_bf, vt_bf).reshape(B, Lc, D)
    return x
```

```python
import functools

import numpy as np
import jax
import jax.numpy as jnp
from jax import lax
from jax.experimental import pallas as pl
from jax.experimental.pallas import tpu as pltpu

D_MODEL = 1024
GRID_W = 64
HEAD_DIM = 64
NA_DIM = D_MODEL // 2
NA_HEADS = NA_DIM // HEAD_DIM
NA_WIN_ROWS = 8
NA_WIN_COLS = 16
POOL_DIM = D_MODEL // 4
POOL_WINDOWS = (2, 4, 8, 16)
POOL_GROUPS = len(POOL_WINDOWS)
POOL_GROUP_DIM = POOL_DIM // POOL_GROUPS
ML_DIM = D_MODEL // 4
ML_HEADS = ML_DIM // HEAD_DIM
ML_CONV_W = 5
ML_N_GATES = 4 * ML_HEADS
PEER_KEYS = 128
PEER_EXPERTS = PEER_KEYS * PEER_KEYS
PEER_HEADS = 8
PEER_TOPK = 16
PEER_DKEY = 256
ROPE_BASE = 10000.0
EPS = 1e-6

PEER_TOKEN_TILE = 1024
PEER_EXPERT_TILE = 1024
ROUTE_TOKEN_TILE = 512
ROUTE_LANE_CHUNK = 256
IN_TOKEN_TILE = 256
NA_ROWS_PER_STEP = 4
ADA_COL_TILE = 1536
POOL_BLOCK = 128
POOL_HALO = 8
ML_L = 256
VMEM_LIMIT_BYTES = 56 * 1024 * 1024

_GELU_C0 = float(np.sqrt(2.0 / np.pi))
_GELU_C1 = 0.044715 * _GELU_C0


def _gelu_tanh(x):
    inner = x * (_GELU_C0 + _GELU_C1 * (x * x))
    hx = 0.5 * x
    return hx + hx * jnp.tanh(inner)


def _peer_dense_kernel(x_ref, u_ref, v_ref, cnt_ref, e1_ref, rank_ref, e2_ref, o_ref):
    j = pl.program_id(1)

    @pl.when(j == 0)
    def _():
        o_ref[...] = jnp.zeros_like(o_ref)

    te = u_ref.shape[0]
    act = lax.dot_general(u_ref[...], x_ref[...], (((1,), (1,)), ((), ())),
                          preferred_element_type=jnp.float32)
    gates = []
    for k in range(te // PEER_KEYS):
        g = None
        for h in range(PEER_HEADS):
            cnt = cnt_ref[h, k:k + 1, :].astype(jnp.bfloat16)
            e1 = e1_ref[h, k:k + 1, :].astype(jnp.bfloat16)
            term = jnp.where(rank_ref[h] < cnt, e2_ref[h], jnp.zeros((), e2_ref.dtype)) * e1
            g = term if g is None else g + term
        gates.append(g)
    gate = jnp.concatenate(gates, axis=0)
    p = _gelu_tanh(act).astype(jnp.bfloat16) * gate
    o_ref[...] += lax.dot_general(v_ref[...], p, (((0,), (0,)), ((), ())), preferred_element_type=jnp.float32)


def _peer_dense(tok, u_bf, v_bf, cnt, e1, rank, e2):
    n, d = tok.shape
    tn, te = PEER_TOKEN_TILE, PEER_EXPERT_TILE
    assert n % tn == 0 and PEER_EXPERTS % te == 0 and te % PEER_KEYS == 0
    tab_a = pl.BlockSpec((PEER_HEADS, te // PEER_KEYS, tn), lambda i, j: (0, j, i))
    tab_b = pl.BlockSpec((PEER_HEADS, PEER_KEYS, tn), lambda i, j: (0, 0, i))
    return pl.pallas_call(
        _peer_dense_kernel,
        grid=(n // tn, PEER_EXPERTS // te),
        in_specs=[
            pl.BlockSpec((tn, d), lambda i, j: (i, 0)),
            pl.BlockSpec((te, d), lambda i, j: (j, 0)),
            pl.BlockSpec((te, d), lambda i, j: (j, 0)),
            tab_a, tab_a, tab_b, tab_b,
        ],
        out_specs=pl.BlockSpec((d, tn), lambda i, j: (0, i)),
        out_shape=jax.ShapeDtypeStruct((d, n), jnp.float32),
        compiler_params=pltpu.CompilerParams(
            dimension_semantics=("parallel", "arbitrary"), vmem_limit_bytes=VMEM_LIMIT_BYTES),
        name="peer_dense",
    )(tok, u_bf, v_bf, cnt, e1, rank, e2)


def _top16(works, want_ranks):
    lanes = works[0].shape[1]
    rows = lax.broadcasted_iota(jnp.int32, (PEER_TOPK, lanes), 0)
    vals = [jnp.zeros((PEER_TOPK, lanes), jnp.float32) for _ in works]
    ranks = [jnp.full(w.shape, float(PEER_TOPK), jnp.float32) if r else None for w, r in zip(works, want_ranks)]
    tops = [[] for _ in works]
    works = list(works)
    for i in range(PEER_TOPK):
        ms = [jnp.max(w, axis=0, keepdims=True) for w in works]
        hits = [w == m for w, m in zip(works, ms)]
        ranks = [None if r is None else jnp.where(hit, float(i), r) for r, hit in zip(ranks, hits)]
        works = [jnp.where(hit, -jnp.inf, w) for w, hit in zip(works, hits)]
        vals = [jnp.where(rows == i, m, v) for v, m in zip(vals, ms)]
        for t, m in zip(tops, ms):
            t.append(m)
    return tops, vals, ranks


def _peer_route_kernel(x_ref, sc_ref, sh_ref, g_ref, wqt_ref, keys_ref,
                       tok_ref, cnt_ref, e1_ref, rank_ref, e2_ref, s_scr):
    x = x_ref[0]
    y = x * lax.rsqrt(jnp.mean(x * x, axis=-1, keepdims=True) + EPS) * g_ref[...]
    tok = (y * (1.0 + sc_ref[0]) + sh_ref[0]).astype(jnp.bfloat16)
    tok_ref[...] = tok
    qt = lax.dot_general(wqt_ref[...], tok, (((1,), (1,)), ((), ())),
                         preferred_element_type=jnp.float32).astype(jnp.bfloat16)
    dk = PEER_DKEY // 2
    for hp in range(2 * PEER_HEADS):
        s_scr[hp] = jnp.dot(keys_ref[hp], qt[hp * dk:(hp + 1) * dk], preferred_element_type=jnp.float32)

    cw = ROUTE_LANE_CHUNK
    lane_chunks = x.shape[0] // cw
    row8 = lax.broadcasted_iota(jnp.int32, (8, cw), 0)

    def body(it, carry):
        h = it // lane_chunks
        lanes = pl.ds(pl.multiple_of((it % lane_chunks) * cw, cw), cw)
        s1 = s_scr[2 * h, :, lanes]
        s2 = s_scr[2 * h + 1, :, lanes]
        (t1, t2), (v1, v2), (_, rank) = _top16([s1, s2], [False, True])
        groups = [t1[0] + v2, t1[1] + v2[:8]]
        for i in range(2, 8):
            groups.append(jnp.where(row8 < PEER_TOPK // (i + 1), t1[i] + v2[:8], -jnp.inf))
        groups.append(v1[8:] + t2[0])
        (tp,), (vp,), _ = _top16([jnp.concatenate(groups, axis=0)], [False])
        tau = tp[PEER_TOPK - 1]
        z = jnp.sum(jnp.exp(vp - tp[0]), axis=0, keepdims=True)
        cnt = jnp.zeros(s1.shape, jnp.float32)
        for jj in range(PEER_TOPK):
            cnt = jnp.where(s1 + t2[jj] >= tau, float(jj + 1), cnt)
        cnt = jnp.where(s1 >= t1[PEER_TOPK - 1], cnt, 0.0)
        cnt_ref[h, :, lanes] = cnt
        e1_ref[h, :, lanes] = jnp.exp(s1 - t1[0])
        rank_ref[h, :, lanes] = rank.astype(jnp.bfloat16)
        e2_ref[h, :, lanes] = (jnp.exp(s2 - t2[0]) * (1.0 / z)).astype(jnp.bfloat16)
        return carry

    lax.fori_loop(0, PEER_HEADS * lane_chunks, body, 0)


def _peer_route(x3, sc, sh, g, wqt_bf, keys_bf):
    B, T, d = x3.shape
    tn = min(ROUTE_TOKEN_TILE, T)
    assert T % tn == 0 and tn % 128 == 0
    nt = T // tn
    n = B * T
    tab = pl.BlockSpec((PEER_HEADS, PEER_KEYS, tn), lambda b, t: (0, 0, b * nt + t))
    tab_shape = lambda dt: jax.ShapeDtypeStruct((PEER_HEADS, PEER_KEYS, n), dt)
    mod = pl.BlockSpec((1, 1, d), lambda b, t: (b, 0, 0))
    return pl.pallas_call(
        _peer_route_kernel,
        grid=(B, nt),
        in_specs=[
            pl.BlockSpec((1, tn, d), lambda b, t: (b, t, 0)),
            mod, mod,
            pl.BlockSpec((1, d), lambda b, t: (0, 0)),
            pl.BlockSpec(wqt_bf.shape, lambda b, t: (0, 0)),
            pl.BlockSpec(keys_bf.shape, lambda b, t: (0, 0, 0)),
        ],
        out_specs=[pl.BlockSpec((tn, d), lambda b, t: (b * nt + t, 0)), tab, tab, tab, tab],
        out_shape=[jax.ShapeDtypeStruct((n, d), jnp.bfloat16), tab_shape(jnp.float32), tab_shape(jnp.float32),
                   tab_shape(jnp.bfloat16), tab_shape(jnp.bfloat16)],
        scratch_shapes=[pltpu.VMEM((2 * PEER_HEADS, PEER_KEYS, tn), jnp.float32)],
        compiler_params=pltpu.CompilerParams(
            dimension_semantics=("parallel", "parallel"), vmem_limit_bytes=VMEM_LIMIT_BYTES),
        name="peer_route",
    )(x3, sc, sh, g, wqt_bf, keys_bf)


def _residual_t_kernel(x_ref, g_ref, yt_ref, o_ref):
    o_ref[0] = x_ref[0] + g_ref[0] * yt_ref[...].T


def _residual_t(x3, gate, yt):
    B, T, d = x3.shape
    tn = min(ROUTE_TOKEN_TILE, T)
    nt = T // tn
    return pl.pallas_call(
        _residual_t_kernel,
        grid=(B, nt),
        in_specs=[pl.BlockSpec((1, tn, d), lambda b, t: (b, t, 0)),
                  pl.BlockSpec((1, 1, d), lambda b, t: (b, 0, 0)),
                  pl.BlockSpec((d, tn), lambda b, t: (0, b * nt + t))],
        out_specs=pl.BlockSpec((1, tn, d), lambda b, t: (b, t, 0)),
        out_shape=jax.ShapeDtypeStruct(x3.shape, x3.dtype),
        compiler_params=pltpu.CompilerParams(dimension_semantics=("parallel", "parallel")),
        name="residual_t",
    )(x3, gate, yt)


def _peer_block(x3, sc, sh, gate, norm_g, wqt_bf, keys_bf, u_bf, v_bf):
    tok, cnt, e1, rank, e2 = _peer_route(x3, sc, sh, norm_g[None, :], wqt_bf, keys_bf)
    out_t = _peer_dense(tok, u_bf, v_bf, cnt, e1, rank, e2)
    return _residual_t(x3, gate, out_t)


def _attend_pair(q2, segments, lane):
    zero = jnp.zeros((), q2.dtype)
    outs = []
    for half in range(2):
        own = (lane < HEAD_DIM) if half == 0 else (lane >= HEAD_DIM)
        qh = jnp.where(own, q2, zero)
        scores = []
        for k2, _, bias in segments:
            s = lax.dot_general(qh, k2, (((1,), (1,)), ((), ())), preferred_element_type=jnp.float32)
            scores.append(s if bias is None else s + bias[half])
        m = functools.reduce(jnp.maximum, [jnp.max(s, axis=-1, keepdims=True) for s in scores])
        ps = [jnp.exp(s - m) for s in scores]
        denom = functools.reduce(jnp.add, [jnp.sum(p, axis=-1, keepdims=True) for p in ps])
        acc = functools.reduce(jnp.add, [jnp.dot(p.astype(v2.dtype), v2, preferred_element_type=jnp.float32)
                                         for p, (_, v2, _) in zip(ps, segments)])
        outs.append(acc / denom)
    return jnp.where(lane < HEAD_DIM, outs[0], outs[1])


def _na_latent_kernel(q_ref, k_ref, v_ref, kc_ref, vc_ref, bias_ref, o_ref):
    n_rows = k_ref.shape[1] // GRID_W
    union = bias_ref.shape[3] // GRID_W
    u0 = _na_union_start(pl.program_id(1), n_rows, union)
    keys = pl.ds(pl.multiple_of(u0 * GRID_W, GRID_W), union * GRID_W)
    lane = lax.broadcasted_iota(jnp.int32, (q_ref.shape[1], 128), 1)
    for p in range(NA_DIM // 128):
        lanes = slice(p * 128, (p + 1) * 128)
        segs = [(k_ref[0, keys, lanes], v_ref[0, keys, lanes], (bias_ref[0, 2 * p], bias_ref[0, 2 * p + 1])),
                (kc_ref[0, :, lanes], vc_ref[0, :, lanes], None)]
        o_ref[0, :, lanes] = _attend_pair(q_ref[0, :, lanes], segs, lane)


def _na_union_start(g, n_rows, union):
    lo = g * NA_ROWS_PER_STEP - NA_WIN_ROWS // 2
    if isinstance(g, int):
        return min(min(max(lo, 0), n_rows - NA_WIN_ROWS), n_rows - union)
    return jnp.minimum(jnp.clip(lo, 0, n_rows - NA_WIN_ROWS), n_rows - union)


def _na_group_bias(rpb, n_rows):
    G, KR = NA_ROWS_PER_STEP, NA_WIN_ROWS
    union = KR + G - 1
    n_groups = n_rows // G
    cq = np.arange(GRID_W)
    c0 = np.clip(cq - NA_WIN_COLS // 2, 0, GRID_W - NA_WIN_COLS)
    in_win = (cq[None, :] >= c0[:, None]) & (cq[None, :] < c0[:, None] + NA_WIN_COLS)
    bc = np.clip(cq[None, :] - cq[:, None] + (NA_WIN_COLS - 1), 0, 2 * NA_WIN_COLS - 2)

    def layout(g):
        r = g * G + np.arange(G)
        key_row = _na_union_start(g, n_rows, union) + np.arange(union)
        r0 = np.clip(r - KR // 2, 0, n_rows - KR)
        ok = (key_row[None, :] >= r0[:, None]) & (key_row[None, :] < r0[:, None] + KR)
        br = np.clip(key_row[None, :] - r[:, None] + (KR - 1), 0, 2 * KR - 2)
        return ok, br

    layouts = [layout(g) for g in range(n_groups)]
    same = lambda a, b: all(np.array_equal(x, y) for x, y in zip(a, b))
    assert n_groups >= 3 and all(same(layouts[g], layouts[1]) for g in range(1, n_groups - 1))
    toe = jnp.where(in_win[None, None], rpb[:, :, bc], -jnp.inf).astype(jnp.float32)
    tables = []
    for ok, br in (layouts[0], layouts[1], layouts[-1]):
        blocks = jnp.where(ok[None, :, :, None, None], toe[:, br], -jnp.inf)
        tables.append(jnp.transpose(blocks, (0, 1, 3, 2, 4)).reshape(NA_HEADS, G * GRID_W, union * GRID_W))
    return jnp.stack(tables, axis=0)


def _na_latent(q, k, v, k_ctx, v_ctx, rpb):
    B, S, d = q.shape
    Lc = k_ctx.shape[1]
    n_rows = S // GRID_W
    assert n_rows >= NA_WIN_ROWS + NA_ROWS_PER_STEP - 1 and n_rows % NA_ROWS_PER_STEP == 0
    n_groups = n_rows // NA_ROWS_PER_STEP
    bias = _na_group_bias(rpb, n_rows)
    tq = NA_ROWS_PER_STEP * GRID_W
    whole = lambda L: pl.BlockSpec((1, L, d), lambda b, g: (b, 0, 0))
    kind = lambda b, g: (jnp.where(g == 0, 0, jnp.where(g == n_groups - 1, 2, 1)), 0, 0, 0)
    return pl.pallas_call(
        _na_latent_kernel,
        grid=(B, n_groups),
        in_specs=[pl.BlockSpec((1, tq, d), lambda b, g: (b, g, 0)), whole(S), whole(S), whole(Lc), whole(Lc),
                  pl.BlockSpec((1,) + bias.shape[1:], kind)],
        out_specs=pl.BlockSpec((1, tq, d), lambda b, g: (b, g, 0)),
        out_shape=jax.ShapeDtypeStruct((B, S, d), jnp.float32),
        compiler_params=pltpu.CompilerParams(
            dimension_semantics=("parallel", "arbitrary"), vmem_limit_bytes=VMEM_LIMIT_BYTES),
        name="na_latent",
    )(q, k, v, k_ctx, v_ctx, bias)


def _na_context_kernel(q_ref, k_ref, v_ref, o_ref):
    lane = lax.broadcasted_iota(jnp.int32, (q_ref.shape[1], 128), 1)
    for p in range(NA_DIM // 128):
        lanes = slice(p * 128, (p + 1) * 128)
        o_ref[0, :, lanes] = _attend_pair(q_ref[0, :, lanes], [(k_ref[0, :, lanes], v_ref[0, :, lanes], None)], lane)


def _na_context(q, k, v):
    B, Lc, d = q.shape
    blk = pl.BlockSpec((1, Lc, d), lambda b: (b, 0, 0))
    return pl.pallas_call(
        _na_context_kernel,
        grid=(B,),
        in_specs=[blk, blk, blk],
        out_specs=blk,
        out_shape=jax.ShapeDtypeStruct((B, Lc, d), jnp.float32),
        compiler_params=pltpu.CompilerParams(dimension_semantics=("parallel",)),
        name="na_context",
    )(q, k, v)


def _pool_matrices(T):
    tb, halo = POOL_BLOCK, POOL_HALO
    mats = np.zeros((len(POOL_WINDOWS), 3, tb, tb + 2 * halo), np.float32)
    for wi, w in enumerate(POOL_WINDOWS):
        for kind, start in enumerate((0, tb, T - tb)):
            for i in range(tb):
                t = start + i
                lo = min(max(t - w // 2, 0), T - 1)
                hi = min(max(t + (w - w // 2 - 1), 0), T - 1)
                mats[wi, kind, i, lo - start + halo:hi - start + halo + 1] = 1.0 / (hi - lo + 1)
                mats[wi, kind, i, i + halo] -= 1.0
    return jnp.asarray(mats)


def _pool_kernel(u_ref, a_ref, w_ref, s_ref, o_ref):
    T = u_ref.shape[1]
    tb, halo = POOL_BLOCK, POOL_HALO
    nb = T // tb
    group = lax.broadcasted_iota(jnp.int32, (tb, POOL_DIM), 1) // POOL_GROUP_DIM
    zeros = jnp.zeros((halo, POOL_DIM), jnp.float32)
    for blk in range(nb):
        kind = 0 if blk == 0 else (2 if blk == nb - 1 else 1)
        before = zeros if blk == 0 else u_ref[0, blk * tb - halo:blk * tb, :]
        after = zeros if blk == nb - 1 else u_ref[0, (blk + 1) * tb:(blk + 1) * tb + halo, :]
        xcat = jnp.concatenate([before, u_ref[0, blk * tb:(blk + 1) * tb, :], after], axis=0)
        pooled = None
        for wi in range(len(POOL_WINDOWS)):
            pw = jnp.dot(a_ref[wi, kind], xcat, preferred_element_type=jnp.float32, precision=lax.Precision.HIGHEST)
            pooled = pw if pooled is None else jnp.where(group == wi, pw, pooled)
        mixed = jnp.dot(pooled.astype(jnp.bfloat16), w_ref[...], preferred_element_type=jnp.float32)
        o_ref[0, blk * tb:(blk + 1) * tb, :] = mixed * s_ref[...]


def _pool_mix(u, pool_w, pool_scale):
    B, T, d = u.shape
    assert T % POOL_BLOCK == 0 and T >= 2 * POOL_BLOCK
    w_bd = jax.scipy.linalg.block_diag(*[pool_w[g] for g in range(POOL_GROUPS)]).astype(jnp.bfloat16)
    mats = _pool_matrices(T)
    blk = pl.BlockSpec((1, T, d), lambda b: (b, 0, 0))
    return pl.pallas_call(
        _pool_kernel,
        grid=(B,),
        in_specs=[blk, pl.BlockSpec(mats.shape, lambda b: (0, 0, 0, 0)), pl.BlockSpec((d, d), lambda b: (0, 0)),
                  pl.BlockSpec((1, d), lambda b: (0, 0))],
        out_specs=blk,
        out_shape=jax.ShapeDtypeStruct((B, T, d), jnp.float32),
        compiler_params=pltpu.CompilerParams(dimension_semantics=("parallel",)),
        name="pool_mix",
    )(u, mats, w_bd, pool_scale[None, :])


def _log_sigmoid(x):
    return jnp.minimum(x, 0.0) - jnp.log1p(jnp.exp(-jnp.abs(x)))


def _ml_prep(q, k, conv_ref, cos_ref, sin_ref):
    T = q.shape[0]
    row = lax.broadcasted_iota(jnp.int32, (T, 1), 0)
    lane = lax.broadcasted_iota(jnp.int32, (1, ML_DIM), 1) % (HEAD_DIM // 2)
    outs = []
    for idx, x in enumerate((q, k)):
        acc = None
        for j in range(ML_CONV_W):
            d = j - ML_CONV_W // 2
            w = conv_ref[j:j + 1, idx * ML_DIM:(idx + 1) * ML_DIM]
            if d == 0:
                term = x * w
            else:
                shifted = pltpu.roll(x, (-d) % T, 0)
                term = jnp.where((row + d >= 0) & (row + d < T), shifted, 0.0) * w
            acc = term if acc is None else acc + term
        y = acc * jax.nn.sigmoid(acc)
        if cos_ref is not None:
            quarter = HEAD_DIM // 4
            partner = jnp.where(lane < quarter, pltpu.roll(y, ML_DIM - quarter, 1), pltpu.roll(y, quarter, 1))
            y = y * cos_ref[...] + partner * sin_ref[...]
        outs.append(y)
    return outs


def _ml_chain(qs, kts, vs, rows, gi_c, gi_r, bc, br, ci, ii, head, backward, cn_ref, m_ref, slot, lane, tri):
    L = ML_L
    pair = slice((head // 2) * 128, (head // 2) * 128 + 128)
    own = (lane < HEAD_DIM) if head % 2 == 0 else (lane >= HEAD_DIM)
    one_lane = HEAD_DIM if head % 2 == 0 else 0
    q2 = jnp.where(own, qs[rows, pair], jnp.zeros((), jnp.bfloat16))
    v1 = jnp.where(own, vs[rows, pair].astype(jnp.float32), jnp.where(lane == one_lane, 1.0, 0.0))
    own_rows = lax.broadcasted_iota(jnp.int32, (128, L), 0)
    own_rows = (own_rows < HEAD_DIM) if head % 2 == 0 else (own_rows >= HEAD_DIM)
    kt2 = jnp.where(own_rows, kts[pair, rows], jnp.zeros((), jnp.bfloat16))
    m_prev = m_ref[slot]
    cn = cn_ref[slot]
    bcol = bc[:, ci:ci + 1]
    brow = br[ci:ci + 1, :]
    log_d = jnp.where(tri, bcol - brow + gi_r[ii:ii + 1, :], -jnp.inf)
    inter = bcol + m_prev
    m_t = jnp.maximum(inter, jnp.max(log_d, axis=-1, keepdims=True))
    s = jnp.dot(q2, kt2, preferred_element_type=jnp.float32)
    w_intra = (s * jnp.exp(log_d - m_t)).astype(jnp.bfloat16)
    nd = (jnp.dot(w_intra, v1.astype(jnp.bfloat16), preferred_element_type=jnp.float32)
          + jnp.exp(inter - m_t) * jnp.dot(q2, cn.astype(jnp.bfloat16), preferred_element_type=jnp.float32))
    den = nd[:, one_lane:one_lane + 1]
    h = nd / jnp.maximum(jnp.abs(den), jnp.exp(-m_t))
    b_last = bc[0:1, ci:ci + 1] if backward else bc[L - 1:L, ci:ci + 1]
    log_w = b_last - bcol + gi_c[:, ii:ii + 1]
    m_new = jnp.maximum(b_last + m_prev, jnp.max(log_w, axis=0, keepdims=True))
    wv = (jnp.exp(log_w - m_new) * v1).astype(jnp.bfloat16)
    cn_ref[slot] = jnp.exp(b_last + m_prev - m_new) * cn + jnp.dot(kt2, wv, preferred_element_type=jnp.float32)
    m_ref[slot] = m_new
    return h


def _ml_scan(T, qs, kts, vs, gc_ref, gr_ref, gbc, gbr, hf, hb, cn_ref, m_ref, tril, triu):
    L = ML_L
    nc = T // L
    lane = lax.broadcasted_iota(jnp.int32, (L, 128), 1)
    t_idx = lax.broadcasted_iota(jnp.int32, (L, L), 0)
    s_idx = lax.broadcasted_iota(jnp.int32, (L, L), 1)
    hi = lax.Precision.HIGHEST

    def body(c, carry):
        for backward in (False, True):
            cc = (nc - 1 - c) if backward else c
            rows = pl.ds(pl.multiple_of(cc * L, L), L)
            gc = gc_ref[0, rows, :] + gbc
            gr = gr_ref[0, :, rows] + gbr
            bc = jnp.dot(triu if backward else tril, _log_sigmoid(gc), preferred_element_type=jnp.float32, precision=hi)
            br = jnp.dot(_log_sigmoid(gr), tril if backward else triu, preferred_element_type=jnp.float32, precision=hi)
            tri = (s_idx >= t_idx) if backward else (s_idx <= t_idx)
            out = hb if backward else hf
            base = 2 * ML_HEADS if backward else 0
            for pair in range(ML_HEADS // 2):
                hs = []
                for head in (2 * pair, 2 * pair + 1):
                    hs.append(_ml_chain(qs, kts, vs, rows, gc, gr, bc, br, base + ML_HEADS + head, base + head, head,
                                        backward, cn_ref, m_ref, (ML_HEADS if backward else 0) + head, lane, tri))
                out[rows, pair * 128:(pair + 1) * 128] = jnp.where(lane < HEAD_DIM, hs[0], hs[1])
        return carry

    lax.fori_loop(0, nc, body, 0)


def _mlstm_kernel(q_ref, k_ref, v_ref, o_ref, gc_ref, gr_ref, cq_ref, ck_ref, cv_ref, co_ref, cgc_ref, cgr_ref,
                  cos_ref, sin_ref, conv_ref, gbc_ref, gbr_ref, ng_ref, e_ref, tril_ref, triu_ref,
                  y_ref, cy_ref, qs, kts, vs, hf, hb, cqs, ckts, cvs, chf, chb, cn_ref, m_ref):
    cn_ref[...] = jnp.zeros_like(cn_ref)
    m_ref[...] = jnp.zeros_like(m_ref)
    tril, triu = tril_ref[...], triu_ref[...]

    def stage(q_r, k_r, v_r, rope, qd, ktd, vd):
        qp, kp = _ml_prep(q_r[0], k_r[0], conv_ref, cos_ref if rope else None, sin_ref if rope else None)
        qd[...] = qp.astype(jnp.bfloat16)
        ktd[...] = (kp * HEAD_DIM ** -0.5).T.astype(jnp.bfloat16)
        vd[...] = v_r[0].astype(jnp.bfloat16)

    def finish(h_f, h_b, o_r, y_r):
        h = h_f[...] + h_b[...]
        ss = jnp.dot(h * h, e_ref[...], preferred_element_type=jnp.float32, precision=lax.Precision.HIGHEST)
        y_r[0] = h * lax.rsqrt(ss * (1.0 / HEAD_DIM) + EPS) * ng_ref[...] * jax.nn.sigmoid(o_r[0])

    gbc, gbr = gbc_ref[...], gbr_ref[...]
    stage(cq_ref, ck_ref, cv_ref, False, cqs, ckts, cvs)
    _ml_scan(cq_ref.shape[1], cqs, ckts, cvs, cgc_ref, cgr_ref, gbc, gbr, chf, chb, cn_ref, m_ref, tril, triu)
    finish(chf, chb, co_ref, cy_ref)
    stage(q_ref, k_ref, v_ref, True, qs, kts, vs)
    _ml_scan(q_ref.shape[1], qs, kts, vs, gc_ref, gr_ref, gbc, gbr, hf, hb, cn_ref, m_ref, tril, triu)
    finish(hf, hb, o_ref, y_ref)


def _rope_tables(T):
    t = jnp.arange(T)
    row = (t // GRID_W).astype(jnp.float32)
    col = (t % GRID_W).astype(jnp.float32)
    nf = HEAD_DIM // 4
    inv = ROPE_BASE ** (-jnp.arange(nf, dtype=jnp.float32) / nf)
    cr, sr, cc, sc = jnp.cos(row[:, None] * inv), jnp.sin(row[:, None] * inv), jnp.cos(col[:, None] * inv), \
        jnp.sin(col[:, None] * inv)
    cos = jnp.concatenate([cr, cr, cc, cc], axis=-1)
    sin = jnp.concatenate([-sr, sr, -sc, sc], axis=-1)
    return jnp.tile(cos, (1, ML_HEADS)), jnp.tile(sin, (1, ML_HEADS))


def _mlstm_mix(lq, lk, lv, lo, lg, lgr, cq, ck, cv, co, cg, cgr, conv_w, gate_b, norm_g):
    B, T, d = lq.shape
    Lc = cq.shape[1]
    L = ML_L
    assert T % L == 0 and Lc % L == 0
    cos, sin = _rope_tables(T)
    lane_head = np.arange(d) // HEAD_DIM
    same_head = jnp.asarray((lane_head[:, None] == lane_head[None, :]).astype(np.float32))
    tril = jnp.asarray(np.tril(np.ones((L, L), np.float32)))
    seq = lambda n: pl.BlockSpec((1, n, d), lambda b: (b, 0, 0))
    full = lambda a: pl.BlockSpec(a.shape, lambda b: (0,) * a.ndim)
    consts = [cos, sin, conv_w, gate_b[None, :], gate_b[:, None], norm_g[None, :], same_head, tril, tril.T]
    f32, bf16 = jnp.float32, jnp.bfloat16
    return pl.pallas_call(
        _mlstm_kernel,
        grid=(B,),
        in_specs=[seq(T)] * 4 + [pl.BlockSpec((1, T, ML_N_GATES), lambda b: (b, 0, 0)),
                                 pl.BlockSpec((1, ML_N_GATES, T), lambda b: (b, 0, 0))]
        + [seq(Lc)] * 4 + [pl.BlockSpec((1, Lc, ML_N_GATES), lambda b: (b, 0, 0)),
                           pl.BlockSpec((1, ML_N_GATES, Lc), lambda b: (b, 0, 0))]
        + [full(a) for a in consts],
        out_specs=[seq(T), seq(Lc)],
        out_shape=[jax.ShapeDtypeStruct((B, T, d), f32), jax.ShapeDtypeStruct((B, Lc, d), f32)],
        scratch_shapes=[pltpu.VMEM((T, d), bf16), pltpu.VMEM((d, T), bf16), pltpu.VMEM((T, d), bf16),
                        pltpu.VMEM((T, d), f32), pltpu.VMEM((T, d), f32),
                        pltpu.VMEM((Lc, d), bf16), pltpu.VMEM((d, Lc), bf16), pltpu.VMEM((Lc, d), bf16),
                        pltpu.VMEM((Lc, d), f32), pltpu.VMEM((Lc, d), f32),
                        pltpu.VMEM((2 * ML_HEADS, 128, 128), f32), pltpu.VMEM((2 * ML_HEADS, 1, 1), f32)],
        compiler_params=pltpu.CompilerParams(dimension_semantics=("parallel",), vmem_limit_bytes=VMEM_LIMIT_BYTES),
        name="mlstm",
    )(lq, lk, lv, lo, lg, lgr, cq, ck, cv, co, cg, cgr, *consts)


def _ada_kernel(c_ref, w_ref, b_ref, o_ref):
    c = c_ref[...]
    act = (c * jax.nn.sigmoid(c)).astype(jnp.bfloat16)
    o_ref[...] = jnp.dot(act, w_ref[0].astype(jnp.bfloat16), preferred_element_type=jnp.float32) + b_ref[0]


def _ada_mod(cond, w_all, b_all, layer):
    rows, d = cond.shape
    n = w_all.shape[2]
    tile = ADA_COL_TILE
    assert n % tile == 0
    return pl.pallas_call(
        _ada_kernel,
        grid=(n // tile,),
        in_specs=[pl.BlockSpec((rows, d), lambda j: (0, 0)), pl.BlockSpec((1, d, tile), lambda j: (layer, 0, j)),
                  pl.BlockSpec((1, 1, tile), lambda j: (layer, 0, j))],
        out_specs=pl.BlockSpec((rows, tile), lambda j: (0, j)),
        out_shape=jax.ShapeDtypeStruct((rows, n), jnp.float32),
        compiler_params=pltpu.CompilerParams(dimension_semantics=("parallel",)),
        name="ada_mod",
    )(cond, w_all, b_all[:, None, :])


_IN_SPLITS = (("q", NA_DIM), ("k", NA_DIM), ("v", NA_DIM), ("pool", POOL_DIM), ("mq", ML_DIM), ("mk", ML_DIM),
              ("mv", ML_DIM), ("mo", ML_DIM), ("gates", ML_N_GATES))
IN_COLS = sum(n for _, n in _IN_SPLITS)
IN_COLS_PAD = -(-IN_COLS // 128) * 128


def _in_proj_kernel(x_ref, sc_ref, sh_ref, g_ref, w_ref, wgt_ref, gq_ref, gk_ref, e_ref,
                    q_ref, k_ref, v_ref, pool_ref, mq_ref, mk_ref, mv_ref, mo_ref, gc_ref, gr_ref):
    x = x_ref[0]
    y = x * lax.rsqrt(jnp.mean(x * x, axis=-1, keepdims=True) + EPS) * g_ref[...]
    xn = (y * (1.0 + sc_ref[0]) + sh_ref[0]).astype(jnp.bfloat16)
    p = jnp.dot(xn, w_ref[...], preferred_element_type=jnp.float32)
    off = {}
    o = 0
    for name, n in _IN_SPLITS:
        off[name] = slice(o, o + n)
        o += n

    def head_norm(u, gain):
        sq = u * u
        hi = sq.astype(jnp.bfloat16)
        lo = (sq - hi.astype(jnp.float32)).astype(jnp.bfloat16)
        ss = (jnp.dot(hi, e_ref[...], preferred_element_type=jnp.float32)
              + jnp.dot(lo, e_ref[...], preferred_element_type=jnp.float32))
        return u * lax.rsqrt(ss * (1.0 / HEAD_DIM) + EPS) * gain

    q_ref[0] = (head_norm(p[:, off["q"]], gq_ref[...]) * HEAD_DIM ** -0.5).astype(jnp.bfloat16)
    k_ref[0] = head_norm(p[:, off["k"]], gk_ref[...]).astype(jnp.bfloat16)
    v_ref[0] = p[:, off["v"]].astype(jnp.bfloat16)
    pool_ref[0] = p[:, off["pool"]]
    mq_ref[0] = p[:, off["mq"]]
    mk_ref[0] = p[:, off["mk"]]
    mv_ref[0] = p[:, off["mv"]]
    mo_ref[0] = p[:, off["mo"]]
    gc_ref[0] = p[:, off["gates"]]
    gr_ref[0] = lax.dot_general(wgt_ref[...], xn, (((1,), (1,)), ((), ())), preferred_element_type=jnp.float32)


def _in_proj(x3, sc, sh, norm_g, w_in, gq, gk):
    B, T, d = x3.shape
    tn = min(IN_TOKEN_TILE, T)
    assert T % tn == 0
    w_bf = jnp.pad(w_in, ((0, 0), (0, IN_COLS_PAD - IN_COLS))).astype(jnp.bfloat16)
    wgt_bf = w_in[:, IN_COLS - ML_N_GATES:].T.astype(jnp.bfloat16)
    lane_head = np.arange(NA_DIM) // HEAD_DIM
    same_head = jnp.asarray((lane_head[:, None] == lane_head[None, :]).astype(np.float32)).astype(jnp.bfloat16)
    mod = pl.BlockSpec((1, 1, d), lambda b, t: (b, 0, 0))
    full = lambda a: pl.BlockSpec(a.shape, lambda b, t: (0,) * a.ndim)
    tok = lambda n: pl.BlockSpec((1, tn, n), lambda b, t: (b, t, 0))
    shp = lambda n, dt: jax.ShapeDtypeStruct((B, T, n), dt)
    f32, bf16 = jnp.float32, jnp.bfloat16
    consts = [norm_g[None, :], w_bf, wgt_bf, jnp.tile(gq, NA_HEADS)[None, :], jnp.tile(gk, NA_HEADS)[None, :], same_head]
    outs = pl.pallas_call(
        _in_proj_kernel,
        grid=(B, T // tn),
        in_specs=[pl.BlockSpec((1, tn, d), lambda b, t: (b, t, 0)), mod, mod] + [full(a) for a in consts],
        out_specs=[tok(NA_DIM)] * 3 + [tok(POOL_DIM)] + [tok(ML_DIM)] * 4 + [tok(ML_N_GATES)]
        + [pl.BlockSpec((1, ML_N_GATES, tn), lambda b, t: (b, 0, t))],
        out_shape=[shp(NA_DIM, bf16)] * 3 + [shp(POOL_DIM, f32)] + [shp(ML_DIM, f32)] * 4 + [shp(ML_N_GATES, f32)]
        + [jax.ShapeDtypeStruct((B, ML_N_GATES, T), f32)],
        compiler_params=pltpu.CompilerParams(
            dimension_semantics=("parallel", "parallel"), vmem_limit_bytes=VMEM_LIMIT_BYTES),
        name="in_proj",
    )(x3, sc, sh, *consts)
    return dict(zip(("q", "k", "v", "pool", "mq", "mk", "mv", "mo", "gc", "gr"), outs))


def _out_proj_kernel(x_ref, g_ref, na_ref, pool_ref, ml_ref, w_ref, o_ref):
    bf = jnp.bfloat16
    y = jnp.dot(na_ref[0].astype(bf), w_ref[0:NA_DIM, :], preferred_element_type=jnp.float32)
    y += jnp.dot(pool_ref[0].astype(bf), w_ref[NA_DIM:NA_DIM + POOL_DIM, :], preferred_element_type=jnp.float32)
    y += jnp.dot(ml_ref[0].astype(bf), w_ref[NA_DIM + POOL_DIM:, :], preferred_element_type=jnp.float32)
    o_ref[0] = x_ref[0] + g_ref[0] * y


def _out_proj(x3, gate, na, pool, ml, w_out):
    B, T, d = x3.shape
    tn = min(IN_TOKEN_TILE, T)
    tok = lambda n: pl.BlockSpec((1, tn, n), lambda b, t: (b, t, 0))
    return pl.pallas_call(
        _out_proj_kernel,
        grid=(B, T // tn),
        in_specs=[tok(d), pl.BlockSpec((1, 1, d), lambda b, t: (b, 0, 0)), tok(NA_DIM), tok(POOL_DIM), tok(ML_DIM),
                  pl.BlockSpec(w_out.shape, lambda b, t: (0, 0))],
        out_specs=tok(d),
        out_shape=jax.ShapeDtypeStruct(x3.shape, x3.dtype),
        compiler_params=pltpu.CompilerParams(dimension_semantics=("parallel", "parallel")),
        name="out_proj",
    )(x3, gate, na, pool, ml, w_out.astype(jnp.bfloat16))


def kernel(x, c, ctx, c_ctx, w_ada, b_ada, norm1_g, w_in, ml_gate_b, na_q_g, na_k_g, na_rpb, pool_w,
           pool_scale, ml_conv, ml_norm_g, w_out, norm2_g, peer_wq, peer_keys, peer_u, peer_v):
    depth = w_in.shape[0]
    B, S, D = x.shape
    Lc = ctx.shape[1]
    hc = ctx
    for l in range(depth):
        need_ctx = l < depth - 1
        u_bf = peer_u[l].astype(jnp.bfloat16)
        v_bf = peer_v[l].astype(jnp.bfloat16)
        wqt_bf = peer_wq[l].astype(jnp.bfloat16).T
        keys_bf = peer_keys[l].astype(jnp.bfloat16).reshape(2 * PEER_HEADS, PEER_KEYS, PEER_DKEY // 2)
        mod = _ada_mod(jnp.concatenate([c, c_ctx[None, :]], axis=0), w_ada, b_ada, l).reshape(B + 1, 6, 1, D)
        sh1, sc1, g1, sh2, sc2, g2 = (mod[:B, i] for i in range(6))
        csh1, csc1, cg1, csh2, csc2, cg2 = (mod[B:, i] for i in range(6))
        hc1 = hc.reshape(1, B * Lc, D)
        a = _in_proj(x, sc1, sh1, norm1_g[l], w_in[l], na_q_g[l], na_k_g[l])
        ac = _in_proj(hc1, csc1, csh1, norm1_g[l], w_in[l], na_q_g[l], na_k_g[l])
        ac = {k: v.reshape(B, Lc, v.shape[-1]) for k, v in ac.items() if k != "gr"} | {
            "gr": jnp.transpose(ac["gr"].reshape(ML_N_GATES, B, Lc), (1, 0, 2))}
        na_lat = _na_latent(a["q"], a["k"], a["v"], ac["k"], ac["v"], na_rpb[l])
        pool_lat = _pool_mix(a["pool"], pool_w[l], pool_scale[l])
        ml_lat, ml_ctx = _mlstm_mix(a["mq"], a["mk"], a["mv"], a["mo"], a["gc"], a["gr"],
                                    ac["mq"], ac["mk"], ac["mv"], ac["mo"], ac["gc"], ac["gr"],
                                    ml_conv[l], ml_gate_b[l], ml_norm_g[l])
        x = _out_proj(x, g1, na_lat, pool_lat, ml_lat, w_out[l])
        x = _peer_block(x, sc2, sh2, g2, norm2_g[l], wqt_bf, keys_bf, u_bf, v_bf)
        if need_ctx:
            na_ctx = _na_context(ac["q"], ac["k"], ac["v"])
            pool_ctx = _pool_mix(ac["pool"], pool_w[l], pool_scale[l])
            flat = lambda t: t.reshape(1, B * Lc, t.shape[-1])
            hc1 = _out_proj(hc1, cg1, flat(na_ctx), flat(pool_ctx), flat(ml_ctx), w_out[l])
            hc = _peer_block(hc1, csc2, csh2, cg2, norm2_g[l], wqt_bf, keys_bf, u_bf, v_bf).reshape(B, Lc, D)
    return x
```

```python
import functools

import numpy as np
import jax
import jax.numpy as jnp
from jax import lax
from jax.experimental import pallas as pl
from jax.experimental.pallas import tpu as pltpu

D_MODEL = 1024
GRID_W = 64
HEAD_DIM = 64
NA_DIM = D_MODEL // 2
NA_HEADS = NA_DIM // HEAD_DIM
NA_WIN_ROWS = 8
NA_WIN_COLS = 16
POOL_DIM = D_MODEL // 4
POOL_WINDOWS = (2, 4, 8, 16)
POOL_GROUPS = len(POOL_WINDOWS)
POOL_GROUP_DIM = POOL_DIM // POOL_GROUPS
ML_DIM = D_MODEL // 4
ML_HEADS = ML_DIM // HEAD_DIM
ML_CONV_W = 5
ML_N_GATES = 4 * ML_HEADS
PEER_KEYS = 128
PEER_EXPERTS = PEER_KEYS * PEER_KEYS
PEER_HEADS = 8
PEER_TOPK = 16
PEER_DKEY = 256
ROPE_BASE = 10000.0
EPS = 1e-6

PEER_TOKEN_TILE = 1024
PEER_EXPERT_TILE = 1024
ROUTE_TOKEN_TILE = 512
ROUTE_LANE_CHUNK = 256
IN_TOKEN_TILE = 256
NA_ROWS_PER_STEP = 4
ADA_COL_TILE = 1536
CAST_ROW_TILE = 1024
POOL_BLOCK = 128
POOL_HALO = 8
ML_L = 256
VMEM_LIMIT_BYTES = 56 * 1024 * 1024

_GELU_C0 = float(np.sqrt(2.0 / np.pi))
_GELU_C1 = 0.044715 * _GELU_C0


def _gelu_tanh(x):
    inner = x * (_GELU_C0 + _GELU_C1 * (x * x))
    hx = 0.5 * x
    return hx + hx * jnp.tanh(inner)


def _peer_dense_kernel(x_ref, u_ref, v_ref, cnt_ref, e1_ref, rank_ref, e2_ref, o_ref):
    j = pl.program_id(1)

    @pl.when(j == 0)
    def _():
        o_ref[...] = jnp.zeros_like(o_ref)

    te = u_ref.shape[0]
    act = lax.dot_general(u_ref[...], x_ref[...], (((1,), (1,)), ((), ())),
                          preferred_element_type=jnp.float32)
    gates = []
    for k in range(te // PEER_KEYS):
        g = None
        for h in range(PEER_HEADS):
            cnt = cnt_ref[h, k:k + 1, :].astype(jnp.bfloat16)
            e1 = e1_ref[h, k:k + 1, :].astype(jnp.bfloat16)
            term = jnp.where(rank_ref[h] < cnt, e2_ref[h], jnp.zeros((), e2_ref.dtype)) * e1
            g = term if g is None else g + term
        gates.append(g)
    gate = jnp.concatenate(gates, axis=0)
    p = _gelu_tanh(act).astype(jnp.bfloat16) * gate
    o_ref[...] += lax.dot_general(v_ref[...], p, (((0,), (0,)), ((), ())), preferred_element_type=jnp.float32)


def _cast_kernel(x_ref, o_ref):
    o_ref[...] = x_ref[...].astype(o_ref.dtype)


def _cast_bf16(w):
    rows, cols = w.shape
    tr = CAST_ROW_TILE
    assert rows % tr == 0
    blk = pl.BlockSpec((tr, cols), lambda i: (i, 0))
    return pl.pallas_call(
        _cast_kernel, grid=(rows // tr,), in_specs=[blk], out_specs=blk,
        out_shape=jax.ShapeDtypeStruct(w.shape, jnp.bfloat16),
        compiler_params=pltpu.CompilerParams(dimension_semantics=("parallel",)),
        name="cast_bf16",
    )(w)


def _peer_dense(tok, u_bf, v_bf, layer, cnt, e1, rank, e2):
    n, d = tok.shape
    tn, te = PEER_TOKEN_TILE, PEER_EXPERT_TILE
    assert n % tn == 0 and PEER_EXPERTS % te == 0 and te % PEER_KEYS == 0
    first = layer * (PEER_EXPERTS // te)
    tab_a = pl.BlockSpec((PEER_HEADS, te // PEER_KEYS, tn), lambda i, j: (0, j, i))
    tab_b = pl.BlockSpec((PEER_HEADS, PEER_KEYS, tn), lambda i, j: (0, 0, i))
    return pl.pallas_call(
        _peer_dense_kernel,
        grid=(n // tn, PEER_EXPERTS // te),
        in_specs=[
            pl.BlockSpec((tn, d), lambda i, j: (i, 0)),
            pl.BlockSpec((te, d), lambda i, j: (first + j, 0)),
            pl.BlockSpec((te, d), lambda i, j: (first + j, 0)),
            tab_a, tab_a, tab_b, tab_b,
        ],
        out_specs=pl.BlockSpec((d, tn), lambda i, j: (0, i)),
        out_shape=jax.ShapeDtypeStruct((d, n), jnp.float32),
        compiler_params=pltpu.CompilerParams(
            dimension_semantics=("parallel", "arbitrary"), vmem_limit_bytes=VMEM_LIMIT_BYTES),
        name="peer_dense",
    )(tok, u_bf, v_bf, cnt, e1, rank, e2)


def _top16(works, want_ranks):
    lanes = works[0].shape[1]
    rows = lax.broadcasted_iota(jnp.int32, (PEER_TOPK, lanes), 0)
    vals = [jnp.zeros((PEER_TOPK, lanes), jnp.float32) for _ in works]
    ranks = [jnp.full(w.shape, float(PEER_TOPK), jnp.float32) if r else None for w, r in zip(works, want_ranks)]
    tops = [[] for _ in works]
    works = list(works)
    for i in range(PEER_TOPK):
        ms = [jnp.max(w, axis=0, keepdims=True) for w in works]
        hits = [w == m for w, m in zip(works, ms)]
        ranks = [None if r is None else jnp.where(hit, float(i), r) for r, hit in zip(ranks, hits)]
        works = [jnp.where(hit, -jnp.inf, w) for w, hit in zip(works, hits)]
        vals = [jnp.where(rows == i, m, v) for v, m in zip(vals, ms)]
        for t, m in zip(tops, ms):
            t.append(m)
    return tops, vals, ranks


def _peer_route_kernel(x_ref, sc_ref, sh_ref, g_ref, wqt_ref, keys_ref,
                       tok_ref, cnt_ref, e1_ref, rank_ref, e2_ref, s_scr):
    x = x_ref[0]
    y = x * lax.rsqrt(jnp.mean(x * x, axis=-1, keepdims=True) + EPS) * g_ref[...]
    tok = (y * (1.0 + sc_ref[0]) + sh_ref[0]).astype(jnp.bfloat16)
    tok_ref[...] = tok
    qt = lax.dot_general(wqt_ref[...], tok, (((1,), (1,)), ((), ())),
                         preferred_element_type=jnp.float32).astype(jnp.bfloat16)
    dk = PEER_DKEY // 2
    for hp in range(2 * PEER_HEADS):
        s_scr[hp] = jnp.dot(keys_ref[hp], qt[hp * dk:(hp + 1) * dk], preferred_element_type=jnp.float32)

    cw = ROUTE_LANE_CHUNK
    lane_chunks = x.shape[0] // cw
    row8 = lax.broadcasted_iota(jnp.int32, (8, cw), 0)

    def body(it, carry):
        h = it // lane_chunks
        lanes = pl.ds(pl.multiple_of((it % lane_chunks) * cw, cw), cw)
        s1 = s_scr[2 * h, :, lanes]
        s2 = s_scr[2 * h + 1, :, lanes]
        (t1, t2), (v1, v2), (_, rank) = _top16([s1, s2], [False, True])
        groups = [t1[0] + v2, t1[1] + v2[:8]]
        for i in range(2, 8):
            groups.append(jnp.where(row8 < PEER_TOPK // (i + 1), t1[i] + v2[:8], -jnp.inf))
        groups.append(v1[8:] + t2[0])
        (tp,), (vp,), _ = _top16([jnp.concatenate(groups, axis=0)], [False])
        tau = tp[PEER_TOPK - 1]
        z = jnp.sum(jnp.exp(vp - tp[0]), axis=0, keepdims=True)
        kept = jnp.zeros(v1.shape, jnp.float32)
        for jj in range(PEER_TOPK):
            kept = jnp.where(v1 + t2[jj] >= tau, float(jj + 1), kept)
        cnt = jnp.zeros(s1.shape, jnp.float32)
        for i in range(PEER_TOPK):
            cnt = jnp.where(s1 == t1[i], kept[i:i + 1], cnt)
        cnt_ref[h, :, lanes] = cnt
        e1_ref[h, :, lanes] = jnp.exp(s1 - t1[0])
        rank_ref[h, :, lanes] = rank.astype(jnp.bfloat16)
        e2_ref[h, :, lanes] = (jnp.exp(s2 - t2[0]) * (1.0 / z)).astype(jnp.bfloat16)
        return carry

    lax.fori_loop(0, PEER_HEADS * lane_chunks, body, 0)


def _peer_route(x3, sc, sh, g, wqt_bf, keys_bf):
    B, T, d = x3.shape
    tn = min(ROUTE_TOKEN_TILE, T)
    assert T % tn == 0 and tn % 128 == 0
    nt = T // tn
    n = B * T
    tab = pl.BlockSpec((PEER_HEADS, PEER_KEYS, tn), lambda b, t: (0, 0, b * nt + t))
    tab_shape = lambda dt: jax.ShapeDtypeStruct((PEER_HEADS, PEER_KEYS, n), dt)
    mod = pl.BlockSpec((1, 1, d), lambda b, t: (b, 0, 0))
    return pl.pallas_call(
        _peer_route_kernel,
        grid=(B, nt),
        in_specs=[
            pl.BlockSpec((1, tn, d), lambda b, t: (b, t, 0)),
            mod, mod,
            pl.BlockSpec((1, d), lambda b, t: (0, 0)),
            pl.BlockSpec(wqt_bf.shape, lambda b, t: (0, 0)),
            pl.BlockSpec(keys_bf.shape, lambda b, t: (0, 0, 0)),
        ],
        out_specs=[pl.BlockSpec((tn, d), lambda b, t: (b * nt + t, 0)), tab, tab, tab, tab],
        out_shape=[jax.ShapeDtypeStruct((n, d), jnp.bfloat16), tab_shape(jnp.float32), tab_shape(jnp.float32),
                   tab_shape(jnp.bfloat16), tab_shape(jnp.bfloat16)],
        scratch_shapes=[pltpu.VMEM((2 * PEER_HEADS, PEER_KEYS, tn), jnp.float32)],
        compiler_params=pltpu.CompilerParams(
            dimension_semantics=("parallel", "parallel"), vmem_limit_bytes=VMEM_LIMIT_BYTES),
        name="peer_route",
    )(x3, sc, sh, g, wqt_bf, keys_bf)


def _residual_t_kernel(x_ref, g_ref, yt_ref, o_ref):
    o_ref[0] = x_ref[0] + g_ref[0] * yt_ref[...].T


def _residual_t(x3, gate, yt):
    B, T, d = x3.shape
    tn = min(ROUTE_TOKEN_TILE, T)
    nt = T // tn
    return pl.pallas_call(
        _residual_t_kernel,
        grid=(B, nt),
        in_specs=[pl.BlockSpec((1, tn, d), lambda b, t: (b, t, 0)),
                  pl.BlockSpec((1, 1, d), lambda b, t: (b, 0, 0)),
                  pl.BlockSpec((d, tn), lambda b, t: (0, b * nt + t))],
        out_specs=pl.BlockSpec((1, tn, d), lambda b, t: (b, t, 0)),
        out_shape=jax.ShapeDtypeStruct(x3.shape, x3.dtype),
        compiler_params=pltpu.CompilerParams(dimension_semantics=("parallel", "parallel")),
        name="residual_t",
    )(x3, gate, yt)


def _peer_block(x3, sc, sh, gate, norm_g, wqt_bf, keys_bf, u_bf, v_bf, layer):
    tok, cnt, e1, rank, e2 = _peer_route(x3, sc, sh, norm_g[None, :], wqt_bf, keys_bf)
    out_t = _peer_dense(tok, u_bf, v_bf, layer, cnt, e1, rank, e2)
    return _residual_t(x3, gate, out_t)


def _attend_pair(q2, segments, lane):
    zero = jnp.zeros((), q2.dtype)
    outs = []
    for half in range(2):
        own = (lane < HEAD_DIM) if half == 0 else (lane >= HEAD_DIM)
        qh = jnp.where(own, q2, zero)
        scores = []
        for k2, _, bias in segments:
            s = lax.dot_general(qh, k2, (((1,), (1,)), ((), ())), preferred_element_type=jnp.float32)
            scores.append(s if bias is None else s + bias[half])
        m = functools.reduce(jnp.maximum, [jnp.max(s, axis=-1, keepdims=True) for s in scores])
        ps = [jnp.exp(s - m) for s in scores]
        denom = functools.reduce(jnp.add, [jnp.sum(p, axis=-1, keepdims=True) for p in ps])
        acc = functools.reduce(jnp.add, [jnp.dot(p.astype(v2.dtype), v2, preferred_element_type=jnp.float32)
                                         for p, (_, v2, _) in zip(ps, segments)])
        outs.append(acc / denom)
    return jnp.where(lane < HEAD_DIM, outs[0], outs[1])


def _na_latent_kernel(q_ref, k_ref, v_ref, kc_ref, vc_ref, bias_ref, o_ref):
    n_rows = k_ref.shape[1] // GRID_W
    union = bias_ref.shape[3] // GRID_W
    u0 = _na_union_start(pl.program_id(1), n_rows, union)
    keys = pl.ds(pl.multiple_of(u0 * GRID_W, GRID_W), union * GRID_W)
    lane = lax.broadcasted_iota(jnp.int32, (q_ref.shape[1], 128), 1)
    for p in range(NA_DIM // 128):
        lanes = slice(p * 128, (p + 1) * 128)
        segs = [(k_ref[0, keys, lanes], v_ref[0, keys, lanes], (bias_ref[0, 2 * p], bias_ref[0, 2 * p + 1])),
                (kc_ref[0, :, lanes], vc_ref[0, :, lanes], None)]
        o_ref[0, :, lanes] = _attend_pair(q_ref[0, :, lanes], segs, lane)


def _na_union_start(g, n_rows, union):
    lo = g * NA_ROWS_PER_STEP - NA_WIN_ROWS // 2
    if isinstance(g, int):
        return min(min(max(lo, 0), n_rows - NA_WIN_ROWS), n_rows - union)
    return jnp.minimum(jnp.clip(lo, 0, n_rows - NA_WIN_ROWS), n_rows - union)


def _na_group_bias(rpb, n_rows):
    G, KR = NA_ROWS_PER_STEP, NA_WIN_ROWS
    union = KR + G - 1
    n_groups = n_rows // G
    cq = np.arange(GRID_W)
    c0 = np.clip(cq - NA_WIN_COLS // 2, 0, GRID_W - NA_WIN_COLS)
    in_win = (cq[None, :] >= c0[:, None]) & (cq[None, :] < c0[:, None] + NA_WIN_COLS)
    bc = np.clip(cq[None, :] - cq[:, None] + (NA_WIN_COLS - 1), 0, 2 * NA_WIN_COLS - 2)

    def layout(g):
        r = g * G + np.arange(G)
        key_row = _na_union_start(g, n_rows, union) + np.arange(union)
        r0 = np.clip(r - KR // 2, 0, n_rows - KR)
        ok = (key_row[None, :] >= r0[:, None]) & (key_row[None, :] < r0[:, None] + KR)
        br = np.clip(key_row[None, :] - r[:, None] + (KR - 1), 0, 2 * KR - 2)
        return ok, br

    layouts = [layout(g) for g in range(n_groups)]
    same = lambda a, b: all(np.array_equal(x, y) for x, y in zip(a, b))
    assert n_groups >= 3 and all(same(layouts[g], layouts[1]) for g in range(1, n_groups - 1))
    toe = jnp.where(in_win[None, None], rpb[:, :, bc], -jnp.inf).astype(jnp.float32)
    tables = []
    for ok, br in (layouts[0], layouts[1], layouts[-1]):
        blocks = jnp.where(ok[None, :, :, None, None], toe[:, br], -jnp.inf)
        tables.append(jnp.transpose(blocks, (0, 1, 3, 2, 4)).reshape(NA_HEADS, G * GRID_W, union * GRID_W))
    return jnp.stack(tables, axis=0)


def _na_latent(q, k, v, k_ctx, v_ctx, rpb):
    B, S, d = q.shape
    Lc = k_ctx.shape[1]
    n_rows = S // GRID_W
    assert n_rows >= NA_WIN_ROWS + NA_ROWS_PER_STEP - 1 and n_rows % NA_ROWS_PER_STEP == 0
    n_groups = n_rows // NA_ROWS_PER_STEP
    bias = _na_group_bias(rpb, n_rows)
    tq = NA_ROWS_PER_STEP * GRID_W
    whole = lambda L: pl.BlockSpec((1, L, d), lambda b, g: (b, 0, 0))
    kind = lambda b, g: (jnp.where(g == 0, 0, jnp.where(g == n_groups - 1, 2, 1)), 0, 0, 0)
    return pl.pallas_call(
        _na_latent_kernel,
        grid=(B, n_groups),
        in_specs=[pl.BlockSpec((1, tq, d), lambda b, g: (b, g, 0)), whole(S), whole(S), whole(Lc), whole(Lc),
                  pl.BlockSpec((1,) + bias.shape[1:], kind)],
        out_specs=pl.BlockSpec((1, tq, d), lambda b, g: (b, g, 0)),
        out_shape=jax.ShapeDtypeStruct((B, S, d), jnp.float32),
        compiler_params=pltpu.CompilerParams(
            dimension_semantics=("parallel", "arbitrary"), vmem_limit_bytes=VMEM_LIMIT_BYTES),
        name="na_latent",
    )(q, k, v, k_ctx, v_ctx, bias)


def _na_context_kernel(q_ref, k_ref, v_ref, o_ref):
    lane = lax.broadcasted_iota(jnp.int32, (q_ref.shape[1], 128), 1)
    for p in range(NA_DIM // 128):
        lanes = slice(p * 128, (p + 1) * 128)
        o_ref[0, :, lanes] = _attend_pair(q_ref[0, :, lanes], [(k_ref[0, :, lanes], v_ref[0, :, lanes], None)], lane)


def _na_context(q, k, v):
    B, Lc, d = q.shape
    blk = pl.BlockSpec((1, Lc, d), lambda b: (b, 0, 0))
    return pl.pallas_call(
        _na_context_kernel,
        grid=(B,),
        in_specs=[blk, blk, blk],
        out_specs=blk,
        out_shape=jax.ShapeDtypeStruct((B, Lc, d), jnp.float32),
        compiler_params=pltpu.CompilerParams(dimension_semantics=("parallel",)),
        name="na_context",
    )(q, k, v)


def _pool_matrices(T):
    tb, halo = POOL_BLOCK, POOL_HALO
    mats = np.zeros((len(POOL_WINDOWS), 3, tb, tb + 2 * halo), np.float32)
    for wi, w in enumerate(POOL_WINDOWS):
        for kind, start in enumerate((0, tb, T - tb)):
            for i in range(tb):
                t = start + i
                lo = min(max(t - w // 2, 0), T - 1)
                hi = min(max(t + (w - w // 2 - 1), 0), T - 1)
                mats[wi, kind, i, lo - start + halo:hi - start + halo + 1] = 1.0 / (hi - lo + 1)
                mats[wi, kind, i, i + halo] -= 1.0
    return jnp.asarray(mats)


def _pool_kernel(u_ref, a_ref, w_ref, s_ref, o_ref):
    T = u_ref.shape[1]
    tb, halo = POOL_BLOCK, POOL_HALO
    nb = T // tb
    group = lax.broadcasted_iota(jnp.int32, (tb, POOL_DIM), 1) // POOL_GROUP_DIM
    zeros = jnp.zeros((halo, POOL_DIM), jnp.float32)
    for blk in range(nb):
        kind = 0 if blk == 0 else (2 if blk == nb - 1 else 1)
        before = zeros if blk == 0 else u_ref[0, blk * tb - halo:blk * tb, :]
        after = zeros if blk == nb - 1 else u_ref[0, (blk + 1) * tb:(blk + 1) * tb + halo, :]
        xcat = jnp.concatenate([before, u_ref[0, blk * tb:(blk + 1) * tb, :], after], axis=0)
        pooled = None
        for wi in range(len(POOL_WINDOWS)):
            pw = jnp.dot(a_ref[wi, kind], xcat, preferred_element_type=jnp.float32, precision=lax.Precision.HIGHEST)
            pooled = pw if pooled is None else jnp.where(group == wi, pw, pooled)
        mixed = jnp.dot(pooled.astype(jnp.bfloat16), w_ref[...], preferred_element_type=jnp.float32)
        o_ref[0, blk * tb:(blk + 1) * tb, :] = mixed * s_ref[...]


def _pool_mix(u, pool_w, pool_scale):
    B, T, d = u.shape
    assert T % POOL_BLOCK == 0 and T >= 2 * POOL_BLOCK
    w_bd = jax.scipy.linalg.block_diag(*[pool_w[g] for g in range(POOL_GROUPS)]).astype(jnp.bfloat16)
    mats = _pool_matrices(T)
    blk = pl.BlockSpec((1, T, d), lambda b: (b, 0, 0))
    return pl.pallas_call(
        _pool_kernel,
        grid=(B,),
        in_specs=[blk, pl.BlockSpec(mats.shape, lambda b: (0, 0, 0, 0)), pl.BlockSpec((d, d), lambda b: (0, 0)),
                  pl.BlockSpec((1, d), lambda b: (0, 0))],
        out_specs=blk,
        out_shape=jax.ShapeDtypeStruct((B, T, d), jnp.float32),
        compiler_params=pltpu.CompilerParams(dimension_semantics=("parallel",)),
        name="pool_mix",
    )(u, mats, w_bd, pool_scale[None, :])


def _log_sigmoid(x):
    return jnp.minimum(x, 0.0) - jnp.log1p(jnp.exp(-jnp.abs(x)))


def _ml_prep(q, k, conv_ref, cos_ref, sin_ref):
    T = q.shape[0]
    row = lax.broadcasted_iota(jnp.int32, (T, 1), 0)
    lane = lax.broadcasted_iota(jnp.int32, (1, ML_DIM), 1) % (HEAD_DIM // 2)
    outs = []
    for idx, x in enumerate((q, k)):
        acc = None
        for j in range(ML_CONV_W):
            d = j - ML_CONV_W // 2
            w = conv_ref[j:j + 1, idx * ML_DIM:(idx + 1) * ML_DIM]
            if d == 0:
                term = x * w
            else:
                shifted = pltpu.roll(x, (-d) % T, 0)
                term = jnp.where((row + d >= 0) & (row + d < T), shifted, 0.0) * w
            acc = term if acc is None else acc + term
        y = acc * jax.nn.sigmoid(acc)
        if cos_ref is not None:
            quarter = HEAD_DIM // 4
            partner = jnp.where(lane < quarter, pltpu.roll(y, ML_DIM - quarter, 1), pltpu.roll(y, quarter, 1))
            y = y * cos_ref[...] + partner * sin_ref[...]
        outs.append(y)
    return outs


def _ml_chain(qs, kts, vs, rows, gi_c, gi_r, bc, br, ci, ii, head, backward, cn_ref, m_ref, slot, lane, tri):
    L = ML_L
    pair = slice((head // 2) * 128, (head // 2) * 128 + 128)
    own = (lane < HEAD_DIM) if head % 2 == 0 else (lane >= HEAD_DIM)
    one_lane = HEAD_DIM if head % 2 == 0 else 0
    q2 = jnp.where(own, qs[rows, pair], jnp.zeros((), jnp.bfloat16))
    v1 = jnp.where(own, vs[rows, pair].astype(jnp.float32), jnp.where(lane == one_lane, 1.0, 0.0))
    own_rows = lax.broadcasted_iota(jnp.int32, (128, L), 0)
    own_rows = (own_rows < HEAD_DIM) if head % 2 == 0 else (own_rows >= HEAD_DIM)
    kt2 = jnp.where(own_rows, kts[pair, rows], jnp.zeros((), jnp.bfloat16))
    m_prev = m_ref[slot]
    cn = cn_ref[slot]
    bcol = bc[:, ci:ci + 1]
    brow = br[ci:ci + 1, :]
    log_d = jnp.where(tri, bcol - brow + gi_r[ii:ii + 1, :], -jnp.inf)
    inter = bcol + m_prev
    m_t = jnp.maximum(inter, jnp.max(log_d, axis=-1, keepdims=True))
    s = jnp.dot(q2, kt2, preferred_element_type=jnp.float32)
    w_intra = (s * jnp.exp(log_d - m_t)).astype(jnp.bfloat16)
    nd = (jnp.dot(w_intra, v1.astype(jnp.bfloat16), preferred_element_type=jnp.float32)
          + jnp.exp(inter - m_t) * jnp.dot(q2, cn.astype(jnp.bfloat16), preferred_element_type=jnp.float32))
    den = nd[:, one_lane:one_lane + 1]
    h = nd / jnp.maximum(jnp.abs(den), jnp.exp(-m_t))
    b_last = bc[0:1, ci:ci + 1] if backward else bc[L - 1:L, ci:ci + 1]
    log_w = b_last - bcol + gi_c[:, ii:ii + 1]
    m_new = jnp.maximum(b_last + m_prev, jnp.max(log_w, axis=0, keepdims=True))
    wv = (jnp.exp(log_w - m_new) * v1).astype(jnp.bfloat16)
    cn_ref[slot] = jnp.exp(b_last + m_prev - m_new) * cn + jnp.dot(kt2, wv, preferred_element_type=jnp.float32)
    m_ref[slot] = m_new
    return h


def _ml_scan(T, qs, kts, vs, gc_ref, gr_ref, gbc, gbr, hf, hb, cn_ref, m_ref, tril, triu):
    L = ML_L
    nc = T // L
    lane = lax.broadcasted_iota(jnp.int32, (L, 128), 1)
    t_idx = lax.broadcasted_iota(jnp.int32, (L, L), 0)
    s_idx = lax.broadcasted_iota(jnp.int32, (L, L), 1)
    hi = lax.Precision.HIGHEST

    def body(c, carry):
        for backward in (False, True):
            cc = (nc - 1 - c) if backward else c
            rows = pl.ds(pl.multiple_of(cc * L, L), L)
            gc = gc_ref[0, rows, :] + gbc
            gr = gr_ref[0, :, rows] + gbr
            bc = jnp.dot(triu if backward else tril, _log_sigmoid(gc), preferred_element_type=jnp.float32, precision=hi)
            br = jnp.dot(_log_sigmoid(gr), tril if backward else triu, preferred_element_type=jnp.float32, precision=hi)
            tri = (s_idx >= t_idx) if backward else (s_idx <= t_idx)
            out = hb if backward else hf
            base = 2 * ML_HEADS if backward else 0
            for pair in range(ML_HEADS // 2):
                hs = []
                for head in (2 * pair, 2 * pair + 1):
                    hs.append(_ml_chain(qs, kts, vs, rows, gc, gr, bc, br, base + ML_HEADS + head, base + head, head,
                                        backward, cn_ref, m_ref, (ML_HEADS if backward else 0) + head, lane, tri))
                out[rows, pair * 128:(pair + 1) * 128] = jnp.where(lane < HEAD_DIM, hs[0], hs[1])
        return carry

    lax.fori_loop(0, nc, body, 0)


def _mlstm_kernel(q_ref, k_ref, v_ref, o_ref, gc_ref, gr_ref, cq_ref, ck_ref, cv_ref, co_ref, cgc_ref, cgr_ref,
                  cos_ref, sin_ref, conv_ref, gbc_ref, gbr_ref, ng_ref, e_ref, tril_ref, triu_ref,
                  y_ref, cy_ref, qs, kts, vs, hf, hb, cqs, ckts, cvs, chf, chb, cn_ref, m_ref):
    cn_ref[...] = jnp.zeros_like(cn_ref)
    m_ref[...] = jnp.zeros_like(m_ref)
    tril, triu = tril_ref[...], triu_ref[...]

    def stage(q_r, k_r, v_r, rope, qd, ktd, vd):
        qp, kp = _ml_prep(q_r[0], k_r[0], conv_ref, cos_ref if rope else None, sin_ref if rope else None)
        qd[...] = qp.astype(jnp.bfloat16)
        ktd[...] = (kp * HEAD_DIM ** -0.5).T.astype(jnp.bfloat16)
        vd[...] = v_r[0].astype(jnp.bfloat16)

    def finish(h_f, h_b, o_r, y_r):
        h = h_f[...] + h_b[...]
        ss = jnp.dot(h * h, e_ref[...], preferred_element_type=jnp.float32, precision=lax.Precision.HIGHEST)
        y_r[0] = h * lax.rsqrt(ss * (1.0 / HEAD_DIM) + EPS) * ng_ref[...] * jax.nn.sigmoid(o_r[0])

    gbc, gbr = gbc_ref[...], gbr_ref[...]
    stage(cq_ref, ck_ref, cv_ref, False, cqs, ckts, cvs)
    _ml_scan(cq_ref.shape[1], cqs, ckts, cvs, cgc_ref, cgr_ref, gbc, gbr, chf, chb, cn_ref, m_ref, tril, triu)
    finish(chf, chb, co_ref, cy_ref)
    stage(q_ref, k_ref, v_ref, True, qs, kts, vs)
    _ml_scan(q_ref.shape[1], qs, kts, vs, gc_ref, gr_ref, gbc, gbr, hf, hb, cn_ref, m_ref, tril, triu)
    finish(hf, hb, o_ref, y_ref)


def _rope_tables(T):
    t = jnp.arange(T)
    row = (t // GRID_W).astype(jnp.float32)
    col = (t % GRID_W).astype(jnp.float32)
    nf = HEAD_DIM // 4
    inv = ROPE_BASE ** (-jnp.arange(nf, dtype=jnp.float32) / nf)
    cr, sr, cc, sc = jnp.cos(row[:, None] * inv), jnp.sin(row[:, None] * inv), jnp.cos(col[:, None] * inv), \
        jnp.sin(col[:, None] * inv)
    cos = jnp.concatenate([cr, cr, cc, cc], axis=-1)
    sin = jnp.concatenate([-sr, sr, -sc, sc], axis=-1)
    return jnp.tile(cos, (1, ML_HEADS)), jnp.tile(sin, (1, ML_HEADS))


def _mlstm_mix(lq, lk, lv, lo, lg, lgr, cq, ck, cv, co, cg, cgr, conv_w, gate_b, norm_g):
    B, T, d = lq.shape
    Lc = cq.shape[1]
    L = ML_L
    assert T % L == 0 and Lc % L == 0
    cos, sin = _rope_tables(T)
    lane_head = np.arange(d) // HEAD_DIM
    same_head = jnp.asarray((lane_head[:, None] == lane_head[None, :]).astype(np.float32))
    tril = jnp.asarray(np.tril(np.ones((L, L), np.float32)))
    seq = lambda n: pl.BlockSpec((1, n, d), lambda b: (b, 0, 0))
    full = lambda a: pl.BlockSpec(a.shape, lambda b: (0,) * a.ndim)
    consts = [cos, sin, conv_w, gate_b[None, :], gate_b[:, None], norm_g[None, :], same_head, tril, tril.T]
    f32, bf16 = jnp.float32, jnp.bfloat16
    return pl.pallas_call(
        _mlstm_kernel,
        grid=(B,),
        in_specs=[seq(T)] * 4 + [pl.BlockSpec((1, T, ML_N_GATES), lambda b: (b, 0, 0)),
                                 pl.BlockSpec((1, ML_N_GATES, T), lambda b: (b, 0, 0))]
        + [seq(Lc)] * 4 + [pl.BlockSpec((1, Lc, ML_N_GATES), lambda b: (b, 0, 0)),
                           pl.BlockSpec((1, ML_N_GATES, Lc), lambda b: (b, 0, 0))]
        + [full(a) for a in consts],
        out_specs=[seq(T), seq(Lc)],
        out_shape=[jax.ShapeDtypeStruct((B, T, d), f32), jax.ShapeDtypeStruct((B, Lc, d), f32)],
        scratch_shapes=[pltpu.VMEM((T, d), bf16), pltpu.VMEM((d, T), bf16), pltpu.VMEM((T, d), bf16),
                        pltpu.VMEM((T, d), f32), pltpu.VMEM((T, d), f32),
                        pltpu.VMEM((Lc, d), bf16), pltpu.VMEM((d, Lc), bf16), pltpu.VMEM((Lc, d), bf16),
                        pltpu.VMEM((Lc, d), f32), pltpu.VMEM((Lc, d), f32),
                        pltpu.VMEM((2 * ML_HEADS, 128, 128), f32), pltpu.VMEM((2 * ML_HEADS, 1, 1), f32)],
        compiler_params=pltpu.CompilerParams(dimension_semantics=("parallel",), vmem_limit_bytes=VMEM_LIMIT_BYTES),
        name="mlstm",
    )(lq, lk, lv, lo, lg, lgr, cq, ck, cv, co, cg, cgr, *consts)


def _ada_kernel(c_ref, w_ref, b_ref, o_ref):
    c = c_ref[...]
    act = (c * jax.nn.sigmoid(c)).astype(jnp.bfloat16)
    o_ref[...] = jnp.dot(act, w_ref[0].astype(jnp.bfloat16), preferred_element_type=jnp.float32) + b_ref[0]


def _ada_mod(cond, w_all, b_all, layer):
    rows, d = cond.shape
    n = w_all.shape[2]
    tile = ADA_COL_TILE
    assert n % tile == 0
    return pl.pallas_call(
        _ada_kernel,
        grid=(n // tile,),
        in_specs=[pl.BlockSpec((rows, d), lambda j: (0, 0)), pl.BlockSpec((1, d, tile), lambda j: (layer, 0, j)),
                  pl.BlockSpec((1, 1, tile), lambda j: (layer, 0, j))],
        out_specs=pl.BlockSpec((rows, tile), lambda j: (0, j)),
        out_shape=jax.ShapeDtypeStruct((rows, n), jnp.float32),
        compiler_params=pltpu.CompilerParams(dimension_semantics=("parallel",)),
        name="ada_mod",
    )(cond, w_all, b_all[:, None, :])


_IN_SPLITS = (("q", NA_DIM), ("k", NA_DIM), ("v", NA_DIM), ("pool", POOL_DIM), ("mq", ML_DIM), ("mk", ML_DIM),
              ("mv", ML_DIM), ("mo", ML_DIM), ("gates", ML_N_GATES))
IN_COLS = sum(n for _, n in _IN_SPLITS)
IN_COLS_PAD = -(-IN_COLS // 128) * 128


def _in_proj_kernel(x_ref, sc_ref, sh_ref, g_ref, w_ref, wgt_ref, gq_ref, gk_ref, e_ref,
                    q_ref, k_ref, v_ref, pool_ref, mq_ref, mk_ref, mv_ref, mo_ref, gc_ref, gr_ref):
    x = x_ref[0]
    y = x * lax.rsqrt(jnp.mean(x * x, axis=-1, keepdims=True) + EPS) * g_ref[...]
    xn = (y * (1.0 + sc_ref[0]) + sh_ref[0]).astype(jnp.bfloat16)
    p = jnp.dot(xn, w_ref[...], preferred_element_type=jnp.float32)
    off = {}
    o = 0
    for name, n in _IN_SPLITS:
        off[name] = slice(o, o + n)
        o += n

    def head_norm(u, gain):
        sq = u * u
        hi = sq.astype(jnp.bfloat16)
        lo = (sq - hi.astype(jnp.float32)).astype(jnp.bfloat16)
        ss = (jnp.dot(hi, e_ref[...], preferred_element_type=jnp.float32)
              + jnp.dot(lo, e_ref[...], preferred_element_type=jnp.float32))
        return u * lax.rsqrt(ss * (1.0 / HEAD_DIM) + EPS) * gain

    q_ref[0] = (head_norm(p[:, off["q"]], gq_ref[...]) * HEAD_DIM ** -0.5).astype(jnp.bfloat16)
    k_ref[0] = head_norm(p[:, off["k"]], gk_ref[...]).astype(jnp.bfloat16)
    v_ref[0] = p[:, off["v"]].astype(jnp.bfloat16)
    pool_ref[0] = p[:, off["pool"]]
    mq_ref[0] = p[:, off["mq"]]
    mk_ref[0] = p[:, off["mk"]]
    mv_ref[0] = p[:, off["mv"]]
    mo_ref[0] = p[:, off["mo"]]
    gc_ref[0] = p[:, off["gates"]]
    gr_ref[0] = lax.dot_general(wgt_ref[...], xn, (((1,), (1,)), ((), ())), preferred_element_type=jnp.float32)


def _in_proj(x3, sc, sh, norm_g, w_in, gq, gk):
    B, T, d = x3.shape
    tn = min(IN_TOKEN_TILE, T)
    assert T % tn == 0
    w_bf = jnp.pad(w_in, ((0, 0), (0, IN_COLS_PAD - IN_COLS))).astype(jnp.bfloat16)
    wgt_bf = w_in[:, IN_COLS - ML_N_GATES:].T.astype(jnp.bfloat16)
    lane_head = np.arange(NA_DIM) // HEAD_DIM
    same_head = jnp.asarray((lane_head[:, None] == lane_head[None, :]).astype(np.float32)).astype(jnp.bfloat16)
    mod = pl.BlockSpec((1, 1, d), lambda b, t: (b, 0, 0))
    full = lambda a: pl.BlockSpec(a.shape, lambda b, t: (0,) * a.ndim)
    tok = lambda n: pl.BlockSpec((1, tn, n), lambda b, t: (b, t, 0))
    shp = lambda n, dt: jax.ShapeDtypeStruct((B, T, n), dt)
    f32, bf16 = jnp.float32, jnp.bfloat16
    consts = [norm_g[None, :], w_bf, wgt_bf, jnp.tile(gq, NA_HEADS)[None, :], jnp.tile(gk, NA_HEADS)[None, :], same_head]
    outs = pl.pallas_call(
        _in_proj_kernel,
        grid=(B, T // tn),
        in_specs=[pl.BlockSpec((1, tn, d), lambda b, t: (b, t, 0)), mod, mod] + [full(a) for a in consts],
        out_specs=[tok(NA_DIM)] * 3 + [tok(POOL_DIM)] + [tok(ML_DIM)] * 4 + [tok(ML_N_GATES)]
        + [pl.BlockSpec((1, ML_N_GATES, tn), lambda b, t: (b, 0, t))],
        out_shape=[shp(NA_DIM, bf16)] * 3 + [shp(POOL_DIM, f32)] + [shp(ML_DIM, f32)] * 4 + [shp(ML_N_GATES, f32)]
        + [jax.ShapeDtypeStruct((B, ML_N_GATES, T), f32)],
        compiler_params=pltpu.CompilerParams(
            dimension_semantics=("parallel", "parallel"), vmem_limit_bytes=VMEM_LIMIT_BYTES),
        name="in_proj",
    )(x3, sc, sh, *consts)
    return dict(zip(("q", "k", "v", "pool", "mq", "mk", "mv", "mo", "gc", "gr"), outs))


def _out_proj_kernel(x_ref, g_ref, na_ref, pool_ref, ml_ref, w_ref, o_ref):
    bf = jnp.bfloat16
    y = jnp.dot(na_ref[0].astype(bf), w_ref[0:NA_DIM, :], preferred_element_type=jnp.float32)
    y += jnp.dot(pool_ref[0].astype(bf), w_ref[NA_DIM:NA_DIM + POOL_DIM, :], preferred_element_type=jnp.float32)
    y += jnp.dot(ml_ref[0].astype(bf), w_ref[NA_DIM + POOL_DIM:, :], preferred_element_type=jnp.float32)
    o_ref[0] = x_ref[0] + g_ref[0] * y


def _out_proj(x3, gate, na, pool, ml, w_out):
    B, T, d = x3.shape
    tn = min(IN_TOKEN_TILE, T)
    tok = lambda n: pl.BlockSpec((1, tn, n), lambda b, t: (b, t, 0))
    return pl.pallas_call(
        _out_proj_kernel,
        grid=(B, T // tn),
        in_specs=[tok(d), pl.BlockSpec((1, 1, d), lambda b, t: (b, 0, 0)), tok(NA_DIM), tok(POOL_DIM), tok(ML_DIM),
                  pl.BlockSpec(w_out.shape, lambda b, t: (0, 0))],
        out_specs=tok(d),
        out_shape=jax.ShapeDtypeStruct(x3.shape, x3.dtype),
        compiler_params=pltpu.CompilerParams(dimension_semantics=("parallel", "parallel")),
        name="out_proj",
    )(x3, gate, na, pool, ml, w_out.astype(jnp.bfloat16))


def kernel(x, c, ctx, c_ctx, w_ada, b_ada, norm1_g, w_in, ml_gate_b, na_q_g, na_k_g, na_rpb, pool_w,
           pool_scale, ml_conv, ml_norm_g, w_out, norm2_g, peer_wq, peer_keys, peer_u, peer_v):
    depth = w_in.shape[0]
    B, S, D = x.shape
    Lc = ctx.shape[1]
    hc = ctx
    u_bf = _cast_bf16(peer_u.reshape(depth * PEER_EXPERTS, D))
    v_bf = _cast_bf16(peer_v.reshape(depth * PEER_EXPERTS, D))
    for l in range(depth):
        need_ctx = l < depth - 1
        wqt_bf = peer_wq[l].astype(jnp.bfloat16).T
        keys_bf = peer_keys[l].astype(jnp.bfloat16).reshape(2 * PEER_HEADS, PEER_KEYS, PEER_DKEY // 2)
        mod = _ada_mod(jnp.concatenate([c, c_ctx[None, :]], axis=0), w_ada, b_ada, l).reshape(B + 1, 6, 1, D)
        sh1, sc1, g1, sh2, sc2, g2 = (mod[:B, i] for i in range(6))
        csh1, csc1, cg1, csh2, csc2, cg2 = (mod[B:, i] for i in range(6))
        hc1 = hc.reshape(1, B * Lc, D)
        a = _in_proj(x, sc1, sh1, norm1_g[l], w_in[l], na_q_g[l], na_k_g[l])
        ac = _in_proj(hc1, csc1, csh1, norm1_g[l], w_in[l], na_q_g[l], na_k_g[l])
        ac = {k: v.reshape(B, Lc, v.shape[-1]) for k, v in ac.items() if k != "gr"} | {
            "gr": jnp.transpose(ac["gr"].reshape(ML_N_GATES, B, Lc), (1, 0, 2))}
        na_lat = _na_latent(a["q"], a["k"], a["v"], ac["k"], ac["v"], na_rpb[l])
        pool_lat = _pool_mix(a["pool"], pool_w[l], pool_scale[l])
        ml_lat, ml_ctx = _mlstm_mix(a["mq"], a["mk"], a["mv"], a["mo"], a["gc"], a["gr"],
                                    ac["mq"], ac["mk"], ac["mv"], ac["mo"], ac["gc"], ac["gr"],
                                    ml_conv[l], ml_gate_b[l], ml_norm_g[l])
        x = _out_proj(x, g1, na_lat, pool_lat, ml_lat, w_out[l])
        x = _peer_block(x, sc2, sh2, g2, norm2_g[l], wqt_bf, keys_bf, u_bf, v_bf, l)
        if need_ctx:
            na_ctx = _na_context(ac["q"], ac["k"], ac["v"])
            pool_ctx = _pool_mix(ac["pool"], pool_w[l], pool_scale[l])
            flat = lambda t: t.reshape(1, B * Lc, t.shape[-1])
            hc1 = _out_proj(hc1, cg1, flat(na_ctx), flat(pool_ctx), flat(ml_ctx), w_out[l])
            hc = _peer_block(hc1, csc2, csh2, cg2, norm2_g[l], wqt_bf, keys_bf, u_bf, v_bf, l).reshape(B, Lc, D)
    return x
```

```python
import functools

import numpy as np
import jax
import jax.numpy as jnp
from jax import lax
from jax.experimental import pallas as pl
from jax.experimental.pallas import tpu as pltpu

D_MODEL = 1024
GRID_W = 64
HEAD_DIM = 64
NA_DIM = D_MODEL // 2
NA_HEADS = NA_DIM // HEAD_DIM
NA_WIN_ROWS = 8
NA_WIN_COLS = 16
POOL_DIM = D_MODEL // 4
POOL_WINDOWS = (2, 4, 8, 16)
POOL_GROUPS = len(POOL_WINDOWS)
POOL_GROUP_DIM = POOL_DIM // POOL_GROUPS
ML_DIM = D_MODEL // 4
ML_HEADS = ML_DIM // HEAD_DIM
ML_CONV_W = 5
ML_N_GATES = 4 * ML_HEADS
PEER_KEYS = 128
PEER_EXPERTS = PEER_KEYS * PEER_KEYS
PEER_HEADS = 8
PEER_TOPK = 16
PEER_DKEY = 256
ROPE_BASE = 10000.0
EPS = 1e-6

PEER_TOKEN_TILE = 1024
PEER_EXPERT_TILE = 1024
ROUTE_TOKEN_TILE = 512
ROUTE_LANE_CHUNK = 256
IN_TOKEN_TILE = 256
NA_ROWS_PER_STEP = 4
ADA_COL_TILE = 1536
CAST_ROW_TILE = 1024
POOL_BLOCK = 128
POOL_HALO = 8
ML_L = 256
VMEM_LIMIT_BYTES = 56 * 1024 * 1024

_GELU_C0 = float(np.sqrt(2.0 / np.pi))
_GELU_C1 = 0.044715 * _GELU_C0


def _gelu_tanh(x):
    inner = x * (_GELU_C0 + _GELU_C1 * (x * x))
    hx = 0.5 * x
    return hx + hx * jnp.tanh(inner)


def _peer_dense_kernel(x_ref, u_ref, v_ref, cnt_ref, e1_ref, rank_ref, e2_ref, o_ref):
    j = pl.program_id(1)

    @pl.when(j == 0)
    def _():
        o_ref[...] = jnp.zeros_like(o_ref)

    te = u_ref.shape[0]
    act = lax.dot_general(u_ref[...], x_ref[...], (((1,), (1,)), ((), ())),
                          preferred_element_type=jnp.float32)
    gates = []
    for k in range(te // PEER_KEYS):
        g = None
        for h in range(PEER_HEADS):
            cnt = cnt_ref[h, k:k + 1, :].astype(jnp.bfloat16)
            e1 = e1_ref[h, k:k + 1, :].astype(jnp.bfloat16)
            term = jnp.where(rank_ref[h] < cnt, e2_ref[h], jnp.zeros((), e2_ref.dtype)) * e1
            g = term if g is None else g + term
        gates.append(g)
    gate = jnp.concatenate(gates, axis=0)
    p = _gelu_tanh(act.astype(jnp.bfloat16)) * gate
    o_ref[...] += lax.dot_general(v_ref[...], p, (((0,), (0,)), ((), ())), preferred_element_type=jnp.float32)


def _cast_kernel(x_ref, o_ref):
    o_ref[...] = x_ref[...].astype(o_ref.dtype)


def _cast_bf16(w):
    rows, cols = w.shape
    tr = CAST_ROW_TILE
    assert rows % tr == 0
    blk = pl.BlockSpec((tr, cols), lambda i: (i, 0))
    return pl.pallas_call(
        _cast_kernel, grid=(rows // tr,), in_specs=[blk], out_specs=blk,
        out_shape=jax.ShapeDtypeStruct(w.shape, jnp.bfloat16),
        compiler_params=pltpu.CompilerParams(dimension_semantics=("parallel",)),
        name="cast_bf16",
    )(w)


def _peer_dense(tok, u_bf, v_bf, layer, cnt, e1, rank, e2):
    n, d = tok.shape
    tn, te = PEER_TOKEN_TILE, PEER_EXPERT_TILE
    assert n % tn == 0 and PEER_EXPERTS % te == 0 and te % PEER_KEYS == 0
    first = layer * (PEER_EXPERTS // te)
    tab_a = pl.BlockSpec((PEER_HEADS, te // PEER_KEYS, tn), lambda i, j: (0, j, i))
    tab_b = pl.BlockSpec((PEER_HEADS, PEER_KEYS, tn), lambda i, j: (0, 0, i))
    return pl.pallas_call(
        _peer_dense_kernel,
        grid=(n // tn, PEER_EXPERTS // te),
        in_specs=[
            pl.BlockSpec((tn, d), lambda i, j: (i, 0)),
            pl.BlockSpec((te, d), lambda i, j: (first + j, 0)),
            pl.BlockSpec((te, d), lambda i, j: (first + j, 0)),
            tab_a, tab_a, tab_b, tab_b,
        ],
        out_specs=pl.BlockSpec((d, tn), lambda i, j: (0, i)),
        out_shape=jax.ShapeDtypeStruct((d, n), jnp.float32),
        compiler_params=pltpu.CompilerParams(
            dimension_semantics=("parallel", "arbitrary"), vmem_limit_bytes=VMEM_LIMIT_BYTES),
        name="peer_dense",
    )(tok, u_bf, v_bf, cnt, e1, rank, e2)


def _top16(works, want_ranks):
    lanes = works[0].shape[1]
    rows = lax.broadcasted_iota(jnp.int32, (PEER_TOPK, lanes), 0)
    vals = [jnp.zeros((PEER_TOPK, lanes), jnp.float32) for _ in works]
    ranks = [jnp.full(w.shape, float(PEER_TOPK), jnp.float32) if r else None for w, r in zip(works, want_ranks)]
    tops = [[] for _ in works]
    works = list(works)
    for i in range(PEER_TOPK):
        ms = [jnp.max(w, axis=0, keepdims=True) for w in works]
        hits = [w == m for w, m in zip(works, ms)]
        ranks = [None if r is None else jnp.where(hit, float(i), r) for r, hit in zip(ranks, hits)]
        works = [jnp.where(hit, -jnp.inf, w) for w, hit in zip(works, hits)]
        vals = [jnp.where(rows == i, m, v) for v, m in zip(vals, ms)]
        for t, m in zip(tops, ms):
            t.append(m)
    return tops, vals, ranks


def _peer_route_kernel(x_ref, sc_ref, sh_ref, g_ref, wqt_ref, keys_ref,
                       tok_ref, cnt_ref, e1_ref, rank_ref, e2_ref, s_scr):
    x = x_ref[0]
    y = x * lax.rsqrt(jnp.mean(x * x, axis=-1, keepdims=True) + EPS) * g_ref[...]
    tok = (y * (1.0 + sc_ref[0]) + sh_ref[0]).astype(jnp.bfloat16)
    tok_ref[...] = tok
    qt = lax.dot_general(wqt_ref[...], tok, (((1,), (1,)), ((), ())),
                         preferred_element_type=jnp.float32).astype(jnp.bfloat16)
    dk = PEER_DKEY // 2
    for hp in range(2 * PEER_HEADS):
        s_scr[hp] = jnp.dot(keys_ref[hp], qt[hp * dk:(hp + 1) * dk], preferred_element_type=jnp.float32)

    cw = ROUTE_LANE_CHUNK
    lane_chunks = x.shape[0] // cw
    row8 = lax.broadcasted_iota(jnp.int32, (8, cw), 0)

    def body(it, carry):
        h = it // lane_chunks
        lanes = pl.ds(pl.multiple_of((it % lane_chunks) * cw, cw), cw)
        s1 = s_scr[2 * h, :, lanes]
        s2 = s_scr[2 * h + 1, :, lanes]
        (t1, t2), (v1, v2), (_, rank) = _top16([s1, s2], [False, True])
        groups = [t1[0] + v2, t1[1] + v2[:8]]
        for i in range(2, 8):
            groups.append(jnp.where(row8 < PEER_TOPK // (i + 1), t1[i] + v2[:8], -jnp.inf))
        groups.append(v1[8:] + t2[0])
        (tp,), (vp,), _ = _top16([jnp.concatenate(groups, axis=0)], [False])
        tau = tp[PEER_TOPK - 1]
        z = jnp.sum(jnp.exp(vp - tp[0]), axis=0, keepdims=True)
        kept = jnp.zeros(v1.shape, jnp.float32)
        for jj in range(PEER_TOPK):
            kept = jnp.where(v1 + t2[jj] >= tau, float(jj + 1), kept)
        cnt = jnp.zeros(s1.shape, jnp.float32)
        for i in range(PEER_TOPK):
            cnt = jnp.where(s1 == t1[i], kept[i:i + 1], cnt)
        cnt_ref[h, :, lanes] = cnt
        e1_ref[h, :, lanes] = jnp.exp(s1 - t1[0])
        rank_ref[h, :, lanes] = rank.astype(jnp.bfloat16)
        e2_ref[h, :, lanes] = (jnp.exp(s2 - t2[0]) * (1.0 / z)).astype(jnp.bfloat16)
        return carry

    lax.fori_loop(0, PEER_HEADS * lane_chunks, body, 0)


def _peer_route(x3, sc, sh, g, wqt_bf, keys_bf):
    B, T, d = x3.shape
    tn = min(ROUTE_TOKEN_TILE, T)
    assert T % tn == 0 and tn % 128 == 0
    nt = T // tn
    n = B * T
    tab = pl.BlockSpec((PEER_HEADS, PEER_KEYS, tn), lambda b, t: (0, 0, b * nt + t))
    tab_shape = lambda dt: jax.ShapeDtypeStruct((PEER_HEADS, PEER_KEYS, n), dt)
    mod = pl.BlockSpec((1, 1, d), lambda b, t: (b, 0, 0))
    return pl.pallas_call(
        _peer_route_kernel,
        grid=(B, nt),
        in_specs=[
            pl.BlockSpec((1, tn, d), lambda b, t: (b, t, 0)),
            mod, mod,
            pl.BlockSpec((1, d), lambda b, t: (0, 0)),
            pl.BlockSpec(wqt_bf.shape, lambda b, t: (0, 0)),
            pl.BlockSpec(keys_bf.shape, lambda b, t: (0, 0, 0)),
        ],
        out_specs=[pl.BlockSpec((tn, d), lambda b, t: (b * nt + t, 0)), tab, tab, tab, tab],
        out_shape=[jax.ShapeDtypeStruct((n, d), jnp.bfloat16), tab_shape(jnp.float32), tab_shape(jnp.float32),
                   tab_shape(jnp.bfloat16), tab_shape(jnp.bfloat16)],
        scratch_shapes=[pltpu.VMEM((2 * PEER_HEADS, PEER_KEYS, tn), jnp.float32)],
        compiler_params=pltpu.CompilerParams(
            dimension_semantics=("parallel", "parallel"), vmem_limit_bytes=VMEM_LIMIT_BYTES),
        name="peer_route",
    )(x3, sc, sh, g, wqt_bf, keys_bf)


def _residual_t_kernel(x_ref, g_ref, yt_ref, o_ref):
    o_ref[0] = x_ref[0] + g_ref[0] * yt_ref[...].T


def _residual_t(x3, gate, yt):
    B, T, d = x3.shape
    tn = min(ROUTE_TOKEN_TILE, T)
    nt = T // tn
    return pl.pallas_call(
        _residual_t_kernel,
        grid=(B, nt),
        in_specs=[pl.BlockSpec((1, tn, d), lambda b, t: (b, t, 0)),
                  pl.BlockSpec((1, 1, d), lambda b, t: (b, 0, 0)),
                  pl.BlockSpec((d, tn), lambda b, t: (0, b * nt + t))],
        out_specs=pl.BlockSpec((1, tn, d), lambda b, t: (b, t, 0)),
        out_shape=jax.ShapeDtypeStruct(x3.shape, x3.dtype),
        compiler_params=pltpu.CompilerParams(dimension_semantics=("parallel", "parallel")),
        name="residual_t",
    )(x3, gate, yt)


def _peer_block(x3, sc, sh, gate, norm_g, wqt_bf, keys_bf, u_bf, v_bf, layer):
    tok, cnt, e1, rank, e2 = _peer_route(x3, sc, sh, norm_g[None, :], wqt_bf, keys_bf)
    out_t = _peer_dense(tok, u_bf, v_bf, layer, cnt, e1, rank, e2)
    return _residual_t(x3, gate, out_t)


def _attend_pair(q2, segments, lane):
    zero = jnp.zeros((), q2.dtype)
    outs = []
    for half in range(2):
        own = (lane < HEAD_DIM) if half == 0 else (lane >= HEAD_DIM)
        qh = jnp.where(own, q2, zero)
        scores = []
        for k2, _, bias in segments:
            s = lax.dot_general(qh, k2, (((1,), (1,)), ((), ())), preferred_element_type=jnp.float32)
            scores.append(s if bias is None else s + bias[half])
        m = functools.reduce(jnp.maximum, [jnp.max(s, axis=-1, keepdims=True) for s in scores])
        ps = [jnp.exp(s - m) for s in scores]
        denom = functools.reduce(jnp.add, [jnp.sum(p, axis=-1, keepdims=True) for p in ps])
        acc = functools.reduce(jnp.add, [jnp.dot(p.astype(v2.dtype), v2, preferred_element_type=jnp.float32)
                                         for p, (_, v2, _) in zip(ps, segments)])
        outs.append(acc / denom)
    return jnp.where(lane < HEAD_DIM, outs[0], outs[1])


def _na_latent_kernel(q_ref, k_ref, v_ref, kc_ref, vc_ref, bias_ref, o_ref):
    n_rows = k_ref.shape[1] // GRID_W
    union = bias_ref.shape[3] // GRID_W
    u0 = _na_union_start(pl.program_id(1), n_rows, union)
    keys = pl.ds(pl.multiple_of(u0 * GRID_W, GRID_W), union * GRID_W)
    lane = lax.broadcasted_iota(jnp.int32, (q_ref.shape[1], 128), 1)
    for p in range(NA_DIM // 128):
        lanes = slice(p * 128, (p + 1) * 128)
        segs = [(k_ref[0, keys, lanes], v_ref[0, keys, lanes], (bias_ref[0, 2 * p], bias_ref[0, 2 * p + 1])),
                (kc_ref[0, :, lanes], vc_ref[0, :, lanes], None)]
        o_ref[0, :, lanes] = _attend_pair(q_ref[0, :, lanes], segs, lane)


def _na_union_start(g, n_rows, union):
    lo = g * NA_ROWS_PER_STEP - NA_WIN_ROWS // 2
    if isinstance(g, int):
        return min(min(max(lo, 0), n_rows - NA_WIN_ROWS), n_rows - union)
    return jnp.minimum(jnp.clip(lo, 0, n_rows - NA_WIN_ROWS), n_rows - union)


def _na_group_bias(rpb, n_rows):
    G, KR = NA_ROWS_PER_STEP, NA_WIN_ROWS
    union = KR + G - 1
    n_groups = n_rows // G
    cq = np.arange(GRID_W)
    c0 = np.clip(cq - NA_WIN_COLS // 2, 0, GRID_W - NA_WIN_COLS)
    in_win = (cq[None, :] >= c0[:, None]) & (cq[None, :] < c0[:, None] + NA_WIN_COLS)
    bc = np.clip(cq[None, :] - cq[:, None] + (NA_WIN_COLS - 1), 0, 2 * NA_WIN_COLS - 2)

    def layout(g):
        r = g * G + np.arange(G)
        key_row = _na_union_start(g, n_rows, union) + np.arange(union)
        r0 = np.clip(r - KR // 2, 0, n_rows - KR)
        ok = (key_row[None, :] >= r0[:, None]) & (key_row[None, :] < r0[:, None] + KR)
        br = np.clip(key_row[None, :] - r[:, None] + (KR - 1), 0, 2 * KR - 2)
        return ok, br

    layouts = [layout(g) for g in range(n_groups)]
    same = lambda a, b: all(np.array_equal(x, y) for x, y in zip(a, b))
    assert n_groups >= 3 and all(same(layouts[g], layouts[1]) for g in range(1, n_groups - 1))
    toe = jnp.where(in_win[None, None], rpb[:, :, bc], -jnp.inf).astype(jnp.float32)
    tables = []
    for ok, br in (layouts[0], layouts[1], layouts[-1]):
        blocks = jnp.where(ok[None, :, :, None, None], toe[:, br], -jnp.inf)
        tables.append(jnp.transpose(blocks, (0, 1, 3, 2, 4)).reshape(NA_HEADS, G * GRID_W, union * GRID_W))
    return jnp.stack(tables, axis=0)


def _na_latent(q, k, v, k_ctx, v_ctx, rpb):
    B, S, d = q.shape
    Lc = k_ctx.shape[1]
    n_rows = S // GRID_W
    assert n_rows >= NA_WIN_ROWS + NA_ROWS_PER_STEP - 1 and n_rows % NA_ROWS_PER_STEP == 0
    n_groups = n_rows // NA_ROWS_PER_STEP
    bias = _na_group_bias(rpb, n_rows)
    tq = NA_ROWS_PER_STEP * GRID_W
    whole = lambda L: pl.BlockSpec((1, L, d), lambda b, g: (b, 0, 0))
    kind = lambda b, g: (jnp.where(g == 0, 0, jnp.where(g == n_groups - 1, 2, 1)), 0, 0, 0)
    return pl.pallas_call(
        _na_latent_kernel,
        grid=(B, n_groups),
        in_specs=[pl.BlockSpec((1, tq, d), lambda b, g: (b, g, 0)), whole(S), whole(S), whole(Lc), whole(Lc),
                  pl.BlockSpec((1,) + bias.shape[1:], kind)],
        out_specs=pl.BlockSpec((1, tq, d), lambda b, g: (b, g, 0)),
        out_shape=jax.ShapeDtypeStruct((B, S, d), jnp.float32),
        compiler_params=pltpu.CompilerParams(
            dimension_semantics=("parallel", "arbitrary"), vmem_limit_bytes=VMEM_LIMIT_BYTES),
        name="na_latent",
    )(q, k, v, k_ctx, v_ctx, bias)


def _na_context_kernel(q_ref, k_ref, v_ref, o_ref):
    lane = lax.broadcasted_iota(jnp.int32, (q_ref.shape[1], 128), 1)
    for p in range(NA_DIM // 128):
        lanes = slice(p * 128, (p + 1) * 128)
        o_ref[0, :, lanes] = _attend_pair(q_ref[0, :, lanes], [(k_ref[0, :, lanes], v_ref[0, :, lanes], None)], lane)


def _na_context(q, k, v):
    B, Lc, d = q.shape
    blk = pl.BlockSpec((1, Lc, d), lambda b: (b, 0, 0))
    return pl.pallas_call(
        _na_context_kernel,
        grid=(B,),
        in_specs=[blk, blk, blk],
        out_specs=blk,
        out_shape=jax.ShapeDtypeStruct((B, Lc, d), jnp.float32),
        compiler_params=pltpu.CompilerParams(dimension_semantics=("parallel",)),
        name="na_context",
    )(q, k, v)


def _pool_matrices(T):
    tb, halo = POOL_BLOCK, POOL_HALO
    mats = np.zeros((len(POOL_WINDOWS), 3, tb, tb + 2 * halo), np.float32)
    for wi, w in enumerate(POOL_WINDOWS):
        for kind, start in enumerate((0, tb, T - tb)):
            for i in range(tb):
                t = start + i
                lo = min(max(t - w // 2, 0), T - 1)
                hi = min(max(t + (w - w // 2 - 1), 0), T - 1)
                mats[wi, kind, i, lo - start + halo:hi - start + halo + 1] = 1.0 / (hi - lo + 1)
                mats[wi, kind, i, i + halo] -= 1.0
    return jnp.asarray(mats)


def _pool_kernel(u_ref, a_ref, w_ref, s_ref, o_ref):
    T = u_ref.shape[1]
    tb, halo = POOL_BLOCK, POOL_HALO
    nb = T // tb
    group = lax.broadcasted_iota(jnp.int32, (tb, POOL_DIM), 1) // POOL_GROUP_DIM
    zeros = jnp.zeros((halo, POOL_DIM), jnp.float32)
    for blk in range(nb):
        kind = 0 if blk == 0 else (2 if blk == nb - 1 else 1)
        before = zeros if blk == 0 else u_ref[0, blk * tb - halo:blk * tb, :]
        after = zeros if blk == nb - 1 else u_ref[0, (blk + 1) * tb:(blk + 1) * tb + halo, :]
        xcat = jnp.concatenate([before, u_ref[0, blk * tb:(blk + 1) * tb, :], after], axis=0)
        pooled = None
        for wi in range(len(POOL_WINDOWS)):
            pw = jnp.dot(a_ref[wi, kind], xcat, preferred_element_type=jnp.float32, precision=lax.Precision.HIGHEST)
            pooled = pw if pooled is None else jnp.where(group == wi, pw, pooled)
        mixed = jnp.dot(pooled.astype(jnp.bfloat16), w_ref[...], preferred_element_type=jnp.float32)
        o_ref[0, blk * tb:(blk + 1) * tb, :] = mixed * s_ref[...]


def _pool_mix(u, pool_w, pool_scale):
    B, T, d = u.shape
    assert T % POOL_BLOCK == 0 and T >= 2 * POOL_BLOCK
    w_bd = jax.scipy.linalg.block_diag(*[pool_w[g] for g in range(POOL_GROUPS)]).astype(jnp.bfloat16)
    mats = _pool_matrices(T)
    blk = pl.BlockSpec((1, T, d), lambda b: (b, 0, 0))
    return pl.pallas_call(
        _pool_kernel,
        grid=(B,),
        in_specs=[blk, pl.BlockSpec(mats.shape, lambda b: (0, 0, 0, 0)), pl.BlockSpec((d, d), lambda b: (0, 0)),
                  pl.BlockSpec((1, d), lambda b: (0, 0))],
        out_specs=blk,
        out_shape=jax.ShapeDtypeStruct((B, T, d), jnp.float32),
        compiler_params=pltpu.CompilerParams(dimension_semantics=("parallel",)),
        name="pool_mix",
    )(u, mats, w_bd, pool_scale[None, :])


def _log_sigmoid(x):
    return jnp.minimum(x, 0.0) - jnp.log1p(jnp.exp(-jnp.abs(x)))


def _ml_prep(q, k, conv_ref, cos_ref, sin_ref):
    T = q.shape[0]
    row = lax.broadcasted_iota(jnp.int32, (T, 1), 0)
    lane = lax.broadcasted_iota(jnp.int32, (1, ML_DIM), 1) % (HEAD_DIM // 2)
    outs = []
    for idx, x in enumerate((q, k)):
        acc = None
        for j in range(ML_CONV_W):
            d = j - ML_CONV_W // 2
            w = conv_ref[j:j + 1, idx * ML_DIM:(idx + 1) * ML_DIM]
            if d == 0:
                term = x * w
            else:
                shifted = pltpu.roll(x, (-d) % T, 0)
                term = jnp.where((row + d >= 0) & (row + d < T), shifted, 0.0) * w
            acc = term if acc is None else acc + term
        y = acc * jax.nn.sigmoid(acc)
        if cos_ref is not None:
            quarter = HEAD_DIM // 4
            partner = jnp.where(lane < quarter, pltpu.roll(y, ML_DIM - quarter, 1), pltpu.roll(y, quarter, 1))
            y = y * cos_ref[...] + partner * sin_ref[...]
        outs.append(y)
    return outs


def _ml_chain(qs, kts, vs, rows, gi_c, gi_r, bc, br, ci, ii, head, backward, cn_ref, m_ref, slot, lane, tri):
    L = ML_L
    pair = slice((head // 2) * 128, (head // 2) * 128 + 128)
    own = (lane < HEAD_DIM) if head % 2 == 0 else (lane >= HEAD_DIM)
    one_lane = HEAD_DIM if head % 2 == 0 else 0
    q2 = jnp.where(own, qs[rows, pair], jnp.zeros((), jnp.bfloat16))
    v1 = jnp.where(own, vs[rows, pair].astype(jnp.float32), jnp.where(lane == one_lane, 1.0, 0.0))
    own_rows = lax.broadcasted_iota(jnp.int32, (128, L), 0)
    own_rows = (own_rows < HEAD_DIM) if head % 2 == 0 else (own_rows >= HEAD_DIM)
    kt2 = jnp.where(own_rows, kts[pair, rows], jnp.zeros((), jnp.bfloat16))
    m_prev = m_ref[slot]
    cn = cn_ref[slot]
    bcol = bc[:, ci:ci + 1]
    brow = br[ci:ci + 1, :]
    log_d = jnp.where(tri, bcol - brow + gi_r[ii:ii + 1, :], -jnp.inf)
    inter = bcol + m_prev
    m_t = jnp.maximum(inter, jnp.max(log_d, axis=-1, keepdims=True))
    s = jnp.dot(q2, kt2, preferred_element_type=jnp.float32)
    w_intra = (s * jnp.exp(log_d - m_t)).astype(jnp.bfloat16)
    nd = (jnp.dot(w_intra, v1.astype(jnp.bfloat16), preferred_element_type=jnp.float32)
          + jnp.exp(inter - m_t) * jnp.dot(q2, cn.astype(jnp.bfloat16), preferred_element_type=jnp.float32))
    den = nd[:, one_lane:one_lane + 1]
    h = nd / jnp.maximum(jnp.abs(den), jnp.exp(-m_t))
    b_last = bc[0:1, ci:ci + 1] if backward else bc[L - 1:L, ci:ci + 1]
    log_w = b_last - bcol + gi_c[:, ii:ii + 1]
    m_new = jnp.maximum(b_last + m_prev, jnp.max(log_w, axis=0, keepdims=True))
    wv = (jnp.exp(log_w - m_new) * v1).astype(jnp.bfloat16)
    cn_ref[slot] = jnp.exp(b_last + m_prev - m_new) * cn + jnp.dot(kt2, wv, preferred_element_type=jnp.float32)
    m_ref[slot] = m_new
    return h


def _ml_scan(T, qs, kts, vs, gc_ref, gr_ref, gbc, gbr, hf, hb, cn_ref, m_ref, tril, triu):
    L = ML_L
    nc = T // L
    lane = lax.broadcasted_iota(jnp.int32, (L, 128), 1)
    t_idx = lax.broadcasted_iota(jnp.int32, (L, L), 0)
    s_idx = lax.broadcasted_iota(jnp.int32, (L, L), 1)
    hi = lax.Precision.HIGHEST

    def body(c, carry):
        for backward in (False, True):
            cc = (nc - 1 - c) if backward else c
            rows = pl.ds(pl.multiple_of(cc * L, L), L)
            gc = gc_ref[0, rows, :] + gbc
            gr = gr_ref[0, :, rows] + gbr
            bc = jnp.dot(triu if backward else tril, _log_sigmoid(gc), preferred_element_type=jnp.float32, precision=hi)
            br = jnp.dot(_log_sigmoid(gr), tril if backward else triu, preferred_element_type=jnp.float32, precision=hi)
            tri = (s_idx >= t_idx) if backward else (s_idx <= t_idx)
            out = hb if backward else hf
            base = 2 * ML_HEADS if backward else 0
            for pair in range(ML_HEADS // 2):
                hs = []
                for head in (2 * pair, 2 * pair + 1):
                    hs.append(_ml_chain(qs, kts, vs, rows, gc, gr, bc, br, base + ML_HEADS + head, base + head, head,
                                        backward, cn_ref, m_ref, (ML_HEADS if backward else 0) + head, lane, tri))
                out[rows, pair * 128:(pair + 1) * 128] = jnp.where(lane < HEAD_DIM, hs[0], hs[1])
        return carry

    lax.fori_loop(0, nc, body, 0)


def _mlstm_kernel(q_ref, k_ref, v_ref, o_ref, gc_ref, gr_ref, cq_ref, ck_ref, cv_ref, co_ref, cgc_ref, cgr_ref,
                  cos_ref, sin_ref, conv_ref, gbc_ref, gbr_ref, ng_ref, e_ref, tril_ref, triu_ref,
                  y_ref, cy_ref, qs, kts, vs, hf, hb, cqs, ckts, cvs, chf, chb, cn_ref, m_ref):
    cn_ref[...] = jnp.zeros_like(cn_ref)
    m_ref[...] = jnp.zeros_like(m_ref)
    tril, triu = tril_ref[...], triu_ref[...]

    def stage(q_r, k_r, v_r, rope, qd, ktd, vd):
        qp, kp = _ml_prep(q_r[0], k_r[0], conv_ref, cos_ref if rope else None, sin_ref if rope else None)
        qd[...] = qp.astype(jnp.bfloat16)
        ktd[...] = (kp * HEAD_DIM ** -0.5).T.astype(jnp.bfloat16)
        vd[...] = v_r[0].astype(jnp.bfloat16)

    def finish(h_f, h_b, o_r, y_r):
        h = h_f[...] + h_b[...]
        ss = jnp.dot(h * h, e_ref[...], preferred_element_type=jnp.float32, precision=lax.Precision.HIGHEST)
        y_r[0] = h * lax.rsqrt(ss * (1.0 / HEAD_DIM) + EPS) * ng_ref[...] * jax.nn.sigmoid(o_r[0])

    gbc, gbr = gbc_ref[...], gbr_ref[...]
    stage(cq_ref, ck_ref, cv_ref, False, cqs, ckts, cvs)
    _ml_scan(cq_ref.shape[1], cqs, ckts, cvs, cgc_ref, cgr_ref, gbc, gbr, chf, chb, cn_ref, m_ref, tril, triu)
    finish(chf, chb, co_ref, cy_ref)
    stage(q_ref, k_ref, v_ref, True, qs, kts, vs)
    _ml_scan(q_ref.shape[1], qs, kts, vs, gc_ref, gr_ref, gbc, gbr, hf, hb, cn_ref, m_ref, tril, triu)
    finish(hf, hb, o_ref, y_ref)


def _rope_tables(T):
    t = jnp.arange(T)
    row = (t // GRID_W).astype(jnp.float32)
    col = (t % GRID_W).astype(jnp.float32)
    nf = HEAD_DIM // 4
    inv = ROPE_BASE ** (-jnp.arange(nf, dtype=jnp.float32) / nf)
    cr, sr, cc, sc = jnp.cos(row[:, None] * inv), jnp.sin(row[:, None] * inv), jnp.cos(col[:, None] * inv), \
        jnp.sin(col[:, None] * inv)
    cos = jnp.concatenate([cr, cr, cc, cc], axis=-1)
    sin = jnp.concatenate([-sr, sr, -sc, sc], axis=-1)
    return jnp.tile(cos, (1, ML_HEADS)), jnp.tile(sin, (1, ML_HEADS))


def _mlstm_mix(lq, lk, lv, lo, lg, lgr, cq, ck, cv, co, cg, cgr, conv_w, gate_b, norm_g):
    B, T, d = lq.shape
    Lc = cq.shape[1]
    L = ML_L
    assert T % L == 0 and Lc % L == 0
    cos, sin = _rope_tables(T)
    lane_head = np.arange(d) // HEAD_DIM
    same_head = jnp.asarray((lane_head[:, None] == lane_head[None, :]).astype(np.float32))
    tril = jnp.asarray(np.tril(np.ones((L, L), np.float32)))
    seq = lambda n: pl.BlockSpec((1, n, d), lambda b: (b, 0, 0))
    full = lambda a: pl.BlockSpec(a.shape, lambda b: (0,) * a.ndim)
    consts = [cos, sin, conv_w, gate_b[None, :], gate_b[:, None], norm_g[None, :], same_head, tril, tril.T]
    f32, bf16 = jnp.float32, jnp.bfloat16
    return pl.pallas_call(
        _mlstm_kernel,
        grid=(B,),
        in_specs=[seq(T)] * 4 + [pl.BlockSpec((1, T, ML_N_GATES), lambda b: (b, 0, 0)),
                                 pl.BlockSpec((1, ML_N_GATES, T), lambda b: (b, 0, 0))]
        + [seq(Lc)] * 4 + [pl.BlockSpec((1, Lc, ML_N_GATES), lambda b: (b, 0, 0)),
                           pl.BlockSpec((1, ML_N_GATES, Lc), lambda b: (b, 0, 0))]
        + [full(a) for a in consts],
        out_specs=[seq(T), seq(Lc)],
        out_shape=[jax.ShapeDtypeStruct((B, T, d), f32), jax.ShapeDtypeStruct((B, Lc, d), f32)],
        scratch_shapes=[pltpu.VMEM((T, d), bf16), pltpu.VMEM((d, T), bf16), pltpu.VMEM((T, d), bf16),
                        pltpu.VMEM((T, d), f32), pltpu.VMEM((T, d), f32),
                        pltpu.VMEM((Lc, d), bf16), pltpu.VMEM((d, Lc), bf16), pltpu.VMEM((Lc, d), bf16),
                        pltpu.VMEM((Lc, d), f32), pltpu.VMEM((Lc, d), f32),
                        pltpu.VMEM((2 * ML_HEADS, 128, 128), f32), pltpu.VMEM((2 * ML_HEADS, 1, 1), f32)],
        compiler_params=pltpu.CompilerParams(dimension_semantics=("parallel",), vmem_limit_bytes=VMEM_LIMIT_BYTES),
        name="mlstm",
    )(lq, lk, lv, lo, lg, lgr, cq, ck, cv, co, cg, cgr, *consts)


def _ada_kernel(c_ref, w_ref, b_ref, o_ref):
    c = c_ref[...]
    act = (c * jax.nn.sigmoid(c)).astype(jnp.bfloat16)
    o_ref[...] = jnp.dot(act, w_ref[0].astype(jnp.bfloat16), preferred_element_type=jnp.float32) + b_ref[0]


def _ada_mod(cond, w_all, b_all, layer):
    rows, d = cond.shape
    n = w_all.shape[2]
    tile = ADA_COL_TILE
    assert n % tile == 0
    return pl.pallas_call(
        _ada_kernel,
        grid=(n // tile,),
        in_specs=[pl.BlockSpec((rows, d), lambda j: (0, 0)), pl.BlockSpec((1, d, tile), lambda j: (layer, 0, j)),
                  pl.BlockSpec((1, 1, tile), lambda j: (layer, 0, j))],
        out_specs=pl.BlockSpec((rows, tile), lambda j: (0, j)),
        out_shape=jax.ShapeDtypeStruct((rows, n), jnp.float32),
        compiler_params=pltpu.CompilerParams(dimension_semantics=("parallel",)),
        name="ada_mod",
    )(cond, w_all, b_all[:, None, :])


_IN_SPLITS = (("q", NA_DIM), ("k", NA_DIM), ("v", NA_DIM), ("pool", POOL_DIM), ("mq", ML_DIM), ("mk", ML_DIM),
              ("mv", ML_DIM), ("mo", ML_DIM), ("gates", ML_N_GATES))
IN_COLS = sum(n for _, n in _IN_SPLITS)
IN_COLS_PAD = -(-IN_COLS // 128) * 128


def _in_proj_kernel(x_ref, sc_ref, sh_ref, g_ref, w_ref, wgt_ref, gq_ref, gk_ref, e_ref,
                    q_ref, k_ref, v_ref, pool_ref, mq_ref, mk_ref, mv_ref, mo_ref, gc_ref, gr_ref):
    x = x_ref[0]
    y = x * lax.rsqrt(jnp.mean(x * x, axis=-1, keepdims=True) + EPS) * g_ref[...]
    xn = (y * (1.0 + sc_ref[0]) + sh_ref[0]).astype(jnp.bfloat16)
    p = jnp.dot(xn, w_ref[...], preferred_element_type=jnp.float32)
    off = {}
    o = 0
    for name, n in _IN_SPLITS:
        off[name] = slice(o, o + n)
        o += n

    def head_norm(u, gain):
        sq = u * u
        hi = sq.astype(jnp.bfloat16)
        lo = (sq - hi.astype(jnp.float32)).astype(jnp.bfloat16)
        ss = (jnp.dot(hi, e_ref[...], preferred_element_type=jnp.float32)
              + jnp.dot(lo, e_ref[...], preferred_element_type=jnp.float32))
        return u * lax.rsqrt(ss * (1.0 / HEAD_DIM) + EPS) * gain

    q_ref[0] = (head_norm(p[:, off["q"]], gq_ref[...]) * HEAD_DIM ** -0.5).astype(jnp.bfloat16)
    k_ref[0] = head_norm(p[:, off["k"]], gk_ref[...]).astype(jnp.bfloat16)
    v_ref[0] = p[:, off["v"]].astype(jnp.bfloat16)
    pool_ref[0] = p[:, off["pool"]]
    mq_ref[0] = p[:, off["mq"]]
    mk_ref[0] = p[:, off["mk"]]
    mv_ref[0] = p[:, off["mv"]]
    mo_ref[0] = p[:, off["mo"]]
    gc_ref[0] = p[:, off["gates"]]
    gr_ref[0] = lax.dot_general(wgt_ref[...], xn, (((1,), (1,)), ((), ())), preferred_element_type=jnp.float32)


def _in_proj(x3, sc, sh, norm_g, w_in, gq, gk):
    B, T, d = x3.shape
    tn = min(IN_TOKEN_TILE, T)
    assert T % tn == 0
    w_bf = jnp.pad(w_in, ((0, 0), (0, IN_COLS_PAD - IN_COLS))).astype(jnp.bfloat16)
    wgt_bf = w_in[:, IN_COLS - ML_N_GATES:].T.astype(jnp.bfloat16)
    lane_head = np.arange(NA_DIM) // HEAD_DIM
    same_head = jnp.asarray((lane_head[:, None] == lane_head[None, :]).astype(np.float32)).astype(jnp.bfloat16)
    mod = pl.BlockSpec((1, 1, d), lambda b, t: (b, 0, 0))
    full = lambda a: pl.BlockSpec(a.shape, lambda b, t: (0,) * a.ndim)
    tok = lambda n: pl.BlockSpec((1, tn, n), lambda b, t: (b, t, 0))
    shp = lambda n, dt: jax.ShapeDtypeStruct((B, T, n), dt)
    f32, bf16 = jnp.float32, jnp.bfloat16
    consts = [norm_g[None, :], w_bf, wgt_bf, jnp.tile(gq, NA_HEADS)[None, :], jnp.tile(gk, NA_HEADS)[None, :], same_head]
    outs = pl.pallas_call(
        _in_proj_kernel,
        grid=(B, T // tn),
        in_specs=[pl.BlockSpec((1, tn, d), lambda b, t: (b, t, 0)), mod, mod] + [full(a) for a in consts],
        out_specs=[tok(NA_DIM)] * 3 + [tok(POOL_DIM)] + [tok(ML_DIM)] * 4 + [tok(ML_N_GATES)]
        + [pl.BlockSpec((1, ML_N_GATES, tn), lambda b, t: (b, 0, t))],
        out_shape=[shp(NA_DIM, bf16)] * 3 + [shp(POOL_DIM, f32)] + [shp(ML_DIM, f32)] * 4 + [shp(ML_N_GATES, f32)]
        + [jax.ShapeDtypeStruct((B, ML_N_GATES, T), f32)],
        compiler_params=pltpu.CompilerParams(
            dimension_semantics=("parallel", "parallel"), vmem_limit_bytes=VMEM_LIMIT_BYTES),
        name="in_proj",
    )(x3, sc, sh, *consts)
    return dict(zip(("q", "k", "v", "pool", "mq", "mk", "mv", "mo", "gc", "gr"), outs))


def _out_proj_kernel(x_ref, g_ref, na_ref, pool_ref, ml_ref, w_ref, o_ref):
    bf = jnp.bfloat16
    y = jnp.dot(na_ref[0].astype(bf), w_ref[0:NA_DIM, :], preferred_element_type=jnp.float32)
    y += jnp.dot(pool_ref[0].astype(bf), w_ref[NA_DIM:NA_DIM + POOL_DIM, :], preferred_element_type=jnp.float32)
    y += jnp.dot(ml_ref[0].astype(bf), w_ref[NA_DIM + POOL_DIM:, :], preferred_element_type=jnp.float32)
    o_ref[0] = x_ref[0] + g_ref[0] * y


def _out_proj(x3, gate, na, pool, ml, w_out):
    B, T, d = x3.shape
    tn = min(IN_TOKEN_TILE, T)
    tok = lambda n: pl.BlockSpec((1, tn, n), lambda b, t: (b, t, 0))
    return pl.pallas_call(
        _out_proj_kernel,
        grid=(B, T // tn),
        in_specs=[tok(d), pl.BlockSpec((1, 1, d), lambda b, t: (b, 0, 0)), tok(NA_DIM), tok(POOL_DIM), tok(ML_DIM),
                  pl.BlockSpec(w_out.shape, lambda b, t: (0, 0))],
        out_specs=tok(d),
        out_shape=jax.ShapeDtypeStruct(x3.shape, x3.dtype),
        compiler_params=pltpu.CompilerParams(dimension_semantics=("parallel", "parallel")),
        name="out_proj",
    )(x3, gate, na, pool, ml, w_out.astype(jnp.bfloat16))


def kernel(x, c, ctx, c_ctx, w_ada, b_ada, norm1_g, w_in, ml_gate_b, na_q_g, na_k_g, na_rpb, pool_w,
           pool_scale, ml_conv, ml_norm_g, w_out, norm2_g, peer_wq, peer_keys, peer_u, peer_v):
    depth = w_in.shape[0]
    B, S, D = x.shape
    Lc = ctx.shape[1]
    hc = ctx
    u_bf = _cast_bf16(peer_u.reshape(depth * PEER_EXPERTS, D))
    v_bf = _cast_bf16(peer_v.reshape(depth * PEER_EXPERTS, D))
    for l in range(depth):
        need_ctx = l < depth - 1
        wqt_bf = peer_wq[l].astype(jnp.bfloat16).T
        keys_bf = peer_keys[l].astype(jnp.bfloat16).reshape(2 * PEER_HEADS, PEER_KEYS, PEER_DKEY // 2)
        mod = _ada_mod(jnp.concatenate([c, c_ctx[None, :]], axis=0), w_ada, b_ada, l).reshape(B + 1, 6, 1, D)
        sh1, sc1, g1, sh2, sc2, g2 = (mod[:B, i] for i in range(6))
        csh1, csc1, cg1, csh2, csc2, cg2 = (mod[B:, i] for i in range(6))
        hc1 = hc.reshape(1, B * Lc, D)
        a = _in_proj(x, sc1, sh1, norm1_g[l], w_in[l], na_q_g[l], na_k_g[l])
        ac = _in_proj(hc1, csc1, csh1, norm1_g[l], w_in[l], na_q_g[l], na_k_g[l])
        ac = {k: v.reshape(B, Lc, v.shape[-1]) for k, v in ac.items() if k != "gr"} | {
            "gr": jnp.transpose(ac["gr"].reshape(ML_N_GATES, B, Lc), (1, 0, 2))}
        na_lat = _na_latent(a["q"], a["k"], a["v"], ac["k"], ac["v"], na_rpb[l])
        pool_lat = _pool_mix(a["pool"], pool_w[l], pool_scale[l])
        ml_lat, ml_ctx = _mlstm_mix(a["mq"], a["mk"], a["mv"], a["mo"], a["gc"], a["gr"],
                                    ac["mq"], ac["mk"], ac["mv"], ac["mo"], ac["gc"], ac["gr"],
                                    ml_conv[l], ml_gate_b[l], ml_norm_g[l])
        x = _out_proj(x, g1, na_lat, pool_lat, ml_lat, w_out[l])
        x = _peer_block(x, sc2, sh2, g2, norm2_g[l], wqt_bf, keys_bf, u_bf, v_bf, l)
        if need_ctx:
            na_ctx = _na_context(ac["q"], ac["k"], ac["v"])
            pool_ctx = _pool_mix(ac["pool"], pool_w[l], pool_scale[l])
            flat = lambda t: t.reshape(1, B * Lc, t.shape[-1])
            hc1 = _out_proj(hc1, cg1, flat(na_ctx), flat(pool_ctx), flat(ml_ctx), w_out[l])
            hc = _peer_block(hc1, csc2, csh2, cg2, norm2_g[l], wqt_bf, keys_bf, u_bf, v_bf, l).reshape(B, Lc, D)
    return x
```

```python
import functools

import numpy as np
import jax
import jax.numpy as jnp
from jax import lax
from jax.experimental import pallas as pl
from jax.experimental.pallas import tpu as pltpu

D_MODEL = 1024
GRID_W = 64
HEAD_DIM = 64
NA_DIM = D_MODEL // 2
NA_HEADS = NA_DIM // HEAD_DIM
NA_WIN_ROWS = 8
NA_WIN_COLS = 16
POOL_DIM = D_MODEL // 4
POOL_WINDOWS = (2, 4, 8, 16)
POOL_GROUPS = len(POOL_WINDOWS)
POOL_GROUP_DIM = POOL_DIM // POOL_GROUPS
ML_DIM = D_MODEL // 4
ML_HEADS = ML_DIM // HEAD_DIM
ML_CONV_W = 5
ML_N_GATES = 4 * ML_HEADS
PEER_KEYS = 128
PEER_EXPERTS = PEER_KEYS * PEER_KEYS
PEER_HEADS = 8
PEER_TOPK = 16
PEER_DKEY = 256
ROPE_BASE = 10000.0
EPS = 1e-6

LANES = 128
SUBLANES = 8
V7X_VMEM_BYTES = 64 * 1024 * 1024

PEER_TOKEN_TILE = 1024
PEER_EXPERT_TILE = 1024
ROUTE_TOKEN_TILE = 512
ROUTE_LANE_CHUNK = 2 * LANES
IN_TOKEN_TILE = 256
NA_ROWS_PER_STEP = 4
ADA_COL_TILE = 1536
CAST_ROW_TILE = 1024
POOL_BLOCK = 128
POOL_HALO = SUBLANES
ML_L = 256
VMEM_LIMIT_BYTES = V7X_VMEM_BYTES - 8 * 1024 * 1024

_GELU_C0 = float(np.sqrt(2.0 / np.pi))
_GELU_C1 = 0.044715 * _GELU_C0


def _gelu_tanh(x):
    inner = x * (_GELU_C0 + _GELU_C1 * (x * x))
    hx = 0.5 * x
    return hx + hx * jnp.tanh(inner)


def _peer_dense_kernel(x_ref, u_ref, v_ref, cnt_ref, e1_ref, rank_ref, e2_ref, o_ref):
    j = pl.program_id(1)

    @pl.when(j == 0)
    def _():
        o_ref[...] = jnp.zeros_like(o_ref)

    te = u_ref.shape[0]
    act = lax.dot_general(u_ref[...], x_ref[...], (((1,), (1,)), ((), ())),
                          preferred_element_type=jnp.float32)
    gates = []
    for k in range(te // PEER_KEYS):
        g = None
        for h in range(PEER_HEADS):
            cnt = cnt_ref[h, k:k + 1, :].astype(jnp.bfloat16)
            e1 = e1_ref[h, k:k + 1, :].astype(jnp.bfloat16)
            term = jnp.where(rank_ref[h] < cnt, e2_ref[h], jnp.zeros((), e2_ref.dtype)) * e1
            g = term if g is None else g + term
        gates.append(g)
    gate = jnp.concatenate(gates, axis=0)
    p = _gelu_tanh(act.astype(jnp.bfloat16)) * gate
    o_ref[...] += lax.dot_general(v_ref[...], p, (((0,), (0,)), ((), ())), preferred_element_type=jnp.float32)


def _cast_kernel(x_ref, o_ref):
    o_ref[...] = x_ref[...].astype(o_ref.dtype)


def _cast_bf16(w):
    rows, cols = w.shape
    tr = CAST_ROW_TILE
    assert rows % tr == 0
    blk = pl.BlockSpec((tr, cols), lambda i: (i, 0))
    return pl.pallas_call(
        _cast_kernel, grid=(rows // tr,), in_specs=[blk], out_specs=blk,
        out_shape=jax.ShapeDtypeStruct(w.shape, jnp.bfloat16),
        compiler_params=pltpu.CompilerParams(dimension_semantics=("parallel",)),
        name="cast_bf16",
    )(w)


def _peer_dense(tok, u_bf, v_bf, layer, cnt, e1, rank, e2):
    n, d = tok.shape
    tn, te = PEER_TOKEN_TILE, PEER_EXPERT_TILE
    assert n % tn == 0 and PEER_EXPERTS % te == 0 and te % PEER_KEYS == 0
    first = layer * (PEER_EXPERTS // te)
    tab_a = pl.BlockSpec((PEER_HEADS, te // PEER_KEYS, tn), lambda i, j: (0, j, i))
    tab_b = pl.BlockSpec((PEER_HEADS, PEER_KEYS, tn), lambda i, j: (0, 0, i))
    return pl.pallas_call(
        _peer_dense_kernel,
        grid=(n // tn, PEER_EXPERTS // te),
        in_specs=[
            pl.BlockSpec((tn, d), lambda i, j: (i, 0)),
            pl.BlockSpec((te, d), lambda i, j: (first + j, 0)),
            pl.BlockSpec((te, d), lambda i, j: (first + j, 0)),
            tab_a, tab_a, tab_b, tab_b,
        ],
        out_specs=pl.BlockSpec((d, tn), lambda i, j: (0, i)),
        out_shape=jax.ShapeDtypeStruct((d, n), jnp.float32),
        compiler_params=pltpu.CompilerParams(
            dimension_semantics=("parallel", "arbitrary"), vmem_limit_bytes=VMEM_LIMIT_BYTES),
        name="peer_dense",
    )(tok, u_bf, v_bf, cnt, e1, rank, e2)


def _top16(works, want_ranks):
    lanes = works[0].shape[1]
    rows = lax.broadcasted_iota(jnp.int32, (PEER_TOPK, lanes), 0)
    vals = [jnp.zeros((PEER_TOPK, lanes), jnp.float32) for _ in works]
    ranks = [jnp.full(w.shape, float(PEER_TOPK), jnp.float32) if r else None for w, r in zip(works, want_ranks)]
    tops = [[] for _ in works]
    works = list(works)
    for i in range(PEER_TOPK):
        ms = [jnp.max(w, axis=0, keepdims=True) for w in works]
        hits = [w == m for w, m in zip(works, ms)]
        ranks = [None if r is None else jnp.where(hit, float(i), r) for r, hit in zip(ranks, hits)]
        works = [jnp.where(hit, -jnp.inf, w) for w, hit in zip(works, hits)]
        vals = [jnp.where(rows == i, m, v) for v, m in zip(vals, ms)]
        for t, m in zip(tops, ms):
            t.append(m)
    return tops, vals, ranks


def _peer_route_kernel(x_ref, g1_ref, na_ref, pool_ref, ml_ref, wo_ref, sc_ref, sh_ref, g_ref, wqt_ref, keys_ref,
                       xo_ref, tok_ref, cnt_ref, e1_ref, rank_ref, e2_ref, s_scr):
    bf = jnp.bfloat16
    mix = jnp.dot(na_ref[0].astype(bf), wo_ref[0:NA_DIM, :], preferred_element_type=jnp.float32)
    mix += jnp.dot(pool_ref[0].astype(bf), wo_ref[NA_DIM:NA_DIM + POOL_DIM, :], preferred_element_type=jnp.float32)
    mix += jnp.dot(ml_ref[0].astype(bf), wo_ref[NA_DIM + POOL_DIM:, :], preferred_element_type=jnp.float32)
    x = x_ref[0] + g1_ref[0] * mix
    xo_ref[0] = x
    y = x * lax.rsqrt(jnp.mean(x * x, axis=-1, keepdims=True) + EPS) * g_ref[...]
    tok = (y * (1.0 + sc_ref[0]) + sh_ref[0]).astype(jnp.bfloat16)
    tok_ref[...] = tok
    qt = lax.dot_general(wqt_ref[...], tok, (((1,), (1,)), ((), ())),
                         preferred_element_type=jnp.float32).astype(jnp.bfloat16)
    dk = PEER_DKEY // 2
    for hp in range(2 * PEER_HEADS):
        s_scr[hp] = jnp.dot(keys_ref[hp], qt[hp * dk:(hp + 1) * dk], preferred_element_type=jnp.float32)

    cw = ROUTE_LANE_CHUNK
    lane_chunks = x.shape[0] // cw
    half = PEER_TOPK // 2
    row_half = lax.broadcasted_iota(jnp.int32, (half, cw), 0)

    def body(it, carry):
        h = it // lane_chunks
        lanes = pl.ds(pl.multiple_of((it % lane_chunks) * cw, cw), cw)
        s1 = s_scr[2 * h, :, lanes]
        s2 = s_scr[2 * h + 1, :, lanes]
        (t1, t2), (v1, v2), (_, rank) = _top16([s1, s2], [False, True])
        groups = [t1[0] + v2, t1[1] + v2[:half]]
        for i in range(2, half):
            groups.append(jnp.where(row_half < PEER_TOPK // (i + 1), t1[i] + v2[:half], -jnp.inf))
        groups.append(v1[half:] + t2[0])
        (tp,), (vp,), _ = _top16([jnp.concatenate(groups, axis=0)], [False])
        tau = tp[PEER_TOPK - 1]
        z = jnp.sum(jnp.exp(vp - tp[0]), axis=0, keepdims=True)
        kept = jnp.zeros(v1.shape, jnp.float32)
        for jj in range(PEER_TOPK):
            kept = jnp.where(v1 + t2[jj] >= tau, float(jj + 1), kept)
        cnt = jnp.zeros(s1.shape, jnp.float32)
        for i in range(PEER_TOPK):
            cnt = jnp.where(s1 == t1[i], kept[i:i + 1], cnt)
        cnt_ref[h, :, lanes] = cnt
        e1_ref[h, :, lanes] = jnp.exp(s1 - t1[0])
        rank_ref[h, :, lanes] = rank.astype(jnp.bfloat16)
        e2_ref[h, :, lanes] = (jnp.exp(s2 - t2[0]) * (1.0 / z)).astype(jnp.bfloat16)
        return carry

    lax.fori_loop(0, PEER_HEADS * lane_chunks, body, 0)


def _peer_route(x3, gate1, na, pool, ml, wo_bf, sc, sh, g, wqt_bf, keys_bf):
    B, T, d = x3.shape
    tn = min(ROUTE_TOKEN_TILE, T)
    assert T % tn == 0 and tn % ROUTE_LANE_CHUNK == 0
    nt = T // tn
    n = B * T
    tab = pl.BlockSpec((PEER_HEADS, PEER_KEYS, tn), lambda b, t: (0, 0, b * nt + t))
    tab_shape = lambda dt: jax.ShapeDtypeStruct((PEER_HEADS, PEER_KEYS, n), dt)
    mod = pl.BlockSpec((1, 1, d), lambda b, t: (b, 0, 0))
    tok = lambda c: pl.BlockSpec((1, tn, c), lambda b, t: (b, t, 0))
    const = lambda a: pl.BlockSpec(a.shape, lambda b, t: (0,) * a.ndim, pipeline_mode=pl.Buffered(1))
    return pl.pallas_call(
        _peer_route_kernel,
        grid=(B, nt),
        in_specs=[tok(d), mod, tok(NA_DIM), tok(POOL_DIM), tok(ML_DIM), const(wo_bf), mod, mod,
                  pl.BlockSpec((1, d), lambda b, t: (0, 0)), const(wqt_bf), const(keys_bf)],
        out_specs=[tok(d), pl.BlockSpec((tn, d), lambda b, t: (b * nt + t, 0)), tab, tab, tab, tab],
        out_shape=[jax.ShapeDtypeStruct(x3.shape, x3.dtype), jax.ShapeDtypeStruct((n, d), jnp.bfloat16),
                   tab_shape(jnp.float32), tab_shape(jnp.float32), tab_shape(jnp.bfloat16), tab_shape(jnp.bfloat16)],
        scratch_shapes=[pltpu.VMEM((2 * PEER_HEADS, PEER_KEYS, tn), jnp.float32)],
        compiler_params=pltpu.CompilerParams(
            dimension_semantics=("parallel", "parallel"), vmem_limit_bytes=VMEM_LIMIT_BYTES),
        name="peer_route",
    )(x3, gate1, na, pool, ml, wo_bf, sc, sh, g, wqt_bf, keys_bf)


def _residual_t_kernel(x_ref, g_ref, yt_ref, o_ref):
    o_ref[0] = x_ref[0] + g_ref[0] * yt_ref[...].T


def _residual_t(x3, gate, yt):
    B, T, d = x3.shape
    tn = min(ROUTE_TOKEN_TILE, T)
    nt = T // tn
    return pl.pallas_call(
        _residual_t_kernel,
        grid=(B, nt),
        in_specs=[pl.BlockSpec((1, tn, d), lambda b, t: (b, t, 0)),
                  pl.BlockSpec((1, 1, d), lambda b, t: (b, 0, 0)),
                  pl.BlockSpec((d, tn), lambda b, t: (0, b * nt + t))],
        out_specs=pl.BlockSpec((1, tn, d), lambda b, t: (b, t, 0)),
        out_shape=jax.ShapeDtypeStruct(x3.shape, x3.dtype),
        compiler_params=pltpu.CompilerParams(dimension_semantics=("parallel", "parallel")),
        name="residual_t",
    )(x3, gate, yt)


def _mix_and_peer(x3, gate1, na, pool, ml, wo_bf, sc, sh, gate2, norm_g, wqt_bf, keys_bf, u_bf, v_bf, layer):
    x1, tok, cnt, e1, rank, e2 = _peer_route(x3, gate1, na, pool, ml, wo_bf, sc, sh, norm_g[None, :], wqt_bf, keys_bf)
    out_t = _peer_dense(tok, u_bf, v_bf, layer, cnt, e1, rank, e2)
    return _residual_t(x1, gate2, out_t)


def _attend_pair(q2, segments, lane):
    zero = jnp.zeros((), q2.dtype)
    outs = []
    for half in range(2):
        own = (lane < HEAD_DIM) if half == 0 else (lane >= HEAD_DIM)
        qh = jnp.where(own, q2, zero)
        scores = []
        for k2, _, bias in segments:
            s = lax.dot_general(qh, k2, (((1,), (1,)), ((), ())), preferred_element_type=jnp.float32)
            scores.append(s if bias is None else s + bias[half])
        m = functools.reduce(jnp.maximum, [jnp.max(s, axis=-1, keepdims=True) for s in scores])
        ps = [jnp.exp(s - m) for s in scores]
        denom = functools.reduce(jnp.add, [jnp.sum(p, axis=-1, keepdims=True) for p in ps])
        acc = functools.reduce(jnp.add, [jnp.dot(p.astype(v2.dtype), v2, preferred_element_type=jnp.float32)
                                         for p, (_, v2, _) in zip(ps, segments)])
        outs.append(acc / denom)
    return jnp.where(lane < HEAD_DIM, outs[0], outs[1])


def _na_latent_kernel(q_ref, k_ref, v_ref, kc_ref, vc_ref, bias_ref, o_ref):
    n_rows = k_ref.shape[1] // GRID_W
    union = bias_ref.shape[3] // GRID_W
    u0 = _na_union_start(pl.program_id(1), n_rows, union)
    keys = pl.ds(pl.multiple_of(u0 * GRID_W, GRID_W), union * GRID_W)
    lane = lax.broadcasted_iota(jnp.int32, (q_ref.shape[1], LANES), 1)
    for p in range(NA_DIM // LANES):
        lanes = slice(p * LANES, (p + 1) * LANES)
        segs = [(k_ref[0, keys, lanes], v_ref[0, keys, lanes], (bias_ref[0, 2 * p], bias_ref[0, 2 * p + 1])),
                (kc_ref[0, :, lanes], vc_ref[0, :, lanes], None)]
        o_ref[0, :, lanes] = _attend_pair(q_ref[0, :, lanes], segs, lane)


def _na_union_start(g, n_rows, union):
    lo = g * NA_ROWS_PER_STEP - NA_WIN_ROWS // 2
    if isinstance(g, int):
        return min(min(max(lo, 0), n_rows - NA_WIN_ROWS), n_rows - union)
    return jnp.minimum(jnp.clip(lo, 0, n_rows - NA_WIN_ROWS), n_rows - union)


def _na_group_bias(rpb, n_rows):
    G, KR = NA_ROWS_PER_STEP, NA_WIN_ROWS
    union = KR + G - 1
    n_groups = n_rows // G
    cq = np.arange(GRID_W)
    c0 = np.clip(cq - NA_WIN_COLS // 2, 0, GRID_W - NA_WIN_COLS)
    in_win = (cq[None, :] >= c0[:, None]) & (cq[None, :] < c0[:, None] + NA_WIN_COLS)
    bc = np.clip(cq[None, :] - cq[:, None] + (NA_WIN_COLS - 1), 0, 2 * NA_WIN_COLS - 2)

    def layout(g):
        r = g * G + np.arange(G)
        key_row = _na_union_start(g, n_rows, union) + np.arange(union)
        r0 = np.clip(r - KR // 2, 0, n_rows - KR)
        ok = (key_row[None, :] >= r0[:, None]) & (key_row[None, :] < r0[:, None] + KR)
        br = np.clip(key_row[None, :] - r[:, None] + (KR - 1), 0, 2 * KR - 2)
        return ok, br

    layouts = [layout(g) for g in range(n_groups)]
    same = lambda a, b: all(np.array_equal(x, y) for x, y in zip(a, b))
    assert n_groups >= 3 and all(same(layouts[g], layouts[1]) for g in range(1, n_groups - 1))
    toe = jnp.where(in_win[None, None], rpb[:, :, bc], -jnp.inf).astype(jnp.float32)
    tables = []
    for ok, br in (layouts[0], layouts[1], layouts[-1]):
        blocks = jnp.where(ok[None, :, :, None, None], toe[:, br], -jnp.inf)
        tables.append(jnp.transpose(blocks, (0, 1, 3, 2, 4)).reshape(NA_HEADS, G * GRID_W, union * GRID_W))
    return jnp.stack(tables, axis=0)


def _na_latent(q, k, v, k_ctx, v_ctx, rpb):
    B, S, d = q.shape
    Lc = k_ctx.shape[1]
    n_rows = S // GRID_W
    assert n_rows >= NA_WIN_ROWS + NA_ROWS_PER_STEP - 1 and n_rows % NA_ROWS_PER_STEP == 0
    n_groups = n_rows // NA_ROWS_PER_STEP
    bias = _na_group_bias(rpb, n_rows)
    tq = NA_ROWS_PER_STEP * GRID_W
    whole = lambda L: pl.BlockSpec((1, L, d), lambda b, g: (b, 0, 0))
    kind = lambda b, g: (jnp.where(g == 0, 0, jnp.where(g == n_groups - 1, 2, 1)), 0, 0, 0)
    return pl.pallas_call(
        _na_latent_kernel,
        grid=(B, n_groups),
        in_specs=[pl.BlockSpec((1, tq, d), lambda b, g: (b, g, 0)), whole(S), whole(S), whole(Lc), whole(Lc),
                  pl.BlockSpec((1,) + bias.shape[1:], kind)],
        out_specs=pl.BlockSpec((1, tq, d), lambda b, g: (b, g, 0)),
        out_shape=jax.ShapeDtypeStruct((B, S, d), jnp.float32),
        compiler_params=pltpu.CompilerParams(
            dimension_semantics=("parallel", "arbitrary"), vmem_limit_bytes=VMEM_LIMIT_BYTES),
        name="na_latent",
    )(q, k, v, k_ctx, v_ctx, bias)


def _na_context_kernel(q_ref, k_ref, v_ref, o_ref):
    lane = lax.broadcasted_iota(jnp.int32, (q_ref.shape[1], LANES), 1)
    for p in range(NA_DIM // LANES):
        lanes = slice(p * LANES, (p + 1) * LANES)
        o_ref[0, :, lanes] = _attend_pair(q_ref[0, :, lanes], [(k_ref[0, :, lanes], v_ref[0, :, lanes], None)], lane)


def _na_context(q, k, v):
    B, Lc, d = q.shape
    blk = pl.BlockSpec((1, Lc, d), lambda b: (b, 0, 0))
    return pl.pallas_call(
        _na_context_kernel,
        grid=(B,),
        in_specs=[blk, blk, blk],
        out_specs=blk,
        out_shape=jax.ShapeDtypeStruct((B, Lc, d), jnp.float32),
        compiler_params=pltpu.CompilerParams(dimension_semantics=("parallel",)),
        name="na_context",
    )(q, k, v)


def _pool_matrices(T):
    tb, halo = POOL_BLOCK, POOL_HALO
    mats = np.zeros((len(POOL_WINDOWS), 3, tb, tb + 2 * halo), np.float32)
    for wi, w in enumerate(POOL_WINDOWS):
        for kind, start in enumerate((0, tb, T - tb)):
            for i in range(tb):
                t = start + i
                lo = min(max(t - w // 2, 0), T - 1)
                hi = min(max(t + (w - w // 2 - 1), 0), T - 1)
                mats[wi, kind, i, lo - start + halo:hi - start + halo + 1] = 1.0 / (hi - lo + 1)
                mats[wi, kind, i, i + halo] -= 1.0
    return jnp.asarray(mats)


def _pool_kernel(u_ref, a_ref, w_ref, s_ref, o_ref):
    T = u_ref.shape[1]
    tb, halo = POOL_BLOCK, POOL_HALO
    nb = T // tb
    group = lax.broadcasted_iota(jnp.int32, (tb, POOL_DIM), 1) // POOL_GROUP_DIM
    zeros = jnp.zeros((halo, POOL_DIM), jnp.float32)
    for blk in range(nb):
        kind = 0 if blk == 0 else (2 if blk == nb - 1 else 1)
        before = zeros if blk == 0 else u_ref[0, blk * tb - halo:blk * tb, :]
        after = zeros if blk == nb - 1 else u_ref[0, (blk + 1) * tb:(blk + 1) * tb + halo, :]
        xcat = jnp.concatenate([before, u_ref[0, blk * tb:(blk + 1) * tb, :], after], axis=0)
        pooled = None
        for wi in range(len(POOL_WINDOWS)):
            pw = jnp.dot(a_ref[wi, kind], xcat, preferred_element_type=jnp.float32, precision=lax.Precision.HIGHEST)
            pooled = pw if pooled is None else jnp.where(group == wi, pw, pooled)
        mixed = jnp.dot(pooled.astype(jnp.bfloat16), w_ref[...], preferred_element_type=jnp.float32)
        o_ref[0, blk * tb:(blk + 1) * tb, :] = mixed * s_ref[...]


def _pool_mix(u, pool_w, pool_scale):
    B, T, d = u.shape
    assert T % POOL_BLOCK == 0 and T >= 2 * POOL_BLOCK
    w_bd = jax.scipy.linalg.block_diag(*[pool_w[g] for g in range(POOL_GROUPS)]).astype(jnp.bfloat16)
    mats = _pool_matrices(T)
    blk = pl.BlockSpec((1, T, d), lambda b: (b, 0, 0))
    return pl.pallas_call(
        _pool_kernel,
        grid=(B,),
        in_specs=[blk, pl.BlockSpec(mats.shape, lambda b: (0, 0, 0, 0)), pl.BlockSpec((d, d), lambda b: (0, 0)),
                  pl.BlockSpec((1, d), lambda b: (0, 0))],
        out_specs=blk,
        out_shape=jax.ShapeDtypeStruct((B, T, d), jnp.float32),
        compiler_params=pltpu.CompilerParams(dimension_semantics=("parallel",)),
        name="pool_mix",
    )(u, mats, w_bd, pool_scale[None, :])


def _log_sigmoid(x):
    return jnp.minimum(x, 0.0) - jnp.log1p(jnp.exp(-jnp.abs(x)))


def _ml_prep(q, k, conv_ref, cos_ref, sin_ref):
    T = q.shape[0]
    row = lax.broadcasted_iota(jnp.int32, (T, 1), 0)
    lane = lax.broadcasted_iota(jnp.int32, (1, ML_DIM), 1) % (HEAD_DIM // 2)
    outs = []
    for idx, x in enumerate((q, k)):
        acc = None
        for j in range(ML_CONV_W):
            d = j - ML_CONV_W // 2
            w = conv_ref[j:j + 1, idx * ML_DIM:(idx + 1) * ML_DIM]
            if d == 0:
                term = x * w
            else:
                shifted = pltpu.roll(x, (-d) % T, 0)
                term = jnp.where((row + d >= 0) & (row + d < T), shifted, 0.0) * w
            acc = term if acc is None else acc + term
        y = acc * jax.nn.sigmoid(acc)
        if cos_ref is not None:
            quarter = HEAD_DIM // 4
            partner = jnp.where(lane < quarter, pltpu.roll(y, ML_DIM - quarter, 1), pltpu.roll(y, quarter, 1))
            y = y * cos_ref[...] + partner * sin_ref[...]
        outs.append(y)
    return outs


def _ml_chain(qs, kts, vs, rows, gi_c, gi_r, bc, br, ci, ii, head, backward, cn_ref, m_ref, slot, lane, tri):
    L = ML_L
    pair = slice((head // 2) * LANES, (head // 2 + 1) * LANES)
    own = (lane < HEAD_DIM) if head % 2 == 0 else (lane >= HEAD_DIM)
    one_lane = HEAD_DIM if head % 2 == 0 else 0
    q2 = jnp.where(own, qs[rows, pair], jnp.zeros((), jnp.bfloat16))
    v1 = jnp.where(own, vs[rows, pair].astype(jnp.float32), jnp.where(lane == one_lane, 1.0, 0.0))
    own_rows = lax.broadcasted_iota(jnp.int32, (LANES, L), 0)
    own_rows = (own_rows < HEAD_DIM) if head % 2 == 0 else (own_rows >= HEAD_DIM)
    kt2 = jnp.where(own_rows, kts[pair, rows], jnp.zeros((), jnp.bfloat16))
    m_prev = m_ref[slot]
    cn = cn_ref[slot]
    bcol = bc[:, ci:ci + 1]
    brow = br[ci:ci + 1, :]
    log_d = jnp.where(tri, bcol - brow + gi_r[ii:ii + 1, :], -jnp.inf)
    inter = bcol + m_prev
    m_t = jnp.maximum(inter, jnp.max(log_d, axis=-1, keepdims=True))
    s = jnp.dot(q2, kt2, preferred_element_type=jnp.float32)
    w_intra = (s * jnp.exp(log_d - m_t)).astype(jnp.bfloat16)
    nd = (jnp.dot(w_intra, v1.astype(jnp.bfloat16), preferred_element_type=jnp.float32)
          + jnp.exp(inter - m_t) * jnp.dot(q2, cn.astype(jnp.bfloat16), preferred_element_type=jnp.float32))
    den = nd[:, one_lane:one_lane + 1]
    h = nd / jnp.maximum(jnp.abs(den), jnp.exp(-m_t))
    b_last = bc[0:1, ci:ci + 1] if backward else bc[L - 1:L, ci:ci + 1]
    log_w = b_last - bcol + gi_c[:, ii:ii + 1]
    m_new = jnp.maximum(b_last + m_prev, jnp.max(log_w, axis=0, keepdims=True))
    wv = (jnp.exp(log_w - m_new) * v1).astype(jnp.bfloat16)
    cn_ref[slot] = jnp.exp(b_last + m_prev - m_new) * cn + jnp.dot(kt2, wv, preferred_element_type=jnp.float32)
    m_ref[slot] = m_new
    return h


def _ml_scan(T, qs, kts, vs, gc_ref, gr_ref, gbc, gbr, hf, hb, cn_ref, m_ref, tril, triu):
    L = ML_L
    nc = T // L
    lane = lax.broadcasted_iota(jnp.int32, (L, LANES), 1)
    t_idx = lax.broadcasted_iota(jnp.int32, (L, L), 0)
    s_idx = lax.broadcasted_iota(jnp.int32, (L, L), 1)
    hi = lax.Precision.HIGHEST

    def body(c, carry):
        for backward in (False, True):
            cc = (nc - 1 - c) if backward else c
            rows = pl.ds(pl.multiple_of(cc * L, L), L)
            gc = gc_ref[0, rows, :] + gbc
            gr = gr_ref[0, :, rows] + gbr
            bc = jnp.dot(triu if backward else tril, _log_sigmoid(gc), preferred_element_type=jnp.float32, precision=hi)
            br = jnp.dot(_log_sigmoid(gr), tril if backward else triu, preferred_element_type=jnp.float32, precision=hi)
            tri = (s_idx >= t_idx) if backward else (s_idx <= t_idx)
            out = hb if backward else hf
            base = 2 * ML_HEADS if backward else 0
            for pair in range(ML_HEADS // 2):
                hs = []
                for head in (2 * pair, 2 * pair + 1):
                    hs.append(_ml_chain(qs, kts, vs, rows, gc, gr, bc, br, base + ML_HEADS + head, base + head, head,
                                        backward, cn_ref, m_ref, (ML_HEADS if backward else 0) + head, lane, tri))
                out[rows, pair * LANES:(pair + 1) * LANES] = jnp.where(lane < HEAD_DIM, hs[0], hs[1])
        return carry

    lax.fori_loop(0, nc, body, 0)


def _mlstm_kernel(q_ref, k_ref, v_ref, o_ref, gc_ref, gr_ref, cq_ref, ck_ref, cv_ref, co_ref, cgc_ref, cgr_ref,
                  cos_ref, sin_ref, conv_ref, gbc_ref, gbr_ref, ng_ref, e_ref, tril_ref, triu_ref,
                  y_ref, cy_ref, qs, kts, vs, hf, hb, cqs, ckts, cvs, chf, chb, cn_ref, m_ref):
    cn_ref[...] = jnp.zeros_like(cn_ref)
    m_ref[...] = jnp.zeros_like(m_ref)
    tril, triu = tril_ref[...], triu_ref[...]

    def stage(q_r, k_r, v_r, rope, qd, ktd, vd):
        qp, kp = _ml_prep(q_r[0], k_r[0], conv_ref, cos_ref if rope else None, sin_ref if rope else None)
        qd[...] = qp.astype(jnp.bfloat16)
        ktd[...] = (kp * HEAD_DIM ** -0.5).T.astype(jnp.bfloat16)
        vd[...] = v_r[0].astype(jnp.bfloat16)

    def finish(h_f, h_b, o_r, y_r):
        h = h_f[...] + h_b[...]
        ss = jnp.dot(h * h, e_ref[...], preferred_element_type=jnp.float32, precision=lax.Precision.HIGHEST)
        y_r[0] = h * lax.rsqrt(ss * (1.0 / HEAD_DIM) + EPS) * ng_ref[...] * jax.nn.sigmoid(o_r[0])

    gbc, gbr = gbc_ref[...], gbr_ref[...]
    stage(cq_ref, ck_ref, cv_ref, False, cqs, ckts, cvs)
    _ml_scan(cq_ref.shape[1], cqs, ckts, cvs, cgc_ref, cgr_ref, gbc, gbr, chf, chb, cn_ref, m_ref, tril, triu)
    finish(chf, chb, co_ref, cy_ref)
    stage(q_ref, k_ref, v_ref, True, qs, kts, vs)
    _ml_scan(q_ref.shape[1], qs, kts, vs, gc_ref, gr_ref, gbc, gbr, hf, hb, cn_ref, m_ref, tril, triu)
    finish(hf, hb, o_ref, y_ref)


def _rope_tables(T):
    t = jnp.arange(T)
    row = (t // GRID_W).astype(jnp.float32)
    col = (t % GRID_W).astype(jnp.float32)
    nf = HEAD_DIM // 4
    inv = ROPE_BASE ** (-jnp.arange(nf, dtype=jnp.float32) / nf)
    cr, sr, cc, sc = jnp.cos(row[:, None] * inv), jnp.sin(row[:, None] * inv), jnp.cos(col[:, None] * inv), \
        jnp.sin(col[:, None] * inv)
    cos = jnp.concatenate([cr, cr, cc, cc], axis=-1)
    sin = jnp.concatenate([-sr, sr, -sc, sc], axis=-1)
    return jnp.tile(cos, (1, ML_HEADS)), jnp.tile(sin, (1, ML_HEADS))


def _mlstm_mix(lq, lk, lv, lo, lg, lgr, cq, ck, cv, co, cg, cgr, conv_w, gate_b, norm_g):
    B, T, d = lq.shape
    Lc = cq.shape[1]
    L = ML_L
    assert T % L == 0 and Lc % L == 0
    cos, sin = _rope_tables(T)
    lane_head = np.arange(d) // HEAD_DIM
    same_head = jnp.asarray((lane_head[:, None] == lane_head[None, :]).astype(np.float32))
    tril = jnp.asarray(np.tril(np.ones((L, L), np.float32)))
    seq = lambda n: pl.BlockSpec((1, n, d), lambda b: (b, 0, 0))
    full = lambda a: pl.BlockSpec(a.shape, lambda b: (0,) * a.ndim)
    consts = [cos, sin, conv_w, gate_b[None, :], gate_b[:, None], norm_g[None, :], same_head, tril, tril.T]
    f32, bf16 = jnp.float32, jnp.bfloat16
    return pl.pallas_call(
        _mlstm_kernel,
        grid=(B,),
        in_specs=[seq(T)] * 4 + [pl.BlockSpec((1, T, ML_N_GATES), lambda b: (b, 0, 0)),
                                 pl.BlockSpec((1, ML_N_GATES, T), lambda b: (b, 0, 0))]
        + [seq(Lc)] * 4 + [pl.BlockSpec((1, Lc, ML_N_GATES), lambda b: (b, 0, 0)),
                           pl.BlockSpec((1, ML_N_GATES, Lc), lambda b: (b, 0, 0))]
        + [full(a) for a in consts],
        out_specs=[seq(T), seq(Lc)],
        out_shape=[jax.ShapeDtypeStruct((B, T, d), f32), jax.ShapeDtypeStruct((B, Lc, d), f32)],
        scratch_shapes=[pltpu.VMEM((T, d), bf16), pltpu.VMEM((d, T), bf16), pltpu.VMEM((T, d), bf16),
                        pltpu.VMEM((T, d), f32), pltpu.VMEM((T, d), f32),
                        pltpu.VMEM((Lc, d), bf16), pltpu.VMEM((d, Lc), bf16), pltpu.VMEM((Lc, d), bf16),
                        pltpu.VMEM((Lc, d), f32), pltpu.VMEM((Lc, d), f32),
                        pltpu.VMEM((2 * ML_HEADS, LANES, LANES), f32), pltpu.VMEM((2 * ML_HEADS, 1, 1), f32)],
        compiler_params=pltpu.CompilerParams(dimension_semantics=("parallel",), vmem_limit_bytes=VMEM_LIMIT_BYTES),
        name="mlstm",
    )(lq, lk, lv, lo, lg, lgr, cq, ck, cv, co, cg, cgr, *consts)


def _ada_kernel(c_ref, w_ref, b_ref, o_ref):
    c = c_ref[...]
    act = (c * jax.nn.sigmoid(c)).astype(jnp.bfloat16)
    o_ref[...] = jnp.dot(act, w_ref[0].astype(jnp.bfloat16), preferred_element_type=jnp.float32) + b_ref[0]


def _ada_mod(cond, w_all, b_all, layer):
    rows, d = cond.shape
    n = w_all.shape[2]
    tile = ADA_COL_TILE
    assert n % tile == 0
    return pl.pallas_call(
        _ada_kernel,
        grid=(n // tile,),
        in_specs=[pl.BlockSpec((rows, d), lambda j: (0, 0)), pl.BlockSpec((1, d, tile), lambda j: (layer, 0, j)),
                  pl.BlockSpec((1, 1, tile), lambda j: (layer, 0, j))],
        out_specs=pl.BlockSpec((rows, tile), lambda j: (0, j)),
        out_shape=jax.ShapeDtypeStruct((rows, n), jnp.float32),
        compiler_params=pltpu.CompilerParams(dimension_semantics=("parallel",)),
        name="ada_mod",
    )(cond, w_all, b_all[:, None, :])


_IN_SPLITS = (("q", NA_DIM), ("k", NA_DIM), ("v", NA_DIM), ("pool", POOL_DIM), ("mq", ML_DIM), ("mk", ML_DIM),
              ("mv", ML_DIM), ("mo", ML_DIM), ("gates", ML_N_GATES))
IN_COLS = sum(n for _, n in _IN_SPLITS)
IN_COLS_PAD = -(-IN_COLS // LANES) * LANES


def _in_proj_kernel(x_ref, sc_ref, sh_ref, g_ref, w_ref, wgt_ref, gq_ref, gk_ref, e_ref,
                    q_ref, k_ref, v_ref, pool_ref, mq_ref, mk_ref, mv_ref, mo_ref, gc_ref, gr_ref):
    x = x_ref[0]
    y = x * lax.rsqrt(jnp.mean(x * x, axis=-1, keepdims=True) + EPS) * g_ref[...]
    xn = (y * (1.0 + sc_ref[0]) + sh_ref[0]).astype(jnp.bfloat16)
    p = jnp.dot(xn, w_ref[...], preferred_element_type=jnp.float32)
    off = {}
    o = 0
    for name, n in _IN_SPLITS:
        off[name] = slice(o, o + n)
        o += n

    def head_norm(u, gain):
        sq = u * u
        hi = sq.astype(jnp.bfloat16)
        lo = (sq - hi.astype(jnp.float32)).astype(jnp.bfloat16)
        ss = (jnp.dot(hi, e_ref[...], preferred_element_type=jnp.float32)
              + jnp.dot(lo, e_ref[...], preferred_element_type=jnp.float32))
        return u * lax.rsqrt(ss * (1.0 / HEAD_DIM) + EPS) * gain

    q_ref[0] = (head_norm(p[:, off["q"]], gq_ref[...]) * HEAD_DIM ** -0.5).astype(jnp.bfloat16)
    k_ref[0] = head_norm(p[:, off["k"]], gk_ref[...]).astype(jnp.bfloat16)
    v_ref[0] = p[:, off["v"]].astype(jnp.bfloat16)
    pool_ref[0] = p[:, off["pool"]]
    mq_ref[0] = p[:, off["mq"]]
    mk_ref[0] = p[:, off["mk"]]
    mv_ref[0] = p[:, off["mv"]]
    mo_ref[0] = p[:, off["mo"]]
    gc_ref[0] = p[:, off["gates"]]
    gr_ref[0] = lax.dot_general(wgt_ref[...], xn, (((1,), (1,)), ((), ())), preferred_element_type=jnp.float32)


def _in_proj(x3, sc, sh, norm_g, w_in, gq, gk):
    B, T, d = x3.shape
    tn = min(IN_TOKEN_TILE, T)
    assert T % tn == 0
    w_bf = jnp.pad(w_in, ((0, 0), (0, IN_COLS_PAD - IN_COLS))).astype(jnp.bfloat16)
    wgt_bf = w_in[:, IN_COLS - ML_N_GATES:].T.astype(jnp.bfloat16)
    lane_head = np.arange(NA_DIM) // HEAD_DIM
    same_head = jnp.asarray((lane_head[:, None] == lane_head[None, :]).astype(np.float32)).astype(jnp.bfloat16)
    mod = pl.BlockSpec((1, 1, d), lambda b, t: (b, 0, 0))
    full = lambda a: pl.BlockSpec(a.shape, lambda b, t: (0,) * a.ndim)
    tok = lambda n: pl.BlockSpec((1, tn, n), lambda b, t: (b, t, 0))
    shp = lambda n, dt: jax.ShapeDtypeStruct((B, T, n), dt)
    f32, bf16 = jnp.float32, jnp.bfloat16
    consts = [norm_g[None, :], w_bf, wgt_bf, jnp.tile(gq, NA_HEADS)[None, :], jnp.tile(gk, NA_HEADS)[None, :], same_head]
    outs = pl.pallas_call(
        _in_proj_kernel,
        grid=(B, T // tn),
        in_specs=[pl.BlockSpec((1, tn, d), lambda b, t: (b, t, 0)), mod, mod] + [full(a) for a in consts],
        out_specs=[tok(NA_DIM)] * 3 + [tok(POOL_DIM)] + [tok(ML_DIM)] * 4 + [tok(ML_N_GATES)]
        + [pl.BlockSpec((1, ML_N_GATES, tn), lambda b, t: (b, 0, t))],
        out_shape=[shp(NA_DIM, bf16)] * 3 + [shp(POOL_DIM, f32)] + [shp(ML_DIM, f32)] * 4 + [shp(ML_N_GATES, f32)]
        + [jax.ShapeDtypeStruct((B, ML_N_GATES, T), f32)],
        compiler_params=pltpu.CompilerParams(
            dimension_semantics=("parallel", "parallel"), vmem_limit_bytes=VMEM_LIMIT_BYTES),
        name="in_proj",
    )(x3, sc, sh, *consts)
    return dict(zip(("q", "k", "v", "pool", "mq", "mk", "mv", "mo", "gc", "gr"), outs))


def kernel(x, c, ctx, c_ctx, w_ada, b_ada, norm1_g, w_in, ml_gate_b, na_q_g, na_k_g, na_rpb, pool_w,
           pool_scale, ml_conv, ml_norm_g, w_out, norm2_g, peer_wq, peer_keys, peer_u, peer_v):
    depth = w_in.shape[0]
    B, S, D = x.shape
    Lc = ctx.shape[1]
    hc = ctx
    u_bf = _cast_bf16(peer_u.reshape(depth * PEER_EXPERTS, D))
    v_bf = _cast_bf16(peer_v.reshape(depth * PEER_EXPERTS, D))
    for l in range(depth):
        need_ctx = l < depth - 1
        wqt_bf = peer_wq[l].astype(jnp.bfloat16).T
        keys_bf = peer_keys[l].astype(jnp.bfloat16).reshape(2 * PEER_HEADS, PEER_KEYS, PEER_DKEY // 2)
        mod = _ada_mod(jnp.concatenate([c, c_ctx[None, :]], axis=0), w_ada, b_ada, l).reshape(B + 1, 6, 1, D)
        sh1, sc1, g1, sh2, sc2, g2 = (mod[:B, i] for i in range(6))
        csh1, csc1, cg1, csh2, csc2, cg2 = (mod[B:, i] for i in range(6))
        hc1 = hc.reshape(1, B * Lc, D)
        a = _in_proj(x, sc1, sh1, norm1_g[l], w_in[l], na_q_g[l], na_k_g[l])
        ac = _in_proj(hc1, csc1, csh1, norm1_g[l], w_in[l], na_q_g[l], na_k_g[l])
        ac = {k: v.reshape(B, Lc, v.shape[-1]) for k, v in ac.items() if k != "gr"} | {
            "gr": jnp.transpose(ac["gr"].reshape(ML_N_GATES, B, Lc), (1, 0, 2))}
        na_lat = _na_latent(a["q"], a["k"], a["v"], ac["k"], ac["v"], na_rpb[l])
        pool_lat = _pool_mix(a["pool"], pool_w[l], pool_scale[l])
        ml_lat, ml_ctx = _mlstm_mix(a["mq"], a["mk"], a["mv"], a["mo"], a["gc"], a["gr"],
                                    ac["mq"], ac["mk"], ac["mv"], ac["mo"], ac["gc"], ac["gr"],
                                    ml_conv[l], ml_gate_b[l], ml_norm_g[l])
        wo_bf = w_out[l].astype(jnp.bfloat16)
        x = _mix_and_peer(x, g1, na_lat, pool_lat, ml_lat, wo_bf, sc2, sh2, g2, norm2_g[l], wqt_bf, keys_bf,
                          u_bf, v_bf, l)
        if need_ctx:
            na_ctx = _na_context(ac["q"], ac["k"], ac["v"])
            pool_ctx = _pool_mix(ac["pool"], pool_w[l], pool_scale[l])
            flat = lambda t: t.reshape(1, B * Lc, t.shape[-1])
            hc = _mix_and_peer(hc1, cg1, flat(na_ctx), flat(pool_ctx), flat(ml_ctx), wo_bf, csc2, csh2, cg2,
                               norm2_g[l], wqt_bf, keys_bf, u_bf, v_bf, l).reshape(B, Lc, D)
    return x
```

```python
import functools

import numpy as np
import jax
import jax.numpy as jnp
from jax import lax
from jax.experimental import pallas as pl
from jax.experimental.pallas import tpu as pltpu

D_MODEL = 1024
GRID_W = 64
HEAD_DIM = 64
NA_DIM = D_MODEL // 2
NA_HEADS = NA_DIM // HEAD_DIM
NA_WIN_ROWS = 8
NA_WIN_COLS = 16
POOL_DIM = D_MODEL // 4
POOL_WINDOWS = (2, 4, 8, 16)
POOL_GROUPS = len(POOL_WINDOWS)
POOL_GROUP_DIM = POOL_DIM // POOL_GROUPS
ML_DIM = D_MODEL // 4
ML_HEADS = ML_DIM // HEAD_DIM
ML_CONV_W = 5
ML_N_GATES = 4 * ML_HEADS
PEER_KEYS = 128
PEER_EXPERTS = PEER_KEYS * PEER_KEYS
PEER_HEADS = 8
PEER_TOPK = 16
PEER_DKEY = 256
ROPE_BASE = 10000.0
EPS = 1e-6

LANES = 128
SUBLANES = 8
V7X_VMEM_BYTES = 64 * 1024 * 1024

PEER_TOKEN_TILE = 1024
PEER_EXPERT_TILE = 1024
ROUTE_TOKEN_TILE = 512
ROUTE_LANE_CHUNK = 2 * LANES
IN_TOKEN_TILE = 256
NA_ROWS_PER_STEP = 4
NA_UNION_ROWS = -(-(NA_WIN_ROWS + NA_ROWS_PER_STEP - 1) // 2) * 2
ADA_COL_TILE = 1536
CAST_ROW_TILE = 1024
POOL_BLOCK = 128
POOL_HALO = SUBLANES
ML_L = 256
VMEM_LIMIT_BYTES = V7X_VMEM_BYTES - 8 * 1024 * 1024

_GELU_C0 = float(np.sqrt(2.0 / np.pi))
_GELU_C1 = 0.044715 * _GELU_C0


def _gelu_tanh(x):
    inner = x * (_GELU_C0 + _GELU_C1 * (x * x))
    hx = 0.5 * x
    return hx + hx * jnp.tanh(inner)


def _peer_dense_kernel(x_ref, u_ref, v_ref, cnt_ref, e1_ref, rank_ref, e2_ref, o_ref):
    j = pl.program_id(1)

    @pl.when(j == 0)
    def _():
        o_ref[...] = jnp.zeros_like(o_ref)

    te = u_ref.shape[0]
    act = lax.dot_general(u_ref[...], x_ref[...], (((1,), (1,)), ((), ())),
                          preferred_element_type=jnp.float32)
    gates = []
    for k in range(te // PEER_KEYS):
        g = None
        for h in range(PEER_HEADS):
            cnt = cnt_ref[h, k:k + 1, :].astype(jnp.bfloat16)
            e1 = e1_ref[h, k:k + 1, :].astype(jnp.bfloat16)
            term = jnp.where(rank_ref[h] < cnt, e2_ref[h], jnp.zeros((), e2_ref.dtype)) * e1
            g = term if g is None else g + term
        gates.append(g)
    gate = jnp.concatenate(gates, axis=0)
    p = _gelu_tanh(act.astype(jnp.bfloat16)) * gate
    o_ref[...] += lax.dot_general(v_ref[...], p, (((0,), (0,)), ((), ())), preferred_element_type=jnp.float32)


def _cast_kernel(x_ref, o_ref):
    o_ref[...] = x_ref[...].astype(o_ref.dtype)


def _cast_bf16(w):
    rows, cols = w.shape
    tr = CAST_ROW_TILE
    assert rows % tr == 0
    blk = pl.BlockSpec((tr, cols), lambda i: (i, 0))
    return pl.pallas_call(
        _cast_kernel, grid=(rows // tr,), in_specs=[blk], out_specs=blk,
        out_shape=jax.ShapeDtypeStruct(w.shape, jnp.bfloat16),
        compiler_params=pltpu.CompilerParams(dimension_semantics=("parallel",)),
        name="cast_bf16",
    )(w)


def _peer_dense(tok, u_bf, v_bf, layer, cnt, e1, rank, e2):
    n, d = tok.shape
    tn, te = PEER_TOKEN_TILE, PEER_EXPERT_TILE
    assert n % tn == 0 and PEER_EXPERTS % te == 0 and te % PEER_KEYS == 0
    first = layer * (PEER_EXPERTS // te)
    tab_a = pl.BlockSpec((PEER_HEADS, te // PEER_KEYS, tn), lambda i, j: (0, j, i))
    tab_b = pl.BlockSpec((PEER_HEADS, PEER_KEYS, tn), lambda i, j: (0, 0, i))
    return pl.pallas_call(
        _peer_dense_kernel,
        grid=(n // tn, PEER_EXPERTS // te),
        in_specs=[
            pl.BlockSpec((tn, d), lambda i, j: (i, 0)),
            pl.BlockSpec((te, d), lambda i, j: (first + j, 0)),
            pl.BlockSpec((te, d), lambda i, j: (first + j, 0)),
            tab_a, tab_a, tab_b, tab_b,
        ],
        out_specs=pl.BlockSpec((d, tn), lambda i, j: (0, i)),
        out_shape=jax.ShapeDtypeStruct((d, n), jnp.float32),
        compiler_params=pltpu.CompilerParams(
            dimension_semantics=("parallel", "arbitrary"), vmem_limit_bytes=VMEM_LIMIT_BYTES),
        name="peer_dense",
    )(tok, u_bf, v_bf, cnt, e1, rank, e2)


def _top16(works, want_ranks):
    lanes = works[0].shape[1]
    rows = lax.broadcasted_iota(jnp.int32, (PEER_TOPK, lanes), 0)
    vals = [jnp.zeros((PEER_TOPK, lanes), jnp.float32) for _ in works]
    ranks = [jnp.full(w.shape, float(PEER_TOPK), jnp.float32) if r else None for w, r in zip(works, want_ranks)]
    tops = [[] for _ in works]
    works = list(works)
    for i in range(PEER_TOPK):
        ms = [jnp.max(w, axis=0, keepdims=True) for w in works]
        hits = [w == m for w, m in zip(works, ms)]
        ranks = [None if r is None else jnp.where(hit, float(i), r) for r, hit in zip(ranks, hits)]
        works = [jnp.where(hit, -jnp.inf, w) for w, hit in zip(works, hits)]
        vals = [jnp.where(rows == i, m, v) for v, m in zip(vals, ms)]
        for t, m in zip(tops, ms):
            t.append(m)
    return tops, vals, ranks


def _peer_route_kernel(x_ref, g1_ref, na_ref, pool_ref, ml_ref, wo_ref, sc_ref, sh_ref, g_ref, wqt_ref, keys_ref,
                       xo_ref, tok_ref, cnt_ref, e1_ref, rank_ref, e2_ref, s_scr):
    bf = jnp.bfloat16
    mix = jnp.dot(na_ref[0].astype(bf), wo_ref[0:NA_DIM, :], preferred_element_type=jnp.float32)
    mix += jnp.dot(pool_ref[0].astype(bf), wo_ref[NA_DIM:NA_DIM + POOL_DIM, :], preferred_element_type=jnp.float32)
    mix += jnp.dot(ml_ref[0].astype(bf), wo_ref[NA_DIM + POOL_DIM:, :], preferred_element_type=jnp.float32)
    x = x_ref[0] + g1_ref[0] * mix
    xo_ref[0] = x
    y = x * lax.rsqrt(jnp.mean(x * x, axis=-1, keepdims=True) + EPS) * g_ref[...]
    tok = (y * (1.0 + sc_ref[0]) + sh_ref[0]).astype(jnp.bfloat16)
    tok_ref[...] = tok
    qt = lax.dot_general(wqt_ref[...], tok, (((1,), (1,)), ((), ())),
                         preferred_element_type=jnp.float32).astype(jnp.bfloat16)
    dk = PEER_DKEY // 2
    for hp in range(2 * PEER_HEADS):
        s_scr[hp] = jnp.dot(keys_ref[hp], qt[hp * dk:(hp + 1) * dk], preferred_element_type=jnp.float32)

    cw = ROUTE_LANE_CHUNK
    lane_chunks = x.shape[0] // cw
    half = PEER_TOPK // 2
    row_half = lax.broadcasted_iota(jnp.int32, (half, cw), 0)

    def body(it, carry):
        h = it // lane_chunks
        lanes = pl.ds(pl.multiple_of((it % lane_chunks) * cw, cw), cw)
        s1 = s_scr[2 * h, :, lanes]
        s2 = s_scr[2 * h + 1, :, lanes]
        (t1, t2), (v1, v2), (_, rank) = _top16([s1, s2], [False, True])
        groups = [t1[0] + v2, t1[1] + v2[:half]]
        for i in range(2, half):
            groups.append(jnp.where(row_half < PEER_TOPK // (i + 1), t1[i] + v2[:half], -jnp.inf))
        groups.append(v1[half:] + t2[0])
        (tp,), (vp,), _ = _top16([jnp.concatenate(groups, axis=0)], [False])
        tau = tp[PEER_TOPK - 1]
        z = jnp.sum(jnp.exp(vp - tp[0]), axis=0, keepdims=True)
        kept = jnp.zeros(v1.shape, jnp.float32)
        for jj in range(PEER_TOPK):
            kept = jnp.where(v1 + t2[jj] >= tau, float(jj + 1), kept)
        cnt = jnp.zeros(s1.shape, jnp.float32)
        for i in range(PEER_TOPK):
            cnt = jnp.where(s1 == t1[i], kept[i:i + 1], cnt)
        cnt_ref[h, :, lanes] = cnt
        e1_ref[h, :, lanes] = jnp.exp(s1 - t1[0])
        rank_ref[h, :, lanes] = rank.astype(jnp.bfloat16)
        e2_ref[h, :, lanes] = (jnp.exp(s2 - t2[0]) * (1.0 / z)).astype(jnp.bfloat16)
        return carry

    lax.fori_loop(0, PEER_HEADS * lane_chunks, body, 0)


def _peer_route(x3, gate1, na, pool, ml, wo_bf, sc, sh, g, wqt_bf, keys_bf):
    B, T, d = x3.shape
    tn = min(ROUTE_TOKEN_TILE, T)
    assert T % tn == 0 and tn % ROUTE_LANE_CHUNK == 0
    nt = T // tn
    n = B * T
    tab = pl.BlockSpec((PEER_HEADS, PEER_KEYS, tn), lambda b, t: (0, 0, b * nt + t))
    tab_shape = lambda dt: jax.ShapeDtypeStruct((PEER_HEADS, PEER_KEYS, n), dt)
    mod = pl.BlockSpec((1, 1, d), lambda b, t: (b, 0, 0))
    tok = lambda c: pl.BlockSpec((1, tn, c), lambda b, t: (b, t, 0))
    const = lambda a: pl.BlockSpec(a.shape, lambda b, t: (0,) * a.ndim, pipeline_mode=pl.Buffered(1))
    return pl.pallas_call(
        _peer_route_kernel,
        grid=(B, nt),
        in_specs=[tok(d), mod, tok(NA_DIM), tok(POOL_DIM), tok(ML_DIM), const(wo_bf), mod, mod,
                  pl.BlockSpec((1, d), lambda b, t: (0, 0)), const(wqt_bf), const(keys_bf)],
        out_specs=[tok(d), pl.BlockSpec((tn, d), lambda b, t: (b * nt + t, 0)), tab, tab, tab, tab],
        out_shape=[jax.ShapeDtypeStruct(x3.shape, x3.dtype), jax.ShapeDtypeStruct((n, d), jnp.bfloat16),
                   tab_shape(jnp.float32), tab_shape(jnp.float32), tab_shape(jnp.bfloat16), tab_shape(jnp.bfloat16)],
        scratch_shapes=[pltpu.VMEM((2 * PEER_HEADS, PEER_KEYS, tn), jnp.float32)],
        compiler_params=pltpu.CompilerParams(
            dimension_semantics=("parallel", "parallel"), vmem_limit_bytes=VMEM_LIMIT_BYTES),
        name="peer_route",
    )(x3, gate1, na, pool, ml, wo_bf, sc, sh, g, wqt_bf, keys_bf)


def _residual_t_kernel(x_ref, g_ref, yt_ref, o_ref):
    o_ref[0] = x_ref[0] + g_ref[0] * yt_ref[...].T


def _residual_t(x3, gate, yt):
    B, T, d = x3.shape
    tn = min(ROUTE_TOKEN_TILE, T)
    nt = T // tn
    return pl.pallas_call(
        _residual_t_kernel,
        grid=(B, nt),
        in_specs=[pl.BlockSpec((1, tn, d), lambda b, t: (b, t, 0)),
                  pl.BlockSpec((1, 1, d), lambda b, t: (b, 0, 0)),
                  pl.BlockSpec((d, tn), lambda b, t: (0, b * nt + t))],
        out_specs=pl.BlockSpec((1, tn, d), lambda b, t: (b, t, 0)),
        out_shape=jax.ShapeDtypeStruct(x3.shape, x3.dtype),
        compiler_params=pltpu.CompilerParams(dimension_semantics=("parallel", "parallel")),
        name="residual_t",
    )(x3, gate, yt)


def _mix_and_peer(x3, gate1, na, pool, ml, wo_bf, sc, sh, gate2, norm_g, wqt_bf, keys_bf, u_bf, v_bf, layer):
    x1, tok, cnt, e1, rank, e2 = _peer_route(x3, gate1, na, pool, ml, wo_bf, sc, sh, norm_g[None, :], wqt_bf, keys_bf)
    out_t = _peer_dense(tok, u_bf, v_bf, layer, cnt, e1, rank, e2)
    return _residual_t(x1, gate2, out_t)


def _attend_pair(q2, segments, lane):
    zero = jnp.zeros((), q2.dtype)
    outs = []
    for half in range(2):
        own = (lane < HEAD_DIM) if half == 0 else (lane >= HEAD_DIM)
        qh = jnp.where(own, q2, zero)
        scores = []
        for k2, _, bias in segments:
            s = lax.dot_general(qh, k2, (((1,), (1,)), ((), ())), preferred_element_type=jnp.float32)
            scores.append(s if bias is None else s + bias[half])
        m = functools.reduce(jnp.maximum, [jnp.max(s, axis=-1, keepdims=True) for s in scores])
        ps = [jnp.exp(s - m) for s in scores]
        denom = functools.reduce(jnp.add, [jnp.sum(p, axis=-1, keepdims=True) for p in ps])
        acc = functools.reduce(jnp.add, [jnp.dot(p.astype(v2.dtype), v2, preferred_element_type=jnp.float32)
                                         for p, (_, v2, _) in zip(ps, segments)])
        outs.append(acc / denom)
    return jnp.where(lane < HEAD_DIM, outs[0], outs[1])


def _na_latent_kernel(q_ref, k_ref, v_ref, kc_ref, vc_ref, bank_ref, o_ref):
    G, KR = NA_ROWS_PER_STEP, NA_WIN_ROWS
    n_rows = k_ref.shape[1] // GRID_W
    first = pl.program_id(1) * G
    u0 = jnp.minimum(jnp.clip(first - KR // 2, 0, n_rows - KR), n_rows - NA_UNION_ROWS)
    keys = pl.ds(pl.multiple_of(u0 * GRID_W, GRID_W), NA_UNION_ROWS * GRID_W)
    key_row = u0 + lax.broadcasted_iota(jnp.int32, (1, NA_UNION_ROWS * GRID_W), 1) // GRID_W
    lane = lax.broadcasted_iota(jnp.int32, (q_ref.shape[1], LANES), 1)

    def head_bias(h):
        rows = []
        for i in range(G):
            r = first + i
            r0 = jnp.clip(r - KR // 2, 0, n_rows - KR)
            pairs = [bank_ref[h, jnp.clip(u0 + 2 * t + 1 - r + KR - 1, 0, 2 * KR - 1)]
                     for t in range(NA_UNION_ROWS // 2)]
            in_strip = (key_row >= r0) & (key_row < r0 + KR)
            rows.append(jnp.where(in_strip, jnp.concatenate(pairs, axis=1), -jnp.inf))
        return jnp.concatenate(rows, axis=0)

    for p in range(NA_DIM // LANES):
        lanes = slice(p * LANES, (p + 1) * LANES)
        segs = [(k_ref[0, keys, lanes], v_ref[0, keys, lanes], (head_bias(2 * p), head_bias(2 * p + 1))),
                (kc_ref[0, :, lanes], vc_ref[0, :, lanes], None)]
        o_ref[0, :, lanes] = _attend_pair(q_ref[0, :, lanes], segs, lane)


def _na_bias_bank(rpb):
    cq = np.arange(GRID_W)
    c0 = np.clip(cq - NA_WIN_COLS // 2, 0, GRID_W - NA_WIN_COLS)
    in_win = (cq[None, :] >= c0[:, None]) & (cq[None, :] < c0[:, None] + NA_WIN_COLS)
    bc = np.clip(cq[None, :] - cq[:, None] + (NA_WIN_COLS - 1), 0, 2 * NA_WIN_COLS - 2)
    toe = jnp.where(in_win[None, None], rpb[:, :, bc], -jnp.inf).astype(jnp.float32)
    none = jnp.full((NA_HEADS, 1, GRID_W, GRID_W), -jnp.inf, jnp.float32)
    return jnp.concatenate([jnp.concatenate([none, toe], axis=1), jnp.concatenate([toe, none], axis=1)], axis=3)


def _na_latent(q, k, v, k_ctx, v_ctx, rpb):
    B, S, d = q.shape
    Lc = k_ctx.shape[1]
    n_rows = S // GRID_W
    assert n_rows >= NA_UNION_ROWS and n_rows % NA_ROWS_PER_STEP == 0
    bank = _na_bias_bank(rpb)
    tq = NA_ROWS_PER_STEP * GRID_W
    whole = lambda L: pl.BlockSpec((1, L, d), lambda b, g: (b, 0, 0))
    return pl.pallas_call(
        _na_latent_kernel,
        grid=(B, n_rows // NA_ROWS_PER_STEP),
        in_specs=[pl.BlockSpec((1, tq, d), lambda b, g: (b, g, 0)), whole(S), whole(S), whole(Lc), whole(Lc),
                  pl.BlockSpec(bank.shape, lambda b, g: (0, 0, 0, 0))],
        out_specs=pl.BlockSpec((1, tq, d), lambda b, g: (b, g, 0)),
        out_shape=jax.ShapeDtypeStruct((B, S, d), jnp.float32),
        compiler_params=pltpu.CompilerParams(
            dimension_semantics=("parallel", "arbitrary"), vmem_limit_bytes=VMEM_LIMIT_BYTES),
        name="na_latent",
    )(q, k, v, k_ctx, v_ctx, bank)


def _na_context_kernel(q_ref, k_ref, v_ref, o_ref):
    lane = lax.broadcasted_iota(jnp.int32, (q_ref.shape[1], LANES), 1)
    for p in range(NA_DIM // LANES):
        lanes = slice(p * LANES, (p + 1) * LANES)
        o_ref[0, :, lanes] = _attend_pair(q_ref[0, :, lanes], [(k_ref[0, :, lanes], v_ref[0, :, lanes], None)], lane)


def _na_context(q, k, v):
    B, Lc, d = q.shape
    blk = pl.BlockSpec((1, Lc, d), lambda b: (b, 0, 0))
    return pl.pallas_call(
        _na_context_kernel,
        grid=(B,),
        in_specs=[blk, blk, blk],
        out_specs=blk,
        out_shape=jax.ShapeDtypeStruct((B, Lc, d), jnp.float32),
        compiler_params=pltpu.CompilerParams(dimension_semantics=("parallel",)),
        name="na_context",
    )(q, k, v)


def _pool_matrices(T):
    tb, halo = POOL_BLOCK, POOL_HALO
    mats = np.zeros((len(POOL_WINDOWS), 3, tb, tb + 2 * halo), np.float32)
    for wi, w in enumerate(POOL_WINDOWS):
        for kind, start in enumerate((0, tb, T - tb)):
            for i in range(tb):
                t = start + i
                lo = min(max(t - w // 2, 0), T - 1)
                hi = min(max(t + (w - w // 2 - 1), 0), T - 1)
                mats[wi, kind, i, lo - start + halo:hi - start + halo + 1] = 1.0 / (hi - lo + 1)
                mats[wi, kind, i, i + halo] -= 1.0
    return jnp.asarray(mats)


def _pool_kernel(u_ref, a_ref, w_ref, s_ref, o_ref):
    T = u_ref.shape[1]
    tb, halo = POOL_BLOCK, POOL_HALO
    nb = T // tb
    group = lax.broadcasted_iota(jnp.int32, (tb, POOL_DIM), 1) // POOL_GROUP_DIM
    zeros = jnp.zeros((halo, POOL_DIM), jnp.float32)
    for blk in range(nb):
        kind = 0 if blk == 0 else (2 if blk == nb - 1 else 1)
        before = zeros if blk == 0 else u_ref[0, blk * tb - halo:blk * tb, :]
        after = zeros if blk == nb - 1 else u_ref[0, (blk + 1) * tb:(blk + 1) * tb + halo, :]
        xcat = jnp.concatenate([before, u_ref[0, blk * tb:(blk + 1) * tb, :], after], axis=0)
        pooled = None
        for wi in range(len(POOL_WINDOWS)):
            pw = jnp.dot(a_ref[wi, kind], xcat, preferred_element_type=jnp.float32, precision=lax.Precision.HIGHEST)
            pooled = pw if pooled is None else jnp.where(group == wi, pw, pooled)
        mixed = jnp.dot(pooled.astype(jnp.bfloat16), w_ref[...], preferred_element_type=jnp.float32)
        o_ref[0, blk * tb:(blk + 1) * tb, :] = mixed * s_ref[...]


def _pool_mix(u, pool_w, pool_scale):
    B, T, d = u.shape
    assert T % POOL_BLOCK == 0 and T >= 2 * POOL_BLOCK
    w_bd = jax.scipy.linalg.block_diag(*[pool_w[g] for g in range(POOL_GROUPS)]).astype(jnp.bfloat16)
    mats = _pool_matrices(T)
    blk = pl.BlockSpec((1, T, d), lambda b: (b, 0, 0))
    return pl.pallas_call(
        _pool_kernel,
        grid=(B,),
        in_specs=[blk, pl.BlockSpec(mats.shape, lambda b: (0, 0, 0, 0)), pl.BlockSpec((d, d), lambda b: (0, 0)),
                  pl.BlockSpec((1, d), lambda b: (0, 0))],
        out_specs=blk,
        out_shape=jax.ShapeDtypeStruct((B, T, d), jnp.float32),
        compiler_params=pltpu.CompilerParams(dimension_semantics=("parallel",)),
        name="pool_mix",
    )(u, mats, w_bd, pool_scale[None, :])


def _log_sigmoid(x):
    return jnp.minimum(x, 0.0) - jnp.log1p(jnp.exp(-jnp.abs(x)))


def _ml_prep(q, k, conv_ref, cos_ref, sin_ref):
    T = q.shape[0]
    row = lax.broadcasted_iota(jnp.int32, (T, 1), 0)
    lane = lax.broadcasted_iota(jnp.int32, (1, ML_DIM), 1) % (HEAD_DIM // 2)
    outs = []
    for idx, x in enumerate((q, k)):
        acc = None
        for j in range(ML_CONV_W):
            d = j - ML_CONV_W // 2
            w = conv_ref[j:j + 1, idx * ML_DIM:(idx + 1) * ML_DIM]
            if d == 0:
                term = x * w
            else:
                shifted = pltpu.roll(x, (-d) % T, 0)
                term = jnp.where((row + d >= 0) & (row + d < T), shifted, 0.0) * w
            acc = term if acc is None else acc + term
        y = acc * jax.nn.sigmoid(acc)
        if cos_ref is not None:
            quarter = HEAD_DIM // 4
            partner = jnp.where(lane < quarter, pltpu.roll(y, ML_DIM - quarter, 1), pltpu.roll(y, quarter, 1))
            y = y * cos_ref[...] + partner * sin_ref[...]
        outs.append(y)
    return outs


def _ml_chain(qs, kts, vs, rows, gi_c, gi_r, bc, br, ci, ii, head, backward, cn_ref, m_ref, slot, lane, tri):
    L = ML_L
    pair = slice((head // 2) * LANES, (head // 2 + 1) * LANES)
    own = (lane < HEAD_DIM) if head % 2 == 0 else (lane >= HEAD_DIM)
    one_lane = HEAD_DIM if head % 2 == 0 else 0
    q2 = jnp.where(own, qs[rows, pair], jnp.zeros((), jnp.bfloat16))
    v1 = jnp.where(own, vs[rows, pair].astype(jnp.float32), jnp.where(lane == one_lane, 1.0, 0.0))
    own_rows = lax.broadcasted_iota(jnp.int32, (LANES, L), 0)
    own_rows = (own_rows < HEAD_DIM) if head % 2 == 0 else (own_rows >= HEAD_DIM)
    kt2 = jnp.where(own_rows, kts[pair, rows], jnp.zeros((), jnp.bfloat16))
    m_prev = m_ref[slot]
    cn = cn_ref[slot]
    bcol = bc[:, ci:ci + 1]
    brow = br[ci:ci + 1, :]
    log_d = jnp.where(tri, bcol - brow + gi_r[ii:ii + 1, :], -jnp.inf)
    inter = bcol + m_prev
    m_t = jnp.maximum(inter, jnp.max(log_d, axis=-1, keepdims=True))
    s = jnp.dot(q2, kt2, preferred_element_type=jnp.float32)
    w_intra = (s * jnp.exp(log_d - m_t)).astype(jnp.bfloat16)
    nd = (jnp.dot(w_intra, v1.astype(jnp.bfloat16), preferred_element_type=jnp.float32)
          + jnp.exp(inter - m_t) * jnp.dot(q2, cn.astype(jnp.bfloat16), preferred_element_type=jnp.float32))
    den = nd[:, one_lane:one_lane + 1]
    h = nd / jnp.maximum(jnp.abs(den), jnp.exp(-m_t))
    b_last = bc[0:1, ci:ci + 1] if backward else bc[L - 1:L, ci:ci + 1]
    log_w = b_last - bcol + gi_c[:, ii:ii + 1]
    m_new = jnp.maximum(b_last + m_prev, jnp.max(log_w, axis=0, keepdims=True))
    wv = (jnp.exp(log_w - m_new) * v1).astype(jnp.bfloat16)
    cn_ref[slot] = jnp.exp(b_last + m_prev - m_new) * cn + jnp.dot(kt2, wv, preferred_element_type=jnp.float32)
    m_ref[slot] = m_new
    return h


def _ml_scan(T, qs, kts, vs, gc_ref, gr_ref, gbc, gbr, hf, hb, cn_ref, m_ref, tril, triu):
    L = ML_L
    nc = T // L
    lane = lax.broadcasted_iota(jnp.int32, (L, LANES), 1)
    t_idx = lax.broadcasted_iota(jnp.int32, (L, L), 0)
    s_idx = lax.broadcasted_iota(jnp.int32, (L, L), 1)
    hi = lax.Precision.HIGHEST

    def body(c, carry):
        for backward in (False, True):
            cc = (nc - 1 - c) if backward else c
            rows = pl.ds(pl.multiple_of(cc * L, L), L)
            gc = gc_ref[0, rows, :] + gbc
            gr = gr_ref[0, :, rows] + gbr
            bc = jnp.dot(triu if backward else tril, _log_sigmoid(gc), preferred_element_type=jnp.float32, precision=hi)
            br = jnp.dot(_log_sigmoid(gr), tril if backward else triu, preferred_element_type=jnp.float32, precision=hi)
            tri = (s_idx >= t_idx) if backward else (s_idx <= t_idx)
            out = hb if backward else hf
            base = 2 * ML_HEADS if backward else 0
            for pair in range(ML_HEADS // 2):
                hs = []
                for head in (2 * pair, 2 * pair + 1):
                    hs.append(_ml_chain(qs, kts, vs, rows, gc, gr, bc, br, base + ML_HEADS + head, base + head, head,
                                        backward, cn_ref, m_ref, (ML_HEADS if backward else 0) + head, lane, tri))
                out[rows, pair * LANES:(pair + 1) * LANES] = jnp.where(lane < HEAD_DIM, hs[0], hs[1])
        return carry

    lax.fori_loop(0, nc, body, 0)


def _mlstm_kernel(q_ref, k_ref, v_ref, o_ref, gc_ref, gr_ref, cq_ref, ck_ref, cv_ref, co_ref, cgc_ref, cgr_ref,
                  cos_ref, sin_ref, conv_ref, gbc_ref, gbr_ref, ng_ref, e_ref, tril_ref, triu_ref,
                  y_ref, cy_ref, qs, kts, vs, hf, hb, cqs, ckts, cvs, chf, chb, cn_ref, m_ref):
    cn_ref[...] = jnp.zeros_like(cn_ref)
    m_ref[...] = jnp.zeros_like(m_ref)
    tril, triu = tril_ref[...], triu_ref[...]

    def stage(q_r, k_r, v_r, rope, qd, ktd, vd):
        qp, kp = _ml_prep(q_r[0], k_r[0], conv_ref, cos_ref if rope else None, sin_ref if rope else None)
        qd[...] = qp.astype(jnp.bfloat16)
        ktd[...] = (kp * HEAD_DIM ** -0.5).T.astype(jnp.bfloat16)
        vd[...] = v_r[0].astype(jnp.bfloat16)

    def finish(h_f, h_b, o_r, y_r):
        h = h_f[...] + h_b[...]
        ss = jnp.dot(h * h, e_ref[...], preferred_element_type=jnp.float32, precision=lax.Precision.HIGHEST)
        y_r[0] = h * lax.rsqrt(ss * (1.0 / HEAD_DIM) + EPS) * ng_ref[...] * jax.nn.sigmoid(o_r[0])

    gbc, gbr = gbc_ref[...], gbr_ref[...]
    stage(cq_ref, ck_ref, cv_ref, False, cqs, ckts, cvs)
    _ml_scan(cq_ref.shape[1], cqs, ckts, cvs, cgc_ref, cgr_ref, gbc, gbr, chf, chb, cn_ref, m_ref, tril, triu)
    finish(chf, chb, co_ref, cy_ref)
    stage(q_ref, k_ref, v_ref, True, qs, kts, vs)
    _ml_scan(q_ref.shape[1], qs, kts, vs, gc_ref, gr_ref, gbc, gbr, hf, hb, cn_ref, m_ref, tril, triu)
    finish(hf, hb, o_ref, y_ref)


def _rope_tables(T):
    t = jnp.arange(T)
    row = (t // GRID_W).astype(jnp.float32)
    col = (t % GRID_W).astype(jnp.float32)
    nf = HEAD_DIM // 4
    inv = ROPE_BASE ** (-jnp.arange(nf, dtype=jnp.float32) / nf)
    cr, sr, cc, sc = jnp.cos(row[:, None] * inv), jnp.sin(row[:, None] * inv), jnp.cos(col[:, None] * inv), \
        jnp.sin(col[:, None] * inv)
    cos = jnp.concatenate([cr, cr, cc, cc], axis=-1)
    sin = jnp.concatenate([-sr, sr, -sc, sc], axis=-1)
    return jnp.tile(cos, (1, ML_HEADS)), jnp.tile(sin, (1, ML_HEADS))


def _mlstm_mix(lq, lk, lv, lo, lg, lgr, cq, ck, cv, co, cg, cgr, conv_w, gate_b, norm_g):
    B, T, d = lq.shape
    Lc = cq.shape[1]
    L = ML_L
    assert T % L == 0 and Lc % L == 0
    cos, sin = _rope_tables(T)
    lane_head = np.arange(d) // HEAD_DIM
    same_head = jnp.asarray((lane_head[:, None] == lane_head[None, :]).astype(np.float32))
    tril = jnp.asarray(np.tril(np.ones((L, L), np.float32)))
    seq = lambda n: pl.BlockSpec((1, n, d), lambda b: (b, 0, 0))
    full = lambda a: pl.BlockSpec(a.shape, lambda b: (0,) * a.ndim)
    consts = [cos, sin, conv_w, gate_b[None, :], gate_b[:, None], norm_g[None, :], same_head, tril, tril.T]
    f32, bf16 = jnp.float32, jnp.bfloat16
    return pl.pallas_call(
        _mlstm_kernel,
        grid=(B,),
        in_specs=[seq(T)] * 4 + [pl.BlockSpec((1, T, ML_N_GATES), lambda b: (b, 0, 0)),
                                 pl.BlockSpec((1, ML_N_GATES, T), lambda b: (b, 0, 0))]
        + [seq(Lc)] * 4 + [pl.BlockSpec((1, Lc, ML_N_GATES), lambda b: (b, 0, 0)),
                           pl.BlockSpec((1, ML_N_GATES, Lc), lambda b: (b, 0, 0))]
        + [full(a) for a in consts],
        out_specs=[seq(T), seq(Lc)],
        out_shape=[jax.ShapeDtypeStruct((B, T, d), f32), jax.ShapeDtypeStruct((B, Lc, d), f32)],
        scratch_shapes=[pltpu.VMEM((T, d), bf16), pltpu.VMEM((d, T), bf16), pltpu.VMEM((T, d), bf16),
                        pltpu.VMEM((T, d), f32), pltpu.VMEM((T, d), f32),
                        pltpu.VMEM((Lc, d), bf16), pltpu.VMEM((d, Lc), bf16), pltpu.VMEM((Lc, d), bf16),
                        pltpu.VMEM((Lc, d), f32), pltpu.VMEM((Lc, d), f32),
                        pltpu.VMEM((2 * ML_HEADS, LANES, LANES), f32), pltpu.VMEM((2 * ML_HEADS, 1, 1), f32)],
        compiler_params=pltpu.CompilerParams(dimension_semantics=("parallel",), vmem_limit_bytes=VMEM_LIMIT_BYTES),
        name="mlstm",
    )(lq, lk, lv, lo, lg, lgr, cq, ck, cv, co, cg, cgr, *consts)


def _ada_kernel(c_ref, w_ref, b_ref, o_ref):
    c = c_ref[...]
    act = (c * jax.nn.sigmoid(c)).astype(jnp.bfloat16)
    o_ref[...] = jnp.dot(act, w_ref[0].astype(jnp.bfloat16), preferred_element_type=jnp.float32) + b_ref[0]


def _ada_mod(cond, w_all, b_all, layer):
    rows, d = cond.shape
    n = w_all.shape[2]
    tile = ADA_COL_TILE
    assert n % tile == 0
    return pl.pallas_call(
        _ada_kernel,
        grid=(n // tile,),
        in_specs=[pl.BlockSpec((rows, d), lambda j: (0, 0)), pl.BlockSpec((1, d, tile), lambda j: (layer, 0, j)),
                  pl.BlockSpec((1, 1, tile), lambda j: (layer, 0, j))],
        out_specs=pl.BlockSpec((rows, tile), lambda j: (0, j)),
        out_shape=jax.ShapeDtypeStruct((rows, n), jnp.float32),
        compiler_params=pltpu.CompilerParams(dimension_semantics=("parallel",)),
        name="ada_mod",
    )(cond, w_all, b_all[:, None, :])


_IN_SPLITS = (("q", NA_DIM), ("k", NA_DIM), ("v", NA_DIM), ("pool", POOL_DIM), ("mq", ML_DIM), ("mk", ML_DIM),
              ("mv", ML_DIM), ("mo", ML_DIM), ("gates", ML_N_GATES))
IN_COLS = sum(n for _, n in _IN_SPLITS)
IN_COLS_PAD = -(-IN_COLS // LANES) * LANES


def _in_proj_kernel(x_ref, sc_ref, sh_ref, g_ref, w_ref, wgt_ref, gq_ref, gk_ref, e_ref,
                    q_ref, k_ref, v_ref, pool_ref, mq_ref, mk_ref, mv_ref, mo_ref, gc_ref, gr_ref):
    x = x_ref[0]
    y = x * lax.rsqrt(jnp.mean(x * x, axis=-1, keepdims=True) + EPS) * g_ref[...]
    xn = (y * (1.0 + sc_ref[0]) + sh_ref[0]).astype(jnp.bfloat16)
    p = jnp.dot(xn, w_ref[...], preferred_element_type=jnp.float32)
    off = {}
    o = 0
    for name, n in _IN_SPLITS:
        off[name] = slice(o, o + n)
        o += n

    def head_norm(u, gain):
        sq = u * u
        hi = sq.astype(jnp.bfloat16)
        lo = (sq - hi.astype(jnp.float32)).astype(jnp.bfloat16)
        ss = (jnp.dot(hi, e_ref[...], preferred_element_type=jnp.float32)
              + jnp.dot(lo, e_ref[...], preferred_element_type=jnp.float32))
        return u * lax.rsqrt(ss * (1.0 / HEAD_DIM) + EPS) * gain

    q_ref[0] = (head_norm(p[:, off["q"]], gq_ref[...]) * HEAD_DIM ** -0.5).astype(jnp.bfloat16)
    k_ref[0] = head_norm(p[:, off["k"]], gk_ref[...]).astype(jnp.bfloat16)
    v_ref[0] = p[:, off["v"]].astype(jnp.bfloat16)
    pool_ref[0] = p[:, off["pool"]]
    mq_ref[0] = p[:, off["mq"]]
    mk_ref[0] = p[:, off["mk"]]
    mv_ref[0] = p[:, off["mv"]]
    mo_ref[0] = p[:, off["mo"]]
    gc_ref[0] = p[:, off["gates"]]
    gr_ref[0] = lax.dot_general(wgt_ref[...], xn, (((1,), (1,)), ((), ())), preferred_element_type=jnp.float32)


def _in_proj(x3, sc, sh, norm_g, w_in, gq, gk):
    B, T, d = x3.shape
    tn = min(IN_TOKEN_TILE, T)
    assert T % tn == 0
    w_bf = jnp.pad(w_in, ((0, 0), (0, IN_COLS_PAD - IN_COLS))).astype(jnp.bfloat16)
    wgt_bf = w_in[:, IN_COLS - ML_N_GATES:].T.astype(jnp.bfloat16)
    lane_head = np.arange(NA_DIM) // HEAD_DIM
    same_head = jnp.asarray((lane_head[:, None] == lane_head[None, :]).astype(np.float32)).astype(jnp.bfloat16)
    mod = pl.BlockSpec((1, 1, d), lambda b, t: (b, 0, 0))
    full = lambda a: pl.BlockSpec(a.shape, lambda b, t: (0,) * a.ndim)
    tok = lambda n: pl.BlockSpec((1, tn, n), lambda b, t: (b, t, 0))
    shp = lambda n, dt: jax.ShapeDtypeStruct((B, T, n), dt)
    f32, bf16 = jnp.float32, jnp.bfloat16
    consts = [norm_g[None, :], w_bf, wgt_bf, jnp.tile(gq, NA_HEADS)[None, :], jnp.tile(gk, NA_HEADS)[None, :], same_head]
    outs = pl.pallas_call(
        _in_proj_kernel,
        grid=(B, T // tn),
        in_specs=[pl.BlockSpec((1, tn, d), lambda b, t: (b, t, 0)), mod, mod] + [full(a) for a in consts],
        out_specs=[tok(NA_DIM)] * 3 + [tok(POOL_DIM)] + [tok(ML_DIM)] * 4 + [tok(ML_N_GATES)]
        + [pl.BlockSpec((1, ML_N_GATES, tn), lambda b, t: (b, 0, t))],
        out_shape=[shp(NA_DIM, bf16)] * 3 + [shp(POOL_DIM, f32)] + [shp(ML_DIM, f32)] * 4 + [shp(ML_N_GATES, f32)]
        + [jax.ShapeDtypeStruct((B, ML_N_GATES, T), f32)],
        compiler_params=pltpu.CompilerParams(
            dimension_semantics=("parallel", "parallel"), vmem_limit_bytes=VMEM_LIMIT_BYTES),
        name="in_proj",
    )(x3, sc, sh, *consts)
    return dict(zip(("q", "k", "v", "pool", "mq", "mk", "mv", "mo", "gc", "gr"), outs))


def kernel(x, c, ctx, c_ctx, w_ada, b_ada, norm1_g, w_in, ml_gate_b, na_q_g, na_k_g, na_rpb, pool_w,
           pool_scale, ml_conv, ml_norm_g, w_out, norm2_g, peer_wq, peer_keys, peer_u, peer_v):
    depth = w_in.shape[0]
    B, S, D = x.shape
    Lc = ctx.shape[1]
    hc = ctx
    u_bf = _cast_bf16(peer_u.reshape(depth * PEER_EXPERTS, D))
    v_bf = _cast_bf16(peer_v.reshape(depth * PEER_EXPERTS, D))
    for l in range(depth):
        need_ctx = l < depth - 1
        wqt_bf = peer_wq[l].astype(jnp.bfloat16).T
        keys_bf = peer_keys[l].astype(jnp.bfloat16).reshape(2 * PEER_HEADS, PEER_KEYS, PEER_DKEY // 2)
        mod = _ada_mod(jnp.concatenate([c, c_ctx[None, :]], axis=0), w_ada, b_ada, l).reshape(B + 1, 6, 1, D)
        sh1, sc1, g1, sh2, sc2, g2 = (mod[:B, i] for i in range(6))
        csh1, csc1, cg1, csh2, csc2, cg2 = (mod[B:, i] for i in range(6))
        hc1 = hc.reshape(1, B * Lc, D)
        a = _in_proj(x, sc1, sh1, norm1_g[l], w_in[l], na_q_g[l], na_k_g[l])
        ac = _in_proj(hc1, csc1, csh1, norm1_g[l], w_in[l], na_q_g[l], na_k_g[l])
        ac = {k: v.reshape(B, Lc, v.shape[-1]) for k, v in ac.items() if k != "gr"} | {
            "gr": jnp.transpose(ac["gr"].reshape(ML_N_GATES, B, Lc), (1, 0, 2))}
        na_lat = _na_latent(a["q"], a["k"], a["v"], ac["k"], ac["v"], na_rpb[l])
        pool_lat = _pool_mix(a["pool"], pool_w[l], pool_scale[l])
        ml_lat, ml_ctx = _mlstm_mix(a["mq"], a["mk"], a["mv"], a["mo"], a["gc"], a["gr"],
                                    ac["mq"], ac["mk"], ac["mv"], ac["mo"], ac["gc"], ac["gr"],
                                    ml_conv[l], ml_gate_b[l], ml_norm_g[l])
        wo_bf = w_out[l].astype(jnp.bfloat16)
        x = _mix_and_peer(x, g1, na_lat, pool_lat, ml_lat, wo_bf, sc2, sh2, g2, norm2_g[l], wqt_bf, keys_bf,
                          u_bf, v_bf, l)
        if need_ctx:
            na_ctx = _na_context(ac["q"], ac["k"], ac["v"])
            pool_ctx = _pool_mix(ac["pool"], pool_w[l], pool_scale[l])
            flat = lambda t: t.reshape(1, B * Lc, t.shape[-1])
            hc = _mix_and_peer(hc1, cg1, flat(na_ctx), flat(pool_ctx), flat(ml_ctx), wo_bf, csc2, csh2, cg2,
                               norm2_g[l], wqt_bf, keys_bf, u_bf, v_bf, l).reshape(B, Lc, D)
    return x
```

```python
import functools

import numpy as np
import jax
import jax.numpy as jnp
from jax import lax
from jax.experimental import pallas as pl
from jax.experimental.pallas import tpu as pltpu

D_MODEL = 1024
GRID_W = 64
HEAD_DIM = 64
NA_DIM = D_MODEL // 2
NA_HEADS = NA_DIM // HEAD_DIM
NA_WIN_ROWS = 8
NA_WIN_COLS = 16
POOL_DIM = D_MODEL // 4
POOL_WINDOWS = (2, 4, 8, 16)
POOL_GROUPS = len(POOL_WINDOWS)
POOL_GROUP_DIM = POOL_DIM // POOL_GROUPS
ML_DIM = D_MODEL // 4
ML_HEADS = ML_DIM // HEAD_DIM
ML_CONV_W = 5
ML_N_GATES = 4 * ML_HEADS
PEER_KEYS = 128
PEER_EXPERTS = PEER_KEYS * PEER_KEYS
PEER_HEADS = 8
PEER_TOPK = 16
PEER_DKEY = 256
ROPE_BASE = 10000.0
EPS = 1e-6

LANES = 128
SUBLANES = 8
V7X_VMEM_BYTES = 64 * 1024 * 1024

PEER_TOKEN_TILE = 1024
PEER_EXPERT_TILE = 1024
ROUTE_TOKEN_TILE = 512
ROUTE_LANE_CHUNK = 2 * LANES
IN_TOKEN_TILE = 256
NA_ROWS_PER_STEP = 4
NA_UNION_ROWS = -(-(NA_WIN_ROWS + NA_ROWS_PER_STEP - 1) // 2) * 2
ADA_COL_TILE = 1536
CAST_ROW_TILE = 1024
POOL_BLOCK = 128
POOL_HALO = SUBLANES
ML_L = 256
VMEM_LIMIT_BYTES = V7X_VMEM_BYTES - 8 * 1024 * 1024

_GELU_C0 = float(np.sqrt(2.0 / np.pi))
_GELU_C1 = 0.044715 * _GELU_C0


def _gelu_tanh(x):
    inner = x * (_GELU_C0 + _GELU_C1 * (x * x))
    hx = 0.5 * x
    return hx + hx * jnp.tanh(inner)


def _peer_dense_kernel(x_ref, u_ref, v_ref, cnt_ref, e1_ref, rank_ref, e2_ref, o_ref):
    j = pl.program_id(1)

    @pl.when(j == 0)
    def _():
        o_ref[...] = jnp.zeros_like(o_ref)

    te = u_ref.shape[0]
    act = lax.dot_general(u_ref[...], x_ref[...], (((1,), (1,)), ((), ())),
                          preferred_element_type=jnp.float32)
    gates = []
    for k in range(te // PEER_KEYS):
        g = None
        for h in range(PEER_HEADS):
            cnt = cnt_ref[h, k:k + 1, :].astype(jnp.bfloat16)
            e1 = e1_ref[h, k:k + 1, :].astype(jnp.bfloat16)
            term = jnp.where(rank_ref[h] < cnt, e2_ref[h], jnp.zeros((), e2_ref.dtype)) * e1
            g = term if g is None else g + term
        gates.append(g)
    gate = jnp.concatenate(gates, axis=0)
    p = _gelu_tanh(act.astype(jnp.bfloat16)) * gate
    o_ref[...] += lax.dot_general(v_ref[...], p, (((0,), (0,)), ((), ())), preferred_element_type=jnp.float32)


def _cast_kernel(x_ref, o_ref):
    o_ref[...] = x_ref[...].astype(o_ref.dtype)


def _cast_bf16(w):
    rows, cols = w.shape
    tr = CAST_ROW_TILE
    assert rows % tr == 0
    blk = pl.BlockSpec((tr, cols), lambda i: (i, 0))
    return pl.pallas_call(
        _cast_kernel, grid=(rows // tr,), in_specs=[blk], out_specs=blk,
        out_shape=jax.ShapeDtypeStruct(w.shape, jnp.bfloat16),
        compiler_params=pltpu.CompilerParams(dimension_semantics=("parallel",)),
        name="cast_bf16",
    )(w)


def _peer_dense(tok, u_bf, v_bf, layer, cnt, e1, rank, e2):
    n, d = tok.shape
    tn, te = PEER_TOKEN_TILE, PEER_EXPERT_TILE
    assert n % tn == 0 and PEER_EXPERTS % te == 0 and te % PEER_KEYS == 0
    first = layer * (PEER_EXPERTS // te)
    tab_a = pl.BlockSpec((PEER_HEADS, te // PEER_KEYS, tn), lambda i, j: (0, j, i))
    tab_b = pl.BlockSpec((PEER_HEADS, PEER_KEYS, tn), lambda i, j: (0, 0, i))
    return pl.pallas_call(
        _peer_dense_kernel,
        grid=(n // tn, PEER_EXPERTS // te),
        in_specs=[
            pl.BlockSpec((tn, d), lambda i, j: (i, 0)),
            pl.BlockSpec((te, d), lambda i, j: (first + j, 0)),
            pl.BlockSpec((te, d), lambda i, j: (first + j, 0)),
            tab_a, tab_a, tab_b, tab_b,
        ],
        out_specs=pl.BlockSpec((d, tn), lambda i, j: (0, i)),
        out_shape=jax.ShapeDtypeStruct((d, n), jnp.float32),
        compiler_params=pltpu.CompilerParams(
            dimension_semantics=("parallel", "arbitrary"), vmem_limit_bytes=VMEM_LIMIT_BYTES),
        name="peer_dense",
    )(tok, u_bf, v_bf, cnt, e1, rank, e2)


def _top16(works, want_ranks):
    lanes = works[0].shape[1]
    rows = lax.broadcasted_iota(jnp.int32, (PEER_TOPK, lanes), 0)
    vals = [jnp.zeros((PEER_TOPK, lanes), jnp.float32) for _ in works]
    ranks = [jnp.full(w.shape, float(PEER_TOPK), jnp.float32) if r else None for w, r in zip(works, want_ranks)]
    tops = [[] for _ in works]
    works = list(works)
    for i in range(PEER_TOPK):
        ms = [jnp.max(w, axis=0, keepdims=True) for w in works]
        hits = [w == m for w, m in zip(works, ms)]
        ranks = [None if r is None else jnp.where(hit, float(i), r) for r, hit in zip(ranks, hits)]
        works = [jnp.where(hit, -jnp.inf, w) for w, hit in zip(works, hits)]
        vals = [jnp.where(rows == i, m, v) for v, m in zip(vals, ms)]
        for t, m in zip(tops, ms):
            t.append(m)
    return tops, vals, ranks


def _peer_route_kernel(x_ref, g1_ref, na_ref, pool_ref, ml_ref, wo_ref, sc_ref, sh_ref, g_ref, wqt_ref, keys_ref,
                       xo_ref, tok_ref, cnt_ref, e1_ref, rank_ref, e2_ref, s_scr):
    bf = jnp.bfloat16
    mix = jnp.dot(na_ref[0].astype(bf), wo_ref[0:NA_DIM, :], preferred_element_type=jnp.float32)
    mix += jnp.dot(pool_ref[0].astype(bf), wo_ref[NA_DIM:NA_DIM + POOL_DIM, :], preferred_element_type=jnp.float32)
    mix += jnp.dot(ml_ref[0].astype(bf), wo_ref[NA_DIM + POOL_DIM:, :], preferred_element_type=jnp.float32)
    x = x_ref[0] + g1_ref[0] * mix
    xo_ref[0] = x
    y = x * lax.rsqrt(jnp.mean(x * x, axis=-1, keepdims=True) + EPS) * g_ref[...]
    tok = (y * (1.0 + sc_ref[0]) + sh_ref[0]).astype(jnp.bfloat16)
    tok_ref[...] = tok
    qt = lax.dot_general(wqt_ref[...], tok, (((1,), (1,)), ((), ())),
                         preferred_element_type=jnp.float32).astype(jnp.bfloat16)
    dk = PEER_DKEY // 2
    for hp in range(2 * PEER_HEADS):
        s_scr[hp] = jnp.dot(keys_ref[hp], qt[hp * dk:(hp + 1) * dk], preferred_element_type=jnp.float32)

    cw = ROUTE_LANE_CHUNK
    lane_chunks = x.shape[0] // cw
    half = PEER_TOPK // 2
    row_half = lax.broadcasted_iota(jnp.int32, (half, cw), 0)

    def body(it, carry):
        h = it // lane_chunks
        lanes = pl.ds(pl.multiple_of((it % lane_chunks) * cw, cw), cw)
        s1 = s_scr[2 * h, :, lanes]
        s2 = s_scr[2 * h + 1, :, lanes]
        (t1, t2), (v1, v2), (_, rank) = _top16([s1, s2], [False, True])
        groups = [t1[0] + v2, t1[1] + v2[:half]]
        for i in range(2, half):
            groups.append(jnp.where(row_half < PEER_TOPK // (i + 1), t1[i] + v2[:half], -jnp.inf))
        groups.append(v1[half:] + t2[0])
        (tp,), (vp,), _ = _top16([jnp.concatenate(groups, axis=0)], [False])
        tau = tp[PEER_TOPK - 1]
        z = jnp.sum(jnp.exp(vp - tp[0]), axis=0, keepdims=True)
        kept = jnp.zeros(v1.shape, jnp.float32)
        for jj in range(PEER_TOPK):
            kept = jnp.where(v1 + t2[jj] >= tau, float(jj + 1), kept)
        cnt = jnp.zeros(s1.shape, jnp.float32)
        for i in range(PEER_TOPK):
            cnt = jnp.where(s1 == t1[i], kept[i:i + 1], cnt)
        cnt_ref[h, :, lanes] = cnt
        e1_ref[h, :, lanes] = jnp.exp(s1 - t1[0])
        rank_ref[h, :, lanes] = rank.astype(jnp.bfloat16)
        e2_ref[h, :, lanes] = (jnp.exp(s2 - t2[0]) * (1.0 / z)).astype(jnp.bfloat16)
        return carry

    lax.fori_loop(0, PEER_HEADS * lane_chunks, body, 0)


def _peer_route(x3, gate1, na, pool, ml, wo_bf, sc, sh, g, wqt_bf, keys_bf):
    B, T, d = x3.shape
    tn = min(ROUTE_TOKEN_TILE, T)
    assert T % tn == 0 and tn % ROUTE_LANE_CHUNK == 0
    nt = T // tn
    n = B * T
    tab = pl.BlockSpec((PEER_HEADS, PEER_KEYS, tn), lambda b, t: (0, 0, b * nt + t))
    tab_shape = lambda dt: jax.ShapeDtypeStruct((PEER_HEADS, PEER_KEYS, n), dt)
    mod = pl.BlockSpec((1, 1, d), lambda b, t: (b, 0, 0))
    tok = lambda c: pl.BlockSpec((1, tn, c), lambda b, t: (b, t, 0))
    const = lambda a: pl.BlockSpec(a.shape, lambda b, t: (0,) * a.ndim, pipeline_mode=pl.Buffered(1))
    return pl.pallas_call(
        _peer_route_kernel,
        grid=(B, nt),
        in_specs=[tok(d), mod, tok(NA_DIM), tok(POOL_DIM), tok(ML_DIM), const(wo_bf), mod, mod,
                  pl.BlockSpec((1, d), lambda b, t: (0, 0)), const(wqt_bf), const(keys_bf)],
        out_specs=[tok(d), pl.BlockSpec((tn, d), lambda b, t: (b * nt + t, 0)), tab, tab, tab, tab],
        out_shape=[jax.ShapeDtypeStruct(x3.shape, x3.dtype), jax.ShapeDtypeStruct((n, d), jnp.bfloat16),
                   tab_shape(jnp.float32), tab_shape(jnp.float32), tab_shape(jnp.bfloat16), tab_shape(jnp.bfloat16)],
        scratch_shapes=[pltpu.VMEM((2 * PEER_HEADS, PEER_KEYS, tn), jnp.float32)],
        compiler_params=pltpu.CompilerParams(
            dimension_semantics=("parallel", "parallel"), vmem_limit_bytes=VMEM_LIMIT_BYTES),
        name="peer_route",
    )(x3, gate1, na, pool, ml, wo_bf, sc, sh, g, wqt_bf, keys_bf)


def _residual_t_kernel(x_ref, g_ref, yt_ref, o_ref):
    o_ref[0] = x_ref[0] + g_ref[0] * yt_ref[...].T


def _residual_t(x3, gate, yt):
    B, T, d = x3.shape
    tn = min(ROUTE_TOKEN_TILE, T)
    nt = T // tn
    return pl.pallas_call(
        _residual_t_kernel,
        grid=(B, nt),
        in_specs=[pl.BlockSpec((1, tn, d), lambda b, t: (b, t, 0)),
                  pl.BlockSpec((1, 1, d), lambda b, t: (b, 0, 0)),
                  pl.BlockSpec((d, tn), lambda b, t: (0, b * nt + t))],
        out_specs=pl.BlockSpec((1, tn, d), lambda b, t: (b, t, 0)),
        out_shape=jax.ShapeDtypeStruct(x3.shape, x3.dtype),
        compiler_params=pltpu.CompilerParams(dimension_semantics=("parallel", "parallel")),
        name="residual_t",
    )(x3, gate, yt)


def _mix_and_peer(x3, gate1, na, pool, ml, wo_bf, sc, sh, norm_g, wqt_bf, keys_bf, u_bf, v_bf, layer):
    x1, tok, cnt, e1, rank, e2 = _peer_route(x3, gate1, na, pool, ml, wo_bf, sc, sh, norm_g[None, :], wqt_bf, keys_bf)
    return x1, _peer_dense(tok, u_bf, v_bf, layer, cnt, e1, rank, e2)


def _attend_pair(q2, segments, lane):
    zero = jnp.zeros((), q2.dtype)
    outs = []
    for half in range(2):
        own = (lane < HEAD_DIM) if half == 0 else (lane >= HEAD_DIM)
        qh = jnp.where(own, q2, zero)
        scores = []
        for k2, _, bias in segments:
            s = lax.dot_general(qh, k2, (((1,), (1,)), ((), ())), preferred_element_type=jnp.float32)
            scores.append(s if bias is None else s + bias[half])
        m = functools.reduce(jnp.maximum, [jnp.max(s, axis=-1, keepdims=True) for s in scores])
        ps = [jnp.exp(s - m) for s in scores]
        denom = functools.reduce(jnp.add, [jnp.sum(p, axis=-1, keepdims=True) for p in ps])
        acc = functools.reduce(jnp.add, [jnp.dot(p.astype(v2.dtype), v2, preferred_element_type=jnp.float32)
                                         for p, (_, v2, _) in zip(ps, segments)])
        outs.append(acc / denom)
    return jnp.where(lane < HEAD_DIM, outs[0], outs[1])


def _na_latent_kernel(q_ref, k_ref, v_ref, kc_ref, vc_ref, bank_ref, o_ref):
    G, KR = NA_ROWS_PER_STEP, NA_WIN_ROWS
    n_rows = k_ref.shape[1] // GRID_W
    first = pl.program_id(1) * G
    u0 = jnp.minimum(jnp.clip(first - KR // 2, 0, n_rows - KR), n_rows - NA_UNION_ROWS)
    keys = pl.ds(pl.multiple_of(u0 * GRID_W, GRID_W), NA_UNION_ROWS * GRID_W)
    key_row = u0 + lax.broadcasted_iota(jnp.int32, (1, NA_UNION_ROWS * GRID_W), 1) // GRID_W
    lane = lax.broadcasted_iota(jnp.int32, (q_ref.shape[1], LANES), 1)

    def head_bias(h):
        rows = []
        for i in range(G):
            r = first + i
            r0 = jnp.clip(r - KR // 2, 0, n_rows - KR)
            pairs = [bank_ref[h, jnp.clip(u0 + 2 * t + 1 - r + KR - 1, 0, 2 * KR - 1)]
                     for t in range(NA_UNION_ROWS // 2)]
            in_strip = (key_row >= r0) & (key_row < r0 + KR)
            rows.append(jnp.where(in_strip, jnp.concatenate(pairs, axis=1), -jnp.inf))
        return jnp.concatenate(rows, axis=0)

    for p in range(NA_DIM // LANES):
        lanes = slice(p * LANES, (p + 1) * LANES)
        segs = [(k_ref[0, keys, lanes], v_ref[0, keys, lanes], (head_bias(2 * p), head_bias(2 * p + 1))),
                (kc_ref[0, :, lanes], vc_ref[0, :, lanes], None)]
        o_ref[0, :, lanes] = _attend_pair(q_ref[0, :, lanes], segs, lane)


def _na_bias_bank(rpb):
    cq = np.arange(GRID_W)
    c0 = np.clip(cq - NA_WIN_COLS // 2, 0, GRID_W - NA_WIN_COLS)
    in_win = (cq[None, :] >= c0[:, None]) & (cq[None, :] < c0[:, None] + NA_WIN_COLS)
    bc = np.clip(cq[None, :] - cq[:, None] + (NA_WIN_COLS - 1), 0, 2 * NA_WIN_COLS - 2)
    onehot = jnp.asarray((bc[None] == np.arange(2 * NA_WIN_COLS - 1)[:, None, None]).astype(np.float32))
    toe = jnp.einsum('hrj,jqk->hrqk', rpb.astype(jnp.float32), onehot, precision=lax.Precision.HIGHEST)
    toe = jnp.where(in_win[None, None], toe, -jnp.inf)
    none = jnp.full((NA_HEADS, 1, GRID_W, GRID_W), -jnp.inf, jnp.float32)
    return jnp.concatenate([jnp.concatenate([none, toe], axis=1), jnp.concatenate([toe, none], axis=1)], axis=3)


def _na_latent(q, k, v, k_ctx, v_ctx, rpb):
    B, S, d = q.shape
    Lc = k_ctx.shape[1]
    n_rows = S // GRID_W
    assert n_rows >= NA_UNION_ROWS and n_rows % NA_ROWS_PER_STEP == 0
    bank = _na_bias_bank(rpb)
    tq = NA_ROWS_PER_STEP * GRID_W
    whole = lambda L: pl.BlockSpec((1, L, d), lambda b, g: (b, 0, 0))
    return pl.pallas_call(
        _na_latent_kernel,
        grid=(B, n_rows // NA_ROWS_PER_STEP),
        in_specs=[pl.BlockSpec((1, tq, d), lambda b, g: (b, g, 0)), whole(S), whole(S), whole(Lc), whole(Lc),
                  pl.BlockSpec(bank.shape, lambda b, g: (0, 0, 0, 0))],
        out_specs=pl.BlockSpec((1, tq, d), lambda b, g: (b, g, 0)),
        out_shape=jax.ShapeDtypeStruct((B, S, d), jnp.float32),
        compiler_params=pltpu.CompilerParams(
            dimension_semantics=("parallel", "arbitrary"), vmem_limit_bytes=VMEM_LIMIT_BYTES),
        name="na_latent",
    )(q, k, v, k_ctx, v_ctx, bank)


def _na_context_kernel(q_ref, k_ref, v_ref, o_ref):
    lane = lax.broadcasted_iota(jnp.int32, (q_ref.shape[1], LANES), 1)
    for p in range(NA_DIM // LANES):
        lanes = slice(p * LANES, (p + 1) * LANES)
        o_ref[0, :, lanes] = _attend_pair(q_ref[0, :, lanes], [(k_ref[0, :, lanes], v_ref[0, :, lanes], None)], lane)


def _na_context(q, k, v):
    B, Lc, d = q.shape
    blk = pl.BlockSpec((1, Lc, d), lambda b: (b, 0, 0))
    return pl.pallas_call(
        _na_context_kernel,
        grid=(B,),
        in_specs=[blk, blk, blk],
        out_specs=blk,
        out_shape=jax.ShapeDtypeStruct((B, Lc, d), jnp.float32),
        compiler_params=pltpu.CompilerParams(dimension_semantics=("parallel",)),
        name="na_context",
    )(q, k, v)


def _pool_matrices(T):
    tb, halo = POOL_BLOCK, POOL_HALO
    mats = np.zeros((len(POOL_WINDOWS), 3, tb, tb + 2 * halo), np.float32)
    for wi, w in enumerate(POOL_WINDOWS):
        for kind, start in enumerate((0, tb, T - tb)):
            for i in range(tb):
                t = start + i
                lo = min(max(t - w // 2, 0), T - 1)
                hi = min(max(t + (w - w // 2 - 1), 0), T - 1)
                mats[wi, kind, i, lo - start + halo:hi - start + halo + 1] = 1.0 / (hi - lo + 1)
                mats[wi, kind, i, i + halo] -= 1.0
    return jnp.asarray(mats)


def _pool_kernel(u_ref, a_ref, w_ref, s_ref, o_ref):
    T = u_ref.shape[1]
    tb, halo = POOL_BLOCK, POOL_HALO
    nb = T // tb
    group = lax.broadcasted_iota(jnp.int32, (tb, POOL_DIM), 1) // POOL_GROUP_DIM
    zeros = jnp.zeros((halo, POOL_DIM), jnp.float32)
    for blk in range(nb):
        kind = 0 if blk == 0 else (2 if blk == nb - 1 else 1)
        before = zeros if blk == 0 else u_ref[0, blk * tb - halo:blk * tb, :]
        after = zeros if blk == nb - 1 else u_ref[0, (blk + 1) * tb:(blk + 1) * tb + halo, :]
        xcat = jnp.concatenate([before, u_ref[0, blk * tb:(blk + 1) * tb, :], after], axis=0)
        pooled = None
        for wi in range(len(POOL_WINDOWS)):
            pw = jnp.dot(a_ref[wi, kind], xcat, preferred_element_type=jnp.float32, precision=lax.Precision.HIGHEST)
            pooled = pw if pooled is None else jnp.where(group == wi, pw, pooled)
        mixed = jnp.dot(pooled.astype(jnp.bfloat16), w_ref[...], preferred_element_type=jnp.float32)
        o_ref[0, blk * tb:(blk + 1) * tb, :] = mixed * s_ref[...]


def _pool_mix(u, pool_w, pool_scale):
    B, T, d = u.shape
    assert T % POOL_BLOCK == 0 and T >= 2 * POOL_BLOCK
    w_bd = jax.scipy.linalg.block_diag(*[pool_w[g] for g in range(POOL_GROUPS)]).astype(jnp.bfloat16)
    mats = _pool_matrices(T)
    blk = pl.BlockSpec((1, T, d), lambda b: (b, 0, 0))
    return pl.pallas_call(
        _pool_kernel,
        grid=(B,),
        in_specs=[blk, pl.BlockSpec(mats.shape, lambda b: (0, 0, 0, 0)), pl.BlockSpec((d, d), lambda b: (0, 0)),
                  pl.BlockSpec((1, d), lambda b: (0, 0))],
        out_specs=blk,
        out_shape=jax.ShapeDtypeStruct((B, T, d), jnp.float32),
        compiler_params=pltpu.CompilerParams(dimension_semantics=("parallel",)),
        name="pool_mix",
    )(u, mats, w_bd, pool_scale[None, :])


def _log_sigmoid(x):
    return jnp.minimum(x, 0.0) - jnp.log1p(jnp.exp(-jnp.abs(x)))


def _ml_prep(q, k, conv_ref, cos_ref, sin_ref):
    T = q.shape[0]
    row = lax.broadcasted_iota(jnp.int32, (T, 1), 0)
    lane = lax.broadcasted_iota(jnp.int32, (1, ML_DIM), 1) % (HEAD_DIM // 2)
    outs = []
    for idx, x in enumerate((q, k)):
        acc = None
        for j in range(ML_CONV_W):
            d = j - ML_CONV_W // 2
            w = conv_ref[j:j + 1, idx * ML_DIM:(idx + 1) * ML_DIM]
            if d == 0:
                term = x * w
            else:
                shifted = pltpu.roll(x, (-d) % T, 0)
                term = jnp.where((row + d >= 0) & (row + d < T), shifted, 0.0) * w
            acc = term if acc is None else acc + term
        y = acc * jax.nn.sigmoid(acc)
        if cos_ref is not None:
            quarter = HEAD_DIM // 4
            partner = jnp.where(lane < quarter, pltpu.roll(y, ML_DIM - quarter, 1), pltpu.roll(y, quarter, 1))
            y = y * cos_ref[...] + partner * sin_ref[...]
        outs.append(y)
    return outs


def _ml_chain(qs, kts, vs, rows, gi_c, gi_r, bc, br, ci, ii, head, backward, cn_ref, m_ref, slot, lane, tri):
    L = ML_L
    pair = slice((head // 2) * LANES, (head // 2 + 1) * LANES)
    own = (lane < HEAD_DIM) if head % 2 == 0 else (lane >= HEAD_DIM)
    one_lane = HEAD_DIM if head % 2 == 0 else 0
    q2 = jnp.where(own, qs[rows, pair], jnp.zeros((), jnp.bfloat16))
    v1 = jnp.where(own, vs[rows, pair].astype(jnp.float32), jnp.where(lane == one_lane, 1.0, 0.0))
    own_rows = lax.broadcasted_iota(jnp.int32, (LANES, L), 0)
    own_rows = (own_rows < HEAD_DIM) if head % 2 == 0 else (own_rows >= HEAD_DIM)
    kt2 = jnp.where(own_rows, kts[pair, rows], jnp.zeros((), jnp.bfloat16))
    m_prev = m_ref[slot]
    cn = cn_ref[slot]
    bcol = bc[:, ci:ci + 1]
    brow = br[ci:ci + 1, :]
    log_d = jnp.where(tri, bcol - brow + gi_r[ii:ii + 1, :], -jnp.inf)
    inter = bcol + m_prev
    m_t = jnp.maximum(inter, jnp.max(log_d, axis=-1, keepdims=True))
    s = jnp.dot(q2, kt2, preferred_element_type=jnp.float32)
    w_intra = (s * jnp.exp(log_d - m_t)).astype(jnp.bfloat16)
    nd = (jnp.dot(w_intra, v1.astype(jnp.bfloat16), preferred_element_type=jnp.float32)
          + jnp.exp(inter - m_t) * jnp.dot(q2, cn.astype(jnp.bfloat16), preferred_element_type=jnp.float32))
    den = nd[:, one_lane:one_lane + 1]
    h = nd / jnp.maximum(jnp.abs(den), jnp.exp(-m_t))
    b_last = bc[0:1, ci:ci + 1] if backward else bc[L - 1:L, ci:ci + 1]
    log_w = b_last - bcol + gi_c[:, ii:ii + 1]
    m_new = jnp.maximum(b_last + m_prev, jnp.max(log_w, axis=0, keepdims=True))
    wv = (jnp.exp(log_w - m_new) * v1).astype(jnp.bfloat16)
    cn_ref[slot] = jnp.exp(b_last + m_prev - m_new) * cn + jnp.dot(kt2, wv, preferred_element_type=jnp.float32)
    m_ref[slot] = m_new
    return h


def _ml_scan(T, qs, kts, vs, gc_ref, gr_ref, gbc, gbr, hf, hb, cn_ref, m_ref, tril, triu):
    L = ML_L
    nc = T // L
    lane = lax.broadcasted_iota(jnp.int32, (L, LANES), 1)
    t_idx = lax.broadcasted_iota(jnp.int32, (L, L), 0)
    s_idx = lax.broadcasted_iota(jnp.int32, (L, L), 1)
    hi = lax.Precision.HIGHEST

    def body(c, carry):
        for backward in (False, True):
            cc = (nc - 1 - c) if backward else c
            rows = pl.ds(pl.multiple_of(cc * L, L), L)
            gc = gc_ref[0, rows, :] + gbc
            gr = gr_ref[0, :, rows] + gbr
            bc = jnp.dot(triu if backward else tril, _log_sigmoid(gc), preferred_element_type=jnp.float32, precision=hi)
            br = jnp.dot(_log_sigmoid(gr), tril if backward else triu, preferred_element_type=jnp.float32, precision=hi)
            tri = (s_idx >= t_idx) if backward else (s_idx <= t_idx)
            out = hb if backward else hf
            base = 2 * ML_HEADS if backward else 0
            for pair in range(ML_HEADS // 2):
                hs = []
                for head in (2 * pair, 2 * pair + 1):
                    hs.append(_ml_chain(qs, kts, vs, rows, gc, gr, bc, br, base + ML_HEADS + head, base + head, head,
                                        backward, cn_ref, m_ref, (ML_HEADS if backward else 0) + head, lane, tri))
                out[rows, pair * LANES:(pair + 1) * LANES] = jnp.where(lane < HEAD_DIM, hs[0], hs[1])
        return carry

    lax.fori_loop(0, nc, body, 0)


def _mlstm_kernel(q_ref, k_ref, v_ref, o_ref, gc_ref, gr_ref, cq_ref, ck_ref, cv_ref, co_ref, cgc_ref, cgr_ref,
                  cos_ref, sin_ref, conv_ref, gbc_ref, gbr_ref, ng_ref, e_ref, tril_ref, triu_ref,
                  y_ref, cy_ref, qs, kts, vs, hf, hb, cqs, ckts, cvs, chf, chb, cn_ref, m_ref):
    cn_ref[...] = jnp.zeros_like(cn_ref)
    m_ref[...] = jnp.zeros_like(m_ref)
    tril, triu = tril_ref[...], triu_ref[...]

    def stage(q_r, k_r, v_r, rope, qd, ktd, vd):
        qp, kp = _ml_prep(q_r[0], k_r[0], conv_ref, cos_ref if rope else None, sin_ref if rope else None)
        qd[...] = qp.astype(jnp.bfloat16)
        ktd[...] = (kp * HEAD_DIM ** -0.5).T.astype(jnp.bfloat16)
        vd[...] = v_r[0].astype(jnp.bfloat16)

    def finish(h_f, h_b, o_r, y_r):
        h = h_f[...] + h_b[...]
        ss = jnp.dot(h * h, e_ref[...], preferred_element_type=jnp.float32, precision=lax.Precision.HIGHEST)
        y_r[0] = h * lax.rsqrt(ss * (1.0 / HEAD_DIM) + EPS) * ng_ref[...] * jax.nn.sigmoid(o_r[0])

    gbc, gbr = gbc_ref[...], gbr_ref[...]
    stage(cq_ref, ck_ref, cv_ref, False, cqs, ckts, cvs)
    _ml_scan(cq_ref.shape[1], cqs, ckts, cvs, cgc_ref, cgr_ref, gbc, gbr, chf, chb, cn_ref, m_ref, tril, triu)
    finish(chf, chb, co_ref, cy_ref)
    stage(q_ref, k_ref, v_ref, True, qs, kts, vs)
    _ml_scan(q_ref.shape[1], qs, kts, vs, gc_ref, gr_ref, gbc, gbr, hf, hb, cn_ref, m_ref, tril, triu)
    finish(hf, hb, o_ref, y_ref)


def _rope_tables(T):
    t = jnp.arange(T)
    row = (t // GRID_W).astype(jnp.float32)
    col = (t % GRID_W).astype(jnp.float32)
    nf = HEAD_DIM // 4
    inv = ROPE_BASE ** (-jnp.arange(nf, dtype=jnp.float32) / nf)
    cr, sr, cc, sc = jnp.cos(row[:, None] * inv), jnp.sin(row[:, None] * inv), jnp.cos(col[:, None] * inv), \
        jnp.sin(col[:, None] * inv)
    cos = jnp.concatenate([cr, cr, cc, cc], axis=-1)
    sin = jnp.concatenate([-sr, sr, -sc, sc], axis=-1)
    return jnp.tile(cos, (1, ML_HEADS)), jnp.tile(sin, (1, ML_HEADS))


def _mlstm_mix(lq, lk, lv, lo, lg, lgr, cq, ck, cv, co, cg, cgr, conv_w, gate_b, norm_g):
    B, T, d = lq.shape
    Lc = cq.shape[1]
    L = ML_L
    assert T % L == 0 and Lc % L == 0
    cos, sin = _rope_tables(T)
    lane_head = np.arange(d) // HEAD_DIM
    same_head = jnp.asarray((lane_head[:, None] == lane_head[None, :]).astype(np.float32))
    tril = jnp.asarray(np.tril(np.ones((L, L), np.float32)))
    seq = lambda n: pl.BlockSpec((1, n, d), lambda b: (b, 0, 0))
    full = lambda a: pl.BlockSpec(a.shape, lambda b: (0,) * a.ndim)
    consts = [cos, sin, conv_w, gate_b[None, :], gate_b[:, None], norm_g[None, :], same_head, tril, tril.T]
    f32, bf16 = jnp.float32, jnp.bfloat16
    return pl.pallas_call(
        _mlstm_kernel,
        grid=(B,),
        in_specs=[seq(T)] * 4 + [pl.BlockSpec((1, T, ML_N_GATES), lambda b: (b, 0, 0)),
                                 pl.BlockSpec((1, ML_N_GATES, T), lambda b: (b, 0, 0))]
        + [seq(Lc)] * 4 + [pl.BlockSpec((1, Lc, ML_N_GATES), lambda b: (b, 0, 0)),
                           pl.BlockSpec((1, ML_N_GATES, Lc), lambda b: (b, 0, 0))]
        + [full(a) for a in consts],
        out_specs=[seq(T), seq(Lc)],
        out_shape=[jax.ShapeDtypeStruct((B, T, d), f32), jax.ShapeDtypeStruct((B, Lc, d), f32)],
        scratch_shapes=[pltpu.VMEM((T, d), bf16), pltpu.VMEM((d, T), bf16), pltpu.VMEM((T, d), bf16),
                        pltpu.VMEM((T, d), f32), pltpu.VMEM((T, d), f32),
                        pltpu.VMEM((Lc, d), bf16), pltpu.VMEM((d, Lc), bf16), pltpu.VMEM((Lc, d), bf16),
                        pltpu.VMEM((Lc, d), f32), pltpu.VMEM((Lc, d), f32),
                        pltpu.VMEM((2 * ML_HEADS, LANES, LANES), f32), pltpu.VMEM((2 * ML_HEADS, 1, 1), f32)],
        compiler_params=pltpu.CompilerParams(dimension_semantics=("parallel",), vmem_limit_bytes=VMEM_LIMIT_BYTES),
        name="mlstm",
    )(lq, lk, lv, lo, lg, lgr, cq, ck, cv, co, cg, cgr, *consts)


def _ada_kernel(c_ref, w_ref, b_ref, o_ref):
    c = c_ref[...]
    act = (c * jax.nn.sigmoid(c)).astype(jnp.bfloat16)
    o_ref[...] = jnp.dot(act, w_ref[0].astype(jnp.bfloat16), preferred_element_type=jnp.float32) + b_ref[0]


def _ada_mod(cond, w_all, b_all, layer):
    rows, d = cond.shape
    n = w_all.shape[2]
    tile = ADA_COL_TILE
    assert n % tile == 0
    return pl.pallas_call(
        _ada_kernel,
        grid=(n // tile,),
        in_specs=[pl.BlockSpec((rows, d), lambda j: (0, 0)), pl.BlockSpec((1, d, tile), lambda j: (layer, 0, j)),
                  pl.BlockSpec((1, 1, tile), lambda j: (layer, 0, j))],
        out_specs=pl.BlockSpec((rows, tile), lambda j: (0, j)),
        out_shape=jax.ShapeDtypeStruct((rows, n), jnp.float32),
        compiler_params=pltpu.CompilerParams(dimension_semantics=("parallel",)),
        name="ada_mod",
    )(cond, w_all, b_all[:, None, :])


_IN_SPLITS = (("q", NA_DIM), ("k", NA_DIM), ("v", NA_DIM), ("pool", POOL_DIM), ("mq", ML_DIM), ("mk", ML_DIM),
              ("mv", ML_DIM), ("mo", ML_DIM), ("gates", ML_N_GATES))
IN_COLS = sum(n for _, n in _IN_SPLITS)
IN_COLS_PAD = -(-IN_COLS // LANES) * LANES


def _in_proj_kernel(*refs, pending):
    if pending:
        (x_ref, g2_ref, yt_ref, sc_ref, sh_ref, g_ref, w_ref, wgt_ref, gq_ref, gk_ref, e_ref, xo_ref,
         q_ref, k_ref, v_ref, pool_ref, mq_ref, mk_ref, mv_ref, mo_ref, gc_ref, gr_ref) = refs
        x = x_ref[0] + g2_ref[0] * yt_ref[...].T
        xo_ref[0] = x
    else:
        (x_ref, sc_ref, sh_ref, g_ref, w_ref, wgt_ref, gq_ref, gk_ref, e_ref,
         q_ref, k_ref, v_ref, pool_ref, mq_ref, mk_ref, mv_ref, mo_ref, gc_ref, gr_ref) = refs
        x = x_ref[0]
    y = x * lax.rsqrt(jnp.mean(x * x, axis=-1, keepdims=True) + EPS) * g_ref[...]
    xn = (y * (1.0 + sc_ref[0]) + sh_ref[0]).astype(jnp.bfloat16)
    p = jnp.dot(xn, w_ref[...], preferred_element_type=jnp.float32)
    off = {}
    o = 0
    for name, n in _IN_SPLITS:
        off[name] = slice(o, o + n)
        o += n

    def head_norm(u, gain):
        sq = u * u
        hi = sq.astype(jnp.bfloat16)
        lo = (sq - hi.astype(jnp.float32)).astype(jnp.bfloat16)
        ss = (jnp.dot(hi, e_ref[...], preferred_element_type=jnp.float32)
              + jnp.dot(lo, e_ref[...], preferred_element_type=jnp.float32))
        return u * lax.rsqrt(ss * (1.0 / HEAD_DIM) + EPS) * gain

    q_ref[0] = (head_norm(p[:, off["q"]], gq_ref[...]) * HEAD_DIM ** -0.5).astype(jnp.bfloat16)
    k_ref[0] = head_norm(p[:, off["k"]], gk_ref[...]).astype(jnp.bfloat16)
    v_ref[0] = p[:, off["v"]].astype(jnp.bfloat16)
    pool_ref[0] = p[:, off["pool"]]
    mq_ref[0] = p[:, off["mq"]]
    mk_ref[0] = p[:, off["mk"]]
    mv_ref[0] = p[:, off["mv"]]
    mo_ref[0] = p[:, off["mo"]]
    gc_ref[0] = p[:, off["gates"]]
    gr_ref[0] = lax.dot_general(wgt_ref[...], xn, (((1,), (1,)), ((), ())), preferred_element_type=jnp.float32)


def _in_proj(x3, sc, sh, norm_g, w_in, gq, gk, pending=None):
    B, T, d = x3.shape
    tn = min(IN_TOKEN_TILE, T)
    assert T % tn == 0
    nt = T // tn
    w_bf = jnp.pad(w_in, ((0, 0), (0, IN_COLS_PAD - IN_COLS))).astype(jnp.bfloat16)
    wgt_bf = w_in[:, IN_COLS - ML_N_GATES:].T.astype(jnp.bfloat16)
    lane_head = np.arange(NA_DIM) // HEAD_DIM
    same_head = jnp.asarray((lane_head[:, None] == lane_head[None, :]).astype(np.float32)).astype(jnp.bfloat16)
    mod = pl.BlockSpec((1, 1, d), lambda b, t: (b, 0, 0))
    full = lambda a: pl.BlockSpec(a.shape, lambda b, t: (0,) * a.ndim)
    tok = lambda n: pl.BlockSpec((1, tn, n), lambda b, t: (b, t, 0))
    shp = lambda n, dt: jax.ShapeDtypeStruct((B, T, n), dt)
    f32, bf16 = jnp.float32, jnp.bfloat16
    consts = [norm_g[None, :], w_bf, wgt_bf, jnp.tile(gq, NA_HEADS)[None, :], jnp.tile(gk, NA_HEADS)[None, :], same_head]
    names = ("q", "k", "v", "pool", "mq", "mk", "mv", "mo", "gc", "gr")
    in_specs = [tok(d), mod, mod] + [full(a) for a in consts]
    out_specs = [tok(NA_DIM)] * 3 + [tok(POOL_DIM)] + [tok(ML_DIM)] * 4 + [tok(ML_N_GATES)] \
        + [pl.BlockSpec((1, ML_N_GATES, tn), lambda b, t: (b, 0, t))]
    out_shape = [shp(NA_DIM, bf16)] * 3 + [shp(POOL_DIM, f32)] + [shp(ML_DIM, f32)] * 4 + [shp(ML_N_GATES, f32)] \
        + [jax.ShapeDtypeStruct((B, ML_N_GATES, T), f32)]
    args = (x3, sc, sh, *consts)
    if pending is not None:
        names = ("x",) + names
        in_specs = [tok(d), mod, pl.BlockSpec((d, tn), lambda b, t: (0, b * nt + t))] + in_specs[1:]
        out_specs, out_shape = [tok(d)] + out_specs, [shp(d, f32)] + out_shape
        args = (x3, *pending, sc, sh, *consts)
    outs = pl.pallas_call(
        functools.partial(_in_proj_kernel, pending=pending is not None),
        grid=(B, nt),
        in_specs=in_specs,
        out_specs=out_specs,
        out_shape=out_shape,
        compiler_params=pltpu.CompilerParams(
            dimension_semantics=("parallel", "parallel"), vmem_limit_bytes=VMEM_LIMIT_BYTES),
        name="in_proj",
    )(*args)
    return dict(zip(names, outs))


def kernel(x, c, ctx, c_ctx, w_ada, b_ada, norm1_g, w_in, ml_gate_b, na_q_g, na_k_g, na_rpb, pool_w,
           pool_scale, ml_conv, ml_norm_g, w_out, norm2_g, peer_wq, peer_keys, peer_u, peer_v):
    depth = w_in.shape[0]
    B, S, D = x.shape
    Lc = ctx.shape[1]
    hc1 = ctx.reshape(1, B * Lc, D)
    u_bf = _cast_bf16(peer_u.reshape(depth * PEER_EXPERTS, D))
    v_bf = _cast_bf16(peer_v.reshape(depth * PEER_EXPERTS, D))
    pend_x = pend_c = None
    for l in range(depth):
        need_ctx = l < depth - 1
        wqt_bf = peer_wq[l].astype(jnp.bfloat16).T
        keys_bf = peer_keys[l].astype(jnp.bfloat16).reshape(2 * PEER_HEADS, PEER_KEYS, PEER_DKEY // 2)
        mod = _ada_mod(jnp.concatenate([c, c_ctx[None, :]], axis=0), w_ada, b_ada, l).reshape(B + 1, 6, 1, D)
        sh1, sc1, g1, sh2, sc2, g2 = (mod[:B, i] for i in range(6))
        csh1, csc1, cg1, csh2, csc2, cg2 = (mod[B:, i] for i in range(6))
        a = _in_proj(x, sc1, sh1, norm1_g[l], w_in[l], na_q_g[l], na_k_g[l], pend_x)
        ac = _in_proj(hc1, csc1, csh1, norm1_g[l], w_in[l], na_q_g[l], na_k_g[l], pend_c)
        x, hc1 = a.get("x", x), ac.pop("x", hc1)
        ac = {k: v.reshape(B, Lc, v.shape[-1]) for k, v in ac.items() if k != "gr"} | {
            "gr": jnp.transpose(ac["gr"].reshape(ML_N_GATES, B, Lc), (1, 0, 2))}
        na_lat = _na_latent(a["q"], a["k"], a["v"], ac["k"], ac["v"], na_rpb[l])
        pool_lat = _pool_mix(a["pool"], pool_w[l], pool_scale[l])
        ml_lat, ml_ctx = _mlstm_mix(a["mq"], a["mk"], a["mv"], a["mo"], a["gc"], a["gr"],
                                    ac["mq"], ac["mk"], ac["mv"], ac["mo"], ac["gc"], ac["gr"],
                                    ml_conv[l], ml_gate_b[l], ml_norm_g[l])
        wo_bf = w_out[l].astype(jnp.bfloat16)
        x, yt = _mix_and_peer(x, g1, na_lat, pool_lat, ml_lat, wo_bf, sc2, sh2, norm2_g[l], wqt_bf, keys_bf,
                              u_bf, v_bf, l)
        pend_x = (g2, yt)
        if need_ctx:
            na_ctx = _na_context(ac["q"], ac["k"], ac["v"])
            pool_ctx = _pool_mix(ac["pool"], pool_w[l], pool_scale[l])
            flat = lambda t: t.reshape(1, B * Lc, t.shape[-1])
            hc1, yt_c = _mix_and_peer(hc1, cg1, flat(na_ctx), flat(pool_ctx), flat(ml_ctx), wo_bf, csc2, csh2,
                                      norm2_g[l], wqt_bf, keys_bf, u_bf, v_bf, l)
            pend_c = (cg2, yt_c)
    return _residual_t(x, *pend_x)
```

```python
import functools

import numpy as np
import jax
import jax.numpy as jnp
from jax import lax
from jax.experimental import pallas as pl
from jax.experimental.pallas import tpu as pltpu

D_MODEL = 1024
GRID_W = 64
HEAD_DIM = 64
NA_DIM = D_MODEL // 2
NA_HEADS = NA_DIM // HEAD_DIM
NA_WIN_ROWS = 8
NA_WIN_COLS = 16
POOL_DIM = D_MODEL // 4
POOL_WINDOWS = (2, 4, 8, 16)
POOL_GROUPS = len(POOL_WINDOWS)
POOL_GROUP_DIM = POOL_DIM // POOL_GROUPS
ML_DIM = D_MODEL // 4
ML_HEADS = ML_DIM // HEAD_DIM
ML_CONV_W = 5
ML_N_GATES = 4 * ML_HEADS
PEER_KEYS = 128
PEER_EXPERTS = PEER_KEYS * PEER_KEYS
PEER_HEADS = 8
PEER_TOPK = 16
PEER_DKEY = 256
ROPE_BASE = 10000.0
EPS = 1e-6

LANES = 128
SUBLANES = 8
V7X_VMEM_BYTES = 64 * 1024 * 1024

PEER_TOKEN_TILE = 1024
PEER_EXPERT_TILE = 1024
ROUTE_TOKEN_TILE = 512
ROUTE_LANE_CHUNK = 2 * LANES
IN_TOKEN_TILE = 256
NA_ROWS_PER_STEP = 4
NA_UNION_ROWS = -(-(NA_WIN_ROWS + NA_ROWS_PER_STEP - 1) // 2) * 2
ADA_COL_TILE = 1536
CAST_ROW_TILE = 1024
POOL_BLOCK = 128
POOL_HALO = SUBLANES
ML_L = 256
VMEM_LIMIT_BYTES = V7X_VMEM_BYTES - 8 * 1024 * 1024

_GELU_C0 = float(np.sqrt(2.0 / np.pi))
_GELU_C1 = 0.044715 * _GELU_C0


def _gelu_tanh(x):
    inner = x * (_GELU_C0 + _GELU_C1 * (x * x))
    hx = 0.5 * x
    return hx + hx * jnp.tanh(inner)


def _peer_dense_kernel(x_ref, u_ref, v_ref, cnt_ref, e1_ref, rank_ref, e2_ref, o_ref):
    j = pl.program_id(1)

    @pl.when(j == 0)
    def _():
        o_ref[...] = jnp.zeros_like(o_ref)

    te = u_ref.shape[0]
    act = lax.dot_general(u_ref[...], x_ref[...], (((1,), (1,)), ((), ())),
                          preferred_element_type=jnp.float32)
    gates = []
    for k in range(te // PEER_KEYS):
        g = None
        for h in range(PEER_HEADS):
            cnt = cnt_ref[h, k:k + 1, :].astype(jnp.bfloat16)
            e1 = e1_ref[h, k:k + 1, :].astype(jnp.bfloat16)
            term = jnp.where(rank_ref[h] < cnt, e2_ref[h], jnp.zeros((), e2_ref.dtype)) * e1
            g = term if g is None else g + term
        gates.append(g)
    gate = jnp.concatenate(gates, axis=0)
    p = _gelu_tanh(act.astype(jnp.bfloat16)) * gate
    o_ref[...] += lax.dot_general(v_ref[...], p, (((0,), (0,)), ((), ())), preferred_element_type=jnp.float32)


def _cast_kernel(x_ref, o_ref):
    o_ref[...] = x_ref[...].astype(o_ref.dtype)


def _cast_bf16(w):
    rows, cols = w.shape
    tr = CAST_ROW_TILE
    assert rows % tr == 0
    blk = pl.BlockSpec((tr, cols), lambda i: (i, 0))
    return pl.pallas_call(
        _cast_kernel, grid=(rows // tr,), in_specs=[blk], out_specs=blk,
        out_shape=jax.ShapeDtypeStruct(w.shape, jnp.bfloat16),
        compiler_params=pltpu.CompilerParams(dimension_semantics=("parallel",)),
        name="cast_bf16",
    )(w)


def _peer_dense(tok, u_bf, v_bf, layer, cnt, e1, rank, e2):
    n, d = tok.shape
    tn, te = PEER_TOKEN_TILE, PEER_EXPERT_TILE
    assert n % tn == 0 and PEER_EXPERTS % te == 0 and te % PEER_KEYS == 0
    first = layer * (PEER_EXPERTS // te)
    tab_a = pl.BlockSpec((PEER_HEADS, te // PEER_KEYS, tn), lambda i, j: (0, j, i))
    tab_b = pl.BlockSpec((PEER_HEADS, PEER_KEYS, tn), lambda i, j: (0, 0, i))
    return pl.pallas_call(
        _peer_dense_kernel,
        grid=(n // tn, PEER_EXPERTS // te),
        in_specs=[
            pl.BlockSpec((tn, d), lambda i, j: (i, 0)),
            pl.BlockSpec((te, d), lambda i, j: (first + j, 0)),
            pl.BlockSpec((te, d), lambda i, j: (first + j, 0)),
            tab_a, tab_a, tab_b, tab_b,
        ],
        out_specs=pl.BlockSpec((d, tn), lambda i, j: (0, i)),
        out_shape=jax.ShapeDtypeStruct((d, n), jnp.float32),
        compiler_params=pltpu.CompilerParams(
            dimension_semantics=("parallel", "arbitrary"), vmem_limit_bytes=VMEM_LIMIT_BYTES),
        name="peer_dense",
    )(tok, u_bf, v_bf, cnt, e1, rank, e2)


def _top16(works, want_ranks):
    lanes = works[0].shape[1]
    rows = lax.broadcasted_iota(jnp.int32, (PEER_TOPK, lanes), 0)
    vals = [jnp.zeros((PEER_TOPK, lanes), jnp.float32) for _ in works]
    ranks = [jnp.full(w.shape, float(PEER_TOPK), jnp.float32) if r else None for w, r in zip(works, want_ranks)]
    tops = [[] for _ in works]
    works = list(works)
    for i in range(PEER_TOPK):
        ms = [jnp.max(w, axis=0, keepdims=True) for w in works]
        hits = [w == m for w, m in zip(works, ms)]
        ranks = [None if r is None else jnp.where(hit, float(i), r) for r, hit in zip(ranks, hits)]
        works = [jnp.where(hit, -jnp.inf, w) for w, hit in zip(works, hits)]
        vals = [jnp.where(rows == i, m, v) for v, m in zip(vals, ms)]
        for t, m in zip(tops, ms):
            t.append(m)
    return tops, vals, ranks


def _peer_route_kernel(x_ref, g1_ref, na_ref, pool_ref, ml_ref, wo_ref, sc_ref, sh_ref, g_ref, wqt_ref, keys_ref,
                       xo_ref, tok_ref, cnt_ref, e1_ref, rank_ref, e2_ref, s_scr):
    bf = jnp.bfloat16
    mix = jnp.dot(na_ref[0].astype(bf), wo_ref[0:NA_DIM, :], preferred_element_type=jnp.float32)
    mix += jnp.dot(pool_ref[0].astype(bf), wo_ref[NA_DIM:NA_DIM + POOL_DIM, :], preferred_element_type=jnp.float32)
    mix += jnp.dot(ml_ref[0].astype(bf), wo_ref[NA_DIM + POOL_DIM:, :], preferred_element_type=jnp.float32)
    x = x_ref[0] + g1_ref[0] * mix
    xo_ref[0] = x
    y = x * lax.rsqrt(jnp.mean(x * x, axis=-1, keepdims=True) + EPS) * g_ref[...]
    tok = (y * (1.0 + sc_ref[0]) + sh_ref[0]).astype(jnp.bfloat16)
    tok_ref[...] = tok
    qt = lax.dot_general(wqt_ref[...], tok, (((1,), (1,)), ((), ())),
                         preferred_element_type=jnp.float32).astype(jnp.bfloat16)
    dk = PEER_DKEY // 2
    for hp in range(2 * PEER_HEADS):
        s_scr[hp] = jnp.dot(keys_ref[hp], qt[hp * dk:(hp + 1) * dk], preferred_element_type=jnp.float32)

    cw = ROUTE_LANE_CHUNK
    lane_chunks = x.shape[0] // cw
    half = PEER_TOPK // 2
    row_half = lax.broadcasted_iota(jnp.int32, (half, cw), 0)

    def body(it, carry):
        h = it // lane_chunks
        lanes = pl.ds(pl.multiple_of((it % lane_chunks) * cw, cw), cw)
        s1 = s_scr[2 * h, :, lanes]
        s2 = s_scr[2 * h + 1, :, lanes]
        (t1, t2), (v1, v2), (_, rank) = _top16([s1, s2], [False, True])
        groups = [t1[0] + v2, t1[1] + v2[:half]]
        for i in range(2, half):
            groups.append(jnp.where(row_half < PEER_TOPK // (i + 1), t1[i] + v2[:half], -jnp.inf))
        groups.append(v1[half:] + t2[0])
        (tp,), (vp,), _ = _top16([jnp.concatenate(groups, axis=0)], [False])
        tau = tp[PEER_TOPK - 1]
        z = jnp.sum(jnp.exp(vp - tp[0]), axis=0, keepdims=True)
        kept = jnp.zeros(v1.shape, jnp.float32)
        for jj in range(PEER_TOPK):
            kept = jnp.where(v1 + t2[jj] >= tau, float(jj + 1), kept)
        cnt = jnp.zeros(s1.shape, jnp.float32)
        for i in range(PEER_TOPK):
            cnt = jnp.where(s1 == t1[i], kept[i:i + 1], cnt)
        cnt_ref[h, :, lanes] = cnt
        e1_ref[h, :, lanes] = jnp.exp(s1 - t1[0])
        rank_ref[h, :, lanes] = rank.astype(jnp.bfloat16)
        e2_ref[h, :, lanes] = (jnp.exp(s2 - t2[0]) * (1.0 / z)).astype(jnp.bfloat16)
        return carry

    lax.fori_loop(0, PEER_HEADS * lane_chunks, body, 0)


def _peer_route(x3, gate1, na, pool, ml, wo_bf, sc, sh, g, wqt_bf, keys_bf):
    B, T, d = x3.shape
    tn = min(ROUTE_TOKEN_TILE, T)
    assert T % tn == 0 and tn % ROUTE_LANE_CHUNK == 0
    nt = T // tn
    n = B * T
    tab = pl.BlockSpec((PEER_HEADS, PEER_KEYS, tn), lambda b, t: (0, 0, b * nt + t))
    tab_shape = lambda dt: jax.ShapeDtypeStruct((PEER_HEADS, PEER_KEYS, n), dt)
    mod = pl.BlockSpec((1, 1, d), lambda b, t: (b, 0, 0))
    tok = lambda c: pl.BlockSpec((1, tn, c), lambda b, t: (b, t, 0))
    const = lambda a: pl.BlockSpec(a.shape, lambda b, t: (0,) * a.ndim, pipeline_mode=pl.Buffered(1))
    return pl.pallas_call(
        _peer_route_kernel,
        grid=(B, nt),
        in_specs=[tok(d), mod, tok(NA_DIM), tok(POOL_DIM), tok(ML_DIM), const(wo_bf), mod, mod,
                  pl.BlockSpec((1, d), lambda b, t: (0, 0)), const(wqt_bf), const(keys_bf)],
        out_specs=[tok(d), pl.BlockSpec((tn, d), lambda b, t: (b * nt + t, 0)), tab, tab, tab, tab],
        out_shape=[jax.ShapeDtypeStruct(x3.shape, x3.dtype), jax.ShapeDtypeStruct((n, d), jnp.bfloat16),
                   tab_shape(jnp.float32), tab_shape(jnp.float32), tab_shape(jnp.bfloat16), tab_shape(jnp.bfloat16)],
        scratch_shapes=[pltpu.VMEM((2 * PEER_HEADS, PEER_KEYS, tn), jnp.float32)],
        compiler_params=pltpu.CompilerParams(
            dimension_semantics=("parallel", "parallel"), vmem_limit_bytes=VMEM_LIMIT_BYTES),
        name="peer_route",
    )(x3, gate1, na, pool, ml, wo_bf, sc, sh, g, wqt_bf, keys_bf)


def _residual_t_kernel(x_ref, g_ref, yt_ref, o_ref):
    o_ref[0] = x_ref[0] + g_ref[0] * yt_ref[...].T


def _residual_t(x3, gate, yt):
    B, T, d = x3.shape
    tn = min(ROUTE_TOKEN_TILE, T)
    nt = T // tn
    return pl.pallas_call(
        _residual_t_kernel,
        grid=(B, nt),
        in_specs=[pl.BlockSpec((1, tn, d), lambda b, t: (b, t, 0)),
                  pl.BlockSpec((1, 1, d), lambda b, t: (b, 0, 0)),
                  pl.BlockSpec((d, tn), lambda b, t: (0, b * nt + t))],
        out_specs=pl.BlockSpec((1, tn, d), lambda b, t: (b, t, 0)),
        out_shape=jax.ShapeDtypeStruct(x3.shape, x3.dtype),
        compiler_params=pltpu.CompilerParams(dimension_semantics=("parallel", "parallel")),
        name="residual_t",
    )(x3, gate, yt)


def _mix_and_peer(x3, gate1, na, pool, ml, wo_bf, sc, sh, norm_g, wqt_bf, keys_bf, u_bf, v_bf, layer):
    x1, tok, cnt, e1, rank, e2 = _peer_route(x3, gate1, na, pool, ml, wo_bf, sc, sh, norm_g[None, :], wqt_bf, keys_bf)
    return x1, _peer_dense(tok, u_bf, v_bf, layer, cnt, e1, rank, e2)


def _attend_pair(q2, segments, lane):
    zero = jnp.zeros((), q2.dtype)
    outs = []
    for half in range(2):
        own = (lane < HEAD_DIM) if half == 0 else (lane >= HEAD_DIM)
        qh = jnp.where(own, q2, zero)
        scores = []
        for k2, _, bias in segments:
            s = lax.dot_general(qh, k2, (((1,), (1,)), ((), ())), preferred_element_type=jnp.float32)
            scores.append(s if bias is None else s + bias[half])
        m = functools.reduce(jnp.maximum, [jnp.max(s, axis=-1, keepdims=True) for s in scores])
        ps = [jnp.exp(s - m) for s in scores]
        denom = functools.reduce(jnp.add, [jnp.sum(p, axis=-1, keepdims=True) for p in ps])
        acc = functools.reduce(jnp.add, [jnp.dot(p.astype(v2.dtype), v2, preferred_element_type=jnp.float32)
                                         for p, (_, v2, _) in zip(ps, segments)])
        outs.append(acc / denom)
    return jnp.where(lane < HEAD_DIM, outs[0], outs[1])


def _na_latent_kernel(q_ref, k_ref, v_ref, kc_ref, vc_ref, bank_ref, o_ref):
    G, KR = NA_ROWS_PER_STEP, NA_WIN_ROWS
    n_rows = k_ref.shape[1] // GRID_W
    first = pl.program_id(1) * G
    u0 = jnp.minimum(jnp.clip(first - KR // 2, 0, n_rows - KR), n_rows - NA_UNION_ROWS)
    keys = pl.ds(pl.multiple_of(u0 * GRID_W, GRID_W), NA_UNION_ROWS * GRID_W)
    key_row = u0 + lax.broadcasted_iota(jnp.int32, (1, NA_UNION_ROWS * GRID_W), 1) // GRID_W
    lane = lax.broadcasted_iota(jnp.int32, (q_ref.shape[1], LANES), 1)

    def head_bias(h):
        rows = []
        for i in range(G):
            r = first + i
            r0 = jnp.clip(r - KR // 2, 0, n_rows - KR)
            pairs = [bank_ref[h, jnp.clip(u0 + 2 * t + 1 - r + KR - 1, 0, 2 * KR - 1)]
                     for t in range(NA_UNION_ROWS // 2)]
            in_strip = (key_row >= r0) & (key_row < r0 + KR)
            rows.append(jnp.where(in_strip, jnp.concatenate(pairs, axis=1), -jnp.inf))
        return jnp.concatenate(rows, axis=0)

    for p in range(NA_DIM // LANES):
        lanes = slice(p * LANES, (p + 1) * LANES)
        segs = [(k_ref[0, keys, lanes], v_ref[0, keys, lanes], (head_bias(2 * p), head_bias(2 * p + 1))),
                (kc_ref[0, :, lanes], vc_ref[0, :, lanes], None)]
        o_ref[0, :, lanes] = _attend_pair(q_ref[0, :, lanes], segs, lane)


def _na_bias_bank(rpb):
    cq = np.arange(GRID_W)
    c0 = np.clip(cq - NA_WIN_COLS // 2, 0, GRID_W - NA_WIN_COLS)
    in_win = (cq[None, :] >= c0[:, None]) & (cq[None, :] < c0[:, None] + NA_WIN_COLS)
    bc = np.clip(cq[None, :] - cq[:, None] + (NA_WIN_COLS - 1), 0, 2 * NA_WIN_COLS - 2)
    onehot = jnp.asarray((bc[None] == np.arange(2 * NA_WIN_COLS - 1)[:, None, None]).astype(np.float32))
    toe = jnp.einsum('hrj,jqk->hrqk', rpb.astype(jnp.float32), onehot, precision=lax.Precision.HIGHEST)
    toe = jnp.where(in_win[None, None], toe, -jnp.inf)
    none = jnp.full((NA_HEADS, 1, GRID_W, GRID_W), -jnp.inf, jnp.float32)
    return jnp.concatenate([jnp.concatenate([none, toe], axis=1), jnp.concatenate([toe, none], axis=1)], axis=3)


def _na_latent(q, k, v, k_ctx, v_ctx, rpb):
    B, S, d = q.shape
    Lc = k_ctx.shape[1]
    n_rows = S // GRID_W
    assert n_rows >= NA_UNION_ROWS and n_rows % NA_ROWS_PER_STEP == 0
    bank = _na_bias_bank(rpb)
    tq = NA_ROWS_PER_STEP * GRID_W
    whole = lambda L: pl.BlockSpec((1, L, d), lambda b, g: (b, 0, 0))
    return pl.pallas_call(
        _na_latent_kernel,
        grid=(B, n_rows // NA_ROWS_PER_STEP),
        in_specs=[pl.BlockSpec((1, tq, d), lambda b, g: (b, g, 0)), whole(S), whole(S), whole(Lc), whole(Lc),
                  pl.BlockSpec(bank.shape, lambda b, g: (0, 0, 0, 0))],
        out_specs=pl.BlockSpec((1, tq, d), lambda b, g: (b, g, 0)),
        out_shape=jax.ShapeDtypeStruct((B, S, d), jnp.float32),
        compiler_params=pltpu.CompilerParams(
            dimension_semantics=("parallel", "arbitrary"), vmem_limit_bytes=VMEM_LIMIT_BYTES),
        name="na_latent",
    )(q, k, v, k_ctx, v_ctx, bank)


def _na_context_kernel(q_ref, k_ref, v_ref, o_ref):
    lane = lax.broadcasted_iota(jnp.int32, (q_ref.shape[1], LANES), 1)
    for p in range(NA_DIM // LANES):
        lanes = slice(p * LANES, (p + 1) * LANES)
        o_ref[0, :, lanes] = _attend_pair(q_ref[0, :, lanes], [(k_ref[0, :, lanes], v_ref[0, :, lanes], None)], lane)


def _na_context(q, k, v):
    B, Lc, d = q.shape
    blk = pl.BlockSpec((1, Lc, d), lambda b: (b, 0, 0))
    return pl.pallas_call(
        _na_context_kernel,
        grid=(B,),
        in_specs=[blk, blk, blk],
        out_specs=blk,
        out_shape=jax.ShapeDtypeStruct((B, Lc, d), jnp.float32),
        compiler_params=pltpu.CompilerParams(dimension_semantics=("parallel",)),
        name="na_context",
    )(q, k, v)


def _pool_matrices(T):
    tb, halo = POOL_BLOCK, POOL_HALO
    band = np.zeros((len(POOL_WINDOWS), 3, tb, tb + 2 * halo), np.float32)
    inv = np.zeros((3, tb, POOL_DIM), np.float32)
    for wi, w in enumerate(POOL_WINDOWS):
        for kind, start in enumerate((0, tb, T - tb)):
            for i in range(tb):
                t = start + i
                lo = min(max(t - w // 2, 0), T - 1)
                hi = min(max(t + (w - w // 2 - 1), 0), T - 1)
                band[wi, kind, i, lo - start + halo:hi - start + halo + 1] = 1.0
                inv[kind, i, wi * POOL_GROUP_DIM:(wi + 1) * POOL_GROUP_DIM] = 1.0 / (hi - lo + 1)
    return jnp.asarray(band, jnp.bfloat16), jnp.asarray(inv)


def _pool_kernel(u_ref, a_ref, inv_ref, w_ref, s_ref, o_ref):
    T = u_ref.shape[1]
    tb, halo = POOL_BLOCK, POOL_HALO
    nb = T // tb
    group = lax.broadcasted_iota(jnp.int32, (tb, POOL_DIM), 1) // POOL_GROUP_DIM
    zeros = jnp.zeros((halo, POOL_DIM), jnp.float32)
    for blk in range(nb):
        kind = 0 if blk == 0 else (2 if blk == nb - 1 else 1)
        x = u_ref[0, blk * tb:(blk + 1) * tb, :]
        before = zeros if blk == 0 else u_ref[0, blk * tb - halo:blk * tb, :]
        after = zeros if blk == nb - 1 else u_ref[0, (blk + 1) * tb:(blk + 1) * tb + halo, :]
        xcat = jnp.concatenate([before, x, after], axis=0)
        hi = xcat.astype(jnp.bfloat16)
        lo = (xcat - hi.astype(jnp.float32)).astype(jnp.bfloat16)
        total = None
        for wi in range(len(POOL_WINDOWS)):
            tw = (jnp.dot(a_ref[wi, kind], hi, preferred_element_type=jnp.float32)
                  + jnp.dot(a_ref[wi, kind], lo, preferred_element_type=jnp.float32))
            total = tw if total is None else jnp.where(group == wi, tw, total)
        pooled = total * inv_ref[kind] - x
        mixed = jnp.dot(pooled.astype(jnp.bfloat16), w_ref[...], preferred_element_type=jnp.float32)
        o_ref[0, blk * tb:(blk + 1) * tb, :] = mixed * s_ref[...]


def _pool_mix(u, pool_w, pool_scale):
    B, T, d = u.shape
    assert T % POOL_BLOCK == 0 and T >= 2 * POOL_BLOCK
    w_bd = jax.scipy.linalg.block_diag(*[pool_w[g] for g in range(POOL_GROUPS)]).astype(jnp.bfloat16)
    band, inv = _pool_matrices(T)
    blk = pl.BlockSpec((1, T, d), lambda b: (b, 0, 0))
    return pl.pallas_call(
        _pool_kernel,
        grid=(B,),
        in_specs=[blk, pl.BlockSpec(band.shape, lambda b: (0, 0, 0, 0)), pl.BlockSpec(inv.shape, lambda b: (0, 0, 0)),
                  pl.BlockSpec((d, d), lambda b: (0, 0)), pl.BlockSpec((1, d), lambda b: (0, 0))],
        out_specs=blk,
        out_shape=jax.ShapeDtypeStruct((B, T, d), jnp.float32),
        compiler_params=pltpu.CompilerParams(dimension_semantics=("parallel",)),
        name="pool_mix",
    )(u, band, inv, w_bd, pool_scale[None, :])


def _log_sigmoid(x):
    return jnp.minimum(x, 0.0) - jnp.log1p(jnp.exp(-jnp.abs(x)))


def _ml_prep(q, k, conv_ref, cos_ref, sin_ref):
    T = q.shape[0]
    row = lax.broadcasted_iota(jnp.int32, (T, 1), 0)
    lane = lax.broadcasted_iota(jnp.int32, (1, ML_DIM), 1) % (HEAD_DIM // 2)
    outs = []
    for idx, x in enumerate((q, k)):
        acc = None
        for j in range(ML_CONV_W):
            d = j - ML_CONV_W // 2
            w = conv_ref[j:j + 1, idx * ML_DIM:(idx + 1) * ML_DIM]
            if d == 0:
                term = x * w
            else:
                shifted = pltpu.roll(x, (-d) % T, 0)
                term = jnp.where((row + d >= 0) & (row + d < T), shifted, 0.0) * w
            acc = term if acc is None else acc + term
        y = acc * jax.nn.sigmoid(acc)
        if cos_ref is not None:
            quarter = HEAD_DIM // 4
            partner = jnp.where(lane < quarter, pltpu.roll(y, ML_DIM - quarter, 1), pltpu.roll(y, quarter, 1))
            y = y * cos_ref[...] + partner * sin_ref[...]
        outs.append(y)
    return outs


def _ml_chain(qs, kts, vs, rows, gi_c, gi_r, bc, br, ci, ii, head, backward, cn_ref, m_ref, slot, lane, tri):
    L = ML_L
    pair = slice((head // 2) * LANES, (head // 2 + 1) * LANES)
    own = (lane < HEAD_DIM) if head % 2 == 0 else (lane >= HEAD_DIM)
    one_lane = HEAD_DIM if head % 2 == 0 else 0
    q2 = jnp.where(own, qs[rows, pair], jnp.zeros((), jnp.bfloat16))
    v1 = jnp.where(own, vs[rows, pair].astype(jnp.float32), jnp.where(lane == one_lane, 1.0, 0.0))
    own_rows = lax.broadcasted_iota(jnp.int32, (LANES, L), 0)
    own_rows = (own_rows < HEAD_DIM) if head % 2 == 0 else (own_rows >= HEAD_DIM)
    kt2 = jnp.where(own_rows, kts[pair, rows], jnp.zeros((), jnp.bfloat16))
    m_prev = m_ref[slot]
    cn = cn_ref[slot]
    bcol = bc[:, ci:ci + 1]
    brow = br[ci:ci + 1, :]
    log_d = jnp.where(tri, bcol - brow + gi_r[ii:ii + 1, :], -jnp.inf)
    inter = bcol + m_prev
    m_t = jnp.maximum(inter, jnp.max(log_d, axis=-1, keepdims=True))
    s = jnp.dot(q2, kt2, preferred_element_type=jnp.float32)
    w_intra = (s * jnp.exp(log_d - m_t)).astype(jnp.bfloat16)
    nd = (jnp.dot(w_intra, v1.astype(jnp.bfloat16), preferred_element_type=jnp.float32)
          + jnp.exp(inter - m_t) * jnp.dot(q2, cn.astype(jnp.bfloat16), preferred_element_type=jnp.float32))
    den = nd[:, one_lane:one_lane + 1]
    h = nd / jnp.maximum(jnp.abs(den), jnp.exp(-m_t))
    b_last = bc[0:1, ci:ci + 1] if backward else bc[L - 1:L, ci:ci + 1]
    log_w = b_last - bcol + gi_c[:, ii:ii + 1]
    m_new = jnp.maximum(b_last + m_prev, jnp.max(log_w, axis=0, keepdims=True))
    wv = (jnp.exp(log_w - m_new) * v1).astype(jnp.bfloat16)
    cn_ref[slot] = jnp.exp(b_last + m_prev - m_new) * cn + jnp.dot(kt2, wv, preferred_element_type=jnp.float32)
    m_ref[slot] = m_new
    return h


def _ml_scan(T, qs, kts, vs, gc_ref, gr_ref, gbc, gbr, hf, hb, cn_ref, m_ref, tril, triu):
    L = ML_L
    nc = T // L
    lane = lax.broadcasted_iota(jnp.int32, (L, LANES), 1)
    t_idx = lax.broadcasted_iota(jnp.int32, (L, L), 0)
    s_idx = lax.broadcasted_iota(jnp.int32, (L, L), 1)
    hi = lax.Precision.HIGHEST

    def body(c, carry):
        for backward in (False, True):
            cc = (nc - 1 - c) if backward else c
            rows = pl.ds(pl.multiple_of(cc * L, L), L)
            gc = gc_ref[0, rows, :] + gbc
            gr = gr_ref[0, :, rows] + gbr
            bc = jnp.dot(triu if backward else tril, _log_sigmoid(gc), preferred_element_type=jnp.float32, precision=hi)
            br = jnp.dot(_log_sigmoid(gr), tril if backward else triu, preferred_element_type=jnp.float32, precision=hi)
            tri = (s_idx >= t_idx) if backward else (s_idx <= t_idx)
            out = hb if backward else hf
            base = 2 * ML_HEADS if backward else 0
            for pair in range(ML_HEADS // 2):
                hs = []
                for head in (2 * pair, 2 * pair + 1):
                    hs.append(_ml_chain(qs, kts, vs, rows, gc, gr, bc, br, base + ML_HEADS + head, base + head, head,
                                        backward, cn_ref, m_ref, (ML_HEADS if backward else 0) + head, lane, tri))
                out[rows, pair * LANES:(pair + 1) * LANES] = jnp.where(lane < HEAD_DIM, hs[0], hs[1])
        return carry

    lax.fori_loop(0, nc, body, 0)


def _mlstm_kernel(q_ref, k_ref, v_ref, o_ref, gc_ref, gr_ref, cq_ref, ck_ref, cv_ref, co_ref, cgc_ref, cgr_ref,
                  cos_ref, sin_ref, conv_ref, gbc_ref, gbr_ref, ng_ref, e_ref, tril_ref, triu_ref,
                  y_ref, cy_ref, qs, kts, vs, hf, hb, cqs, ckts, cvs, chf, chb, cn_ref, m_ref):
    cn_ref[...] = jnp.zeros_like(cn_ref)
    m_ref[...] = jnp.zeros_like(m_ref)
    tril, triu = tril_ref[...], triu_ref[...]

    def stage(q_r, k_r, v_r, rope, qd, ktd, vd):
        qp, kp = _ml_prep(q_r[0], k_r[0], conv_ref, cos_ref if rope else None, sin_ref if rope else None)
        qd[...] = qp.astype(jnp.bfloat16)
        ktd[...] = (kp * HEAD_DIM ** -0.5).T.astype(jnp.bfloat16)
        vd[...] = v_r[0].astype(jnp.bfloat16)

    def finish(h_f, h_b, o_r, y_r):
        h = h_f[...] + h_b[...]
        ss = jnp.dot(h * h, e_ref[...], preferred_element_type=jnp.float32, precision=lax.Precision.HIGHEST)
        y_r[0] = h * lax.rsqrt(ss * (1.0 / HEAD_DIM) + EPS) * ng_ref[...] * jax.nn.sigmoid(o_r[0])

    gbc, gbr = gbc_ref[...], gbr_ref[...]
    stage(cq_ref, ck_ref, cv_ref, False, cqs, ckts, cvs)
    _ml_scan(cq_ref.shape[1], cqs, ckts, cvs, cgc_ref, cgr_ref, gbc, gbr, chf, chb, cn_ref, m_ref, tril, triu)
    finish(chf, chb, co_ref, cy_ref)
    stage(q_ref, k_ref, v_ref, True, qs, kts, vs)
    _ml_scan(q_ref.shape[1], qs, kts, vs, gc_ref, gr_ref, gbc, gbr, hf, hb, cn_ref, m_ref, tril, triu)
    finish(hf, hb, o_ref, y_ref)


def _rope_tables(T):
    t = jnp.arange(T)
    row = (t // GRID_W).astype(jnp.float32)
    col = (t % GRID_W).astype(jnp.float32)
    nf = HEAD_DIM // 4
    inv = ROPE_BASE ** (-jnp.arange(nf, dtype=jnp.float32) / nf)
    cr, sr, cc, sc = jnp.cos(row[:, None] * inv), jnp.sin(row[:, None] * inv), jnp.cos(col[:, None] * inv), \
        jnp.sin(col[:, None] * inv)
    cos = jnp.concatenate([cr, cr, cc, cc], axis=-1)
    sin = jnp.concatenate([-sr, sr, -sc, sc], axis=-1)
    return jnp.tile(cos, (1, ML_HEADS)), jnp.tile(sin, (1, ML_HEADS))


def _mlstm_mix(lq, lk, lv, lo, lg, lgr, cq, ck, cv, co, cg, cgr, conv_w, gate_b, norm_g):
    B, T, d = lq.shape
    Lc = cq.shape[1]
    L = ML_L
    assert T % L == 0 and Lc % L == 0
    cos, sin = _rope_tables(T)
    lane_head = np.arange(d) // HEAD_DIM
    same_head = jnp.asarray((lane_head[:, None] == lane_head[None, :]).astype(np.float32))
    tril = jnp.asarray(np.tril(np.ones((L, L), np.float32)))
    seq = lambda n: pl.BlockSpec((1, n, d), lambda b: (b, 0, 0))
    full = lambda a: pl.BlockSpec(a.shape, lambda b: (0,) * a.ndim)
    consts = [cos, sin, conv_w, gate_b[None, :], gate_b[:, None], norm_g[None, :], same_head, tril, tril.T]
    f32, bf16 = jnp.float32, jnp.bfloat16
    return pl.pallas_call(
        _mlstm_kernel,
        grid=(B,),
        in_specs=[seq(T)] * 4 + [pl.BlockSpec((1, T, ML_N_GATES), lambda b: (b, 0, 0)),
                                 pl.BlockSpec((1, ML_N_GATES, T), lambda b: (b, 0, 0))]
        + [seq(Lc)] * 4 + [pl.BlockSpec((1, Lc, ML_N_GATES), lambda b: (b, 0, 0)),
                           pl.BlockSpec((1, ML_N_GATES, Lc), lambda b: (b, 0, 0))]
        + [full(a) for a in consts],
        out_specs=[seq(T), seq(Lc)],
        out_shape=[jax.ShapeDtypeStruct((B, T, d), f32), jax.ShapeDtypeStruct((B, Lc, d), f32)],
        scratch_shapes=[pltpu.VMEM((T, d), bf16), pltpu.VMEM((d, T), bf16), pltpu.VMEM((T, d), bf16),
                        pltpu.VMEM((T, d), f32), pltpu.VMEM((T, d), f32),
                        pltpu.VMEM((Lc, d), bf16), pltpu.VMEM((d, Lc), bf16), pltpu.VMEM((Lc, d), bf16),
                        pltpu.VMEM((Lc, d), f32), pltpu.VMEM((Lc, d), f32),
                        pltpu.VMEM((2 * ML_HEADS, LANES, LANES), f32), pltpu.VMEM((2 * ML_HEADS, 1, 1), f32)],
        compiler_params=pltpu.CompilerParams(dimension_semantics=("parallel",), vmem_limit_bytes=VMEM_LIMIT_BYTES),
        name="mlstm",
    )(lq, lk, lv, lo, lg, lgr, cq, ck, cv, co, cg, cgr, *consts)


def _ada_kernel(c_ref, w_ref, b_ref, o_ref):
    c = c_ref[...]
    act = (c * jax.nn.sigmoid(c)).astype(jnp.bfloat16)
    o_ref[...] = jnp.dot(act, w_ref[0].astype(jnp.bfloat16), preferred_element_type=jnp.float32) + b_ref[0]


def _ada_mod(cond, w_all, b_all, layer):
    rows, d = cond.shape
    n = w_all.shape[2]
    tile = ADA_COL_TILE
    assert n % tile == 0
    return pl.pallas_call(
        _ada_kernel,
        grid=(n // tile,),
        in_specs=[pl.BlockSpec((rows, d), lambda j: (0, 0)), pl.BlockSpec((1, d, tile), lambda j: (layer, 0, j)),
                  pl.BlockSpec((1, 1, tile), lambda j: (layer, 0, j))],
        out_specs=pl.BlockSpec((rows, tile), lambda j: (0, j)),
        out_shape=jax.ShapeDtypeStruct((rows, n), jnp.float32),
        compiler_params=pltpu.CompilerParams(dimension_semantics=("parallel",)),
        name="ada_mod",
    )(cond, w_all, b_all[:, None, :])


_IN_SPLITS = (("q", NA_DIM), ("k", NA_DIM), ("v", NA_DIM), ("pool", POOL_DIM), ("mq", ML_DIM), ("mk", ML_DIM),
              ("mv", ML_DIM), ("mo", ML_DIM), ("gates", ML_N_GATES))
IN_COLS = sum(n for _, n in _IN_SPLITS)
IN_COLS_PAD = -(-IN_COLS // LANES) * LANES


def _in_proj_kernel(*refs, pending):
    if pending:
        (x_ref, g2_ref, yt_ref, sc_ref, sh_ref, g_ref, w_ref, wgt_ref, gq_ref, gk_ref, e_ref, xo_ref,
         q_ref, k_ref, v_ref, pool_ref, mq_ref, mk_ref, mv_ref, mo_ref, gc_ref, gr_ref) = refs
        x = x_ref[0] + g2_ref[0] * yt_ref[...].T
        xo_ref[0] = x
    else:
        (x_ref, sc_ref, sh_ref, g_ref, w_ref, wgt_ref, gq_ref, gk_ref, e_ref,
         q_ref, k_ref, v_ref, pool_ref, mq_ref, mk_ref, mv_ref, mo_ref, gc_ref, gr_ref) = refs
        x = x_ref[0]
    y = x * lax.rsqrt(jnp.mean(x * x, axis=-1, keepdims=True) + EPS) * g_ref[...]
    xn = (y * (1.0 + sc_ref[0]) + sh_ref[0]).astype(jnp.bfloat16)
    p = jnp.dot(xn, w_ref[...], preferred_element_type=jnp.float32)
    off = {}
    o = 0
    for name, n in _IN_SPLITS:
        off[name] = slice(o, o + n)
        o += n

    def head_norm(u, gain):
        sq = u * u
        hi = sq.astype(jnp.bfloat16)
        lo = (sq - hi.astype(jnp.float32)).astype(jnp.bfloat16)
        ss = (jnp.dot(hi, e_ref[...], preferred_element_type=jnp.float32)
              + jnp.dot(lo, e_ref[...], preferred_element_type=jnp.float32))
        return u * lax.rsqrt(ss * (1.0 / HEAD_DIM) + EPS) * gain

    q_ref[0] = (head_norm(p[:, off["q"]], gq_ref[...]) * HEAD_DIM ** -0.5).astype(jnp.bfloat16)
    k_ref[0] = head_norm(p[:, off["k"]], gk_ref[...]).astype(jnp.bfloat16)
    v_ref[0] = p[:, off["v"]].astype(jnp.bfloat16)
    pool_ref[0] = p[:, off["pool"]]
    mq_ref[0] = p[:, off["mq"]]
    mk_ref[0] = p[:, off["mk"]]
    mv_ref[0] = p[:, off["mv"]]
    mo_ref[0] = p[:, off["mo"]]
    gc_ref[0] = p[:, off["gates"]]
    gr_ref[0] = lax.dot_general(wgt_ref[...], xn, (((1,), (1,)), ((), ())), preferred_element_type=jnp.float32)


def _in_proj(x3, sc, sh, norm_g, w_in, gq, gk, pending=None):
    B, T, d = x3.shape
    tn = min(IN_TOKEN_TILE, T)
    assert T % tn == 0
    nt = T // tn
    w_bf = jnp.pad(w_in, ((0, 0), (0, IN_COLS_PAD - IN_COLS))).astype(jnp.bfloat16)
    wgt_bf = w_in[:, IN_COLS - ML_N_GATES:].T.astype(jnp.bfloat16)
    lane_head = np.arange(NA_DIM) // HEAD_DIM
    same_head = jnp.asarray((lane_head[:, None] == lane_head[None, :]).astype(np.float32)).astype(jnp.bfloat16)
    mod = pl.BlockSpec((1, 1, d), lambda b, t: (b, 0, 0))
    full = lambda a: pl.BlockSpec(a.shape, lambda b, t: (0,) * a.ndim)
    tok = lambda n: pl.BlockSpec((1, tn, n), lambda b, t: (b, t, 0))
    shp = lambda n, dt: jax.ShapeDtypeStruct((B, T, n), dt)
    f32, bf16 = jnp.float32, jnp.bfloat16
    consts = [norm_g[None, :], w_bf, wgt_bf, jnp.tile(gq, NA_HEADS)[None, :], jnp.tile(gk, NA_HEADS)[None, :], same_head]
    names = ("q", "k", "v", "pool", "mq", "mk", "mv", "mo", "gc", "gr")
    in_specs = [tok(d), mod, mod] + [full(a) for a in consts]
    out_specs = [tok(NA_DIM)] * 3 + [tok(POOL_DIM)] + [tok(ML_DIM)] * 4 + [tok(ML_N_GATES)] \
        + [pl.BlockSpec((1, ML_N_GATES, tn), lambda b, t: (b, 0, t))]
    out_shape = [shp(NA_DIM, bf16)] * 3 + [shp(POOL_DIM, f32)] + [shp(ML_DIM, f32)] * 4 + [shp(ML_N_GATES, f32)] \
        + [jax.ShapeDtypeStruct((B, ML_N_GATES, T), f32)]
    args = (x3, sc, sh, *consts)
    if pending is not None:
        names = ("x",) + names
        in_specs = [tok(d), mod, pl.BlockSpec((d, tn), lambda b, t: (0, b * nt + t))] + in_specs[1:]
        out_specs, out_shape = [tok(d)] + out_specs, [shp(d, f32)] + out_shape
        args = (x3, *pending, sc, sh, *consts)
    outs = pl.pallas_call(
        functools.partial(_in_proj_kernel, pending=pending is not None),
        grid=(B, nt),
        in_specs=in_specs,
        out_specs=out_specs,
        out_shape=out_shape,
        compiler_params=pltpu.CompilerParams(
            dimension_semantics=("parallel", "parallel"), vmem_limit_bytes=VMEM_LIMIT_BYTES),
        name="in_proj",
    )(*args)
    return dict(zip(names, outs))


def kernel(x, c, ctx, c_ctx, w_ada, b_ada, norm1_g, w_in, ml_gate_b, na_q_g, na_k_g, na_rpb, pool_w,
           pool_scale, ml_conv, ml_norm_g, w_out, norm2_g, peer_wq, peer_keys, peer_u, peer_v):
    depth = w_in.shape[0]
    B, S, D = x.shape
    Lc = ctx.shape[1]
    hc1 = ctx.reshape(1, B * Lc, D)
    u_bf = _cast_bf16(peer_u.reshape(depth * PEER_EXPERTS, D))
    v_bf = _cast_bf16(peer_v.reshape(depth * PEER_EXPERTS, D))
    pend_x = pend_c = None
    for l in range(depth):
        need_ctx = l < depth - 1
        wqt_bf = peer_wq[l].astype(jnp.bfloat16).T
        keys_bf = peer_keys[l].astype(jnp.bfloat16).reshape(2 * PEER_HEADS, PEER_KEYS, PEER_DKEY // 2)
        mod = _ada_mod(jnp.concatenate([c, c_ctx[None, :]], axis=0), w_ada, b_ada, l).reshape(B + 1, 6, 1, D)
        sh1, sc1, g1, sh2, sc2, g2 = (mod[:B, i] for i in range(6))
        csh1, csc1, cg1, csh2, csc2, cg2 = (mod[B:, i] for i in range(6))
        a = _in_proj(x, sc1, sh1, norm1_g[l], w_in[l], na_q_g[l], na_k_g[l], pend_x)
        ac = _in_proj(hc1, csc1, csh1, norm1_g[l], w_in[l], na_q_g[l], na_k_g[l], pend_c)
        x, hc1 = a.get("x", x), ac.pop("x", hc1)
        ac = {k: v.reshape(B, Lc, v.shape[-1]) for k, v in ac.items() if k != "gr"} | {
            "gr": jnp.transpose(ac["gr"].reshape(ML_N_GATES, B, Lc), (1, 0, 2))}
        na_lat = _na_latent(a["q"], a["k"], a["v"], ac["k"], ac["v"], na_rpb[l])
        pool_lat = _pool_mix(a["pool"], pool_w[l], pool_scale[l])
        ml_lat, ml_ctx = _mlstm_mix(a["mq"], a["mk"], a["mv"], a["mo"], a["gc"], a["gr"],
                                    ac["mq"], ac["mk"], ac["mv"], ac["mo"], ac["gc"], ac["gr"],
                                    ml_conv[l], ml_gate_b[l], ml_norm_g[l])
        wo_bf = w_out[l].astype(jnp.bfloat16)
        x, yt = _mix_and_peer(x, g1, na_lat, pool_lat, ml_lat, wo_bf, sc2, sh2, norm2_g[l], wqt_bf, keys_bf,
                              u_bf, v_bf, l)
        pend_x = (g2, yt)
        if need_ctx:
            na_ctx = _na_context(ac["q"], ac["k"], ac["v"])
            pool_ctx = _pool_mix(ac["pool"], pool_w[l], pool_scale[l])
            flat = lambda t: t.reshape(1, B * Lc, t.shape[-1])
            hc1, yt_c = _mix_and_peer(hc1, cg1, flat(na_ctx), flat(pool_ctx), flat(ml_ctx), wo_bf, csc2, csh2,
                                      norm2_g[l], wqt_bf, keys_bf, u_bf, v_bf, l)
            pend_c = (cg2, yt_c)
    return _residual_t(x, *pend_x)
```

```python
import functools

import numpy as np
import jax
import jax.numpy as jnp
from jax import lax
from jax.experimental import pallas as pl
from jax.experimental.pallas import tpu as pltpu

D_MODEL = 1024
GRID_W = 64
HEAD_DIM = 64
NA_DIM = D_MODEL // 2
NA_HEADS = NA_DIM // HEAD_DIM
NA_WIN_ROWS = 8
NA_WIN_COLS = 16
POOL_DIM = D_MODEL // 4
POOL_WINDOWS = (2, 4, 8, 16)
POOL_GROUPS = len(POOL_WINDOWS)
POOL_GROUP_DIM = POOL_DIM // POOL_GROUPS
ML_DIM = D_MODEL // 4
ML_HEADS = ML_DIM // HEAD_DIM
ML_CONV_W = 5
ML_N_GATES = 4 * ML_HEADS
PEER_KEYS = 128
PEER_EXPERTS = PEER_KEYS * PEER_KEYS
PEER_HEADS = 8
PEER_TOPK = 16
PEER_DKEY = 256
ROPE_BASE = 10000.0
EPS = 1e-6

LANES = 128
SUBLANES = 8
V7X_VMEM_BYTES = 64 * 1024 * 1024

PEER_TOKEN_TILE = 1024
PEER_EXPERT_TILE = 1024
ROUTE_TOKEN_TILE = 512
ROUTE_LANE_CHUNK = 2 * LANES
IN_TOKEN_TILE = 512
NA_ROWS_PER_STEP = 4
NA_UNION_ROWS = -(-(NA_WIN_ROWS + NA_ROWS_PER_STEP - 1) // 2) * 2
ADA_COL_TILE = 1536
CAST_ROW_TILE = 1024
POOL_BLOCK = 128
POOL_HALO = SUBLANES
ML_L = 256
VMEM_LIMIT_BYTES = V7X_VMEM_BYTES - 8 * 1024 * 1024

_GELU_C0 = float(np.sqrt(2.0 / np.pi))
_GELU_C1 = 0.044715 * _GELU_C0


def _gelu_tanh(x):
    inner = x * (_GELU_C0 + _GELU_C1 * (x * x))
    hx = 0.5 * x
    return hx + hx * jnp.tanh(inner)


def _peer_dense_kernel(x_ref, u_ref, v_ref, cnt_ref, e1_ref, rank_ref, e2_ref, o_ref):
    j = pl.program_id(1)

    @pl.when(j == 0)
    def _():
        o_ref[...] = jnp.zeros_like(o_ref)

    te = u_ref.shape[0]
    act = lax.dot_general(u_ref[...], x_ref[...], (((1,), (1,)), ((), ())),
                          preferred_element_type=jnp.float32)
    gates = []
    for k in range(te // PEER_KEYS):
        g = None
        for h in range(PEER_HEADS):
            cnt = cnt_ref[h, k:k + 1, :].astype(jnp.bfloat16)
            e1 = e1_ref[h, k:k + 1, :].astype(jnp.bfloat16)
            term = jnp.where(rank_ref[h] < cnt, e2_ref[h], jnp.zeros((), e2_ref.dtype)) * e1
            g = term if g is None else g + term
        gates.append(g)
    gate = jnp.concatenate(gates, axis=0)
    p = _gelu_tanh(act.astype(jnp.bfloat16)) * gate
    o_ref[...] += lax.dot_general(v_ref[...], p, (((0,), (0,)), ((), ())), preferred_element_type=jnp.float32)


def _cast_kernel(x_ref, o_ref):
    o_ref[...] = x_ref[...].astype(o_ref.dtype)


def _cast_bf16(w):
    rows, cols = w.shape
    tr = CAST_ROW_TILE
    assert rows % tr == 0
    blk = pl.BlockSpec((tr, cols), lambda i: (i, 0))
    return pl.pallas_call(
        _cast_kernel, grid=(rows // tr,), in_specs=[blk], out_specs=blk,
        out_shape=jax.ShapeDtypeStruct(w.shape, jnp.bfloat16),
        compiler_params=pltpu.CompilerParams(dimension_semantics=("parallel",)),
        name="cast_bf16",
    )(w)


def _peer_dense(tok, u_bf, v_bf, layer, cnt, e1, rank, e2):
    n, d = tok.shape
    tn, te = PEER_TOKEN_TILE, PEER_EXPERT_TILE
    assert n % tn == 0 and PEER_EXPERTS % te == 0 and te % PEER_KEYS == 0
    first = layer * (PEER_EXPERTS // te)
    tab_a = pl.BlockSpec((PEER_HEADS, te // PEER_KEYS, tn), lambda i, j: (0, j, i))
    tab_b = pl.BlockSpec((PEER_HEADS, PEER_KEYS, tn), lambda i, j: (0, 0, i))
    return pl.pallas_call(
        _peer_dense_kernel,
        grid=(n // tn, PEER_EXPERTS // te),
        in_specs=[
            pl.BlockSpec((tn, d), lambda i, j: (i, 0)),
            pl.BlockSpec((te, d), lambda i, j: (first + j, 0)),
            pl.BlockSpec((te, d), lambda i, j: (first + j, 0)),
            tab_a, tab_a, tab_b, tab_b,
        ],
        out_specs=pl.BlockSpec((d, tn), lambda i, j: (0, i)),
        out_shape=jax.ShapeDtypeStruct((d, n), jnp.float32),
        compiler_params=pltpu.CompilerParams(
            dimension_semantics=("parallel", "arbitrary"), vmem_limit_bytes=VMEM_LIMIT_BYTES),
        name="peer_dense",
    )(tok, u_bf, v_bf, cnt, e1, rank, e2)


def _top16(works, want_ranks):
    lanes = works[0].shape[1]
    rows = lax.broadcasted_iota(jnp.int32, (PEER_TOPK, lanes), 0)
    vals = [jnp.zeros((PEER_TOPK, lanes), jnp.float32) for _ in works]
    ranks = [jnp.full(w.shape, float(PEER_TOPK), jnp.float32) if r else None for w, r in zip(works, want_ranks)]
    tops = [[] for _ in works]
    works = list(works)
    for i in range(PEER_TOPK):
        ms = [jnp.max(w, axis=0, keepdims=True) for w in works]
        hits = [w == m for w, m in zip(works, ms)]
        ranks = [None if r is None else jnp.where(hit, float(i), r) for r, hit in zip(ranks, hits)]
        works = [jnp.where(hit, -jnp.inf, w) for w, hit in zip(works, hits)]
        vals = [jnp.where(rows == i, m, v) for v, m in zip(vals, ms)]
        for t, m in zip(tops, ms):
            t.append(m)
    return tops, vals, ranks


def _peer_route_kernel(x_ref, g1_ref, na_ref, pool_ref, ml_ref, wo_ref, sc_ref, sh_ref, g_ref, wqt_ref, keys_ref,
                       xo_ref, tok_ref, cnt_ref, e1_ref, rank_ref, e2_ref, s_scr):
    bf = jnp.bfloat16
    mix = jnp.dot(na_ref[0].astype(bf), wo_ref[0:NA_DIM, :], preferred_element_type=jnp.float32)
    mix += jnp.dot(pool_ref[0].astype(bf), wo_ref[NA_DIM:NA_DIM + POOL_DIM, :], preferred_element_type=jnp.float32)
    mix += jnp.dot(ml_ref[0].astype(bf), wo_ref[NA_DIM + POOL_DIM:, :], preferred_element_type=jnp.float32)
    x = x_ref[0] + g1_ref[0] * mix
    xo_ref[0] = x
    y = x * lax.rsqrt(jnp.mean(x * x, axis=-1, keepdims=True) + EPS) * g_ref[...]
    tok = (y * (1.0 + sc_ref[0]) + sh_ref[0]).astype(jnp.bfloat16)
    tok_ref[...] = tok
    qt = lax.dot_general(wqt_ref[...], tok, (((1,), (1,)), ((), ())),
                         preferred_element_type=jnp.float32).astype(jnp.bfloat16)
    dk = PEER_DKEY // 2
    for hp in range(2 * PEER_HEADS):
        s_scr[hp] = jnp.dot(keys_ref[hp], qt[hp * dk:(hp + 1) * dk], preferred_element_type=jnp.float32)

    cw = ROUTE_LANE_CHUNK
    lane_chunks = x.shape[0] // cw
    half = PEER_TOPK // 2
    row_half = lax.broadcasted_iota(jnp.int32, (half, cw), 0)

    def body(it, carry):
        h = it // lane_chunks
        lanes = pl.ds(pl.multiple_of((it % lane_chunks) * cw, cw), cw)
        s1 = s_scr[2 * h, :, lanes]
        s2 = s_scr[2 * h + 1, :, lanes]
        (t1, t2), (v1, v2), (_, rank) = _top16([s1, s2], [False, True])
        groups = [t1[0] + v2, t1[1] + v2[:half]]
        for i in range(2, half):
            groups.append(jnp.where(row_half < PEER_TOPK // (i + 1), t1[i] + v2[:half], -jnp.inf))
        groups.append(v1[half:] + t2[0])
        (tp,), (vp,), _ = _top16([jnp.concatenate(groups, axis=0)], [False])
        tau = tp[PEER_TOPK - 1]
        z = jnp.sum(jnp.exp(vp - tp[0]), axis=0, keepdims=True)
        kept = jnp.zeros(v1.shape, jnp.float32)
        for jj in range(PEER_TOPK):
            kept = jnp.where(v1 + t2[jj] >= tau, float(jj + 1), kept)
        cnt = jnp.zeros(s1.shape, jnp.float32)
        for i in range(PEER_TOPK):
            cnt = jnp.where(s1 == t1[i], kept[i:i + 1], cnt)
        cnt_ref[h, :, lanes] = cnt
        e1_ref[h, :, lanes] = jnp.exp(s1 - t1[0])
        rank_ref[h, :, lanes] = rank.astype(jnp.bfloat16)
        e2_ref[h, :, lanes] = (jnp.exp(s2 - t2[0]) * (1.0 / z)).astype(jnp.bfloat16)
        return carry

    lax.fori_loop(0, PEER_HEADS * lane_chunks, body, 0)


def _peer_route(x3, gate1, na, pool, ml, wo_bf, sc, sh, g, wqt_bf, keys_bf):
    B, T, d = x3.shape
    tn = min(ROUTE_TOKEN_TILE, T)
    assert T % tn == 0 and tn % ROUTE_LANE_CHUNK == 0
    nt = T // tn
    n = B * T
    tab = pl.BlockSpec((PEER_HEADS, PEER_KEYS, tn), lambda b, t: (0, 0, b * nt + t))
    tab_shape = lambda dt: jax.ShapeDtypeStruct((PEER_HEADS, PEER_KEYS, n), dt)
    mod = pl.BlockSpec((1, 1, d), lambda b, t: (b, 0, 0))
    tok = lambda c: pl.BlockSpec((1, tn, c), lambda b, t: (b, t, 0))
    const = lambda a: pl.BlockSpec(a.shape, lambda b, t: (0,) * a.ndim, pipeline_mode=pl.Buffered(1))
    return pl.pallas_call(
        _peer_route_kernel,
        grid=(B, nt),
        in_specs=[tok(d), mod, tok(NA_DIM), tok(POOL_DIM), tok(ML_DIM), const(wo_bf), mod, mod,
                  pl.BlockSpec((1, d), lambda b, t: (0, 0)), const(wqt_bf), const(keys_bf)],
        out_specs=[tok(d), pl.BlockSpec((tn, d), lambda b, t: (b * nt + t, 0)), tab, tab, tab, tab],
        out_shape=[jax.ShapeDtypeStruct(x3.shape, x3.dtype), jax.ShapeDtypeStruct((n, d), jnp.bfloat16),
                   tab_shape(jnp.float32), tab_shape(jnp.float32), tab_shape(jnp.bfloat16), tab_shape(jnp.bfloat16)],
        scratch_shapes=[pltpu.VMEM((2 * PEER_HEADS, PEER_KEYS, tn), jnp.float32)],
        compiler_params=pltpu.CompilerParams(
            dimension_semantics=("parallel", "parallel"), vmem_limit_bytes=VMEM_LIMIT_BYTES),
        name="peer_route",
    )(x3, gate1, na, pool, ml, wo_bf, sc, sh, g, wqt_bf, keys_bf)


def _residual_t_kernel(x_ref, g_ref, yt_ref, o_ref):
    o_ref[0] = x_ref[0] + g_ref[0] * yt_ref[...].T


def _residual_t(x3, gate, yt):
    B, T, d = x3.shape
    tn = min(ROUTE_TOKEN_TILE, T)
    nt = T // tn
    return pl.pallas_call(
        _residual_t_kernel,
        grid=(B, nt),
        in_specs=[pl.BlockSpec((1, tn, d), lambda b, t: (b, t, 0)),
                  pl.BlockSpec((1, 1, d), lambda b, t: (b, 0, 0)),
                  pl.BlockSpec((d, tn), lambda b, t: (0, b * nt + t))],
        out_specs=pl.BlockSpec((1, tn, d), lambda b, t: (b, t, 0)),
        out_shape=jax.ShapeDtypeStruct(x3.shape, x3.dtype),
        compiler_params=pltpu.CompilerParams(dimension_semantics=("parallel", "parallel")),
        name="residual_t",
    )(x3, gate, yt)


def _mix_and_peer(x3, gate1, na, pool, ml, wo_bf, sc, sh, norm_g, wqt_bf, keys_bf, u_bf, v_bf, layer):
    x1, tok, cnt, e1, rank, e2 = _peer_route(x3, gate1, na, pool, ml, wo_bf, sc, sh, norm_g[None, :], wqt_bf, keys_bf)
    return x1, _peer_dense(tok, u_bf, v_bf, layer, cnt, e1, rank, e2)


def _attend_pair(q2, segments, lane):
    zero = jnp.zeros((), q2.dtype)
    outs = []
    for half in range(2):
        own = (lane < HEAD_DIM) if half == 0 else (lane >= HEAD_DIM)
        qh = jnp.where(own, q2, zero)
        scores = []
        for k2, _, bias in segments:
            s = lax.dot_general(qh, k2, (((1,), (1,)), ((), ())), preferred_element_type=jnp.float32)
            scores.append(s if bias is None else s + bias[half])
        m = functools.reduce(jnp.maximum, [jnp.max(s, axis=-1, keepdims=True) for s in scores])
        ps = [jnp.exp(s - m) for s in scores]
        denom = functools.reduce(jnp.add, [jnp.sum(p, axis=-1, keepdims=True) for p in ps])
        acc = functools.reduce(jnp.add, [jnp.dot(p.astype(v2.dtype), v2, preferred_element_type=jnp.float32)
                                         for p, (_, v2, _) in zip(ps, segments)])
        outs.append(acc / denom)
    return jnp.where(lane < HEAD_DIM, outs[0], outs[1])


def _na_latent_kernel(q_ref, k_ref, v_ref, kc_ref, vc_ref, bank_ref, o_ref):
    G, KR = NA_ROWS_PER_STEP, NA_WIN_ROWS
    n_rows = k_ref.shape[1] // GRID_W
    first = pl.program_id(1) * G
    u0 = jnp.minimum(jnp.clip(first - KR // 2, 0, n_rows - KR), n_rows - NA_UNION_ROWS)
    keys = pl.ds(pl.multiple_of(u0 * GRID_W, GRID_W), NA_UNION_ROWS * GRID_W)
    key_row = u0 + lax.broadcasted_iota(jnp.int32, (1, NA_UNION_ROWS * GRID_W), 1) // GRID_W
    lane = lax.broadcasted_iota(jnp.int32, (q_ref.shape[1], LANES), 1)

    def head_bias(h):
        rows = []
        for i in range(G):
            r = first + i
            r0 = jnp.clip(r - KR // 2, 0, n_rows - KR)
            pairs = [bank_ref[h, jnp.clip(u0 + 2 * t + 1 - r + KR - 1, 0, 2 * KR - 1)]
                     for t in range(NA_UNION_ROWS // 2)]
            in_strip = (key_row >= r0) & (key_row < r0 + KR)
            rows.append(jnp.where(in_strip, jnp.concatenate(pairs, axis=1), -jnp.inf))
        return jnp.concatenate(rows, axis=0)

    for p in range(NA_DIM // LANES):
        lanes = slice(p * LANES, (p + 1) * LANES)
        segs = [(k_ref[0, keys, lanes], v_ref[0, keys, lanes], (head_bias(2 * p), head_bias(2 * p + 1))),
                (kc_ref[0, :, lanes], vc_ref[0, :, lanes], None)]
        o_ref[0, :, lanes] = _attend_pair(q_ref[0, :, lanes], segs, lane)


def _na_bias_bank(rpb):
    cq = np.arange(GRID_W)
    c0 = np.clip(cq - NA_WIN_COLS // 2, 0, GRID_W - NA_WIN_COLS)
    in_win = (cq[None, :] >= c0[:, None]) & (cq[None, :] < c0[:, None] + NA_WIN_COLS)
    bc = np.clip(cq[None, :] - cq[:, None] + (NA_WIN_COLS - 1), 0, 2 * NA_WIN_COLS - 2)
    onehot = jnp.asarray((bc[None] == np.arange(2 * NA_WIN_COLS - 1)[:, None, None]).astype(np.float32))
    toe = jnp.einsum('hrj,jqk->hrqk', rpb.astype(jnp.float32), onehot, precision=lax.Precision.HIGHEST)
    toe = jnp.where(in_win[None, None], toe, -jnp.inf)
    none = jnp.full((NA_HEADS, 1, GRID_W, GRID_W), -jnp.inf, jnp.float32)
    return jnp.concatenate([jnp.concatenate([none, toe], axis=1), jnp.concatenate([toe, none], axis=1)], axis=3)


def _na_latent(q, k, v, k_ctx, v_ctx, rpb):
    B, S, d = q.shape
    Lc = k_ctx.shape[1]
    n_rows = S // GRID_W
    assert n_rows >= NA_UNION_ROWS and n_rows % NA_ROWS_PER_STEP == 0
    bank = _na_bias_bank(rpb)
    tq = NA_ROWS_PER_STEP * GRID_W
    whole = lambda L: pl.BlockSpec((1, L, d), lambda b, g: (b, 0, 0))
    return pl.pallas_call(
        _na_latent_kernel,
        grid=(B, n_rows // NA_ROWS_PER_STEP),
        in_specs=[pl.BlockSpec((1, tq, d), lambda b, g: (b, g, 0)), whole(S), whole(S), whole(Lc), whole(Lc),
                  pl.BlockSpec(bank.shape, lambda b, g: (0, 0, 0, 0))],
        out_specs=pl.BlockSpec((1, tq, d), lambda b, g: (b, g, 0)),
        out_shape=jax.ShapeDtypeStruct((B, S, d), jnp.float32),
        compiler_params=pltpu.CompilerParams(
            dimension_semantics=("parallel", "arbitrary"), vmem_limit_bytes=VMEM_LIMIT_BYTES),
        name="na_latent",
    )(q, k, v, k_ctx, v_ctx, bank)


def _na_context_kernel(q_ref, k_ref, v_ref, o_ref):
    lane = lax.broadcasted_iota(jnp.int32, (q_ref.shape[1], LANES), 1)
    for p in range(NA_DIM // LANES):
        lanes = slice(p * LANES, (p + 1) * LANES)
        o_ref[0, :, lanes] = _attend_pair(q_ref[0, :, lanes], [(k_ref[0, :, lanes], v_ref[0, :, lanes], None)], lane)


def _na_context(q, k, v):
    B, Lc, d = q.shape
    blk = pl.BlockSpec((1, Lc, d), lambda b: (b, 0, 0))
    return pl.pallas_call(
        _na_context_kernel,
        grid=(B,),
        in_specs=[blk, blk, blk],
        out_specs=blk,
        out_shape=jax.ShapeDtypeStruct((B, Lc, d), jnp.float32),
        compiler_params=pltpu.CompilerParams(dimension_semantics=("parallel",)),
        name="na_context",
    )(q, k, v)


def _pool_matrices(T):
    tb, halo = POOL_BLOCK, POOL_HALO
    band = np.zeros((len(POOL_WINDOWS), 3, tb, tb + 2 * halo), np.float32)
    inv = np.zeros((3, tb, POOL_DIM), np.float32)
    for wi, w in enumerate(POOL_WINDOWS):
        for kind, start in enumerate((0, tb, T - tb)):
            for i in range(tb):
                t = start + i
                lo = min(max(t - w // 2, 0), T - 1)
                hi = min(max(t + (w - w // 2 - 1), 0), T - 1)
                band[wi, kind, i, lo - start + halo:hi - start + halo + 1] = 1.0
                inv[kind, i, wi * POOL_GROUP_DIM:(wi + 1) * POOL_GROUP_DIM] = 1.0 / (hi - lo + 1)
    return jnp.asarray(band, jnp.bfloat16), jnp.asarray(inv)


def _pool_kernel(u_ref, a_ref, inv_ref, w_ref, s_ref, o_ref):
    T = u_ref.shape[1]
    tb, halo = POOL_BLOCK, POOL_HALO
    nb = T // tb
    group = lax.broadcasted_iota(jnp.int32, (tb, POOL_DIM), 1) // POOL_GROUP_DIM
    zeros = jnp.zeros((halo, POOL_DIM), jnp.float32)
    for blk in range(nb):
        kind = 0 if blk == 0 else (2 if blk == nb - 1 else 1)
        x = u_ref[0, blk * tb:(blk + 1) * tb, :]
        before = zeros if blk == 0 else u_ref[0, blk * tb - halo:blk * tb, :]
        after = zeros if blk == nb - 1 else u_ref[0, (blk + 1) * tb:(blk + 1) * tb + halo, :]
        xcat = jnp.concatenate([before, x, after], axis=0)
        hi = xcat.astype(jnp.bfloat16)
        lo = (xcat - hi.astype(jnp.float32)).astype(jnp.bfloat16)
        total = None
        for wi in range(len(POOL_WINDOWS)):
            tw = (jnp.dot(a_ref[wi, kind], hi, preferred_element_type=jnp.float32)
                  + jnp.dot(a_ref[wi, kind], lo, preferred_element_type=jnp.float32))
            total = tw if total is None else jnp.where(group == wi, tw, total)
        pooled = total * inv_ref[kind] - x
        mixed = jnp.dot(pooled.astype(jnp.bfloat16), w_ref[...], preferred_element_type=jnp.float32)
        o_ref[0, blk * tb:(blk + 1) * tb, :] = mixed * s_ref[...]


def _pool_mix(u, pool_w, pool_scale):
    B, T, d = u.shape
    assert T % POOL_BLOCK == 0 and T >= 2 * POOL_BLOCK
    w_bd = jax.scipy.linalg.block_diag(*[pool_w[g] for g in range(POOL_GROUPS)]).astype(jnp.bfloat16)
    band, inv = _pool_matrices(T)
    blk = pl.BlockSpec((1, T, d), lambda b: (b, 0, 0))
    return pl.pallas_call(
        _pool_kernel,
        grid=(B,),
        in_specs=[blk, pl.BlockSpec(band.shape, lambda b: (0, 0, 0, 0)), pl.BlockSpec(inv.shape, lambda b: (0, 0, 0)),
                  pl.BlockSpec((d, d), lambda b: (0, 0)), pl.BlockSpec((1, d), lambda b: (0, 0))],
        out_specs=blk,
        out_shape=jax.ShapeDtypeStruct((B, T, d), jnp.float32),
        compiler_params=pltpu.CompilerParams(dimension_semantics=("parallel",)),
        name="pool_mix",
    )(u, band, inv, w_bd, pool_scale[None, :])


def _log_sigmoid(x):
    return jnp.minimum(x, 0.0) - jnp.log1p(jnp.exp(-jnp.abs(x)))


def _ml_prep(q, k, conv_ref, cos_ref, sin_ref):
    T = q.shape[0]
    row = lax.broadcasted_iota(jnp.int32, (T, 1), 0)
    lane = lax.broadcasted_iota(jnp.int32, (1, ML_DIM), 1) % (HEAD_DIM // 2)
    outs = []
    for idx, x in enumerate((q, k)):
        acc = None
        for j in range(ML_CONV_W):
            d = j - ML_CONV_W // 2
            w = conv_ref[j:j + 1, idx * ML_DIM:(idx + 1) * ML_DIM]
            if d == 0:
                term = x * w
            else:
                shifted = pltpu.roll(x, (-d) % T, 0)
                term = jnp.where((row + d >= 0) & (row + d < T), shifted, 0.0) * w
            acc = term if acc is None else acc + term
        y = acc * jax.nn.sigmoid(acc)
        if cos_ref is not None:
            quarter = HEAD_DIM // 4
            partner = jnp.where(lane < quarter, pltpu.roll(y, ML_DIM - quarter, 1), pltpu.roll(y, quarter, 1))
            y = y * cos_ref[...] + partner * sin_ref[...]
        outs.append(y)
    return outs


def _ml_chain(qs, kts, vs, rows, gi_c, gi_r, bc, br, ci, ii, head, backward, cn_ref, m_ref, slot, lane, tri):
    L = ML_L
    pair = slice((head // 2) * LANES, (head // 2 + 1) * LANES)
    own = (lane < HEAD_DIM) if head % 2 == 0 else (lane >= HEAD_DIM)
    one_lane = HEAD_DIM if head % 2 == 0 else 0
    q2 = jnp.where(own, qs[rows, pair], jnp.zeros((), jnp.bfloat16))
    v1 = jnp.where(own, vs[rows, pair].astype(jnp.float32), jnp.where(lane == one_lane, 1.0, 0.0))
    own_rows = lax.broadcasted_iota(jnp.int32, (LANES, L), 0)
    own_rows = (own_rows < HEAD_DIM) if head % 2 == 0 else (own_rows >= HEAD_DIM)
    kt2 = jnp.where(own_rows, kts[pair, rows], jnp.zeros((), jnp.bfloat16))
    m_prev = m_ref[slot]
    cn = cn_ref[slot]
    bcol = bc[:, ci:ci + 1]
    brow = br[ci:ci + 1, :]
    log_d = jnp.where(tri, bcol - brow + gi_r[ii:ii + 1, :], -jnp.inf)
    inter = bcol + m_prev
    m_t = jnp.maximum(inter, jnp.max(log_d, axis=-1, keepdims=True))
    s = jnp.dot(q2, kt2, preferred_element_type=jnp.float32)
    w_intra = (s * jnp.exp(log_d - m_t)).astype(jnp.bfloat16)
    nd = (jnp.dot(w_intra, v1.astype(jnp.bfloat16), preferred_element_type=jnp.float32)
          + jnp.exp(inter - m_t) * jnp.dot(q2, cn.astype(jnp.bfloat16), preferred_element_type=jnp.float32))
    den = nd[:, one_lane:one_lane + 1]
    h = nd / jnp.maximum(jnp.abs(den), jnp.exp(-m_t))
    b_last = bc[0:1, ci:ci + 1] if backward else bc[L - 1:L, ci:ci + 1]
    log_w = b_last - bcol + gi_c[:, ii:ii + 1]
    m_new = jnp.maximum(b_last + m_prev, jnp.max(log_w, axis=0, keepdims=True))
    wv = (jnp.exp(log_w - m_new) * v1).astype(jnp.bfloat16)
    cn_ref[slot] = jnp.exp(b_last + m_prev - m_new) * cn + jnp.dot(kt2, wv, preferred_element_type=jnp.float32)
    m_ref[slot] = m_new
    return h


def _ml_scan(T, qs, kts, vs, gc_ref, gr_ref, gbc, gbr, hf, hb, cn_ref, m_ref, tril, triu):
    L = ML_L
    nc = T // L
    lane = lax.broadcasted_iota(jnp.int32, (L, LANES), 1)
    t_idx = lax.broadcasted_iota(jnp.int32, (L, L), 0)
    s_idx = lax.broadcasted_iota(jnp.int32, (L, L), 1)
    hi = lax.Precision.HIGHEST

    def body(c, carry):
        for backward in (False, True):
            cc = (nc - 1 - c) if backward else c
            rows = pl.ds(pl.multiple_of(cc * L, L), L)
            gc = gc_ref[0, rows, :] + gbc
            gr = gr_ref[0, :, rows] + gbr
            bc = jnp.dot(triu if backward else tril, _log_sigmoid(gc), preferred_element_type=jnp.float32, precision=hi)
            br = jnp.dot(_log_sigmoid(gr), tril if backward else triu, preferred_element_type=jnp.float32, precision=hi)
            tri = (s_idx >= t_idx) if backward else (s_idx <= t_idx)
            out = hb if backward else hf
            base = 2 * ML_HEADS if backward else 0
            for pair in range(ML_HEADS // 2):
                hs = []
                for head in (2 * pair, 2 * pair + 1):
                    hs.append(_ml_chain(qs, kts, vs, rows, gc, gr, bc, br, base + ML_HEADS + head, base + head, head,
                                        backward, cn_ref, m_ref, (ML_HEADS if backward else 0) + head, lane, tri))
                out[rows, pair * LANES:(pair + 1) * LANES] = jnp.where(lane < HEAD_DIM, hs[0], hs[1])
        return carry

    lax.fori_loop(0, nc, body, 0)


def _mlstm_kernel(q_ref, k_ref, v_ref, o_ref, gc_ref, gr_ref, cq_ref, ck_ref, cv_ref, co_ref, cgc_ref, cgr_ref,
                  cos_ref, sin_ref, conv_ref, gbc_ref, gbr_ref, ng_ref, e_ref, tril_ref, triu_ref,
                  y_ref, cy_ref, qs, kts, vs, hf, hb, cqs, ckts, cvs, chf, chb, cn_ref, m_ref):
    cn_ref[...] = jnp.zeros_like(cn_ref)
    m_ref[...] = jnp.zeros_like(m_ref)
    tril, triu = tril_ref[...], triu_ref[...]

    def stage(q_r, k_r, v_r, rope, qd, ktd, vd):
        qp, kp = _ml_prep(q_r[0], k_r[0], conv_ref, cos_ref if rope else None, sin_ref if rope else None)
        qd[...] = qp.astype(jnp.bfloat16)
        ktd[...] = (kp * HEAD_DIM ** -0.5).T.astype(jnp.bfloat16)
        vd[...] = v_r[0].astype(jnp.bfloat16)

    def finish(h_f, h_b, o_r, y_r):
        h = h_f[...] + h_b[...]
        ss = jnp.dot(h * h, e_ref[...], preferred_element_type=jnp.float32, precision=lax.Precision.HIGHEST)
        y_r[0] = h * lax.rsqrt(ss * (1.0 / HEAD_DIM) + EPS) * ng_ref[...] * jax.nn.sigmoid(o_r[0])

    gbc, gbr = gbc_ref[...], gbr_ref[...]
    stage(cq_ref, ck_ref, cv_ref, False, cqs, ckts, cvs)
    _ml_scan(cq_ref.shape[1], cqs, ckts, cvs, cgc_ref, cgr_ref, gbc, gbr, chf, chb, cn_ref, m_ref, tril, triu)
    finish(chf, chb, co_ref, cy_ref)
    stage(q_ref, k_ref, v_ref, True, qs, kts, vs)
    _ml_scan(q_ref.shape[1], qs, kts, vs, gc_ref, gr_ref, gbc, gbr, hf, hb, cn_ref, m_ref, tril, triu)
    finish(hf, hb, o_ref, y_ref)


def _rope_tables(T):
    t = jnp.arange(T)
    row = (t // GRID_W).astype(jnp.float32)
    col = (t % GRID_W).astype(jnp.float32)
    nf = HEAD_DIM // 4
    inv = ROPE_BASE ** (-jnp.arange(nf, dtype=jnp.float32) / nf)
    cr, sr, cc, sc = jnp.cos(row[:, None] * inv), jnp.sin(row[:, None] * inv), jnp.cos(col[:, None] * inv), \
        jnp.sin(col[:, None] * inv)
    cos = jnp.concatenate([cr, cr, cc, cc], axis=-1)
    sin = jnp.concatenate([-sr, sr, -sc, sc], axis=-1)
    return jnp.tile(cos, (1, ML_HEADS)), jnp.tile(sin, (1, ML_HEADS))


def _mlstm_mix(lq, lk, lv, lo, lg, lgr, cq, ck, cv, co, cg, cgr, conv_w, gate_b, norm_g):
    B, T, d = lq.shape
    Lc = cq.shape[1]
    L = ML_L
    assert T % L == 0 and Lc % L == 0
    cos, sin = _rope_tables(T)
    lane_head = np.arange(d) // HEAD_DIM
    same_head = jnp.asarray((lane_head[:, None] == lane_head[None, :]).astype(np.float32))
    tril = jnp.asarray(np.tril(np.ones((L, L), np.float32)))
    seq = lambda n: pl.BlockSpec((1, n, d), lambda b: (b, 0, 0))
    full = lambda a: pl.BlockSpec(a.shape, lambda b: (0,) * a.ndim)
    consts = [cos, sin, conv_w, gate_b[None, :], gate_b[:, None], norm_g[None, :], same_head, tril, tril.T]
    f32, bf16 = jnp.float32, jnp.bfloat16
    return pl.pallas_call(
        _mlstm_kernel,
        grid=(B,),
        in_specs=[seq(T)] * 4 + [pl.BlockSpec((1, T, ML_N_GATES), lambda b: (b, 0, 0)),
                                 pl.BlockSpec((1, ML_N_GATES, T), lambda b: (b, 0, 0))]
        + [seq(Lc)] * 4 + [pl.BlockSpec((1, Lc, ML_N_GATES), lambda b: (b, 0, 0)),
                           pl.BlockSpec((1, ML_N_GATES, Lc), lambda b: (b, 0, 0))]
        + [full(a) for a in consts],
        out_specs=[seq(T), seq(Lc)],
        out_shape=[jax.ShapeDtypeStruct((B, T, d), f32), jax.ShapeDtypeStruct((B, Lc, d), f32)],
        scratch_shapes=[pltpu.VMEM((T, d), bf16), pltpu.VMEM((d, T), bf16), pltpu.VMEM((T, d), bf16),
                        pltpu.VMEM((T, d), f32), pltpu.VMEM((T, d), f32),
                        pltpu.VMEM((Lc, d), bf16), pltpu.VMEM((d, Lc), bf16), pltpu.VMEM((Lc, d), bf16),
                        pltpu.VMEM((Lc, d), f32), pltpu.VMEM((Lc, d), f32),
                        pltpu.VMEM((2 * ML_HEADS, LANES, LANES), f32), pltpu.VMEM((2 * ML_HEADS, 1, 1), f32)],
        compiler_params=pltpu.CompilerParams(dimension_semantics=("parallel",), vmem_limit_bytes=VMEM_LIMIT_BYTES),
        name="mlstm",
    )(lq, lk, lv, lo, lg, lgr, cq, ck, cv, co, cg, cgr, *consts)


def _ada_kernel(c_ref, w_ref, b_ref, o_ref):
    c = c_ref[...]
    act = (c * jax.nn.sigmoid(c)).astype(jnp.bfloat16)
    o_ref[...] = jnp.dot(act, w_ref[0].astype(jnp.bfloat16), preferred_element_type=jnp.float32) + b_ref[0]


def _ada_mod(cond, w_all, b_all, layer):
    rows, d = cond.shape
    n = w_all.shape[2]
    tile = ADA_COL_TILE
    assert n % tile == 0
    return pl.pallas_call(
        _ada_kernel,
        grid=(n // tile,),
        in_specs=[pl.BlockSpec((rows, d), lambda j: (0, 0)), pl.BlockSpec((1, d, tile), lambda j: (layer, 0, j)),
                  pl.BlockSpec((1, 1, tile), lambda j: (layer, 0, j))],
        out_specs=pl.BlockSpec((rows, tile), lambda j: (0, j)),
        out_shape=jax.ShapeDtypeStruct((rows, n), jnp.float32),
        compiler_params=pltpu.CompilerParams(dimension_semantics=("parallel",)),
        name="ada_mod",
    )(cond, w_all, b_all[:, None, :])


_IN_SPLITS = (("q", NA_DIM), ("k", NA_DIM), ("v", NA_DIM), ("pool", POOL_DIM), ("mq", ML_DIM), ("mk", ML_DIM),
              ("mv", ML_DIM), ("mo", ML_DIM), ("gates", ML_N_GATES))
IN_COLS = sum(n for _, n in _IN_SPLITS)
IN_COLS_PAD = -(-IN_COLS // LANES) * LANES


def _in_proj_kernel(*refs, pending):
    if pending:
        (x_ref, g2_ref, yt_ref, sc_ref, sh_ref, g_ref, w_ref, wgt_ref, gq_ref, gk_ref, e_ref, xo_ref,
         q_ref, k_ref, v_ref, pool_ref, mq_ref, mk_ref, mv_ref, mo_ref, gc_ref, gr_ref) = refs
        x = x_ref[0] + g2_ref[0] * yt_ref[...].T
        xo_ref[0] = x
    else:
        (x_ref, sc_ref, sh_ref, g_ref, w_ref, wgt_ref, gq_ref, gk_ref, e_ref,
         q_ref, k_ref, v_ref, pool_ref, mq_ref, mk_ref, mv_ref, mo_ref, gc_ref, gr_ref) = refs
        x = x_ref[0]
    y = x * lax.rsqrt(jnp.mean(x * x, axis=-1, keepdims=True) + EPS) * g_ref[...]
    xn = (y * (1.0 + sc_ref[0]) + sh_ref[0]).astype(jnp.bfloat16)
    p = jnp.dot(xn, w_ref[...], preferred_element_type=jnp.float32)
    off = {}
    o = 0
    for name, n in _IN_SPLITS:
        off[name] = slice(o, o + n)
        o += n

    def head_norm(u, gain):
        sq = u * u
        hi = sq.astype(jnp.bfloat16)
        lo = (sq - hi.astype(jnp.float32)).astype(jnp.bfloat16)
        ss = (jnp.dot(hi, e_ref[...], preferred_element_type=jnp.float32)
              + jnp.dot(lo, e_ref[...], preferred_element_type=jnp.float32))
        return u * lax.rsqrt(ss * (1.0 / HEAD_DIM) + EPS) * gain

    q_ref[0] = (head_norm(p[:, off["q"]], gq_ref[...]) * HEAD_DIM ** -0.5).astype(jnp.bfloat16)
    k_ref[0] = head_norm(p[:, off["k"]], gk_ref[...]).astype(jnp.bfloat16)
    v_ref[0] = p[:, off["v"]].astype(jnp.bfloat16)
    pool_ref[0] = p[:, off["pool"]]
    mq_ref[0] = p[:, off["mq"]]
    mk_ref[0] = p[:, off["mk"]]
    mv_ref[0] = p[:, off["mv"]]
    mo_ref[0] = p[:, off["mo"]]
    gc_ref[0] = p[:, off["gates"]]
    gr_ref[0] = lax.dot_general(wgt_ref[...], xn, (((1,), (1,)), ((), ())), preferred_element_type=jnp.float32)


def _in_proj(x3, sc, sh, norm_g, w_in, gq, gk, pending=None):
    B, T, d = x3.shape
    tn = min(IN_TOKEN_TILE, T)
    assert T % tn == 0
    nt = T // tn
    w_bf = jnp.pad(w_in, ((0, 0), (0, IN_COLS_PAD - IN_COLS))).astype(jnp.bfloat16)
    wgt_bf = w_in[:, IN_COLS - ML_N_GATES:].T.astype(jnp.bfloat16)
    lane_head = np.arange(NA_DIM) // HEAD_DIM
    same_head = jnp.asarray((lane_head[:, None] == lane_head[None, :]).astype(np.float32)).astype(jnp.bfloat16)
    mod = pl.BlockSpec((1, 1, d), lambda b, t: (b, 0, 0))
    full = lambda a: pl.BlockSpec(a.shape, lambda b, t: (0,) * a.ndim)
    tok = lambda n: pl.BlockSpec((1, tn, n), lambda b, t: (b, t, 0))
    shp = lambda n, dt: jax.ShapeDtypeStruct((B, T, n), dt)
    f32, bf16 = jnp.float32, jnp.bfloat16
    consts = [norm_g[None, :], w_bf, wgt_bf, jnp.tile(gq, NA_HEADS)[None, :], jnp.tile(gk, NA_HEADS)[None, :], same_head]
    names = ("q", "k", "v", "pool", "mq", "mk", "mv", "mo", "gc", "gr")
    in_specs = [tok(d), mod, mod] + [full(a) for a in consts]
    out_specs = [tok(NA_DIM)] * 3 + [tok(POOL_DIM)] + [tok(ML_DIM)] * 4 + [tok(ML_N_GATES)] \
        + [pl.BlockSpec((1, ML_N_GATES, tn), lambda b, t: (b, 0, t))]
    out_shape = [shp(NA_DIM, bf16)] * 3 + [shp(POOL_DIM, f32)] + [shp(ML_DIM, f32)] * 4 + [shp(ML_N_GATES, f32)] \
        + [jax.ShapeDtypeStruct((B, ML_N_GATES, T), f32)]
    args = (x3, sc, sh, *consts)
    if pending is not None:
        names = ("x",) + names
        in_specs = [tok(d), mod, pl.BlockSpec((d, tn), lambda b, t: (0, b * nt + t))] + in_specs[1:]
        out_specs, out_shape = [tok(d)] + out_specs, [shp(d, f32)] + out_shape
        args = (x3, *pending, sc, sh, *consts)
    outs = pl.pallas_call(
        functools.partial(_in_proj_kernel, pending=pending is not None),
        grid=(B, nt),
        in_specs=in_specs,
        out_specs=out_specs,
        out_shape=out_shape,
        compiler_params=pltpu.CompilerParams(
            dimension_semantics=("parallel", "parallel"), vmem_limit_bytes=VMEM_LIMIT_BYTES),
        name="in_proj",
    )(*args)
    return dict(zip(names, outs))


def kernel(x, c, ctx, c_ctx, w_ada, b_ada, norm1_g, w_in, ml_gate_b, na_q_g, na_k_g, na_rpb, pool_w,
           pool_scale, ml_conv, ml_norm_g, w_out, norm2_g, peer_wq, peer_keys, peer_u, peer_v):
    depth = w_in.shape[0]
    B, S, D = x.shape
    Lc = ctx.shape[1]
    hc1 = ctx.reshape(1, B * Lc, D)
    u_bf = _cast_bf16(peer_u.reshape(depth * PEER_EXPERTS, D))
    v_bf = _cast_bf16(peer_v.reshape(depth * PEER_EXPERTS, D))
    pend_x = pend_c = None
    for l in range(depth):
        need_ctx = l < depth - 1
        wqt_bf = peer_wq[l].astype(jnp.bfloat16).T
        keys_bf = peer_keys[l].astype(jnp.bfloat16).reshape(2 * PEER_HEADS, PEER_KEYS, PEER_DKEY // 2)
        mod = _ada_mod(jnp.concatenate([c, c_ctx[None, :]], axis=0), w_ada, b_ada, l).reshape(B + 1, 6, 1, D)
        sh1, sc1, g1, sh2, sc2, g2 = (mod[:B, i] for i in range(6))
        csh1, csc1, cg1, csh2, csc2, cg2 = (mod[B:, i] for i in range(6))
        a = _in_proj(x, sc1, sh1, norm1_g[l], w_in[l], na_q_g[l], na_k_g[l], pend_x)
        ac = _in_proj(hc1, csc1, csh1, norm1_g[l], w_in[l], na_q_g[l], na_k_g[l], pend_c)
        x, hc1 = a.get("x", x), ac.pop("x", hc1)
        ac = {k: v.reshape(B, Lc, v.shape[-1]) for k, v in ac.items() if k != "gr"} | {
            "gr": jnp.transpose(ac["gr"].reshape(ML_N_GATES, B, Lc), (1, 0, 2))}
        na_lat = _na_latent(a["q"], a["k"], a["v"], ac["k"], ac["v"], na_rpb[l])
        pool_lat = _pool_mix(a["pool"], pool_w[l], pool_scale[l])
        ml_lat, ml_ctx = _mlstm_mix(a["mq"], a["mk"], a["mv"], a["mo"], a["gc"], a["gr"],
                                    ac["mq"], ac["mk"], ac["mv"], ac["mo"], ac["gc"], ac["gr"],
                                    ml_conv[l], ml_gate_b[l], ml_norm_g[l])
        wo_bf = w_out[l].astype(jnp.bfloat16)
        x, yt = _mix_and_peer(x, g1, na_lat, pool_lat, ml_lat, wo_bf, sc2, sh2, norm2_g[l], wqt_bf, keys_bf,
                              u_bf, v_bf, l)
        pend_x = (g2, yt)
        if need_ctx:
            na_ctx = _na_context(ac["q"], ac["k"], ac["v"])
            pool_ctx = _pool_mix(ac["pool"], pool_w[l], pool_scale[l])
            flat = lambda t: t.reshape(1, B * Lc, t.shape[-1])
            hc1, yt_c = _mix_and_peer(hc1, cg1, flat(na_ctx), flat(pool_ctx), flat(ml_ctx), wo_bf, csc2, csh2,
                                      norm2_g[l], wqt_bf, keys_bf, u_bf, v_bf, l)
            pend_c = (cg2, yt_c)
    return _residual_t(x, *pend_x)
```

```python
import functools

import numpy as np
import jax
import jax.numpy as jnp
from jax import lax
from jax.experimental import pallas as pl
from jax.experimental.pallas import tpu as pltpu

D_MODEL = 1024
GRID_W = 64
HEAD_DIM = 64
NA_DIM = D_MODEL // 2
NA_HEADS = NA_DIM // HEAD_DIM
NA_WIN_ROWS = 8
NA_WIN_COLS = 16
POOL_DIM = D_MODEL // 4
POOL_WINDOWS = (2, 4, 8, 16)
POOL_GROUPS = len(POOL_WINDOWS)
POOL_GROUP_DIM = POOL_DIM // POOL_GROUPS
ML_DIM = D_MODEL // 4
ML_HEADS = ML_DIM // HEAD_DIM
ML_CONV_W = 5
ML_N_GATES = 4 * ML_HEADS
PEER_KEYS = 128
PEER_EXPERTS = PEER_KEYS * PEER_KEYS
PEER_HEADS = 8
PEER_TOPK = 16
PEER_DKEY = 256
ROPE_BASE = 10000.0
EPS = 1e-6

LANES = 128
SUBLANES = 8
V7X_VMEM_BYTES = 64 * 1024 * 1024

PEER_TOKEN_TILE = 1024
PEER_EXPERT_TILE = 1024
ROUTE_TOKEN_TILE = 512
ROUTE_LANE_CHUNK = 2 * LANES
IN_TOKEN_TILE = 512
NA_ROWS_PER_STEP = 4
NA_UNION_ROWS = -(-(NA_WIN_ROWS + NA_ROWS_PER_STEP - 1) // 2) * 2
ADA_COL_TILE = 1536
CAST_ROW_TILE = 1024
POOL_BLOCK = 128
POOL_HALO = SUBLANES
ML_L = 256
VMEM_LIMIT_BYTES = V7X_VMEM_BYTES - 8 * 1024 * 1024

_GELU_C0 = float(np.sqrt(2.0 / np.pi))
_GELU_C1 = 0.044715 * _GELU_C0


def _gelu_tanh(x):
    inner = x * (_GELU_C0 + _GELU_C1 * (x * x))
    hx = 0.5 * x
    return hx + hx * jnp.tanh(inner)


def _peer_dense_kernel(x_ref, u_ref, v_ref, cnt_ref, e1_ref, rank_ref, e2_ref, o_ref):
    j = pl.program_id(1)

    @pl.when(j == 0)
    def _():
        o_ref[...] = jnp.zeros_like(o_ref)

    te = u_ref.shape[0]
    act = lax.dot_general(u_ref[...], x_ref[...], (((1,), (1,)), ((), ())),
                          preferred_element_type=jnp.float32)
    tn = x_ref.shape[0]
    packed = 2 * SUBLANES

    def row_tile(ref, h, k):
        return jnp.broadcast_to(ref[h, k:k + 1, :], (packed, tn)).astype(jnp.bfloat16)[None]

    gates = []
    for k in range(te // PEER_KEYS):
        g = None
        for h in range(PEER_HEADS):
            rank = rank_ref[h].reshape(PEER_KEYS // packed, packed, tn)
            e2 = e2_ref[h].reshape(PEER_KEYS // packed, packed, tn)
            term = jnp.where(rank < row_tile(cnt_ref, h, k), e2, jnp.zeros((), e2_ref.dtype)) * row_tile(e1_ref, h, k)
            g = term if g is None else g + term
        gates.append(g.reshape(PEER_KEYS, tn))
    gate = jnp.concatenate(gates, axis=0)
    p = _gelu_tanh(act.astype(jnp.bfloat16)) * gate
    o_ref[...] += lax.dot_general(v_ref[...], p, (((0,), (0,)), ((), ())), preferred_element_type=jnp.float32)


def _cast_kernel(x_ref, o_ref):
    o_ref[...] = x_ref[...].astype(o_ref.dtype)


def _cast_bf16(w):
    rows, cols = w.shape
    tr = CAST_ROW_TILE
    assert rows % tr == 0
    blk = pl.BlockSpec((tr, cols), lambda i: (i, 0))
    return pl.pallas_call(
        _cast_kernel, grid=(rows // tr,), in_specs=[blk], out_specs=blk,
        out_shape=jax.ShapeDtypeStruct(w.shape, jnp.bfloat16),
        compiler_params=pltpu.CompilerParams(dimension_semantics=("parallel",)),
        name="cast_bf16",
    )(w)


def _peer_dense(tok, u_bf, v_bf, layer, cnt, e1, rank, e2):
    n, d = tok.shape
    tn, te = PEER_TOKEN_TILE, PEER_EXPERT_TILE
    assert n % tn == 0 and PEER_EXPERTS % te == 0 and te % PEER_KEYS == 0
    first = layer * (PEER_EXPERTS // te)
    tab_a = pl.BlockSpec((PEER_HEADS, te // PEER_KEYS, tn), lambda i, j: (0, j, i))
    tab_b = pl.BlockSpec((PEER_HEADS, PEER_KEYS, tn), lambda i, j: (0, 0, i))
    return pl.pallas_call(
        _peer_dense_kernel,
        grid=(n // tn, PEER_EXPERTS // te),
        in_specs=[
            pl.BlockSpec((tn, d), lambda i, j: (i, 0)),
            pl.BlockSpec((te, d), lambda i, j: (first + j, 0)),
            pl.BlockSpec((te, d), lambda i, j: (first + j, 0)),
            tab_a, tab_a, tab_b, tab_b,
        ],
        out_specs=pl.BlockSpec((d, tn), lambda i, j: (0, i)),
        out_shape=jax.ShapeDtypeStruct((d, n), jnp.float32),
        compiler_params=pltpu.CompilerParams(
            dimension_semantics=("parallel", "arbitrary"), vmem_limit_bytes=VMEM_LIMIT_BYTES),
        name="peer_dense",
    )(tok, u_bf, v_bf, cnt, e1, rank, e2)


def _top16(works, want_ranks):
    lanes = works[0].shape[1]
    rows = lax.broadcasted_iota(jnp.int32, (PEER_TOPK, lanes), 0)
    vals = [jnp.zeros((PEER_TOPK, lanes), jnp.float32) for _ in works]
    ranks = [jnp.full(w.shape, float(PEER_TOPK), jnp.float32) if r else None for w, r in zip(works, want_ranks)]
    tops = [[] for _ in works]
    works = list(works)
    for i in range(PEER_TOPK):
        ms = [jnp.max(w, axis=0, keepdims=True) for w in works]
        hits = [w == m for w, m in zip(works, ms)]
        ranks = [None if r is None else jnp.where(hit, float(i), r) for r, hit in zip(ranks, hits)]
        works = [jnp.where(hit, -jnp.inf, w) for w, hit in zip(works, hits)]
        vals = [jnp.where(rows == i, m, v) for v, m in zip(vals, ms)]
        for t, m in zip(tops, ms):
            t.append(m)
    return tops, vals, ranks


def _peer_route_kernel(x_ref, g1_ref, na_ref, pool_ref, ml_ref, wo_ref, sc_ref, sh_ref, g_ref, wqt_ref, keys_ref,
                       xo_ref, tok_ref, cnt_ref, e1_ref, rank_ref, e2_ref, s_scr):
    bf = jnp.bfloat16
    mix = jnp.dot(na_ref[0].astype(bf), wo_ref[0:NA_DIM, :], preferred_element_type=jnp.float32)
    mix += jnp.dot(pool_ref[0].astype(bf), wo_ref[NA_DIM:NA_DIM + POOL_DIM, :], preferred_element_type=jnp.float32)
    mix += jnp.dot(ml_ref[0].astype(bf), wo_ref[NA_DIM + POOL_DIM:, :], preferred_element_type=jnp.float32)
    x = x_ref[0] + g1_ref[0] * mix
    xo_ref[0] = x
    y = x * lax.rsqrt(jnp.mean(x * x, axis=-1, keepdims=True) + EPS) * g_ref[...]
    tok = (y * (1.0 + sc_ref[0]) + sh_ref[0]).astype(jnp.bfloat16)
    tok_ref[...] = tok
    qt = lax.dot_general(wqt_ref[...], tok, (((1,), (1,)), ((), ())),
                         preferred_element_type=jnp.float32).astype(jnp.bfloat16)
    dk = PEER_DKEY // 2
    for hp in range(2 * PEER_HEADS):
        s_scr[hp] = jnp.dot(keys_ref[hp], qt[hp * dk:(hp + 1) * dk], preferred_element_type=jnp.float32)

    cw = ROUTE_LANE_CHUNK
    lane_chunks = x.shape[0] // cw
    half = PEER_TOPK // 2
    row_half = lax.broadcasted_iota(jnp.int32, (half, cw), 0)

    def body(it, carry):
        h = it // lane_chunks
        lanes = pl.ds(pl.multiple_of((it % lane_chunks) * cw, cw), cw)
        s1 = s_scr[2 * h, :, lanes]
        s2 = s_scr[2 * h + 1, :, lanes]
        (t1, t2), (v1, v2), (_, rank) = _top16([s1, s2], [False, True])
        groups = [t1[0] + v2, t1[1] + v2[:half]]
        for i in range(2, half):
            groups.append(jnp.where(row_half < PEER_TOPK // (i + 1), t1[i] + v2[:half], -jnp.inf))
        groups.append(v1[half:] + t2[0])
        (tp,), (vp,), _ = _top16([jnp.concatenate(groups, axis=0)], [False])
        tau = tp[PEER_TOPK - 1]
        z = jnp.sum(jnp.exp(vp - tp[0]), axis=0, keepdims=True)
        kept = jnp.zeros(v1.shape, jnp.float32)
        for jj in range(PEER_TOPK):
            kept = jnp.where(v1 + t2[jj] >= tau, float(jj + 1), kept)
        cnt = jnp.zeros(s1.shape, jnp.float32)
        for i in range(PEER_TOPK):
            cnt = jnp.where(s1 == t1[i], kept[i:i + 1], cnt)
        cnt_ref[h, :, lanes] = cnt
        e1_ref[h, :, lanes] = jnp.exp(s1 - t1[0])
        rank_ref[h, :, lanes] = rank.astype(jnp.bfloat16)
        e2_ref[h, :, lanes] = (jnp.exp(s2 - t2[0]) * (1.0 / z)).astype(jnp.bfloat16)
        return carry

    lax.fori_loop(0, PEER_HEADS * lane_chunks, body, 0)


def _peer_route(x3, gate1, na, pool, ml, wo_bf, sc, sh, g, wqt_bf, keys_bf):
    B, T, d = x3.shape
    tn = min(ROUTE_TOKEN_TILE, T)
    assert T % tn == 0 and tn % ROUTE_LANE_CHUNK == 0
    nt = T // tn
    n = B * T
    tab = pl.BlockSpec((PEER_HEADS, PEER_KEYS, tn), lambda b, t: (0, 0, b * nt + t))
    tab_shape = lambda dt: jax.ShapeDtypeStruct((PEER_HEADS, PEER_KEYS, n), dt)
    mod = pl.BlockSpec((1, 1, d), lambda b, t: (b, 0, 0))
    tok = lambda c: pl.BlockSpec((1, tn, c), lambda b, t: (b, t, 0))
    const = lambda a: pl.BlockSpec(a.shape, lambda b, t: (0,) * a.ndim, pipeline_mode=pl.Buffered(1))
    return pl.pallas_call(
        _peer_route_kernel,
        grid=(B, nt),
        in_specs=[tok(d), mod, tok(NA_DIM), tok(POOL_DIM), tok(ML_DIM), const(wo_bf), mod, mod,
                  pl.BlockSpec((1, d), lambda b, t: (0, 0)), const(wqt_bf), const(keys_bf)],
        out_specs=[tok(d), pl.BlockSpec((tn, d), lambda b, t: (b * nt + t, 0)), tab, tab, tab, tab],
        out_shape=[jax.ShapeDtypeStruct(x3.shape, x3.dtype), jax.ShapeDtypeStruct((n, d), jnp.bfloat16),
                   tab_shape(jnp.float32), tab_shape(jnp.float32), tab_shape(jnp.bfloat16), tab_shape(jnp.bfloat16)],
        scratch_shapes=[pltpu.VMEM((2 * PEER_HEADS, PEER_KEYS, tn), jnp.float32)],
        compiler_params=pltpu.CompilerParams(
            dimension_semantics=("parallel", "parallel"), vmem_limit_bytes=VMEM_LIMIT_BYTES),
        name="peer_route",
    )(x3, gate1, na, pool, ml, wo_bf, sc, sh, g, wqt_bf, keys_bf)


def _residual_t_kernel(x_ref, g_ref, yt_ref, o_ref):
    o_ref[0] = x_ref[0] + g_ref[0] * yt_ref[...].T


def _residual_t(x3, gate, yt):
    B, T, d = x3.shape
    tn = min(ROUTE_TOKEN_TILE, T)
    nt = T // tn
    return pl.pallas_call(
        _residual_t_kernel,
        grid=(B, nt),
        in_specs=[pl.BlockSpec((1, tn, d), lambda b, t: (b, t, 0)),
                  pl.BlockSpec((1, 1, d), lambda b, t: (b, 0, 0)),
                  pl.BlockSpec((d, tn), lambda b, t: (0, b * nt + t))],
        out_specs=pl.BlockSpec((1, tn, d), lambda b, t: (b, t, 0)),
        out_shape=jax.ShapeDtypeStruct(x3.shape, x3.dtype),
        compiler_params=pltpu.CompilerParams(dimension_semantics=("parallel", "parallel")),
        name="residual_t",
    )(x3, gate, yt)


def _mix_and_peer(x3, gate1, na, pool, ml, wo_bf, sc, sh, norm_g, wqt_bf, keys_bf, u_bf, v_bf, layer):
    x1, tok, cnt, e1, rank, e2 = _peer_route(x3, gate1, na, pool, ml, wo_bf, sc, sh, norm_g[None, :], wqt_bf, keys_bf)
    return x1, _peer_dense(tok, u_bf, v_bf, layer, cnt, e1, rank, e2)


def _attend_pair(q2, segments, lane):
    zero = jnp.zeros((), q2.dtype)
    outs = []
    for half in range(2):
        own = (lane < HEAD_DIM) if half == 0 else (lane >= HEAD_DIM)
        qh = jnp.where(own, q2, zero)
        scores = []
        for k2, _, bias in segments:
            s = lax.dot_general(qh, k2, (((1,), (1,)), ((), ())), preferred_element_type=jnp.float32)
            scores.append(s if bias is None else s + bias[half])
        m = functools.reduce(jnp.maximum, [jnp.max(s, axis=-1, keepdims=True) for s in scores])
        ps = [jnp.exp(s - m) for s in scores]
        denom = functools.reduce(jnp.add, [jnp.sum(p, axis=-1, keepdims=True) for p in ps])
        acc = functools.reduce(jnp.add, [jnp.dot(p.astype(v2.dtype), v2, preferred_element_type=jnp.float32)
                                         for p, (_, v2, _) in zip(ps, segments)])
        outs.append(acc / denom)
    return jnp.where(lane < HEAD_DIM, outs[0], outs[1])


def _na_latent_kernel(q_ref, k_ref, v_ref, kc_ref, vc_ref, bank_ref, o_ref):
    G, KR = NA_ROWS_PER_STEP, NA_WIN_ROWS
    n_rows = k_ref.shape[1] // GRID_W
    first = pl.program_id(1) * G
    u0 = jnp.minimum(jnp.clip(first - KR // 2, 0, n_rows - KR), n_rows - NA_UNION_ROWS)
    keys = pl.ds(pl.multiple_of(u0 * GRID_W, GRID_W), NA_UNION_ROWS * GRID_W)
    key_row = u0 + lax.broadcasted_iota(jnp.int32, (1, NA_UNION_ROWS * GRID_W), 1) // GRID_W
    lane = lax.broadcasted_iota(jnp.int32, (q_ref.shape[1], LANES), 1)

    def head_bias(h):
        rows = []
        for i in range(G):
            r = first + i
            r0 = jnp.clip(r - KR // 2, 0, n_rows - KR)
            pairs = [bank_ref[h, jnp.clip(u0 + 2 * t + 1 - r + KR - 1, 0, 2 * KR - 1)]
                     for t in range(NA_UNION_ROWS // 2)]
            in_strip = (key_row >= r0) & (key_row < r0 + KR)
            rows.append(jnp.where(in_strip, jnp.concatenate(pairs, axis=1), -jnp.inf))
        return jnp.concatenate(rows, axis=0)

    for p in range(NA_DIM // LANES):
        lanes = slice(p * LANES, (p + 1) * LANES)
        segs = [(k_ref[0, keys, lanes], v_ref[0, keys, lanes], (head_bias(2 * p), head_bias(2 * p + 1))),
                (kc_ref[0, :, lanes], vc_ref[0, :, lanes], None)]
        o_ref[0, :, lanes] = _attend_pair(q_ref[0, :, lanes], segs, lane)


def _na_bias_bank(rpb):
    cq = np.arange(GRID_W)
    c0 = np.clip(cq - NA_WIN_COLS // 2, 0, GRID_W - NA_WIN_COLS)
    in_win = (cq[None, :] >= c0[:, None]) & (cq[None, :] < c0[:, None] + NA_WIN_COLS)
    bc = np.clip(cq[None, :] - cq[:, None] + (NA_WIN_COLS - 1), 0, 2 * NA_WIN_COLS - 2)
    onehot = jnp.asarray((bc[None] == np.arange(2 * NA_WIN_COLS - 1)[:, None, None]).astype(np.float32))
    toe = jnp.einsum('hrj,jqk->hrqk', rpb.astype(jnp.float32), onehot, precision=lax.Precision.HIGHEST)
    toe = jnp.where(in_win[None, None], toe, -jnp.inf)
    none = jnp.full((NA_HEADS, 1, GRID_W, GRID_W), -jnp.inf, jnp.float32)
    return jnp.concatenate([jnp.concatenate([none, toe], axis=1), jnp.concatenate([toe, none], axis=1)], axis=3)


def _na_latent(q, k, v, k_ctx, v_ctx, rpb):
    B, S, d = q.shape
    Lc = k_ctx.shape[1]
    n_rows = S // GRID_W
    assert n_rows >= NA_UNION_ROWS and n_rows % NA_ROWS_PER_STEP == 0
    bank = _na_bias_bank(rpb)
    tq = NA_ROWS_PER_STEP * GRID_W
    whole = lambda L: pl.BlockSpec((1, L, d), lambda b, g: (b, 0, 0))
    return pl.pallas_call(
        _na_latent_kernel,
        grid=(B, n_rows // NA_ROWS_PER_STEP),
        in_specs=[pl.BlockSpec((1, tq, d), lambda b, g: (b, g, 0)), whole(S), whole(S), whole(Lc), whole(Lc),
                  pl.BlockSpec(bank.shape, lambda b, g: (0, 0, 0, 0))],
        out_specs=pl.BlockSpec((1, tq, d), lambda b, g: (b, g, 0)),
        out_shape=jax.ShapeDtypeStruct((B, S, d), jnp.float32),
        compiler_params=pltpu.CompilerParams(
            dimension_semantics=("parallel", "arbitrary"), vmem_limit_bytes=VMEM_LIMIT_BYTES),
        name="na_latent",
    )(q, k, v, k_ctx, v_ctx, bank)


def _na_context_kernel(q_ref, k_ref, v_ref, o_ref):
    lane = lax.broadcasted_iota(jnp.int32, (q_ref.shape[1], LANES), 1)
    for p in range(NA_DIM // LANES):
        lanes = slice(p * LANES, (p + 1) * LANES)
        o_ref[0, :, lanes] = _attend_pair(q_ref[0, :, lanes], [(k_ref[0, :, lanes], v_ref[0, :, lanes], None)], lane)


def _na_context(q, k, v):
    B, Lc, d = q.shape
    blk = pl.BlockSpec((1, Lc, d), lambda b: (b, 0, 0))
    return pl.pallas_call(
        _na_context_kernel,
        grid=(B,),
        in_specs=[blk, blk, blk],
        out_specs=blk,
        out_shape=jax.ShapeDtypeStruct((B, Lc, d), jnp.float32),
        compiler_params=pltpu.CompilerParams(dimension_semantics=("parallel",)),
        name="na_context",
    )(q, k, v)


def _pool_matrices(T):
    tb, halo = POOL_BLOCK, POOL_HALO
    band = np.zeros((len(POOL_WINDOWS), 3, tb, tb + 2 * halo), np.float32)
    inv = np.zeros((3, tb, POOL_DIM), np.float32)
    for wi, w in enumerate(POOL_WINDOWS):
        for kind, start in enumerate((0, tb, T - tb)):
            for i in range(tb):
                t = start + i
                lo = min(max(t - w // 2, 0), T - 1)
                hi = min(max(t + (w - w // 2 - 1), 0), T - 1)
                band[wi, kind, i, lo - start + halo:hi - start + halo + 1] = 1.0
                inv[kind, i, wi * POOL_GROUP_DIM:(wi + 1) * POOL_GROUP_DIM] = 1.0 / (hi - lo + 1)
    return jnp.asarray(band, jnp.bfloat16), jnp.asarray(inv)


def _pool_kernel(u_ref, a_ref, inv_ref, w_ref, s_ref, o_ref):
    T = u_ref.shape[1]
    tb, halo = POOL_BLOCK, POOL_HALO
    nb = T // tb
    group = lax.broadcasted_iota(jnp.int32, (tb, POOL_DIM), 1) // POOL_GROUP_DIM
    zeros = jnp.zeros((halo, POOL_DIM), jnp.float32)
    for blk in range(nb):
        kind = 0 if blk == 0 else (2 if blk == nb - 1 else 1)
        x = u_ref[0, blk * tb:(blk + 1) * tb, :]
        before = zeros if blk == 0 else u_ref[0, blk * tb - halo:blk * tb, :]
        after = zeros if blk == nb - 1 else u_ref[0, (blk + 1) * tb:(blk + 1) * tb + halo, :]
        xcat = jnp.concatenate([before, x, after], axis=0)
        hi = xcat.astype(jnp.bfloat16)
        lo = (xcat - hi.astype(jnp.float32)).astype(jnp.bfloat16)
        total = None
        for wi in range(len(POOL_WINDOWS)):
            tw = (jnp.dot(a_ref[wi, kind], hi, preferred_element_type=jnp.float32)
                  + jnp.dot(a_ref[wi, kind], lo, preferred_element_type=jnp.float32))
            total = tw if total is None else jnp.where(group == wi, tw, total)
        pooled = total * inv_ref[kind] - x
        mixed = jnp.dot(pooled.astype(jnp.bfloat16), w_ref[...], preferred_element_type=jnp.float32)
        o_ref[0, blk * tb:(blk + 1) * tb, :] = mixed * s_ref[...]


def _pool_mix(u, pool_w, pool_scale):
    B, T, d = u.shape
    assert T % POOL_BLOCK == 0 and T >= 2 * POOL_BLOCK
    w_bd = jax.scipy.linalg.block_diag(*[pool_w[g] for g in range(POOL_GROUPS)]).astype(jnp.bfloat16)
    band, inv = _pool_matrices(T)
    blk = pl.BlockSpec((1, T, d), lambda b: (b, 0, 0))
    return pl.pallas_call(
        _pool_kernel,
        grid=(B,),
        in_specs=[blk, pl.BlockSpec(band.shape, lambda b: (0, 0, 0, 0)), pl.BlockSpec(inv.shape, lambda b: (0, 0, 0)),
                  pl.BlockSpec((d, d), lambda b: (0, 0)), pl.BlockSpec((1, d), lambda b: (0, 0))],
        out_specs=blk,
        out_shape=jax.ShapeDtypeStruct((B, T, d), jnp.float32),
        compiler_params=pltpu.CompilerParams(dimension_semantics=("parallel",)),
        name="pool_mix",
    )(u, band, inv, w_bd, pool_scale[None, :])


def _log_sigmoid(x):
    return jnp.minimum(x, 0.0) - jnp.log1p(jnp.exp(-jnp.abs(x)))


def _ml_prep(q, k, conv_ref, cos_ref, sin_ref):
    T = q.shape[0]
    row = lax.broadcasted_iota(jnp.int32, (T, 1), 0)
    lane = lax.broadcasted_iota(jnp.int32, (1, ML_DIM), 1) % (HEAD_DIM // 2)
    outs = []
    for idx, x in enumerate((q, k)):
        acc = None
        for j in range(ML_CONV_W):
            d = j - ML_CONV_W // 2
            w = conv_ref[j:j + 1, idx * ML_DIM:(idx + 1) * ML_DIM]
            if d == 0:
                term = x * w
            else:
                shifted = pltpu.roll(x, (-d) % T, 0)
                term = jnp.where((row + d >= 0) & (row + d < T), shifted, 0.0) * w
            acc = term if acc is None else acc + term
        y = acc * jax.nn.sigmoid(acc)
        if cos_ref is not None:
            quarter = HEAD_DIM // 4
            partner = jnp.where(lane < quarter, pltpu.roll(y, ML_DIM - quarter, 1), pltpu.roll(y, quarter, 1))
            y = y * cos_ref[...] + partner * sin_ref[...]
        outs.append(y)
    return outs


def _ml_chain(qs, kts, vs, rows, gi_c, gi_r, bc, br, ci, ii, head, backward, cn_ref, m_ref, slot, lane, tri):
    L = ML_L
    pair = slice((head // 2) * LANES, (head // 2 + 1) * LANES)
    own = (lane < HEAD_DIM) if head % 2 == 0 else (lane >= HEAD_DIM)
    one_lane = HEAD_DIM if head % 2 == 0 else 0
    q2 = jnp.where(own, qs[rows, pair], jnp.zeros((), jnp.bfloat16))
    v1 = jnp.where(own, vs[rows, pair].astype(jnp.float32), jnp.where(lane == one_lane, 1.0, 0.0))
    own_rows = lax.broadcasted_iota(jnp.int32, (LANES, L), 0)
    own_rows = (own_rows < HEAD_DIM) if head % 2 == 0 else (own_rows >= HEAD_DIM)
    kt2 = jnp.where(own_rows, kts[pair, rows], jnp.zeros((), jnp.bfloat16))
    m_prev = m_ref[slot]
    cn = cn_ref[slot]
    bcol = bc[:, ci:ci + 1]
    brow = br[ci:ci + 1, :]
    log_d = jnp.where(tri, bcol - brow + gi_r[ii:ii + 1, :], -jnp.inf)
    inter = bcol + m_prev
    m_t = jnp.maximum(inter, jnp.max(log_d, axis=-1, keepdims=True))
    s = jnp.dot(q2, kt2, preferred_element_type=jnp.float32)
    w_intra = (s * jnp.exp(log_d - m_t)).astype(jnp.bfloat16)
    nd = (jnp.dot(w_intra, v1.astype(jnp.bfloat16), preferred_element_type=jnp.float32)
          + jnp.exp(inter - m_t) * jnp.dot(q2, cn.astype(jnp.bfloat16), preferred_element_type=jnp.float32))
    den = nd[:, one_lane:one_lane + 1]
    h = nd / jnp.maximum(jnp.abs(den), jnp.exp(-m_t))
    b_last = bc[0:1, ci:ci + 1] if backward else bc[L - 1:L, ci:ci + 1]
    log_w = b_last - bcol + gi_c[:, ii:ii + 1]
    m_new = jnp.maximum(b_last + m_prev, jnp.max(log_w, axis=0, keepdims=True))
    wv = (jnp.exp(log_w - m_new) * v1).astype(jnp.bfloat16)
    cn_ref[slot] = jnp.exp(b_last + m_prev - m_new) * cn + jnp.dot(kt2, wv, preferred_element_type=jnp.float32)
    m_ref[slot] = m_new
    return h


def _ml_scan(T, qs, kts, vs, gc_ref, gr_ref, gbc, gbr, hf, hb, cn_ref, m_ref, tril, triu):
    L = ML_L
    nc = T // L
    lane = lax.broadcasted_iota(jnp.int32, (L, LANES), 1)
    t_idx = lax.broadcasted_iota(jnp.int32, (L, L), 0)
    s_idx = lax.broadcasted_iota(jnp.int32, (L, L), 1)
    hi = lax.Precision.HIGHEST

    def body(c, carry):
        for backward in (False, True):
            cc = (nc - 1 - c) if backward else c
            rows = pl.ds(pl.multiple_of(cc * L, L), L)
            gc = gc_ref[0, rows, :] + gbc
            gr = gr_ref[0, :, rows] + gbr
            bc = jnp.dot(triu if backward else tril, _log_sigmoid(gc), preferred_element_type=jnp.float32, precision=hi)
            br = jnp.dot(_log_sigmoid(gr), tril if backward else triu, preferred_element_type=jnp.float32, precision=hi)
            tri = (s_idx >= t_idx) if backward else (s_idx <= t_idx)
            out = hb if backward else hf
            base = 2 * ML_HEADS if backward else 0
            for pair in range(ML_HEADS // 2):
                hs = []
                for head in (2 * pair, 2 * pair + 1):
                    hs.append(_ml_chain(qs, kts, vs, rows, gc, gr, bc, br, base + ML_HEADS + head, base + head, head,
                                        backward, cn_ref, m_ref, (ML_HEADS if backward else 0) + head, lane, tri))
                out[rows, pair * LANES:(pair + 1) * LANES] = jnp.where(lane < HEAD_DIM, hs[0], hs[1])
        return carry

    lax.fori_loop(0, nc, body, 0)


def _mlstm_kernel(q_ref, k_ref, v_ref, o_ref, gc_ref, gr_ref, cq_ref, ck_ref, cv_ref, co_ref, cgc_ref, cgr_ref,
                  cos_ref, sin_ref, conv_ref, gbc_ref, gbr_ref, ng_ref, e_ref, tril_ref, triu_ref,
                  y_ref, cy_ref, qs, kts, vs, hf, hb, cqs, ckts, cvs, chf, chb, cn_ref, m_ref):
    cn_ref[...] = jnp.zeros_like(cn_ref)
    m_ref[...] = jnp.zeros_like(m_ref)
    tril, triu = tril_ref[...], triu_ref[...]

    def stage(q_r, k_r, v_r, rope, qd, ktd, vd):
        qp, kp = _ml_prep(q_r[0], k_r[0], conv_ref, cos_ref if rope else None, sin_ref if rope else None)
        qd[...] = qp.astype(jnp.bfloat16)
        ktd[...] = (kp * HEAD_DIM ** -0.5).T.astype(jnp.bfloat16)
        vd[...] = v_r[0].astype(jnp.bfloat16)

    def finish(h_f, h_b, o_r, y_r):
        h = h_f[...] + h_b[...]
        ss = jnp.dot(h * h, e_ref[...], preferred_element_type=jnp.float32, precision=lax.Precision.HIGHEST)
        y_r[0] = h * lax.rsqrt(ss * (1.0 / HEAD_DIM) + EPS) * ng_ref[...] * jax.nn.sigmoid(o_r[0])

    gbc, gbr = gbc_ref[...], gbr_ref[...]
    stage(cq_ref, ck_ref, cv_ref, False, cqs, ckts, cvs)
    _ml_scan(cq_ref.shape[1], cqs, ckts, cvs, cgc_ref, cgr_ref, gbc, gbr, chf, chb, cn_ref, m_ref, tril, triu)
    finish(chf, chb, co_ref, cy_ref)
    stage(q_ref, k_ref, v_ref, True, qs, kts, vs)
    _ml_scan(q_ref.shape[1], qs, kts, vs, gc_ref, gr_ref, gbc, gbr, hf, hb, cn_ref, m_ref, tril, triu)
    finish(hf, hb, o_ref, y_ref)


def _rope_tables(T):
    t = jnp.arange(T)
    row = (t // GRID_W).astype(jnp.float32)
    col = (t % GRID_W).astype(jnp.float32)
    nf = HEAD_DIM // 4
    inv = ROPE_BASE ** (-jnp.arange(nf, dtype=jnp.float32) / nf)
    cr, sr, cc, sc = jnp.cos(row[:, None] * inv), jnp.sin(row[:, None] * inv), jnp.cos(col[:, None] * inv), \
        jnp.sin(col[:, None] * inv)
    cos = jnp.concatenate([cr, cr, cc, cc], axis=-1)
    sin = jnp.concatenate([-sr, sr, -sc, sc], axis=-1)
    return jnp.tile(cos, (1, ML_HEADS)), jnp.tile(sin, (1, ML_HEADS))


def _mlstm_mix(lq, lk, lv, lo, lg, lgr, cq, ck, cv, co, cg, cgr, conv_w, gate_b, norm_g):
    B, T, d = lq.shape
    Lc = cq.shape[1]
    L = ML_L
    assert T % L == 0 and Lc % L == 0
    cos, sin = _rope_tables(T)
    lane_head = np.arange(d) // HEAD_DIM
    same_head = jnp.asarray((lane_head[:, None] == lane_head[None, :]).astype(np.float32))
    tril = jnp.asarray(np.tril(np.ones((L, L), np.float32)))
    seq = lambda n: pl.BlockSpec((1, n, d), lambda b: (b, 0, 0))
    full = lambda a: pl.BlockSpec(a.shape, lambda b: (0,) * a.ndim)
    consts = [cos, sin, conv_w, gate_b[None, :], gate_b[:, None], norm_g[None, :], same_head, tril, tril.T]
    f32, bf16 = jnp.float32, jnp.bfloat16
    return pl.pallas_call(
        _mlstm_kernel,
        grid=(B,),
        in_specs=[seq(T)] * 4 + [pl.BlockSpec((1, T, ML_N_GATES), lambda b: (b, 0, 0)),
                                 pl.BlockSpec((1, ML_N_GATES, T), lambda b: (b, 0, 0))]
        + [seq(Lc)] * 4 + [pl.BlockSpec((1, Lc, ML_N_GATES), lambda b: (b, 0, 0)),
                           pl.BlockSpec((1, ML_N_GATES, Lc), lambda b: (b, 0, 0))]
        + [full(a) for a in consts],
        out_specs=[seq(T), seq(Lc)],
        out_shape=[jax.ShapeDtypeStruct((B, T, d), f32), jax.ShapeDtypeStruct((B, Lc, d), f32)],
        scratch_shapes=[pltpu.VMEM((T, d), bf16), pltpu.VMEM((d, T), bf16), pltpu.VMEM((T, d), bf16),
                        pltpu.VMEM((T, d), f32), pltpu.VMEM((T, d), f32),
                        pltpu.VMEM((Lc, d), bf16), pltpu.VMEM((d, Lc), bf16), pltpu.VMEM((Lc, d), bf16),
                        pltpu.VMEM((Lc, d), f32), pltpu.VMEM((Lc, d), f32),
                        pltpu.VMEM((2 * ML_HEADS, LANES, LANES), f32), pltpu.VMEM((2 * ML_HEADS, 1, 1), f32)],
        compiler_params=pltpu.CompilerParams(dimension_semantics=("parallel",), vmem_limit_bytes=VMEM_LIMIT_BYTES),
        name="mlstm",
    )(lq, lk, lv, lo, lg, lgr, cq, ck, cv, co, cg, cgr, *consts)


def _ada_kernel(c_ref, w_ref, b_ref, o_ref):
    c = c_ref[...]
    act = (c * jax.nn.sigmoid(c)).astype(jnp.bfloat16)
    o_ref[...] = jnp.dot(act, w_ref[0].astype(jnp.bfloat16), preferred_element_type=jnp.float32) + b_ref[0]


def _ada_mod(cond, w_all, b_all, layer):
    rows, d = cond.shape
    n = w_all.shape[2]
    tile = ADA_COL_TILE
    assert n % tile == 0
    return pl.pallas_call(
        _ada_kernel,
        grid=(n // tile,),
        in_specs=[pl.BlockSpec((rows, d), lambda j: (0, 0)), pl.BlockSpec((1, d, tile), lambda j: (layer, 0, j)),
                  pl.BlockSpec((1, 1, tile), lambda j: (layer, 0, j))],
        out_specs=pl.BlockSpec((rows, tile), lambda j: (0, j)),
        out_shape=jax.ShapeDtypeStruct((rows, n), jnp.float32),
        compiler_params=pltpu.CompilerParams(dimension_semantics=("parallel",)),
        name="ada_mod",
    )(cond, w_all, b_all[:, None, :])


_IN_SPLITS = (("q", NA_DIM), ("k", NA_DIM), ("v", NA_DIM), ("pool", POOL_DIM), ("mq", ML_DIM), ("mk", ML_DIM),
              ("mv", ML_DIM), ("mo", ML_DIM), ("gates", ML_N_GATES))
IN_COLS = sum(n for _, n in _IN_SPLITS)
IN_COLS_PAD = -(-IN_COLS // LANES) * LANES


def _in_proj_kernel(*refs, pending):
    if pending:
        (x_ref, g2_ref, yt_ref, sc_ref, sh_ref, g_ref, w_ref, wgt_ref, gq_ref, gk_ref, e_ref, xo_ref,
         q_ref, k_ref, v_ref, pool_ref, mq_ref, mk_ref, mv_ref, mo_ref, gc_ref, gr_ref) = refs
        x = x_ref[0] + g2_ref[0] * yt_ref[...].T
        xo_ref[0] = x
    else:
        (x_ref, sc_ref, sh_ref, g_ref, w_ref, wgt_ref, gq_ref, gk_ref, e_ref,
         q_ref, k_ref, v_ref, pool_ref, mq_ref, mk_ref, mv_ref, mo_ref, gc_ref, gr_ref) = refs
        x = x_ref[0]
    y = x * lax.rsqrt(jnp.mean(x * x, axis=-1, keepdims=True) + EPS) * g_ref[...]
    xn = (y * (1.0 + sc_ref[0]) + sh_ref[0]).astype(jnp.bfloat16)
    p = jnp.dot(xn, w_ref[...], preferred_element_type=jnp.float32)
    off = {}
    o = 0
    for name, n in _IN_SPLITS:
        off[name] = slice(o, o + n)
        o += n

    def head_norm(u, gain):
        sq = u * u
        hi = sq.astype(jnp.bfloat16)
        lo = (sq - hi.astype(jnp.float32)).astype(jnp.bfloat16)
        ss = (jnp.dot(hi, e_ref[...], preferred_element_type=jnp.float32)
              + jnp.dot(lo, e_ref[...], preferred_element_type=jnp.float32))
        return u * lax.rsqrt(ss * (1.0 / HEAD_DIM) + EPS) * gain

    q_ref[0] = (head_norm(p[:, off["q"]], gq_ref[...]) * HEAD_DIM ** -0.5).astype(jnp.bfloat16)
    k_ref[0] = head_norm(p[:, off["k"]], gk_ref[...]).astype(jnp.bfloat16)
    v_ref[0] = p[:, off["v"]].astype(jnp.bfloat16)
    pool_ref[0] = p[:, off["pool"]]
    mq_ref[0] = p[:, off["mq"]]
    mk_ref[0] = p[:, off["mk"]]
    mv_ref[0] = p[:, off["mv"]]
    mo_ref[0] = p[:, off["mo"]]
    gc_ref[0] = p[:, off["gates"]]
    gr_ref[0] = lax.dot_general(wgt_ref[...], xn, (((1,), (1,)), ((), ())), preferred_element_type=jnp.float32)


def _in_proj(x3, sc, sh, norm_g, w_in, gq, gk, pending=None):
    B, T, d = x3.shape
    tn = min(IN_TOKEN_TILE, T)
    assert T % tn == 0
    nt = T // tn
    w_bf = jnp.pad(w_in, ((0, 0), (0, IN_COLS_PAD - IN_COLS))).astype(jnp.bfloat16)
    wgt_bf = w_in[:, IN_COLS - ML_N_GATES:].T.astype(jnp.bfloat16)
    lane_head = np.arange(NA_DIM) // HEAD_DIM
    same_head = jnp.asarray((lane_head[:, None] == lane_head[None, :]).astype(np.float32)).astype(jnp.bfloat16)
    mod = pl.BlockSpec((1, 1, d), lambda b, t: (b, 0, 0))
    full = lambda a: pl.BlockSpec(a.shape, lambda b, t: (0,) * a.ndim)
    tok = lambda n: pl.BlockSpec((1, tn, n), lambda b, t: (b, t, 0))
    shp = lambda n, dt: jax.ShapeDtypeStruct((B, T, n), dt)
    f32, bf16 = jnp.float32, jnp.bfloat16
    consts = [norm_g[None, :], w_bf, wgt_bf, jnp.tile(gq, NA_HEADS)[None, :], jnp.tile(gk, NA_HEADS)[None, :], same_head]
    names = ("q", "k", "v", "pool", "mq", "mk", "mv", "mo", "gc", "gr")
    in_specs = [tok(d), mod, mod] + [full(a) for a in consts]
    out_specs = [tok(NA_DIM)] * 3 + [tok(POOL_DIM)] + [tok(ML_DIM)] * 4 + [tok(ML_N_GATES)] \
        + [pl.BlockSpec((1, ML_N_GATES, tn), lambda b, t: (b, 0, t))]
    out_shape = [shp(NA_DIM, bf16)] * 3 + [shp(POOL_DIM, f32)] + [shp(ML_DIM, f32)] * 4 + [shp(ML_N_GATES, f32)] \
        + [jax.ShapeDtypeStruct((B, ML_N_GATES, T), f32)]
    args = (x3, sc, sh, *consts)
    if pending is not None:
        names = ("x",) + names
        in_specs = [tok(d), mod, pl.BlockSpec((d, tn), lambda b, t: (0, b * nt + t))] + in_specs[1:]
        out_specs, out_shape = [tok(d)] + out_specs, [shp(d, f32)] + out_shape
        args = (x3, *pending, sc, sh, *consts)
    outs = pl.pallas_call(
        functools.partial(_in_proj_kernel, pending=pending is not None),
        grid=(B, nt),
        in_specs=in_specs,
        out_specs=out_specs,
        out_shape=out_shape,
        compiler_params=pltpu.CompilerParams(
            dimension_semantics=("parallel", "parallel"), vmem_limit_bytes=VMEM_LIMIT_BYTES),
        name="in_proj",
    )(*args)
    return dict(zip(names, outs))


def kernel(x, c, ctx, c_ctx, w_ada, b_ada, norm1_g, w_in, ml_gate_b, na_q_g, na_k_g, na_rpb, pool_w,
           pool_scale, ml_conv, ml_norm_g, w_out, norm2_g, peer_wq, peer_keys, peer_u, peer_v):
    depth = w_in.shape[0]
    B, S, D = x.shape
    Lc = ctx.shape[1]
    hc1 = ctx.reshape(1, B * Lc, D)
    u_bf = _cast_bf16(peer_u.reshape(depth * PEER_EXPERTS, D))
    v_bf = _cast_bf16(peer_v.reshape(depth * PEER_EXPERTS, D))
    pend_x = pend_c = None
    for l in range(depth):
        need_ctx = l < depth - 1
        wqt_bf = peer_wq[l].astype(jnp.bfloat16).T
        keys_bf = peer_keys[l].astype(jnp.bfloat16).reshape(2 * PEER_HEADS, PEER_KEYS, PEER_DKEY // 2)
        mod = _ada_mod(jnp.concatenate([c, c_ctx[None, :]], axis=0), w_ada, b_ada, l).reshape(B + 1, 6, 1, D)
        sh1, sc1, g1, sh2, sc2, g2 = (mod[:B, i] for i in range(6))
        csh1, csc1, cg1, csh2, csc2, cg2 = (mod[B:, i] for i in range(6))
        a = _in_proj(x, sc1, sh1, norm1_g[l], w_in[l], na_q_g[l], na_k_g[l], pend_x)
        ac = _in_proj(hc1, csc1, csh1, norm1_g[l], w_in[l], na_q_g[l], na_k_g[l], pend_c)
        x, hc1 = a.get("x", x), ac.pop("x", hc1)
        ac = {k: v.reshape(B, Lc, v.shape[-1]) for k, v in ac.items() if k != "gr"} | {
            "gr": jnp.transpose(ac["gr"].reshape(ML_N_GATES, B, Lc), (1, 0, 2))}
        na_lat = _na_latent(a["q"], a["k"], a["v"], ac["k"], ac["v"], na_rpb[l])
        pool_lat = _pool_mix(a["pool"], pool_w[l], pool_scale[l])
        ml_lat, ml_ctx = _mlstm_mix(a["mq"], a["mk"], a["mv"], a["mo"], a["gc"], a["gr"],
                                    ac["mq"], ac["mk"], ac["mv"], ac["mo"], ac["gc"], ac["gr"],
                                    ml_conv[l], ml_gate_b[l], ml_norm_g[l])
        wo_bf = w_out[l].astype(jnp.bfloat16)
        x, yt = _mix_and_peer(x, g1, na_lat, pool_lat, ml_lat, wo_bf, sc2, sh2, norm2_g[l], wqt_bf, keys_bf,
                              u_bf, v_bf, l)
        pend_x = (g2, yt)
        if need_ctx:
            na_ctx = _na_context(ac["q"], ac["k"], ac["v"])
            pool_ctx = _pool_mix(ac["pool"], pool_w[l], pool_scale[l])
            flat = lambda t: t.reshape(1, B * Lc, t.shape[-1])
            hc1, yt_c = _mix_and_peer(hc1, cg1, flat(na_ctx), flat(pool_ctx), flat(ml_ctx), wo_bf, csc2, csh2,
                                      norm2_g[l], wqt_bf, keys_bf, u_bf, v_bf, l)
            pend_c = (cg2, yt_c)
    return _residual_t(x, *pend_x)
```
